```python
import math
import jax, jax.numpy as jnp
from jax import lax
import numpy as np

D_MODEL = 1024
BATCH = 8
SEQ = 2048
DEPTH = 2

N_MIXERS = 2
N_A = (DEPTH + 1) // 2
N_B = DEPTH // 2

HY_ORDER = 2
HY_STREAMS = HY_ORDER + 1
HY_EMB = 33
HY_BANDS = (HY_EMB - 1) // 2
HY_FILT = 64
HY_DECAY_TARGET = 1e-2
HY_FAST_DECAY = 0.3
HY_SLOW_DECAY = 1.5
HY_FILTER_OUT_SCALE = 0.05

HEAD_DIM = 64
N_HEADS = D_MODEL // HEAD_DIM
N_KV = N_HEADS // 4
GROUP = N_HEADS // N_KV
WINDOW = 128
BLOCK = 128
ROT_DIM = HEAD_DIM // 4
ROPE_THETA = 500000.0

D_FF = 4 * D_MODEL
EPS = 1e-6
NEG = -1e30

kernel_name = "hybrid_hyena_swa_sandwich_encoder"


def rms_norm(x, g):
    xf = x.astype(jnp.float32)
    y = xf * lax.rsqrt(jnp.mean(xf * xf, axis=-1, keepdims=True) + EPS)
    return (y * g.astype(jnp.float32)).astype(x.dtype)


def short_conv(u, w, b):
    L = u.shape[1]
    up = jnp.pad(u, ((0, 0), (1, 1), (0, 0)))
    return up[:, :L] * w[0] + up[:, 1:L + 1] * w[1] + up[:, 2:] * w[2] + b


def hyena_filters(L, f_w1, f_b1, f_w2, f_b2, f_w3, f_b3, f_freq, f_wout):
    f32 = jnp.float32
    t = jnp.linspace(0.0, 1.0, L, dtype=f32)[:, None]
    w = (2.0 * math.pi / L) * jnp.arange(L, dtype=f32)[:, None]
    f = jnp.linspace(1e-4, HY_BANDS - 1, HY_BANDS, dtype=f32)[None, :]
    z = jnp.concatenate([t, jnp.cos(f * w), -jnp.sin(f * w)], axis=-1)
    freq = f_freq.astype(f32)
    h = jnp.sin(freq * (z @ f_w1.astype(f32) + f_b1.astype(f32)))
    h = jnp.sin(freq * (h @ f_w2.astype(f32) + f_b2.astype(f32)))
    h = jnp.sin(freq * (h @ f_w3.astype(f32) + f_b3.astype(f32)))
    h = (h @ f_wout.astype(f32)).reshape(L, 2, HY_ORDER, D_MODEL)
    max_decay = math.log(HY_DECAY_TARGET) / HY_FAST_DECAY
    min_decay = math.log(HY_DECAY_TARGET) / HY_SLOW_DECAY
    deltas = jnp.linspace(min_decay, max_decay, D_MODEL, dtype=f32)
    decay = jnp.exp(-t * jnp.abs(deltas)[None, :])
    return h * decay[:, None, None, :]


def long_conv(u, k_hat, bias):
    L = u.shape[1]
    u_hat = jnp.fft.rfft(u, n=2 * L, axis=1)
    y = jnp.fft.irfft(u_hat * k_hat[None], n=2 * L, axis=1)[:, :L]
    return y + u * bias


def hyena_mixer(x, w_in, conv_w, conv_b, f_w1, f_b1, f_w2, f_b2, f_w3, f_b3,
                f_freq, f_wout, bias, w_out):
    f32 = jnp.float32
    B, L, _ = x.shape
    u = (x @ w_in).astype(f32)
    u = short_conv(u, conv_w.astype(f32), conv_b.astype(f32))
    v, x1, x2 = jnp.split(u, HY_STREAMS, axis=-1)
    h = hyena_filters(L, f_w1, f_b1, f_w2, f_b2, f_w3, f_b3, f_freq, f_wout)
    k_full = jnp.concatenate(
        [h[:, 0], jnp.zeros((1, HY_ORDER, D_MODEL), f32), h[:0:-1, 1]], axis=0)
    k_hat = jnp.fft.rfft(k_full, axis=0)
    bias = bias.astype(f32)
    gates = (x1, x2)
    z = v
    for n in range(HY_ORDER):
        z = gates[n] * long_conv(z, k_hat[:, n], bias[n])
    return z.astype(x.dtype) @ w_out


def rope_partial(t, pos):
    f32 = jnp.float32
    half = ROT_DIM // 2
    inv = ROPE_THETA ** (-jnp.arange(0, ROT_DIM, 2, dtype=f32) / ROT_DIM)
    ang = pos.astype(f32)[:, None] * inv[None, :]
    cos = jnp.cos(ang)[None, :, None, :]
    sin = jnp.sin(ang)[None, :, None, :]
    tr = t[..., :ROT_DIM].astype(f32)
    a, b = tr[..., :half], tr[..., half:]
    rot = jnp.concatenate([a * cos - b * sin, b * cos + a * sin], axis=-1)
    return jnp.concatenate([rot.astype(t.dtype), t[..., ROT_DIM:]], axis=-1)


def window_attention(x, w_qkv, sink, w_o):
    f32 = jnp.float32
    B, S, _ = x.shape
    nb = S // BLOCK
    qkv = x @ w_qkv
    q, k, v = jnp.split(qkv, [N_HEADS * HEAD_DIM, (N_HEADS + N_KV) * HEAD_DIM], axis=-1)
    q = q.reshape(B, S, N_HEADS, HEAD_DIM)
    k = k.reshape(B, S, N_KV, HEAD_DIM)
    v = v.reshape(B, S, N_KV, HEAD_DIM)
    pos = jnp.arange(S)
    q = rope_partial(q, pos)
    k = rope_partial(k, pos)
    qb = q.reshape(B, nb, BLOCK, N_KV, GROUP, HEAD_DIM).astype(f32)

    def band(t):
        tp = jnp.pad(t, ((0, 0), (BLOCK, BLOCK), (0, 0), (0, 0)))
        tp = tp.reshape(B, nb + 2, BLOCK, N_KV, HEAD_DIM)
        return jnp.concatenate([tp[:, :-2], tp[:, 1:-1], tp[:, 2:]], axis=2).astype(f32)

    kb, vb = band(k), band(v)
    s = jnp.einsum('bnqhgd,bnkhd->bnhgqk', qb, kb) * (HEAD_DIM ** -0.5)
    qpos = jnp.arange(nb)[:, None] * BLOCK + jnp.arange(BLOCK)[None, :]
    kpos = jnp.arange(nb)[:, None] * BLOCK - BLOCK + jnp.arange(3 * BLOCK)[None, :]
    rel = kpos[:, None, :] - qpos[:, :, None]
    valid = (jnp.abs(rel) <= WINDOW) & (kpos[:, None, :] >= 0) & (kpos[:, None, :] < S)
    s = jnp.where(valid[None, :, None, None, :, :], s, NEG)
    sink_l = sink.astype(f32).reshape(1, 1, N_KV, GROUP, 1, 1)
    sink_b = jnp.broadcast_to(sink_l, s.shape[:-1] + (1,))
    p = jax.nn.softmax(jnp.concatenate([s, sink_b], axis=-1), axis=-1)[..., :-1]
    o = jnp.einsum('bnhgqk,bnkhd->bnqhgd', p, vb)
    o = o.reshape(B, S, N_HEADS * HEAD_DIM).astype(x.dtype)
    return o @ w_o


def squared_relu_mlp(x, w_up, w_down):
    return jnp.square(jax.nn.relu(x @ w_up)) @ w_down


def setup_inputs(seed: int = 0) -> dict:
    key = jax.random.key(seed)
    ks = jax.random.split(key, 32)
    f32 = jnp.float32
    D = D_MODEL

    def nrm(k, shape, scale):
        return jax.random.normal(k, shape, f32) * scale

    def gain(k, shape):
        return 1.0 + 0.05 * jax.random.normal(k, shape, f32)

    return {
        "x": jax.random.normal(ks[0], (BATCH, SEQ, D), f32),
        "norm_mix_pre": gain(ks[1], (DEPTH, D)),
        "norm_mix_post": gain(ks[2], (DEPTH, D)),
        "norm_mlp_pre": gain(ks[3], (DEPTH, D)),
        "norm_mlp_post": gain(ks[4], (DEPTH, D)),
        "w_up": nrm(ks[5], (DEPTH, D, D_FF), D ** -0.5),
        "w_down": nrm(ks[6], (DEPTH, D_FF, D), D_FF ** -0.5),
        "hy_w_in": nrm(ks[7], (N_A, D, HY_STREAMS * D), D ** -0.5),
        "hy_conv_w": nrm(ks[8], (N_A, 3, HY_STREAMS * D), 3 ** -0.5),
        "hy_conv_b": nrm(ks[9], (N_A, HY_STREAMS * D), 0.02),
        "hy_f_w1": nrm(ks[10], (N_A, HY_EMB, HY_FILT), HY_EMB ** -0.5),
        "hy_f_b1": nrm(ks[11], (N_A, HY_FILT), 0.1),
        "hy_f_w2": nrm(ks[12], (N_A, HY_FILT, HY_FILT), HY_FILT ** -0.5),
        "hy_f_b2": nrm(ks[13], (N_A, HY_FILT), 0.1),
        "hy_f_w3": nrm(ks[14], (N_A, HY_FILT, HY_FILT), HY_FILT ** -0.5),
        "hy_f_b3": nrm(ks[15], (N_A, HY_FILT), 0.1),
        "hy_f_freq": gain(ks[16], (N_A, HY_FILT)),
        "hy_f_wout": nrm(ks[17], (N_A, HY_FILT, 2 * HY_ORDER * D), HY_FILTER_OUT_SCALE * HY_FILT ** -0.5),
        "hy_bias": nrm(ks[18], (N_A, HY_ORDER, D), 0.1),
        "hy_w_out": nrm(ks[19], (N_A, D, D), D ** -0.5),
        "at_w_qkv": nrm(ks[20], (N_B, D, (N_HEADS + 2 * N_KV) * HEAD_DIM), D ** -0.5),
        "at_sink": nrm(ks[21], (N_B, N_HEADS), 0.5),
        "at_w_o": nrm(ks[22], (N_B, N_HEADS * HEAD_DIM, D), (N_HEADS * HEAD_DIM) ** -0.5),
    }


def reference(x, norm_mix_pre, norm_mix_post, norm_mlp_pre, norm_mlp_post, w_up, w_down,
              hy_w_in, hy_conv_w, hy_conv_b, hy_f_w1, hy_f_b1, hy_f_w2, hy_f_b2, hy_f_w3,
              hy_f_b3, hy_f_freq, hy_f_wout, hy_bias, hy_w_out, at_w_qkv, at_sink, at_w_o):
    h = x
    for i in range(DEPTH):
        hn = rms_norm(h, norm_mix_pre[i])
        j = i // N_MIXERS
        if i % N_MIXERS == 0:
            m = hyena_mixer(hn, hy_w_in[j], hy_conv_w[j], hy_conv_b[j], hy_f_w1[j], hy_f_b1[j],
                            hy_f_w2[j], hy_f_b2[j], hy_f_w3[j], hy_f_b3[j], hy_f_freq[j],
                            hy_f_wout[j], hy_bias[j], hy_w_out[j])
        else:
            m = window_attention(hn, at_w_qkv[j], at_sink[j], at_w_o[j])
        h = h + rms_norm(m, norm_mix_post[i])
        hn = rms_norm(h, norm_mlp_pre[i])
        h = h + rms_norm(squared_relu_mlp(hn, w_up[i], w_down[i]), norm_mlp_post[i])
    return h
```

```python
import functools
import math

import numpy as np
import jax
import jax.numpy as jnp
from jax import lax
from jax.experimental import pallas as pl
from jax.experimental.pallas import tpu as pltpu

BF16 = jnp.bfloat16
F32 = jnp.float32

EPS = 1e-6
NEG = -1e30

HY_ORDER = 2
HY_EMB = 33
HY_BANDS = (HY_EMB - 1) // 2
HY_DECAY_TARGET = 1e-2
HY_FAST_DECAY = 0.3
HY_SLOW_DECAY = 1.5
HEAD_DIM = 64
KV_GROUP = 4
WINDOW = 128
ROT_DIM = HEAD_DIM // 4
ROPE_THETA = 500000.0

CONV_BLOCKS = 4
CH_BLOCK = 256

VMEM_LIMIT = 56 * 1024 * 1024


def _cparams(n_axes):
    return pltpu.CompilerParams(
        dimension_semantics=("arbitrary",) * n_axes,
        vmem_limit_bytes=VMEM_LIMIT)


def _resident(shape, index_map):
    return pl.BlockSpec(shape, index_map, pipeline_mode=pl.Buffered(1))


def _rms(x, g):
    ms = jnp.mean(x * x, axis=-1, keepdims=True)
    return x * lax.rsqrt(ms + EPS) * g


def _bdot(a, b):
    return jnp.dot(a, b, preferred_element_type=F32)


def _dft_tables(n):
    f = np.arange(n, dtype=np.int64)[:, None]
    r = np.arange(n, dtype=np.int64)[None, :]
    k_pos = ((2 * f + 1) * r) % (4 * n)
    k_neg = ((2 * f + 1) * (r - n)) % (4 * n)
    ang_pos = np.pi * k_pos / (2 * n)
    ang_neg = np.pi * k_neg / (2 * n)
    cpos, spos = np.cos(ang_pos), np.sin(ang_pos)
    cneg, sneg = np.cos(ang_neg), np.sin(ang_neg)
    cneg[:, 0] = 0.0
    sneg[:, 0] = 0.0
    fwd = np.concatenate([cpos, spos], axis=0)
    inv = np.concatenate([cpos.T, spos.T], axis=1) / n
    filt_c = np.concatenate([cneg, cpos], axis=1)
    filt_s = np.concatenate([sneg, spos], axis=1)
    return [np.asarray(a, np.float32) for a in (fwd, inv, filt_c, filt_s)]


def _filter_positions(L):
    t = np.linspace(0.0, 1.0, L)[:, None]
    w = (2.0 * np.pi / L) * np.arange(L)[:, None]
    f = np.linspace(1e-4, HY_BANDS - 1, HY_BANDS)[None, :]
    z = np.concatenate([t, np.cos(f * w), -np.sin(f * w)], axis=-1)
    return np.asarray(z, np.float32)


def _rope_tables(S):
    half = ROT_DIM // 2
    inv = ROPE_THETA ** (-np.arange(0, ROT_DIM, 2, dtype=np.float64) / ROT_DIM)
    ang = np.arange(S, dtype=np.float64)[:, None] * inv[None, :]
    lane = np.arange(128) % HEAD_DIM
    ang_l = ang[:, lane % half]
    cos_t = np.where(lane[None, :] < ROT_DIM, np.cos(ang_l), 1.0)
    sin_lo = np.where(lane[None, :] < half, -np.sin(ang_l), 0.0)
    sin_hi = np.where((lane[None, :] >= half) & (lane[None, :] < ROT_DIM), np.sin(ang_l), 0.0)
    return [np.asarray(a, np.float32) for a in (cos_t, sin_lo, sin_hi)]


def _prenorm_kernel(x_ref, g_ref, o_ref):
    o_ref[...] = _rms(x_ref[...], g_ref[...]).astype(o_ref.dtype)


def _prenorm(x2d, g, tm=1024):
    M, D = x2d.shape
    return pl.pallas_call(
        _prenorm_kernel,
        grid=(M // tm,),
        in_specs=[pl.BlockSpec((tm, D), lambda i: (i, 0)),
                  pl.BlockSpec((1, D), lambda i: (0, 0))],
        out_specs=pl.BlockSpec((tm, D), lambda i: (i, 0)),
        out_shape=jax.ShapeDtypeStruct((M, D), BF16),
        compiler_params=_cparams(1),
        name="prenorm",
    )(x2d, g.reshape(1, D))


def _filter_kernel(z_ref, w1_ref, b1_ref, w2_ref, b2_ref, w3_ref, b3_ref, fr_ref,
                   wout_ref, dl_ref, o_ref, hid_ref, *, n_fwd_blocks):
    j = pl.program_id(0)
    hi = lax.Precision.HIGHEST

    @pl.when(j == 0)
    def _():
        fr = fr_ref[...]
        h = jnp.sin(fr * (jnp.dot(z_ref[...], w1_ref[...], precision=hi) + b1_ref[...]))
        h = jnp.sin(fr * (jnp.dot(h, w2_ref[...], precision=hi) + b2_ref[...]))
        h = jnp.sin(fr * (jnp.dot(h, w3_ref[...], precision=hi) + b3_ref[...]))
        hid_ref[...] = h

    L, W = o_ref.shape
    row = lax.broadcasted_iota(jnp.int32, (L, W), 0)
    t = row.astype(F32) * (1.0 / (L - 1))
    decay = jnp.exp(-t * dl_ref[...])
    val = jnp.dot(hid_ref[...], wout_ref[...], precision=hi) * decay
    keep = jnp.logical_or(row > 0, j < n_fwd_blocks)
    o_ref[...] = jnp.where(keep, val, 0.0)


def _hyena_filters(L, D, f_w1, f_b1, f_w2, f_b2, f_w3, f_b3, f_freq, f_wout):
    FW = f_w1.shape[1]
    z = jnp.asarray(np.pad(_filter_positions(L), ((0, 0), (0, FW - HY_EMB))))
    w1 = jnp.pad(f_w1, ((0, FW - HY_EMB), (0, 0)))
    max_decay = math.log(HY_DECAY_TARGET) / HY_FAST_DECAY
    min_decay = math.log(HY_DECAY_TARGET) / HY_SLOW_DECAY
    absdelta = jnp.asarray(np.abs(np.linspace(min_decay, max_decay, D)).astype(np.float32))[None, :]
    n_blocks = f_wout.shape[1] // D
    small = lambda shape: pl.BlockSpec(shape, lambda j: (0, 0))
    return pl.pallas_call(
        functools.partial(_filter_kernel, n_fwd_blocks=n_blocks // 2),
        grid=(n_blocks,),
        in_specs=[small((L, FW)), small((FW, FW)), small((1, FW)), small((FW, FW)), small((1, FW)),
                  small((FW, FW)), small((1, FW)), small((1, FW)),
                  pl.BlockSpec((FW, D), lambda j: (0, j)),
                  small((1, D))],
        out_specs=pl.BlockSpec((L, D), lambda j: (0, j)),
        out_shape=jax.ShapeDtypeStruct((L, n_blocks * D), F32),
        scratch_shapes=[pltpu.VMEM((L, FW), F32)],
        compiler_params=_cparams(1),
        name="hyena_filter",
    )(z, w1, f_b1.reshape(1, FW), f_w2, f_b2.reshape(1, FW), f_w3, f_b3.reshape(1, FW),
      f_freq.reshape(1, FW), f_wout, absdelta)


def _spectra_kernel(hf_ref, hb_ref, mc_ref, ms_ref, gc_ref, gs_ref, *, P, n):
    mc = mc_ref[...]
    ms = ms_ref[...]
    cpos = mc[:, n:]
    spos = ms[:, n:]
    for d in range(-(P - 1), P):
        if d == 0:
            hf0 = hf_ref[0:n, :]
            hb0 = hb_ref[0:n, :]
            gc = _bdot(cpos, (hf0 + hb0).astype(BF16))
            gs = _bdot(spos, (hf0 - hb0).astype(BF16))
        elif d > 0:
            taps = hf_ref[n * (d - 1):n * (d + 1), :].astype(BF16)
            gc = _bdot(mc, taps)
            gs = _bdot(ms, taps)
        else:
            a = -d
            taps = hb_ref[n * (a - 1):n * (a + 1), :].astype(BF16)
            gc = _bdot(mc, taps)
            gs = -_bdot(ms, taps)
        gc_ref[d + P - 1] = gc
        gs_ref[d + P - 1] = gs


def _filter_spectra(hfilt, mc, ms, P, cw=256):
    L, W = hfilt.shape
    n = L // P
    half = W // 2
    nblk = half // cw
    out_sd = jax.ShapeDtypeStruct((2 * P - 1, n, half), F32)
    return pl.pallas_call(
        functools.partial(_spectra_kernel, P=P, n=n),
        grid=(nblk,),
        in_specs=[pl.BlockSpec((L, cw), lambda j: (0, j)),
                  pl.BlockSpec((L, cw), lambda j: (0, nblk + j)),
                  pl.BlockSpec((n, 2 * n), lambda j: (0, 0)),
                  pl.BlockSpec((n, 2 * n), lambda j: (0, 0))],
        out_specs=[pl.BlockSpec((2 * P - 1, n, cw), lambda j: (0, 0, j)),
                   pl.BlockSpec((2 * P - 1, n, cw), lambda j: (0, 0, j))],
        out_shape=[out_sd, out_sd],
        compiler_params=_cparams(1),
        name="filter_spectra",
    )(hfilt, hfilt, mc, ms)


def _hyena_kernel(hn_ref, wv_ref, w1_ref, w2_ref, cwv_ref, cw1_ref, cw2_ref,
                  cbv_ref, cb1_ref, cb2_ref, gc0_ref, gs0_ref, gc1_ref, gs1_ref,
                  bias_ref, fwd_ref, inv_ref, z_ref,
                  raw_ref, v_ref, x1_ref, x2_ref, uspec_ref, yspec_ref, *, P, n, fchunk):
    L = P * n
    hn = hn_ref[0]
    row = lax.broadcasted_iota(jnp.int32, (n, 1), 0)

    for w_ref, cw_ref, cb_ref, dst in ((wv_ref, cwv_ref, cbv_ref, v_ref),
                                       (w1_ref, cw1_ref, cb1_ref, x1_ref),
                                       (w2_ref, cw2_ref, cb2_ref, x2_ref)):
        raw_ref[...] = _bdot(hn, w_ref[...])
        cw = cw_ref[...]
        cb = cb_ref[...]
        for i in range(P):
            cur = raw_ref[n * i:n * (i + 1), :]
            prev = pltpu.roll(cur, 1, axis=0)
            nxt = pltpu.roll(cur, n - 1, axis=0)
            first = raw_ref[n * i - 1:n * i, :] if i > 0 else jnp.zeros_like(cur[0:1])
            last = raw_ref[n * (i + 1):n * (i + 1) + 1, :] if i < P - 1 else jnp.zeros_like(cur[0:1])
            prev = jnp.where(row == 0, first, prev)
            nxt = jnp.where(row == n - 1, last, nxt)
            dst[n * i:n * (i + 1), :] = prev * cw[0:1] + cur * cw[1:2] + nxt * cw[2:3] + cb

    def long_conv(src_ref, gc_ref, gs_ref):
        fwd = fwd_ref[...]
        for j in range(P):
            uspec_ref[j] = _bdot(fwd, src_ref[n * j:n * (j + 1), :].astype(BF16))

        def body(c, carry):
            f0 = pl.multiple_of(c * fchunk, fchunk)
            uc = [uspec_ref[j, pl.ds(f0, fchunk), :] for j in range(P)]
            us = [uspec_ref[j, pl.ds(n + f0, fchunk), :] for j in range(P)]
            for i in range(P):
                yc = None
                ys = None
                for j in range(P):
                    gc = gc_ref[i - j + P - 1, pl.ds(f0, fchunk), :]
                    gs = gs_ref[i - j + P - 1, pl.ds(f0, fchunk), :]
                    tc = gc * uc[j] - gs * us[j]
                    ts = gc * us[j] + gs * uc[j]
                    yc = tc if yc is None else yc + tc
                    ys = ts if ys is None else ys + ts
                yspec_ref[i, pl.ds(f0, fchunk), :] = yc.astype(BF16)
                yspec_ref[i, pl.ds(n + f0, fchunk), :] = ys.astype(BF16)
            return carry

        lax.fori_loop(0, n // fchunk, body, 0)

    bias = bias_ref[...]
    inv = inv_ref[...]

    long_conv(v_ref, gc0_ref, gs0_ref)
    for i in range(P):
        sl = slice(n * i, n * (i + 1))
        y = _bdot(inv, yspec_ref[i])
        v_ref[sl, :] = x1_ref[sl, :] * (y + v_ref[sl, :] * bias[0:1])

    long_conv(v_ref, gc1_ref, gs1_ref)
    for i in range(P):
        sl = slice(n * i, n * (i + 1))
        y = _bdot(inv, yspec_ref[i])
        z_ref[0, sl, :] = (x2_ref[sl, :] * (y + v_ref[sl, :] * bias[1:2])).astype(z_ref.dtype)


def _hyena_core(hn, w_in, conv_w, conv_b, gc, gs, bias, fwd, inv, P):
    B, L, D = hn.shape
    n = L // P
    cb_n = D // CH_BLOCK
    C = CH_BLOCK
    nd = 2 * P - 1

    def stream_spec(shape, s):
        return pl.BlockSpec(shape, lambda c, b: (0, s * cb_n + c))

    in_specs = (
        [pl.BlockSpec((1, L, D), lambda c, b: (b, 0, 0))]
        + [stream_spec((D, C), s) for s in range(3)]
        + [stream_spec((3, C), s) for s in range(3)]
        + [stream_spec((1, C), s) for s in range(3)]
        + [_resident((nd, n, C), lambda c, b: (0, 0, c)),
           _resident((nd, n, C), lambda c, b: (0, 0, c)),
           _resident((nd, n, C), lambda c, b: (0, 0, cb_n + c)),
           _resident((nd, n, C), lambda c, b: (0, 0, cb_n + c)),
           pl.BlockSpec((HY_ORDER, C), lambda c, b: (0, c)),
           pl.BlockSpec((2 * n, n), lambda c, b: (0, 0)),
           pl.BlockSpec((n, 2 * n), lambda c, b: (0, 0))])
    return pl.pallas_call(
        functools.partial(_hyena_kernel, P=P, n=n, fchunk=32),
        grid=(cb_n, B),
        in_specs=in_specs,
        out_specs=pl.BlockSpec((1, L, C), lambda c, b: (b, 0, c)),
        out_shape=jax.ShapeDtypeStruct((B, L, D), BF16),
        scratch_shapes=[pltpu.VMEM((L, C), F32),
                        pltpu.VMEM((L, C), F32),
                        pltpu.VMEM((L, C), F32),
                        pltpu.VMEM((L, C), F32),
                        pltpu.VMEM((P, 2 * n, C), F32),
                        pltpu.VMEM((P, 2 * n, C), BF16)],
        compiler_params=_cparams(2),
        name="hyena_core",
    )(hn, w_in, w_in, w_in, conv_w, conv_w, conv_w, conv_b, conv_b, conv_b,
      gc, gs, gc, gs, bias, fwd, inv)


def _outproj_kernel(a_ref, w_ref, r_ref, g_ref, o_ref):
    m = _bdot(a_ref[...], w_ref[...])
    o_ref[...] = r_ref[...] + _rms(m, g_ref[...])


def _outproj(a, w, resid, g, tm=512):
    M, K = a.shape
    D = w.shape[1]
    return pl.pallas_call(
        _outproj_kernel,
        grid=(M // tm,),
        in_specs=[pl.BlockSpec((tm, K), lambda i: (i, 0)),
                  _resident((K, D), lambda i: (0, 0)),
                  pl.BlockSpec((tm, D), lambda i: (i, 0)),
                  pl.BlockSpec((1, D), lambda i: (0, 0))],
        out_specs=pl.BlockSpec((tm, D), lambda i: (i, 0)),
        out_shape=jax.ShapeDtypeStruct((M, D), F32),
        compiler_params=_cparams(1),
        name="outproj",
    )(a, w, resid, g.reshape(1, D))


def _mlp_kernel(h_ref, gpre_ref, gpost_ref, wu_ref, wd_ref, o_ref, *, fchunk):
    h = h_ref[...]
    hn = _rms(h, gpre_ref[...]).astype(BF16)
    FF = wu_ref.shape[1]
    acc = None
    for c in range(FF // fchunk):
        a = _bdot(hn, wu_ref[:, c * fchunk:(c + 1) * fchunk])
        a = jnp.maximum(a, 0.0)
        a = (a * a).astype(BF16)
        part = _bdot(a, wd_ref[c * fchunk:(c + 1) * fchunk, :])
        acc = part if acc is None else acc + part
    o_ref[...] = h + _rms(acc, gpost_ref[...])


def _mlp(h, g_pre, g_post, w_up, w_down, tm=512, fchunk=1024):
    M, D = h.shape
    FF = w_up.shape[1]
    return pl.pallas_call(
        functools.partial(_mlp_kernel, fchunk=fchunk),
        grid=(M // tm,),
        in_specs=[pl.BlockSpec((tm, D), lambda i: (i, 0)),
                  pl.BlockSpec((1, D), lambda i: (0, 0)),
                  pl.BlockSpec((1, D), lambda i: (0, 0)),
                  _resident((D, FF), lambda i: (0, 0)),
                  _resident((FF, D), lambda i: (0, 0))],
        out_specs=pl.BlockSpec((tm, D), lambda i: (i, 0)),
        out_shape=jax.ShapeDtypeStruct((M, D), F32),
        compiler_params=_cparams(1),
        name="mlp",
    )(h, g_pre.reshape(1, D), g_post.reshape(1, D), w_up, w_down)


def _qkv_kernel(h_ref, g_ref, w_ref, ct_ref, slo_ref, shi_ref, q_ref, k_ref, v_ref, *, nq, nk):
    hn = _rms(h_ref[...], g_ref[...]).astype(BF16)
    qkv = _bdot(hn, w_ref[...])
    ct = ct_ref[...]
    slo = slo_ref[...]
    shi = shi_ref[...]
    half = ROT_DIM // 2
    scale = HEAD_DIM ** -0.5
    for j in range((nq + nk) // 128):
        t = qkv[:, 128 * j:128 * (j + 1)]
        r = t * ct + pltpu.roll(t, 128 - half, axis=1) * slo + pltpu.roll(t, half, axis=1) * shi
        if 128 * j < nq:
            q_ref[:, 128 * j:128 * (j + 1)] = (r * scale).astype(q_ref.dtype)
        else:
            k_ref[:, 128 * j - nq:128 * (j + 1) - nq] = r.astype(k_ref.dtype)
    v_ref[...] = qkv[:, nq + nk:].astype(v_ref.dtype)


def _qkv(h, g, w, S, tm=512):
    M, D = h.shape
    nq = D
    nk = (w.shape[1] - nq) // 2
    ct, slo, shi = [jnp.asarray(a) for a in _rope_tables(S)]
    pb = S // tm
    tab = pl.BlockSpec((tm, 128), lambda i: (i % pb, 0))
    return pl.pallas_call(
        functools.partial(_qkv_kernel, nq=nq, nk=nk),
        grid=(M // tm,),
        in_specs=[pl.BlockSpec((tm, D), lambda i: (i, 0)),
                  pl.BlockSpec((1, D), lambda i: (0, 0)),
                  _resident((D, nq + 2 * nk), lambda i: (0, 0)),
                  tab, tab, tab],
        out_specs=[pl.BlockSpec((tm, nq), lambda i: (i, 0)),
                   pl.BlockSpec((tm, nk), lambda i: (i, 0)),
                   pl.BlockSpec((tm, nk), lambda i: (i, 0))],
        out_shape=[jax.ShapeDtypeStruct((M, nq), BF16),
                   jax.ShapeDtypeStruct((M, nk), BF16),
                   jax.ShapeDtypeStruct((M, nk), BF16)],
        compiler_params=_cparams(1),
        name="qkv_rope",
    )(h, g.reshape(1, D), w, ct, slo, shi)


def _attn_kernel(sink_ref, q_ref, k_ref, v_ref, o_ref, *, S, n_kv):
    BQ = WINDOW
    BK = 3 * WINDOW
    dn = (((1,), (1,)), ((), ()))

    def body(qb, carry):
        qs = pl.multiple_of(qb * BQ, BQ)
        ks = pl.multiple_of(jnp.clip(qs - WINDOW, 0, S - BK), WINDOW)
        qpos = qs + lax.broadcasted_iota(jnp.int32, (BQ, BK), 0)
        kpos = ks + lax.broadcasted_iota(jnp.int32, (BQ, BK), 1)
        maskadd = jnp.where(jnp.abs(kpos - qpos) <= WINDOW, 0.0, NEG).astype(F32)
        for g in range(n_kv):
            kb = k_ref[0, pl.ds(ks, BK), g * HEAD_DIM:(g + 1) * HEAD_DIM]
            vb = v_ref[0, pl.ds(ks, BK), g * HEAD_DIM:(g + 1) * HEAD_DIM]
            outs = []
            for hh in range(KV_GROUP):
                head = g * KV_GROUP + hh
                qh = q_ref[0, pl.ds(qs, BQ), head * HEAD_DIM:(head + 1) * HEAD_DIM]
                s = lax.dot_general(qh, kb, dn, preferred_element_type=F32) + maskadd
                sink = sink_ref[head]
                m = jnp.maximum(jnp.max(s, axis=-1, keepdims=True), sink)
                p = jnp.exp(s - m)
                denom = jnp.sum(p, axis=-1, keepdims=True) + jnp.exp(sink - m)
                o = _bdot(p.astype(BF16), vb)
                outs.append(o / denom)
            w = KV_GROUP * HEAD_DIM
            o_ref[0, pl.ds(qs, BQ), g * w:(g + 1) * w] = jnp.concatenate(outs, axis=-1).astype(o_ref.dtype)
        return carry

    lax.fori_loop(0, S // BQ, body, 0)


def _attention(q, k, v, sink):
    B, S, NQ = q.shape
    NK = k.shape[2]
    return pl.pallas_call(
        functools.partial(_attn_kernel, S=S, n_kv=NK // HEAD_DIM),
        grid=(B,),
        in_specs=[pl.BlockSpec(memory_space=pltpu.SMEM),
                  pl.BlockSpec((1, S, NQ), lambda b: (b, 0, 0)),
                  pl.BlockSpec((1, S, NK), lambda b: (b, 0, 0)),
                  pl.BlockSpec((1, S, NK), lambda b: (b, 0, 0))],
        out_specs=pl.BlockSpec((1, S, NQ), lambda b: (b, 0, 0)),
        out_shape=jax.ShapeDtypeStruct((B, S, NQ), BF16),
        compiler_params=_cparams(1),
        name="window_attention",
    )(sink, q, k, v)


def kernel(x, norm_mix_pre, norm_mix_post, norm_mlp_pre, norm_mlp_post, w_up, w_down, hy_w_in, hy_conv_w, hy_conv_b, hy_f_w1, hy_f_b1, hy_f_w2, hy_f_b2, hy_f_w3, hy_f_b3, hy_f_freq, hy_f_wout, hy_bias, hy_w_out, at_w_qkv, at_sink, at_w_o):
    B, L, D = x.shape
    M = B * L
    P = CONV_BLOCKS
    n = L // P
    depth = norm_mix_pre.shape[0]
    fwd, inv, filt_c, filt_s = [jnp.asarray(a).astype(BF16) for a in _dft_tables(n)]

    h = x.reshape(M, D)
    for i in range(depth):
        j = i // 2
        if i % 2 == 0:
            hfilt = _hyena_filters(L, D, hy_f_w1[j], hy_f_b1[j], hy_f_w2[j], hy_f_b2[j],
                                   hy_f_w3[j], hy_f_b3[j], hy_f_freq[j], hy_f_wout[j])
            gc, gs = _filter_spectra(hfilt, filt_c, filt_s, P)
            hn = _prenorm(h, norm_mix_pre[i]).reshape(B, L, D)
            z = _hyena_core(hn, hy_w_in[j].astype(BF16), hy_conv_w[j], hy_conv_b[j].reshape(1, -1),
                            gc, gs, hy_bias[j], fwd, inv, P)
            h = _outproj(z.reshape(M, D), hy_w_out[j].astype(BF16), h, norm_mix_post[i])
        else:
            q, k, v = _qkv(h, norm_mix_pre[i], at_w_qkv[j].astype(BF16), L)
            o = _attention(q.reshape(B, L, -1), k.reshape(B, L, -1), v.reshape(B, L, -1), at_sink[j])
            h = _outproj(o.reshape(M, -1), at_w_o[j].astype(BF16), h, norm_mix_post[i])
        h = _mlp(h, norm_mlp_pre[i], norm_mlp_post[i], w_up[i].astype(BF16), w_down[i].astype(BF16))
    return h.reshape(B, L, D)
```

```python
import functools
import math

import numpy as np
import jax
import jax.numpy as jnp
from jax import lax
from jax.experimental import pallas as pl
from jax.experimental.pallas import tpu as pltpu

BF16 = jnp.bfloat16
F32 = jnp.float32

EPS = 1e-6
NEG = -1e30

HY_ORDER = 2
HY_EMB = 33
HY_BANDS = (HY_EMB - 1) // 2
HY_DECAY_TARGET = 1e-2
HY_FAST_DECAY = 0.3
HY_SLOW_DECAY = 1.5
HEAD_DIM = 64
KV_GROUP = 4
WINDOW = 128
ROT_DIM = HEAD_DIM // 4
ROPE_THETA = 500000.0

CONV_BLOCKS = 4
CH_BLOCK = 256

VMEM_LIMIT = 56 * 1024 * 1024


def _cparams(n_axes):
    return pltpu.CompilerParams(
        dimension_semantics=("arbitrary",) * n_axes,
        vmem_limit_bytes=VMEM_LIMIT)


def _resident(shape, index_map):
    return pl.BlockSpec(shape, index_map, pipeline_mode=pl.Buffered(1))


def _rms(x, g):
    ms = jnp.mean(x * x, axis=-1, keepdims=True)
    return x * lax.rsqrt(ms + EPS) * g


def _bdot(a, b):
    return jnp.dot(a, b, preferred_element_type=F32)


def _dft_tables(n):
    f = np.arange(n, dtype=np.int64)[:, None]
    r = np.arange(n, dtype=np.int64)[None, :]
    k_pos = ((2 * f + 1) * r) % (4 * n)
    k_neg = ((2 * f + 1) * (r - n)) % (4 * n)
    ang_pos = np.pi * k_pos / (2 * n)
    ang_neg = np.pi * k_neg / (2 * n)
    cpos, spos = np.cos(ang_pos), np.sin(ang_pos)
    cneg, sneg = np.cos(ang_neg), np.sin(ang_neg)
    cneg[:, 0] = 0.0
    sneg[:, 0] = 0.0
    fwd = np.concatenate([cpos, spos], axis=0)
    inv = np.concatenate([cpos.T, spos.T], axis=1) / n
    filt_c = np.concatenate([cneg, cpos], axis=1)
    filt_s = np.concatenate([sneg, spos], axis=1)
    return [np.asarray(a, np.float32) for a in (fwd, inv, filt_c, filt_s)]


def _filter_positions(L):
    t = np.linspace(0.0, 1.0, L)[:, None]
    w = (2.0 * np.pi / L) * np.arange(L)[:, None]
    f = np.linspace(1e-4, HY_BANDS - 1, HY_BANDS)[None, :]
    z = np.concatenate([t, np.cos(f * w), -np.sin(f * w)], axis=-1)
    return np.asarray(z, np.float32)


def _rope_tables(S):
    half = ROT_DIM // 2
    inv = ROPE_THETA ** (-np.arange(0, ROT_DIM, 2, dtype=np.float64) / ROT_DIM)
    ang = np.arange(S, dtype=np.float64)[:, None] * inv[None, :]
    lane = np.arange(128) % HEAD_DIM
    ang_l = ang[:, lane % half]
    cos_t = np.where(lane[None, :] < ROT_DIM, np.cos(ang_l), 1.0)
    sin_lo = np.where(lane[None, :] < half, -np.sin(ang_l), 0.0)
    sin_hi = np.where((lane[None, :] >= half) & (lane[None, :] < ROT_DIM), np.sin(ang_l), 0.0)
    return [np.asarray(a, np.float32) for a in (cos_t, sin_lo, sin_hi)]


def _prenorm_kernel(x_ref, g_ref, o_ref):
    o_ref[...] = _rms(x_ref[...], g_ref[...]).astype(o_ref.dtype)


def _prenorm(x2d, g, tm=1024):
    M, D = x2d.shape
    return pl.pallas_call(
        _prenorm_kernel,
        grid=(M // tm,),
        in_specs=[pl.BlockSpec((tm, D), lambda i: (i, 0)),
                  pl.BlockSpec((1, D), lambda i: (0, 0))],
        out_specs=pl.BlockSpec((tm, D), lambda i: (i, 0)),
        out_shape=jax.ShapeDtypeStruct((M, D), BF16),
        compiler_params=_cparams(1),
        name="prenorm",
    )(x2d, g.reshape(1, D))


def _filter_kernel(z_ref, w1_ref, b1_ref, w2_ref, b2_ref, w3_ref, b3_ref, fr_ref,
                   wout_ref, dl_ref, o_ref, hid_ref, *, n_fwd_blocks):
    j = pl.program_id(0)
    hi = lax.Precision.HIGHEST

    @pl.when(j == 0)
    def _():
        fr = fr_ref[...]
        h = jnp.sin(fr * (jnp.dot(z_ref[...], w1_ref[...], precision=hi) + b1_ref[...]))
        h = jnp.sin(fr * (jnp.dot(h, w2_ref[...], precision=hi) + b2_ref[...]))
        h = jnp.sin(fr * (jnp.dot(h, w3_ref[...], precision=hi) + b3_ref[...]))
        hid_ref[...] = h

    L, W = o_ref.shape
    row = lax.broadcasted_iota(jnp.int32, (L, W), 0)
    t = row.astype(F32) * (1.0 / (L - 1))
    decay = jnp.exp(-t * dl_ref[...])
    val = jnp.dot(hid_ref[...], wout_ref[...], precision=hi) * decay
    keep = jnp.logical_or(row > 0, j < n_fwd_blocks)
    o_ref[...] = jnp.where(keep, val, 0.0)


def _hyena_filters(L, D, f_w1, f_b1, f_w2, f_b2, f_w3, f_b3, f_freq, f_wout):
    FW = f_w1.shape[1]
    z = jnp.asarray(np.pad(_filter_positions(L), ((0, 0), (0, FW - HY_EMB))))
    w1 = jnp.pad(f_w1, ((0, FW - HY_EMB), (0, 0)))
    max_decay = math.log(HY_DECAY_TARGET) / HY_FAST_DECAY
    min_decay = math.log(HY_DECAY_TARGET) / HY_SLOW_DECAY
    absdelta = jnp.asarray(np.abs(np.linspace(min_decay, max_decay, D)).astype(np.float32))[None, :]
    n_blocks = f_wout.shape[1] // D
    small = lambda shape: pl.BlockSpec(shape, lambda j: (0, 0))
    return pl.pallas_call(
        functools.partial(_filter_kernel, n_fwd_blocks=n_blocks // 2),
        grid=(n_blocks,),
        in_specs=[small((L, FW)), small((FW, FW)), small((1, FW)), small((FW, FW)), small((1, FW)),
                  small((FW, FW)), small((1, FW)), small((1, FW)),
                  pl.BlockSpec((FW, D), lambda j: (0, j)),
                  small((1, D))],
        out_specs=pl.BlockSpec((L, D), lambda j: (0, j)),
        out_shape=jax.ShapeDtypeStruct((L, n_blocks * D), F32),
        scratch_shapes=[pltpu.VMEM((L, FW), F32)],
        compiler_params=_cparams(1),
        name="hyena_filter",
    )(z, w1, f_b1.reshape(1, FW), f_w2, f_b2.reshape(1, FW), f_w3, f_b3.reshape(1, FW),
      f_freq.reshape(1, FW), f_wout, absdelta)


def _spectra_kernel(hf_ref, hb_ref, mc_ref, ms_ref, gc_ref, gs_ref, *, P, n):
    mc = mc_ref[...]
    ms = ms_ref[...]
    cpos = mc[:, n:]
    spos = ms[:, n:]
    for d in range(-(P - 1), P):
        if d == 0:
            hf0 = hf_ref[0:n, :]
            hb0 = hb_ref[0:n, :]
            gc = _bdot(cpos, (hf0 + hb0).astype(BF16))
            gs = _bdot(spos, (hf0 - hb0).astype(BF16))
        elif d > 0:
            taps = hf_ref[n * (d - 1):n * (d + 1), :].astype(BF16)
            gc = _bdot(mc, taps)
            gs = _bdot(ms, taps)
        else:
            a = -d
            taps = hb_ref[n * (a - 1):n * (a + 1), :].astype(BF16)
            gc = _bdot(mc, taps)
            gs = -_bdot(ms, taps)
        gc_ref[d + P - 1] = gc
        gs_ref[d + P - 1] = gs


def _filter_spectra(hfilt, mc, ms, P, cw=256):
    L, W = hfilt.shape
    n = L // P
    half = W // 2
    nblk = half // cw
    out_sd = jax.ShapeDtypeStruct((2 * P - 1, n, half), F32)
    return pl.pallas_call(
        functools.partial(_spectra_kernel, P=P, n=n),
        grid=(nblk,),
        in_specs=[pl.BlockSpec((L, cw), lambda j: (0, j)),
                  pl.BlockSpec((L, cw), lambda j: (0, nblk + j)),
                  pl.BlockSpec((n, 2 * n), lambda j: (0, 0)),
                  pl.BlockSpec((n, 2 * n), lambda j: (0, 0))],
        out_specs=[pl.BlockSpec((2 * P - 1, n, cw), lambda j: (0, 0, j)),
                   pl.BlockSpec((2 * P - 1, n, cw), lambda j: (0, 0, j))],
        out_shape=[out_sd, out_sd],
        compiler_params=_cparams(1),
        name="filter_spectra",
    )(hfilt, hfilt, mc, ms)


def _hyena_kernel(hn_ref, wv_ref, w1_ref, w2_ref, cwv_ref, cw1_ref, cw2_ref,
                  cbv_ref, cb1_ref, cb2_ref, gc0_ref, gs0_ref, gc1_ref, gs1_ref,
                  bias_ref, fwd_ref, inv_ref, z_ref,
                  raw_ref, v_ref, x1_ref, x2_ref, uspec_ref, yspec_ref, *, P, n, fchunk):
    L = P * n
    hn = hn_ref[0]
    row = lax.broadcasted_iota(jnp.int32, (n, 1), 0)

    for w_ref, cw_ref, cb_ref, dst in ((wv_ref, cwv_ref, cbv_ref, v_ref),
                                       (w1_ref, cw1_ref, cb1_ref, x1_ref),
                                       (w2_ref, cw2_ref, cb2_ref, x2_ref)):
        raw_ref[...] = _bdot(hn, w_ref[...])
        cw = cw_ref[...]
        cb = cb_ref[...]
        for i in range(P):
            cur = raw_ref[n * i:n * (i + 1), :]
            prev = pltpu.roll(cur, 1, axis=0)
            nxt = pltpu.roll(cur, n - 1, axis=0)
            first = raw_ref[n * i - 1:n * i, :] if i > 0 else jnp.zeros_like(cur[0:1])
            last = raw_ref[n * (i + 1):n * (i + 1) + 1, :] if i < P - 1 else jnp.zeros_like(cur[0:1])
            prev = jnp.where(row == 0, first, prev)
            nxt = jnp.where(row == n - 1, last, nxt)
            dst[n * i:n * (i + 1), :] = prev * cw[0:1] + cur * cw[1:2] + nxt * cw[2:3] + cb

    def long_conv(src_ref, gc_ref, gs_ref):
        fwd = fwd_ref[...]
        for j in range(P):
            uspec_ref[j] = _bdot(fwd, src_ref[n * j:n * (j + 1), :].astype(BF16))

        def body(c, carry):
            f0 = pl.multiple_of(c * fchunk, fchunk)
            uc = [uspec_ref[j, pl.ds(f0, fchunk), :] for j in range(P)]
            us = [uspec_ref[j, pl.ds(n + f0, fchunk), :] for j in range(P)]
            for i in range(P):
                yc = None
                ys = None
                for j in range(P):
                    gc = gc_ref[i - j + P - 1, pl.ds(f0, fchunk), :]
                    gs = gs_ref[i - j + P - 1, pl.ds(f0, fchunk), :]
                    tc = gc * uc[j] - gs * us[j]
                    ts = gc * us[j] + gs * uc[j]
                    yc = tc if yc is None else yc + tc
                    ys = ts if ys is None else ys + ts
                yspec_ref[i, pl.ds(f0, fchunk), :] = yc.astype(BF16)
                yspec_ref[i, pl.ds(n + f0, fchunk), :] = ys.astype(BF16)
            return carry

        lax.fori_loop(0, n // fchunk, body, 0)

    bias = bias_ref[...]
    inv = inv_ref[...]

    long_conv(v_ref, gc0_ref, gs0_ref)
    for i in range(P):
        sl = slice(n * i, n * (i + 1))
        y = _bdot(inv, yspec_ref[i])
        v_ref[sl, :] = x1_ref[sl, :] * (y + v_ref[sl, :] * bias[0:1])

    long_conv(v_ref, gc1_ref, gs1_ref)
    for i in range(P):
        sl = slice(n * i, n * (i + 1))
        y = _bdot(inv, yspec_ref[i])
        z_ref[0, sl, :] = (x2_ref[sl, :] * (y + v_ref[sl, :] * bias[1:2])).astype(z_ref.dtype)


def _hyena_core(hn, w_in, conv_w, conv_b, gc, gs, bias, fwd, inv, P):
    B, L, D = hn.shape
    n = L // P
    cb_n = D // CH_BLOCK
    C = CH_BLOCK
    nd = 2 * P - 1

    def stream_spec(shape, s):
        return pl.BlockSpec(shape, lambda c, b: (0, s * cb_n + c))

    in_specs = (
        [pl.BlockSpec((1, L, D), lambda c, b: (b, 0, 0))]
        + [stream_spec((D, C), s) for s in range(3)]
        + [stream_spec((3, C), s) for s in range(3)]
        + [stream_spec((1, C), s) for s in range(3)]
        + [_resident((nd, n, C), lambda c, b: (0, 0, c)),
           _resident((nd, n, C), lambda c, b: (0, 0, c)),
           _resident((nd, n, C), lambda c, b: (0, 0, cb_n + c)),
           _resident((nd, n, C), lambda c, b: (0, 0, cb_n + c)),
           pl.BlockSpec((HY_ORDER, C), lambda c, b: (0, c)),
           pl.BlockSpec((2 * n, n), lambda c, b: (0, 0)),
           pl.BlockSpec((n, 2 * n), lambda c, b: (0, 0))])
    return pl.pallas_call(
        functools.partial(_hyena_kernel, P=P, n=n, fchunk=32),
        grid=(cb_n, B),
        in_specs=in_specs,
        out_specs=pl.BlockSpec((1, L, C), lambda c, b: (b, 0, c)),
        out_shape=jax.ShapeDtypeStruct((B, L, D), BF16),
        scratch_shapes=[pltpu.VMEM((L, C), F32),
                        pltpu.VMEM((L, C), F32),
                        pltpu.VMEM((L, C), F32),
                        pltpu.VMEM((L, C), F32),
                        pltpu.VMEM((P, 2 * n, C), F32),
                        pltpu.VMEM((P, 2 * n, C), BF16)],
        compiler_params=_cparams(2),
        name="hyena_core",
    )(hn, w_in, w_in, w_in, conv_w, conv_w, conv_w, conv_b, conv_b, conv_b,
      gc, gs, gc, gs, bias, fwd, inv)


def _outproj_kernel(a_ref, w_ref, r_ref, g_ref, o_ref):
    m = _bdot(a_ref[...], w_ref[...])
    o_ref[...] = r_ref[...] + _rms(m, g_ref[...])


def _outproj(a, w, resid, g, tm=512):
    M, K = a.shape
    D = w.shape[1]
    return pl.pallas_call(
        _outproj_kernel,
        grid=(M // tm,),
        in_specs=[pl.BlockSpec((tm, K), lambda i: (i, 0)),
                  _resident((K, D), lambda i: (0, 0)),
                  pl.BlockSpec((tm, D), lambda i: (i, 0)),
                  pl.BlockSpec((1, D), lambda i: (0, 0))],
        out_specs=pl.BlockSpec((tm, D), lambda i: (i, 0)),
        out_shape=jax.ShapeDtypeStruct((M, D), F32),
        compiler_params=_cparams(1),
        name="outproj",
    )(a, w, resid, g.reshape(1, D))


def _mlp_kernel(h_ref, gpre_ref, gpost_ref, wu_ref, wd_ref, o_ref, *, fchunk):
    h = h_ref[...]
    hn = _rms(h, gpre_ref[...]).astype(BF16)
    FF = wu_ref.shape[1]
    acc = None
    for c in range(FF // fchunk):
        a = _bdot(hn, wu_ref[:, c * fchunk:(c + 1) * fchunk])
        a = jnp.maximum(a, 0.0)
        a = (a * a).astype(BF16)
        part = _bdot(a, wd_ref[c * fchunk:(c + 1) * fchunk, :])
        acc = part if acc is None else acc + part
    o_ref[...] = h + _rms(acc, gpost_ref[...])


def _mlp(h, g_pre, g_post, w_up, w_down, tm=512, fchunk=1024):
    M, D = h.shape
    FF = w_up.shape[1]
    return pl.pallas_call(
        functools.partial(_mlp_kernel, fchunk=fchunk),
        grid=(M // tm,),
        in_specs=[pl.BlockSpec((tm, D), lambda i: (i, 0)),
                  pl.BlockSpec((1, D), lambda i: (0, 0)),
                  pl.BlockSpec((1, D), lambda i: (0, 0)),
                  _resident((D, FF), lambda i: (0, 0)),
                  _resident((FF, D), lambda i: (0, 0))],
        out_specs=pl.BlockSpec((tm, D), lambda i: (i, 0)),
        out_shape=jax.ShapeDtypeStruct((M, D), F32),
        compiler_params=_cparams(1),
        name="mlp",
    )(h, g_pre.reshape(1, D), g_post.reshape(1, D), w_up, w_down)


def _qkv_kernel(h_ref, g_ref, w_ref, ct_ref, slo_ref, shi_ref, q_ref, k_ref, v_ref, *, nq, nk):
    hn = _rms(h_ref[...], g_ref[...]).astype(BF16)
    qkv = _bdot(hn, w_ref[...])
    ct = ct_ref[...]
    slo = slo_ref[...]
    shi = shi_ref[...]
    half = ROT_DIM // 2
    scale = HEAD_DIM ** -0.5
    for j in range((nq + nk) // 128):
        t = qkv[:, 128 * j:128 * (j + 1)]
        r = t * ct + pltpu.roll(t, 128 - half, axis=1) * slo + pltpu.roll(t, half, axis=1) * shi
        if 128 * j < nq:
            q_ref[:, 128 * j:128 * (j + 1)] = (r * scale).astype(q_ref.dtype)
        else:
            k_ref[:, 128 * j - nq:128 * (j + 1) - nq] = r.astype(k_ref.dtype)
    v_ref[...] = qkv[:, nq + nk:].astype(v_ref.dtype)


def _qkv(h, g, w, S, tm=512):
    M, D = h.shape
    nq = D
    nk = (w.shape[1] - nq) // 2
    ct, slo, shi = [jnp.asarray(a) for a in _rope_tables(S)]
    pb = S // tm
    tab = pl.BlockSpec((tm, 128), lambda i: (i % pb, 0))
    return pl.pallas_call(
        functools.partial(_qkv_kernel, nq=nq, nk=nk),
        grid=(M // tm,),
        in_specs=[pl.BlockSpec((tm, D), lambda i: (i, 0)),
                  pl.BlockSpec((1, D), lambda i: (0, 0)),
                  _resident((D, nq + 2 * nk), lambda i: (0, 0)),
                  tab, tab, tab],
        out_specs=[pl.BlockSpec((tm, nq), lambda i: (i, 0)),
                   pl.BlockSpec((tm, nk), lambda i: (i, 0)),
                   pl.BlockSpec((tm, nk), lambda i: (i, 0))],
        out_shape=[jax.ShapeDtypeStruct((M, nq), BF16),
                   jax.ShapeDtypeStruct((M, nk), BF16),
                   jax.ShapeDtypeStruct((M, nk), BF16)],
        compiler_params=_cparams(1),
        name="qkv_rope",
    )(h, g.reshape(1, D), w, ct, slo, shi)


def _attn_kernel(sink_ref, q_ref, k_ref, v_ref, o_ref, krep_ref, vrep_ref, *, S, n_kv):
    BQ = WINDOW
    BK = 3 * WINDOW
    GW = KV_GROUP * HEAD_DIM
    dn = (((1,), (1,)), ((), ()))

    for g in range(n_kv):
        kg = k_ref[0, :, g * HEAD_DIM:(g + 1) * HEAD_DIM]
        vg = v_ref[0, :, g * HEAD_DIM:(g + 1) * HEAD_DIM]
        krep_ref[g] = jnp.concatenate([kg] * KV_GROUP, axis=1)
        vrep_ref[g] = jnp.concatenate([vg] * KV_GROUP, axis=1)

    lane = lax.broadcasted_iota(jnp.int32, (BQ, GW), 1)
    head_lanes = [(lane >= hh * HEAD_DIM) & (lane < (hh + 1) * HEAD_DIM) for hh in range(KV_GROUP)]
    rowblk = lax.broadcasted_iota(jnp.int32, (KV_GROUP * BQ, 1), 0) // BQ

    def body(qb, carry):
        qs = pl.multiple_of(qb * BQ, BQ)
        ks = pl.multiple_of(jnp.clip(qs - WINDOW, 0, S - BK), WINDOW)
        qpos = qs + lax.broadcasted_iota(jnp.int32, (BQ, BK), 0)
        kpos = ks + lax.broadcasted_iota(jnp.int32, (BQ, BK), 1)
        maskadd = jnp.where(jnp.abs(kpos - qpos) <= WINDOW, 0.0, NEG).astype(F32)
        maskadd = jnp.concatenate([maskadd] * KV_GROUP, axis=0)
        for g in range(n_kv):
            qg = q_ref[0, pl.ds(qs, BQ), g * GW:(g + 1) * GW]
            qst = jnp.concatenate([jnp.where(head_lanes[hh], qg, jnp.zeros_like(qg))
                                   for hh in range(KV_GROUP)], axis=0)
            sink = jnp.zeros((KV_GROUP * BQ, 1), F32)
            for hh in range(KV_GROUP):
                sink = jnp.where(rowblk == hh, sink_ref[g * KV_GROUP + hh], sink)
            s = lax.dot_general(qst, krep_ref[g, pl.ds(ks, BK), :], dn,
                                preferred_element_type=F32) + maskadd
            m = jnp.maximum(jnp.max(s, axis=-1, keepdims=True), sink)
            p = jnp.exp(s - m)
            denom = jnp.sum(p, axis=-1, keepdims=True) + jnp.exp(sink - m)
            ost = _bdot(p.astype(BF16), vrep_ref[g, pl.ds(ks, BK), :]) / denom
            o = jnp.where(head_lanes[0], ost[0:BQ], 0.0)
            for hh in range(1, KV_GROUP):
                o = jnp.where(head_lanes[hh], ost[hh * BQ:(hh + 1) * BQ], o)
            o_ref[0, pl.ds(qs, BQ), g * GW:(g + 1) * GW] = o.astype(o_ref.dtype)
        return carry

    lax.fori_loop(0, S // BQ, body, 0)


def _attention(q, k, v, sink):
    B, S, NQ = q.shape
    NK = k.shape[2]
    n_kv = NK // HEAD_DIM
    return pl.pallas_call(
        functools.partial(_attn_kernel, S=S, n_kv=n_kv),
        grid=(B,),
        in_specs=[pl.BlockSpec(memory_space=pltpu.SMEM),
                  pl.BlockSpec((1, S, NQ), lambda b: (b, 0, 0)),
                  pl.BlockSpec((1, S, NK), lambda b: (b, 0, 0)),
                  pl.BlockSpec((1, S, NK), lambda b: (b, 0, 0))],
        out_specs=pl.BlockSpec((1, S, NQ), lambda b: (b, 0, 0)),
        out_shape=jax.ShapeDtypeStruct((B, S, NQ), BF16),
        scratch_shapes=[pltpu.VMEM((n_kv, S, KV_GROUP * HEAD_DIM), BF16),
                        pltpu.VMEM((n_kv, S, KV_GROUP * HEAD_DIM), BF16)],
        compiler_params=_cparams(1),
        name="window_attention",
    )(sink, q, k, v)


def kernel(x, norm_mix_pre, norm_mix_post, norm_mlp_pre, norm_mlp_post, w_up, w_down, hy_w_in, hy_conv_w, hy_conv_b, hy_f_w1, hy_f_b1, hy_f_w2, hy_f_b2, hy_f_w3, hy_f_b3, hy_f_freq, hy_f_wout, hy_bias, hy_w_out, at_w_qkv, at_sink, at_w_o):
    B, L, D = x.shape
    M = B * L
    P = CONV_BLOCKS
    n = L // P
    depth = norm_mix_pre.shape[0]
    fwd, inv, filt_c, filt_s = [jnp.asarray(a).astype(BF16) for a in _dft_tables(n)]

    h = x.reshape(M, D)
    for i in range(depth):
        j = i // 2
        if i % 2 == 0:
            hfilt = _hyena_filters(L, D, hy_f_w1[j], hy_f_b1[j], hy_f_w2[j], hy_f_b2[j],
                                   hy_f_w3[j], hy_f_b3[j], hy_f_freq[j], hy_f_wout[j])
            gc, gs = _filter_spectra(hfilt, filt_c, filt_s, P)
            hn = _prenorm(h, norm_mix_pre[i]).reshape(B, L, D)
            z = _hyena_core(hn, hy_w_in[j].astype(BF16), hy_conv_w[j], hy_conv_b[j].reshape(1, -1),
                            gc, gs, hy_bias[j], fwd, inv, P)
            h = _outproj(z.reshape(M, D), hy_w_out[j].astype(BF16), h, norm_mix_post[i])
        else:
            q, k, v = _qkv(h, norm_mix_pre[i], at_w_qkv[j].astype(BF16), L)
            o = _attention(q.reshape(B, L, -1), k.reshape(B, L, -1), v.reshape(B, L, -1), at_sink[j])
            h = _outproj(o.reshape(M, -1), at_w_o[j].astype(BF16), h, norm_mix_post[i])
        h = _mlp(h, norm_mlp_pre[i], norm_mlp_post[i], w_up[i].astype(BF16), w_down[i].astype(BF16))
    return h.reshape(B, L, D)
```

```python
import functools
import math

import numpy as np
import jax
import jax.numpy as jnp
from jax import lax
from jax.experimental import pallas as pl
from jax.experimental.pallas import tpu as pltpu

BF16 = jnp.bfloat16
F32 = jnp.float32

EPS = 1e-6
NEG = -1e30

HY_ORDER = 2
HY_EMB = 33
HY_BANDS = (HY_EMB - 1) // 2
HY_DECAY_TARGET = 1e-2
HY_FAST_DECAY = 0.3
HY_SLOW_DECAY = 1.5
HEAD_DIM = 64
KV_GROUP = 4
WINDOW = 128
ROT_DIM = HEAD_DIM // 4
ROPE_THETA = 500000.0

CONV_BLOCKS = 4
CH_BLOCK = 256
SPEC_DTYPE = BF16

VMEM_LIMIT = 56 * 1024 * 1024


def _cparams(n_axes):
    return pltpu.CompilerParams(
        dimension_semantics=("arbitrary",) * n_axes,
        vmem_limit_bytes=VMEM_LIMIT)


def _resident(shape, index_map):
    return pl.BlockSpec(shape, index_map, pipeline_mode=pl.Buffered(1))


def _rms(x, g):
    ms = jnp.mean(x * x, axis=-1, keepdims=True)
    return x * lax.rsqrt(ms + EPS) * g


def _bdot(a, b):
    return jnp.dot(a, b, preferred_element_type=F32)


def _dft_tables(n):
    f = np.arange(n, dtype=np.int64)[:, None]
    r = np.arange(n, dtype=np.int64)[None, :]
    k_pos = ((2 * f + 1) * r) % (4 * n)
    k_neg = ((2 * f + 1) * (r - n)) % (4 * n)
    ang_pos = np.pi * k_pos / (2 * n)
    ang_neg = np.pi * k_neg / (2 * n)
    cpos, spos = np.cos(ang_pos), np.sin(ang_pos)
    cneg, sneg = np.cos(ang_neg), np.sin(ang_neg)
    cneg[:, 0] = 0.0
    sneg[:, 0] = 0.0
    fwd = np.concatenate([cpos, spos], axis=0)
    inv = np.concatenate([cpos.T, spos.T], axis=1) / n
    filt_c = np.concatenate([cneg, cpos], axis=1)
    filt_s = np.concatenate([sneg, spos], axis=1)
    return [np.asarray(a, np.float32) for a in (fwd, inv, filt_c, filt_s)]


def _filter_positions(L):
    t = np.linspace(0.0, 1.0, L)[:, None]
    w = (2.0 * np.pi / L) * np.arange(L)[:, None]
    f = np.linspace(1e-4, HY_BANDS - 1, HY_BANDS)[None, :]
    z = np.concatenate([t, np.cos(f * w), -np.sin(f * w)], axis=-1)
    return np.asarray(z, np.float32)


def _rope_tables(S):
    half = ROT_DIM // 2
    inv = ROPE_THETA ** (-np.arange(0, ROT_DIM, 2, dtype=np.float64) / ROT_DIM)
    ang = np.arange(S, dtype=np.float64)[:, None] * inv[None, :]
    lane = np.arange(128) % HEAD_DIM
    ang_l = ang[:, lane % half]
    cos_t = np.where(lane[None, :] < ROT_DIM, np.cos(ang_l), 1.0)
    sin_lo = np.where(lane[None, :] < half, -np.sin(ang_l), 0.0)
    sin_hi = np.where((lane[None, :] >= half) & (lane[None, :] < ROT_DIM), np.sin(ang_l), 0.0)
    return [np.asarray(a, np.float32) for a in (cos_t, sin_lo, sin_hi)]


def _prenorm_kernel(x_ref, g_ref, o_ref):
    o_ref[...] = _rms(x_ref[...], g_ref[...]).astype(o_ref.dtype)


def _prenorm(x2d, g, tm=1024):
    M, D = x2d.shape
    return pl.pallas_call(
        _prenorm_kernel,
        grid=(M // tm,),
        in_specs=[pl.BlockSpec((tm, D), lambda i: (i, 0)),
                  pl.BlockSpec((1, D), lambda i: (0, 0))],
        out_specs=pl.BlockSpec((tm, D), lambda i: (i, 0)),
        out_shape=jax.ShapeDtypeStruct((M, D), BF16),
        compiler_params=_cparams(1),
        name="prenorm",
    )(x2d, g.reshape(1, D))


def _filter_kernel(z_ref, w1_ref, b1_ref, w2_ref, b2_ref, w3_ref, b3_ref, fr_ref,
                   wout_ref, dl_ref, o_ref, hid_ref, *, n_fwd_blocks):
    j = pl.program_id(0)
    hi = lax.Precision.HIGHEST

    @pl.when(j == 0)
    def _():
        fr = fr_ref[...]
        h = jnp.sin(fr * (jnp.dot(z_ref[...], w1_ref[...], precision=hi) + b1_ref[...]))
        h = jnp.sin(fr * (jnp.dot(h, w2_ref[...], precision=hi) + b2_ref[...]))
        h = jnp.sin(fr * (jnp.dot(h, w3_ref[...], precision=hi) + b3_ref[...]))
        hid_ref[...] = h

    L, W = o_ref.shape
    row = lax.broadcasted_iota(jnp.int32, (L, W), 0)
    t = row.astype(F32) * (1.0 / (L - 1))
    decay = jnp.exp(-t * dl_ref[...])
    val = jnp.dot(hid_ref[...], wout_ref[...], precision=hi) * decay
    keep = jnp.logical_or(row > 0, j < n_fwd_blocks)
    o_ref[...] = jnp.where(keep, val, 0.0)


def _hyena_filters(L, D, f_w1, f_b1, f_w2, f_b2, f_w3, f_b3, f_freq, f_wout):
    FW = f_w1.shape[1]
    z = jnp.asarray(np.pad(_filter_positions(L), ((0, 0), (0, FW - HY_EMB))))
    w1 = jnp.pad(f_w1, ((0, FW - HY_EMB), (0, 0)))
    max_decay = math.log(HY_DECAY_TARGET) / HY_FAST_DECAY
    min_decay = math.log(HY_DECAY_TARGET) / HY_SLOW_DECAY
    absdelta = jnp.asarray(np.abs(np.linspace(min_decay, max_decay, D)).astype(np.float32))[None, :]
    n_blocks = f_wout.shape[1] // D
    small = lambda shape: pl.BlockSpec(shape, lambda j: (0, 0))
    return pl.pallas_call(
        functools.partial(_filter_kernel, n_fwd_blocks=n_blocks // 2),
        grid=(n_blocks,),
        in_specs=[small((L, FW)), small((FW, FW)), small((1, FW)), small((FW, FW)), small((1, FW)),
                  small((FW, FW)), small((1, FW)), small((1, FW)),
                  pl.BlockSpec((FW, D), lambda j: (0, j)),
                  small((1, D))],
        out_specs=pl.BlockSpec((L, D), lambda j: (0, j)),
        out_shape=jax.ShapeDtypeStruct((L, n_blocks * D), F32),
        scratch_shapes=[pltpu.VMEM((L, FW), F32)],
        compiler_params=_cparams(1),
        name="hyena_filter",
    )(z, w1, f_b1.reshape(1, FW), f_w2, f_b2.reshape(1, FW), f_w3, f_b3.reshape(1, FW),
      f_freq.reshape(1, FW), f_wout, absdelta)


def _spectra_kernel(hf_ref, hb_ref, mc_ref, ms_ref, gc_ref, gs_ref, *, P, n):
    mc = mc_ref[...]
    ms = ms_ref[...]
    cpos = mc[:, n:]
    spos = ms[:, n:]
    for d in range(-(P - 1), P):
        if d == 0:
            hf0 = hf_ref[0:n, :]
            hb0 = hb_ref[0:n, :]
            gc = _bdot(cpos, (hf0 + hb0).astype(BF16))
            gs = _bdot(spos, (hf0 - hb0).astype(BF16))
        elif d > 0:
            taps = hf_ref[n * (d - 1):n * (d + 1), :].astype(BF16)
            gc = _bdot(mc, taps)
            gs = _bdot(ms, taps)
        else:
            a = -d
            taps = hb_ref[n * (a - 1):n * (a + 1), :].astype(BF16)
            gc = _bdot(mc, taps)
            gs = -_bdot(ms, taps)
        gc_ref[d + P - 1] = gc.astype(gc_ref.dtype)
        gs_ref[d + P - 1] = gs.astype(gs_ref.dtype)


def _filter_spectra(hfilt, mc, ms, P, cw=256):
    L, W = hfilt.shape
    n = L // P
    half = W // 2
    nblk = half // cw
    out_sd = jax.ShapeDtypeStruct((2 * P - 1, n, half), SPEC_DTYPE)
    return pl.pallas_call(
        functools.partial(_spectra_kernel, P=P, n=n),
        grid=(nblk,),
        in_specs=[pl.BlockSpec((L, cw), lambda j: (0, j)),
                  pl.BlockSpec((L, cw), lambda j: (0, nblk + j)),
                  pl.BlockSpec((n, 2 * n), lambda j: (0, 0)),
                  pl.BlockSpec((n, 2 * n), lambda j: (0, 0))],
        out_specs=[pl.BlockSpec((2 * P - 1, n, cw), lambda j: (0, 0, j)),
                   pl.BlockSpec((2 * P - 1, n, cw), lambda j: (0, 0, j))],
        out_shape=[out_sd, out_sd],
        compiler_params=_cparams(1),
        name="filter_spectra",
    )(hfilt, hfilt, mc, ms)


def _hyena_kernel(hn_ref, wv_ref, w1_ref, w2_ref, cwv_ref, cw1_ref, cw2_ref,
                  cbv_ref, cb1_ref, cb2_ref, gc0_ref, gs0_ref, gc1_ref, gs1_ref,
                  bias_ref, fwd_ref, inv_ref, z_ref,
                  raw_ref, v_ref, gate_ref, uspec_ref, yspec_ref, *, P, n, fchunk):
    L = P * n
    trips = n // fchunk
    prows = L // trips
    row = lax.broadcasted_iota(jnp.int32, (n, 1), 0)

    def short_conv(cw_ref, cb_ref, dst):
        cw = cw_ref[...]
        cb = cb_ref[...]
        for i in range(P):
            cur = raw_ref[n * i:n * (i + 1), :]
            prev = pltpu.roll(cur, 1, axis=0)
            nxt = pltpu.roll(cur, n - 1, axis=0)
            first = raw_ref[n * i - 1:n * i, :] if i > 0 else jnp.zeros_like(cur[0:1])
            last = raw_ref[n * (i + 1):n * (i + 1) + 1, :] if i < P - 1 else jnp.zeros_like(cur[0:1])
            prev = jnp.where(row == 0, first, prev)
            nxt = jnp.where(row == n - 1, last, nxt)
            dst[n * i:n * (i + 1), :] = prev * cw[0:1] + cur * cw[1:2] + nxt * cw[2:3] + cb

    def long_conv(src_ref, gc_ref, gs_ref, wgate_ref):
        fwd = fwd_ref[...]
        for j in range(P):
            uspec_ref[j] = _bdot(fwd, src_ref[n * j:n * (j + 1), :].astype(BF16)).astype(uspec_ref.dtype)

        raw_ref[...] = _bdot(hn_ref[0], wgate_ref[...])
        for c in range(n // fchunk):
            fc = slice(c * fchunk, (c + 1) * fchunk)
            fs = slice(n + c * fchunk, n + (c + 1) * fchunk)
            uc = [uspec_ref[j, fc, :] for j in range(P)]
            us = [uspec_ref[j, fs, :] for j in range(P)]
            for i in range(P):
                yc = None
                ys = None
                for j in range(P):
                    gc = gc_ref[i - j + P - 1, fc, :]
                    gs = gs_ref[i - j + P - 1, fc, :]
                    tc = gc * uc[j] - gs * us[j]
                    ts = gc * us[j] + gs * uc[j]
                    yc = tc if yc is None else yc + tc
                    ys = ts if ys is None else ys + ts
                yspec_ref[i, fc, :] = yc.astype(BF16)
                yspec_ref[i, fs, :] = ys.astype(BF16)

    bias = bias_ref[...]
    inv = inv_ref[...]

    raw_ref[...] = _bdot(hn_ref[0], wv_ref[...])
    short_conv(cwv_ref, cbv_ref, v_ref)

    long_conv(v_ref, gc0_ref, gs0_ref, w1_ref)
    short_conv(cw1_ref, cb1_ref, gate_ref)
    for i in range(P):
        sl = slice(n * i, n * (i + 1))
        y = _bdot(inv, yspec_ref[i])
        v_ref[sl, :] = gate_ref[sl, :] * (y + v_ref[sl, :] * bias[0:1])

    long_conv(v_ref, gc1_ref, gs1_ref, w2_ref)
    short_conv(cw2_ref, cb2_ref, gate_ref)
    for i in range(P):
        sl = slice(n * i, n * (i + 1))
        y = _bdot(inv, yspec_ref[i])
        z_ref[0, sl, :] = (gate_ref[sl, :] * (y + v_ref[sl, :] * bias[1:2])).astype(z_ref.dtype)


def _hyena_core(hn, w_in, conv_w, conv_b, gc, gs, bias, fwd, inv, P):
    B, L, D = hn.shape
    n = L // P
    cb_n = D // CH_BLOCK
    C = CH_BLOCK
    nd = 2 * P - 1

    def stream_spec(shape, s):
        return pl.BlockSpec(shape, lambda c, b: (0, s * cb_n + c))

    in_specs = (
        [pl.BlockSpec((1, L, D), lambda c, b: (b, 0, 0))]
        + [stream_spec((D, C), s) for s in range(3)]
        + [stream_spec((3, C), s) for s in range(3)]
        + [stream_spec((1, C), s) for s in range(3)]
        + [_resident((nd, n, C), lambda c, b: (0, 0, c)),
           _resident((nd, n, C), lambda c, b: (0, 0, c)),
           _resident((nd, n, C), lambda c, b: (0, 0, cb_n + c)),
           _resident((nd, n, C), lambda c, b: (0, 0, cb_n + c)),
           pl.BlockSpec((HY_ORDER, C), lambda c, b: (0, c)),
           pl.BlockSpec((2 * n, n), lambda c, b: (0, 0)),
           pl.BlockSpec((n, 2 * n), lambda c, b: (0, 0))])
    return pl.pallas_call(
        functools.partial(_hyena_kernel, P=P, n=n, fchunk=16),
        grid=(cb_n, B),
        in_specs=in_specs,
        out_specs=pl.BlockSpec((1, L, C), lambda c, b: (b, 0, c)),
        out_shape=jax.ShapeDtypeStruct((B, L, D), BF16),
        scratch_shapes=[pltpu.VMEM((L, C), F32),
                        pltpu.VMEM((L, C), F32),
                        pltpu.VMEM((L, C), F32),
                        pltpu.VMEM((P, 2 * n, C), SPEC_DTYPE),
                        pltpu.VMEM((P, 2 * n, C), BF16)],
        compiler_params=_cparams(2),
        name="hyena_core",
    )(hn, w_in, w_in, w_in, conv_w, conv_w, conv_w, conv_b, conv_b, conv_b,
      gc, gs, gc, gs, bias, fwd, inv)


def _tail_kernel(a_ref, wo_ref, r_ref, gmix_ref, gpre_ref, gpost_ref, wu_ref, wd_ref, o_ref, *, fchunk):
    m = _bdot(a_ref[...], wo_ref[...])
    h = r_ref[...] + _rms(m, gmix_ref[...])
    hn = _rms(h, gpre_ref[...]).astype(BF16)
    FF = wu_ref.shape[1]
    acc = None
    for c in range(FF // fchunk):
        a = _bdot(hn, wu_ref[:, c * fchunk:(c + 1) * fchunk])
        a = jnp.maximum(a, 0.0)
        a = (a * a).astype(BF16)
        part = _bdot(a, wd_ref[c * fchunk:(c + 1) * fchunk, :])
        acc = part if acc is None else acc + part
    o_ref[...] = h + _rms(acc, gpost_ref[...])


def _tail(a, w_o, resid, g_mix, g_pre, g_post, w_up, w_down, tm=512, fchunk=1024):
    M, K = a.shape
    D = w_o.shape[1]
    FF = w_up.shape[1]
    row = lambda shape: pl.BlockSpec(shape, lambda i: (i, 0))
    gain = pl.BlockSpec((1, D), lambda i: (0, 0))
    return pl.pallas_call(
        functools.partial(_tail_kernel, fchunk=fchunk),
        grid=(M // tm,),
        in_specs=[row((tm, K)), _resident((K, D), lambda i: (0, 0)), row((tm, D)),
                  gain, gain, gain,
                  _resident((D, FF), lambda i: (0, 0)),
                  _resident((FF, D), lambda i: (0, 0))],
        out_specs=row((tm, D)),
        out_shape=jax.ShapeDtypeStruct((M, D), F32),
        compiler_params=_cparams(1),
        name="mixer_tail_mlp",
    )(a, w_o, resid, g_mix.reshape(1, D), g_pre.reshape(1, D), g_post.reshape(1, D), w_up, w_down)


def _qkv_kernel(h_ref, g_ref, w_ref, ct_ref, slo_ref, shi_ref, q_ref, k_ref, v_ref, *, nq, nk):
    hn = _rms(h_ref[...], g_ref[...]).astype(BF16)
    qkv = _bdot(hn, w_ref[...])
    ct = ct_ref[...]
    slo = slo_ref[...]
    shi = shi_ref[...]
    half = ROT_DIM // 2
    scale = HEAD_DIM ** -0.5
    for j in range((nq + nk) // 128):
        t = qkv[:, 128 * j:128 * (j + 1)]
        r = t * ct + pltpu.roll(t, 128 - half, axis=1) * slo + pltpu.roll(t, half, axis=1) * shi
        if 128 * j < nq:
            q_ref[:, 128 * j:128 * (j + 1)] = (r * scale).astype(q_ref.dtype)
        else:
            k_ref[:, 128 * j - nq:128 * (j + 1) - nq] = r.astype(k_ref.dtype)
    v_ref[...] = qkv[:, nq + nk:].astype(v_ref.dtype)


def _qkv(h, g, w, S, tm=512):
    M, D = h.shape
    nq = D
    nk = (w.shape[1] - nq) // 2
    ct, slo, shi = [jnp.asarray(a) for a in _rope_tables(S)]
    pb = S // tm
    tab = pl.BlockSpec((tm, 128), lambda i: (i % pb, 0))
    return pl.pallas_call(
        functools.partial(_qkv_kernel, nq=nq, nk=nk),
        grid=(M // tm,),
        in_specs=[pl.BlockSpec((tm, D), lambda i: (i, 0)),
                  pl.BlockSpec((1, D), lambda i: (0, 0)),
                  _resident((D, nq + 2 * nk), lambda i: (0, 0)),
                  tab, tab, tab],
        out_specs=[pl.BlockSpec((tm, nq), lambda i: (i, 0)),
                   pl.BlockSpec((tm, nk), lambda i: (i, 0)),
                   pl.BlockSpec((tm, nk), lambda i: (i, 0))],
        out_shape=[jax.ShapeDtypeStruct((M, nq), BF16),
                   jax.ShapeDtypeStruct((M, nk), BF16),
                   jax.ShapeDtypeStruct((M, nk), BF16)],
        compiler_params=_cparams(1),
        name="qkv_rope",
    )(h, g.reshape(1, D), w, ct, slo, shi)


def _attn_kernel(sink_ref, q_ref, k_ref, v_ref, o_ref, krep_ref, vrep_ref, *, S, n_kv):
    BQ = WINDOW
    BK = 3 * WINDOW
    GW = KV_GROUP * HEAD_DIM
    dn = (((1,), (1,)), ((), ()))

    for g in range(n_kv):
        kg = k_ref[0, :, g * HEAD_DIM:(g + 1) * HEAD_DIM]
        vg = v_ref[0, :, g * HEAD_DIM:(g + 1) * HEAD_DIM]
        krep_ref[g] = jnp.concatenate([kg] * KV_GROUP, axis=1)
        vrep_ref[g] = jnp.concatenate([vg] * KV_GROUP, axis=1)

    lane = lax.broadcasted_iota(jnp.int32, (BQ, GW), 1)
    head_lanes = [(lane >= hh * HEAD_DIM) & (lane < (hh + 1) * HEAD_DIM) for hh in range(KV_GROUP)]
    rowblk = lax.broadcasted_iota(jnp.int32, (KV_GROUP * BQ, 1), 0) // BQ

    def body(qb, carry):
        qs = pl.multiple_of(qb * BQ, BQ)
        ks = pl.multiple_of(jnp.clip(qs - WINDOW, 0, S - BK), WINDOW)
        qpos = qs + lax.broadcasted_iota(jnp.int32, (BQ, BK), 0)
        kpos = ks + lax.broadcasted_iota(jnp.int32, (BQ, BK), 1)
        maskadd = jnp.where(jnp.abs(kpos - qpos) <= WINDOW, 0.0, NEG).astype(F32)
        maskadd = jnp.concatenate([maskadd] * KV_GROUP, axis=0)
        for g in range(n_kv):
            qg = q_ref[0, pl.ds(qs, BQ), g * GW:(g + 1) * GW]
            qst = jnp.concatenate([jnp.where(head_lanes[hh], qg, jnp.zeros_like(qg))
                                   for hh in range(KV_GROUP)], axis=0)
            sink = jnp.zeros((KV_GROUP * BQ, 1), F32)
            for hh in range(KV_GROUP):
                sink = jnp.where(rowblk == hh, sink_ref[g * KV_GROUP + hh], sink)
            s = lax.dot_general(qst, krep_ref[g, pl.ds(ks, BK), :], dn,
                                preferred_element_type=F32) + maskadd
            m = jnp.maximum(jnp.max(s, axis=-1, keepdims=True), sink)
            p = jnp.exp(s - m)
            denom = jnp.sum(p, axis=-1, keepdims=True) + jnp.exp(sink - m)
            ost = _bdot(p.astype(BF16), vrep_ref[g, pl.ds(ks, BK), :]) / denom
            o = jnp.where(head_lanes[0], ost[0:BQ], 0.0)
            for hh in range(1, KV_GROUP):
                o = jnp.where(head_lanes[hh], ost[hh * BQ:(hh + 1) * BQ], o)
            o_ref[0, pl.ds(qs, BQ), g * GW:(g + 1) * GW] = o.astype(o_ref.dtype)
        return carry

    lax.fori_loop(0, S // BQ, body, 0)


def _attention(q, k, v, sink):
    B, S, NQ = q.shape
    NK = k.shape[2]
    n_kv = NK // HEAD_DIM
    return pl.pallas_call(
        functools.partial(_attn_kernel, S=S, n_kv=n_kv),
        grid=(B,),
        in_specs=[pl.BlockSpec(memory_space=pltpu.SMEM),
                  pl.BlockSpec((1, S, NQ), lambda b: (b, 0, 0)),
                  pl.BlockSpec((1, S, NK), lambda b: (b, 0, 0)),
                  pl.BlockSpec((1, S, NK), lambda b: (b, 0, 0))],
        out_specs=pl.BlockSpec((1, S, NQ), lambda b: (b, 0, 0)),
        out_shape=jax.ShapeDtypeStruct((B, S, NQ), BF16),
        scratch_shapes=[pltpu.VMEM((n_kv, S, KV_GROUP * HEAD_DIM), BF16),
                        pltpu.VMEM((n_kv, S, KV_GROUP * HEAD_DIM), BF16)],
        compiler_params=_cparams(1),
        name="window_attention",
    )(sink, q, k, v)


def kernel(x, norm_mix_pre, norm_mix_post, norm_mlp_pre, norm_mlp_post, w_up, w_down, hy_w_in, hy_conv_w, hy_conv_b, hy_f_w1, hy_f_b1, hy_f_w2, hy_f_b2, hy_f_w3, hy_f_b3, hy_f_freq, hy_f_wout, hy_bias, hy_w_out, at_w_qkv, at_sink, at_w_o):
    B, L, D = x.shape
    M = B * L
    P = CONV_BLOCKS
    n = L // P
    depth = norm_mix_pre.shape[0]
    fwd, inv, filt_c, filt_s = [jnp.asarray(a).astype(BF16) for a in _dft_tables(n)]

    h = x.reshape(M, D)
    for i in range(depth):
        j = i // 2
        if i % 2 == 0:
            hfilt = _hyena_filters(L, D, hy_f_w1[j], hy_f_b1[j], hy_f_w2[j], hy_f_b2[j],
                                   hy_f_w3[j], hy_f_b3[j], hy_f_freq[j], hy_f_wout[j])
            gc, gs = _filter_spectra(hfilt, filt_c, filt_s, P)
            hn = _prenorm(h, norm_mix_pre[i]).reshape(B, L, D)
            z = _hyena_core(hn, hy_w_in[j].astype(BF16), hy_conv_w[j], hy_conv_b[j].reshape(1, -1),
                            gc, gs, hy_bias[j], fwd, inv, P)
            a, w_o = z.reshape(M, D), hy_w_out[j]
        else:
            q, k, v = _qkv(h, norm_mix_pre[i], at_w_qkv[j].astype(BF16), L)
            o = _attention(q.reshape(B, L, -1), k.reshape(B, L, -1), v.reshape(B, L, -1), at_sink[j])
            a, w_o = o.reshape(M, -1), at_w_o[j]
        h = _tail(a, w_o.astype(BF16), h, norm_mix_post[i], norm_mlp_pre[i], norm_mlp_post[i],
                  w_up[i].astype(BF16), w_down[i].astype(BF16))
    return h.reshape(B, L, D)
```

```python
import functools
import math

import numpy as np
import jax
import jax.numpy as jnp
from jax import lax
from jax.experimental import pallas as pl
from jax.experimental.pallas import tpu as pltpu

BF16 = jnp.bfloat16
F32 = jnp.float32

EPS = 1e-6
NEG = -1e30

HY_ORDER = 2
HY_EMB = 33
HY_BANDS = (HY_EMB - 1) // 2
HY_DECAY_TARGET = 1e-2
HY_FAST_DECAY = 0.3
HY_SLOW_DECAY = 1.5
HEAD_DIM = 64
KV_GROUP = 4
WINDOW = 128
ROT_DIM = HEAD_DIM // 4
ROPE_THETA = 500000.0

CONV_BLOCKS = 4
CH_BLOCK = 256
SPEC_DTYPE = BF16

VMEM_LIMIT = 56 * 1024 * 1024


def _cparams(n_axes):
    return pltpu.CompilerParams(
        dimension_semantics=("arbitrary",) * n_axes,
        vmem_limit_bytes=VMEM_LIMIT)


def _resident(shape, index_map):
    return pl.BlockSpec(shape, index_map, pipeline_mode=pl.Buffered(1))


def _rms(x, g):
    ms = jnp.mean(x * x, axis=-1, keepdims=True)
    return x * lax.rsqrt(ms + EPS) * g


def _bdot(a, b):
    return jnp.dot(a, b, preferred_element_type=F32)


def _dft_tables(n):
    f = np.arange(n, dtype=np.int64)[:, None]
    r = np.arange(n, dtype=np.int64)[None, :]
    k_pos = ((2 * f + 1) * r) % (4 * n)
    k_neg = ((2 * f + 1) * (r - n)) % (4 * n)
    ang_pos = np.pi * k_pos / (2 * n)
    ang_neg = np.pi * k_neg / (2 * n)
    cpos, spos = np.cos(ang_pos), np.sin(ang_pos)
    cneg, sneg = np.cos(ang_neg), np.sin(ang_neg)
    cneg[:, 0] = 0.0
    sneg[:, 0] = 0.0
    fwd = np.concatenate([cpos, spos], axis=0)
    inv = np.concatenate([cpos.T, spos.T], axis=1) / n
    filt_c = np.concatenate([cneg, cpos], axis=1)
    filt_s = np.concatenate([sneg, spos], axis=1)
    return [np.asarray(a, np.float32) for a in (fwd, inv, filt_c, filt_s)]


def _filter_positions(L):
    t = np.linspace(0.0, 1.0, L)[:, None]
    w = (2.0 * np.pi / L) * np.arange(L)[:, None]
    f = np.linspace(1e-4, HY_BANDS - 1, HY_BANDS)[None, :]
    z = np.concatenate([t, np.cos(f * w), -np.sin(f * w)], axis=-1)
    return np.asarray(z, np.float32)


def _rope_tables(S):
    half = ROT_DIM // 2
    inv = ROPE_THETA ** (-np.arange(0, ROT_DIM, 2, dtype=np.float64) / ROT_DIM)
    ang = np.arange(S, dtype=np.float64)[:, None] * inv[None, :]
    lane = np.arange(128) % HEAD_DIM
    ang_l = ang[:, lane % half]
    cos_t = np.where(lane[None, :] < ROT_DIM, np.cos(ang_l), 1.0)
    sin_lo = np.where(lane[None, :] < half, -np.sin(ang_l), 0.0)
    sin_hi = np.where((lane[None, :] >= half) & (lane[None, :] < ROT_DIM), np.sin(ang_l), 0.0)
    return [np.asarray(a, np.float32) for a in (cos_t, sin_lo, sin_hi)]


def _prenorm_kernel(x_ref, g_ref, o_ref):
    o_ref[...] = _rms(x_ref[...], g_ref[...]).astype(o_ref.dtype)


def _prenorm(x2d, g, tm=1024):
    M, D = x2d.shape
    return pl.pallas_call(
        _prenorm_kernel,
        grid=(M // tm,),
        in_specs=[pl.BlockSpec((tm, D), lambda i: (i, 0)),
                  pl.BlockSpec((1, D), lambda i: (0, 0))],
        out_specs=pl.BlockSpec((tm, D), lambda i: (i, 0)),
        out_shape=jax.ShapeDtypeStruct((M, D), BF16),
        compiler_params=_cparams(1),
        name="prenorm",
    )(x2d, g.reshape(1, D))


def _filter_kernel(z_ref, w1_ref, b1_ref, w2_ref, b2_ref, w3_ref, b3_ref, fr_ref,
                   wout_ref, dl_ref, o_ref, hid_ref, *, n_fwd_blocks):
    j = pl.program_id(0)
    hi = lax.Precision.HIGHEST

    @pl.when(j == 0)
    def _():
        fr = fr_ref[...]
        h = jnp.sin(fr * (jnp.dot(z_ref[...], w1_ref[...], precision=hi) + b1_ref[...]))
        h = jnp.sin(fr * (jnp.dot(h, w2_ref[...], precision=hi) + b2_ref[...]))
        h = jnp.sin(fr * (jnp.dot(h, w3_ref[...], precision=hi) + b3_ref[...]))
        hid_ref[...] = h

    L, W = o_ref.shape
    row = lax.broadcasted_iota(jnp.int32, (L, W), 0)
    t = row.astype(F32) * (1.0 / (L - 1))
    decay = jnp.exp(-t * dl_ref[...])
    val = jnp.dot(hid_ref[...], wout_ref[...], precision=hi) * decay
    keep = jnp.logical_or(row > 0, j < n_fwd_blocks)
    o_ref[...] = jnp.where(keep, val, 0.0)


def _hyena_filters(L, D, f_w1, f_b1, f_w2, f_b2, f_w3, f_b3, f_freq, f_wout):
    FW = f_w1.shape[1]
    z = jnp.asarray(np.pad(_filter_positions(L), ((0, 0), (0, FW - HY_EMB))))
    w1 = jnp.pad(f_w1, ((0, FW - HY_EMB), (0, 0)))
    max_decay = math.log(HY_DECAY_TARGET) / HY_FAST_DECAY
    min_decay = math.log(HY_DECAY_TARGET) / HY_SLOW_DECAY
    absdelta = jnp.asarray(np.abs(np.linspace(min_decay, max_decay, D)).astype(np.float32))[None, :]
    n_blocks = f_wout.shape[1] // D
    small = lambda shape: pl.BlockSpec(shape, lambda j: (0, 0))
    return pl.pallas_call(
        functools.partial(_filter_kernel, n_fwd_blocks=n_blocks // 2),
        grid=(n_blocks,),
        in_specs=[small((L, FW)), small((FW, FW)), small((1, FW)), small((FW, FW)), small((1, FW)),
                  small((FW, FW)), small((1, FW)), small((1, FW)),
                  pl.BlockSpec((FW, D), lambda j: (0, j)),
                  small((1, D))],
        out_specs=pl.BlockSpec((L, D), lambda j: (0, j)),
        out_shape=jax.ShapeDtypeStruct((L, n_blocks * D), F32),
        scratch_shapes=[pltpu.VMEM((L, FW), F32)],
        compiler_params=_cparams(1),
        name="hyena_filter",
    )(z, w1, f_b1.reshape(1, FW), f_w2, f_b2.reshape(1, FW), f_w3, f_b3.reshape(1, FW),
      f_freq.reshape(1, FW), f_wout, absdelta)


def _spectra_kernel(hf_ref, hb_ref, bias_ref, mc_ref, ms_ref, gc_ref, gs_ref, *, P, n):
    mc = mc_ref[...]
    ms = ms_ref[...]
    cpos = mc[:, n:]
    spos = ms[:, n:]
    for d in range(-(P - 1), P):
        if d == 0:
            hf0 = hf_ref[0:n, :]
            hb0 = hb_ref[0:n, :]
            gc = _bdot(cpos, (hf0 + hb0).astype(BF16)) + bias_ref[...]
            gs = _bdot(spos, (hf0 - hb0).astype(BF16))
        elif d > 0:
            taps = hf_ref[n * (d - 1):n * (d + 1), :].astype(BF16)
            gc = _bdot(mc, taps)
            gs = _bdot(ms, taps)
        else:
            a = -d
            taps = hb_ref[n * (a - 1):n * (a + 1), :].astype(BF16)
            gc = _bdot(mc, taps)
            gs = -_bdot(ms, taps)
        gc_ref[d + P - 1] = gc.astype(gc_ref.dtype)
        gs_ref[d + P - 1] = gs.astype(gs_ref.dtype)


def _filter_spectra(hfilt, bias, mc, ms, P, cw=256):
    L, W = hfilt.shape
    n = L // P
    half = W // 2
    nblk = half // cw
    out_sd = jax.ShapeDtypeStruct((2 * P - 1, n, half), SPEC_DTYPE)
    return pl.pallas_call(
        functools.partial(_spectra_kernel, P=P, n=n),
        grid=(nblk,),
        in_specs=[pl.BlockSpec((L, cw), lambda j: (0, j)),
                  pl.BlockSpec((L, cw), lambda j: (0, nblk + j)),
                  pl.BlockSpec((1, cw), lambda j: (0, j)),
                  pl.BlockSpec((n, 2 * n), lambda j: (0, 0)),
                  pl.BlockSpec((n, 2 * n), lambda j: (0, 0))],
        out_specs=[pl.BlockSpec((2 * P - 1, n, cw), lambda j: (0, 0, j)),
                   pl.BlockSpec((2 * P - 1, n, cw), lambda j: (0, 0, j))],
        out_shape=[out_sd, out_sd],
        compiler_params=_cparams(1),
        name="filter_spectra",
    )(hfilt, hfilt, bias.reshape(1, half), mc, ms)


def _hyena_kernel(hn_ref, wv_ref, w1_ref, w2_ref, cwv_ref, cw1_ref, cw2_ref,
                  cbv_ref, cb1_ref, cb2_ref, gc0_ref, gs0_ref, gc1_ref, gs1_ref,
                  fwd_ref, inv_ref, z_ref,
                  raw_ref, v_ref, gate_ref, uspec_ref, yspec_ref, *, P, n, fchunk):
    row = lax.broadcasted_iota(jnp.int32, (n, 1), 0)

    def short_conv(cw_ref, cb_ref, dst):
        cw = cw_ref[...]
        cb = cb_ref[...]
        for i in range(P):
            cur = raw_ref[n * i:n * (i + 1), :]
            prev = pltpu.roll(cur, 1, axis=0)
            nxt = pltpu.roll(cur, n - 1, axis=0)
            first = raw_ref[n * i - 1:n * i, :] if i > 0 else jnp.zeros_like(cur[0:1])
            last = raw_ref[n * (i + 1):n * (i + 1) + 1, :] if i < P - 1 else jnp.zeros_like(cur[0:1])
            prev = jnp.where(row == 0, first, prev)
            nxt = jnp.where(row == n - 1, last, nxt)
            out = prev * cw[0:1] + cur * cw[1:2] + nxt * cw[2:3] + cb
            dst[n * i:n * (i + 1), :] = out.astype(dst.dtype)

    def long_conv(src_ref, gc_ref, gs_ref, wgate_ref):
        fwd = fwd_ref[...]
        for j in range(P):
            uspec_ref[j] = _bdot(fwd, src_ref[n * j:n * (j + 1), :]).astype(uspec_ref.dtype)

        raw_ref[...] = _bdot(hn_ref[0], wgate_ref[...].astype(BF16))
        for c in range(n // fchunk):
            fc = slice(c * fchunk, (c + 1) * fchunk)
            fs = slice(n + c * fchunk, n + (c + 1) * fchunk)
            uc = [uspec_ref[j, fc, :] for j in range(P)]
            us = [uspec_ref[j, fs, :] for j in range(P)]
            for i in range(P):
                yc = None
                ys = None
                for j in range(P):
                    gc = gc_ref[i - j + P - 1, fc, :]
                    gs = gs_ref[i - j + P - 1, fc, :]
                    tc = gc * uc[j] - gs * us[j]
                    ts = gc * us[j] + gs * uc[j]
                    yc = tc if yc is None else yc + tc
                    ys = ts if ys is None else ys + ts
                yspec_ref[i, fc, :] = yc.astype(BF16)
                yspec_ref[i, fs, :] = ys.astype(BF16)

    inv = inv_ref[...]

    raw_ref[...] = _bdot(hn_ref[0], wv_ref[...].astype(BF16))
    short_conv(cwv_ref, cbv_ref, v_ref)

    long_conv(v_ref, gc0_ref, gs0_ref, w1_ref)
    short_conv(cw1_ref, cb1_ref, gate_ref)
    for i in range(P):
        sl = slice(n * i, n * (i + 1))
        v_ref[sl, :] = (gate_ref[sl, :] * _bdot(inv, yspec_ref[i])).astype(v_ref.dtype)

    long_conv(v_ref, gc1_ref, gs1_ref, w2_ref)
    short_conv(cw2_ref, cb2_ref, gate_ref)
    for i in range(P):
        sl = slice(n * i, n * (i + 1))
        z_ref[0, sl, :] = (gate_ref[sl, :] * _bdot(inv, yspec_ref[i])).astype(z_ref.dtype)


def _hyena_core(hn, w_in, conv_w, conv_b, gc, gs, fwd, inv, P):
    B, L, D = hn.shape
    n = L // P
    cb_n = D // CH_BLOCK
    C = CH_BLOCK
    nd = 2 * P - 1

    def stream_spec(shape, s):
        return pl.BlockSpec(shape, lambda c, b: (0, s * cb_n + c))

    in_specs = (
        [pl.BlockSpec((1, L, D), lambda c, b: (b, 0, 0))]
        + [stream_spec((D, C), s) for s in range(3)]
        + [stream_spec((3, C), s) for s in range(3)]
        + [stream_spec((1, C), s) for s in range(3)]
        + [_resident((nd, n, C), lambda c, b: (0, 0, c)),
           _resident((nd, n, C), lambda c, b: (0, 0, c)),
           _resident((nd, n, C), lambda c, b: (0, 0, cb_n + c)),
           _resident((nd, n, C), lambda c, b: (0, 0, cb_n + c)),
           pl.BlockSpec((2 * n, n), lambda c, b: (0, 0)),
           pl.BlockSpec((n, 2 * n), lambda c, b: (0, 0))])
    return pl.pallas_call(
        functools.partial(_hyena_kernel, P=P, n=n, fchunk=16),
        grid=(cb_n, B),
        in_specs=in_specs,
        out_specs=pl.BlockSpec((1, L, C), lambda c, b: (b, 0, c)),
        out_shape=jax.ShapeDtypeStruct((B, L, D), BF16),
        scratch_shapes=[pltpu.VMEM((L, C), F32),
                        pltpu.VMEM((L, C), BF16),
                        pltpu.VMEM((L, C), F32),
                        pltpu.VMEM((P, 2 * n, C), SPEC_DTYPE),
                        pltpu.VMEM((P, 2 * n, C), BF16)],
        compiler_params=_cparams(2),
        name="hyena_core",
    )(hn, w_in, w_in, w_in, conv_w, conv_w, conv_w, conv_b, conv_b, conv_b,
      gc, gs, gc, gs, fwd, inv)


def _stream_cast(src_hbm, layer, dst_ref, stage_ref, sem_ref):
    rows = stage_ref.shape[1]
    n_chunks = dst_ref.shape[0] // rows

    def copy(c):
        return pltpu.make_async_copy(src_hbm.at[layer, pl.ds(c * rows, rows), :],
                                     stage_ref.at[c % 2], sem_ref.at[c % 2])

    copy(0).start()
    for c in range(n_chunks):
        if c + 1 < n_chunks:
            copy(c + 1).start()
        copy(c).wait()
        dst_ref[c * rows:(c + 1) * rows, :] = stage_ref[c % 2].astype(dst_ref.dtype)


def _tail_kernel(a_ref, wo32_ref, r_ref, gmix_ref, gpre_ref, gpost_ref, wu_hbm, wd_hbm, o_ref,
                 wo_ref, wu_ref, wd_ref, stage_u, stage_d, sem_u, sem_d, *, fchunk, layer):
    @pl.when(pl.program_id(0) == 0)
    def _():
        wo_ref[...] = wo32_ref[...].astype(wo_ref.dtype)
        _stream_cast(wu_hbm, layer, wu_ref, stage_u, sem_u)
        _stream_cast(wd_hbm, layer, wd_ref, stage_d, sem_d)

    m = _bdot(a_ref[...], wo_ref[...])
    h = r_ref[...] + _rms(m, gmix_ref[...])
    hn = _rms(h, gpre_ref[...]).astype(BF16)
    FF = wu_ref.shape[1]
    acc = None
    for c in range(FF // fchunk):
        a = _bdot(hn, wu_ref[:, c * fchunk:(c + 1) * fchunk])
        a = jnp.maximum(a, 0.0)
        a = (a * a).astype(BF16)
        part = _bdot(a, wd_ref[c * fchunk:(c + 1) * fchunk, :])
        acc = part if acc is None else acc + part
    o_ref[...] = h + _rms(acc, gpost_ref[...])


def _tail(a, w_o, resid, g_mix, g_pre, g_post, w_up_all, w_down_all, layer, tm=512, fchunk=1024):
    M, K = a.shape
    D = w_o.shape[1]
    FF = w_up_all.shape[2]
    stage_bytes = 1024 * 1024
    rows_u = stage_bytes // (FF * 4)
    rows_d = stage_bytes // (D * 4)
    row = lambda shape: pl.BlockSpec(shape, lambda i: (i, 0))
    gain = pl.BlockSpec((1, D), lambda i: (0, 0))
    return pl.pallas_call(
        functools.partial(_tail_kernel, fchunk=fchunk, layer=layer),
        grid=(M // tm,),
        in_specs=[row((tm, K)), _resident((K, D), lambda i: (0, 0)), row((tm, D)),
                  gain, gain, gain,
                  pl.BlockSpec(memory_space=pl.ANY),
                  pl.BlockSpec(memory_space=pl.ANY)],
        out_specs=row((tm, D)),
        out_shape=jax.ShapeDtypeStruct((M, D), F32),
        scratch_shapes=[pltpu.VMEM((K, D), BF16),
                        pltpu.VMEM((D, FF), BF16),
                        pltpu.VMEM((FF, D), BF16),
                        pltpu.VMEM((2, rows_u, FF), F32),
                        pltpu.VMEM((2, rows_d, D), F32),
                        pltpu.SemaphoreType.DMA((2,)),
                        pltpu.SemaphoreType.DMA((2,))],
        compiler_params=_cparams(1),
        name="mixer_tail_mlp",
    )(a, w_o, resid, g_mix.reshape(1, D), g_pre.reshape(1, D), g_post.reshape(1, D),
      w_up_all, w_down_all)


def _qkv_kernel(h_ref, g_ref, w32_ref, ct_ref, slo_ref, shi_ref, q_ref, k_ref, v_ref, w_ref, *, nq, nk):
    @pl.when(pl.program_id(0) == 0)
    def _():
        w_ref[...] = w32_ref[...].astype(w_ref.dtype)

    hn = _rms(h_ref[...], g_ref[...]).astype(BF16)
    qkv = _bdot(hn, w_ref[...])
    ct = ct_ref[...]
    slo = slo_ref[...]
    shi = shi_ref[...]
    half = ROT_DIM // 2
    scale = HEAD_DIM ** -0.5
    for j in range((nq + nk) // 128):
        t = qkv[:, 128 * j:128 * (j + 1)]
        r = t * ct + pltpu.roll(t, 128 - half, axis=1) * slo + pltpu.roll(t, half, axis=1) * shi
        if 128 * j < nq:
            q_ref[:, 128 * j:128 * (j + 1)] = (r * scale).astype(q_ref.dtype)
        else:
            k_ref[:, 128 * j - nq:128 * (j + 1) - nq] = r.astype(k_ref.dtype)
    v_ref[...] = qkv[:, nq + nk:].astype(v_ref.dtype)


def _qkv(h, g, w, S, tm=512):
    M, D = h.shape
    nq = D
    nk = (w.shape[1] - nq) // 2
    ct, slo, shi = [jnp.asarray(a) for a in _rope_tables(S)]
    pb = S // tm
    tab = pl.BlockSpec((tm, 128), lambda i: (i % pb, 0))
    return pl.pallas_call(
        functools.partial(_qkv_kernel, nq=nq, nk=nk),
        grid=(M // tm,),
        in_specs=[pl.BlockSpec((tm, D), lambda i: (i, 0)),
                  pl.BlockSpec((1, D), lambda i: (0, 0)),
                  _resident((D, nq + 2 * nk), lambda i: (0, 0)),
                  tab, tab, tab],
        out_specs=[pl.BlockSpec((tm, nq), lambda i: (i, 0)),
                   pl.BlockSpec((tm, nk), lambda i: (i, 0)),
                   pl.BlockSpec((tm, nk), lambda i: (i, 0))],
        out_shape=[jax.ShapeDtypeStruct((M, nq), BF16),
                   jax.ShapeDtypeStruct((M, nk), BF16),
                   jax.ShapeDtypeStruct((M, nk), BF16)],
        scratch_shapes=[pltpu.VMEM(w.shape, BF16)],
        compiler_params=_cparams(1),
        name="qkv_rope",
    )(h, g.reshape(1, D), w, ct, slo, shi)


def _attn_kernel(sink_ref, q_ref, k_ref, v_ref, o_ref, krep_ref, vrep_ref, *, S, n_kv):
    BQ = WINDOW
    BK = 3 * WINDOW
    GW = KV_GROUP * HEAD_DIM
    dn = (((1,), (1,)), ((), ()))

    for g in range(n_kv):
        kg = k_ref[0, :, g * HEAD_DIM:(g + 1) * HEAD_DIM]
        vg = v_ref[0, :, g * HEAD_DIM:(g + 1) * HEAD_DIM]
        krep_ref[g] = jnp.concatenate([kg] * KV_GROUP, axis=1)
        vrep_ref[g] = jnp.concatenate([vg] * KV_GROUP, axis=1)

    lane = lax.broadcasted_iota(jnp.int32, (BQ, GW), 1)
    head_lanes = [(lane >= hh * HEAD_DIM) & (lane < (hh + 1) * HEAD_DIM) for hh in range(KV_GROUP)]
    rowblk = lax.broadcasted_iota(jnp.int32, (KV_GROUP * BQ, 1), 0) // BQ

    def body(qb, carry):
        qs = pl.multiple_of(qb * BQ, BQ)
        ks = pl.multiple_of(jnp.clip(qs - WINDOW, 0, S - BK), WINDOW)
        qpos = qs + lax.broadcasted_iota(jnp.int32, (BQ, BK), 0)
        kpos = ks + lax.broadcasted_iota(jnp.int32, (BQ, BK), 1)
        maskadd = jnp.where(jnp.abs(kpos - qpos) <= WINDOW, 0.0, NEG).astype(F32)
        maskadd = jnp.concatenate([maskadd] * KV_GROUP, axis=0)
        for g in range(n_kv):
            qg = q_ref[0, pl.ds(qs, BQ), g * GW:(g + 1) * GW]
            qst = jnp.concatenate([jnp.where(head_lanes[hh], qg, jnp.zeros_like(qg))
                                   for hh in range(KV_GROUP)], axis=0)
            sink = jnp.zeros((KV_GROUP * BQ, 1), F32)
            for hh in range(KV_GROUP):
                sink = jnp.where(rowblk == hh, sink_ref[g * KV_GROUP + hh], sink)
            s = lax.dot_general(qst, krep_ref[g, pl.ds(ks, BK), :], dn,
                                preferred_element_type=F32) + maskadd
            m = jnp.maximum(jnp.max(s, axis=-1, keepdims=True), sink)
            p = jnp.exp(s - m)
            denom = jnp.sum(p, axis=-1, keepdims=True) + jnp.exp(sink - m)
            ost = _bdot(p.astype(BF16), vrep_ref[g, pl.ds(ks, BK), :]) / denom
            o = jnp.where(head_lanes[0], ost[0:BQ], 0.0)
            for hh in range(1, KV_GROUP):
                o = jnp.where(head_lanes[hh], ost[hh * BQ:(hh + 1) * BQ], o)
            o_ref[0, pl.ds(qs, BQ), g * GW:(g + 1) * GW] = o.astype(o_ref.dtype)
        return carry

    lax.fori_loop(0, S // BQ, body, 0)


def _attention(q, k, v, sink):
    B, S, NQ = q.shape
    NK = k.shape[2]
    n_kv = NK // HEAD_DIM
    return pl.pallas_call(
        functools.partial(_attn_kernel, S=S, n_kv=n_kv),
        grid=(B,),
        in_specs=[pl.BlockSpec(memory_space=pltpu.SMEM),
                  pl.BlockSpec((1, S, NQ), lambda b: (b, 0, 0)),
                  pl.BlockSpec((1, S, NK), lambda b: (b, 0, 0)),
                  pl.BlockSpec((1, S, NK), lambda b: (b, 0, 0))],
        out_specs=pl.BlockSpec((1, S, NQ), lambda b: (b, 0, 0)),
        out_shape=jax.ShapeDtypeStruct((B, S, NQ), BF16),
        scratch_shapes=[pltpu.VMEM((n_kv, S, KV_GROUP * HEAD_DIM), BF16),
                        pltpu.VMEM((n_kv, S, KV_GROUP * HEAD_DIM), BF16)],
        compiler_params=_cparams(1),
        name="window_attention",
    )(sink, q, k, v)


def kernel(x, norm_mix_pre, norm_mix_post, norm_mlp_pre, norm_mlp_post, w_up, w_down, hy_w_in, hy_conv_w, hy_conv_b, hy_f_w1, hy_f_b1, hy_f_w2, hy_f_b2, hy_f_w3, hy_f_b3, hy_f_freq, hy_f_wout, hy_bias, hy_w_out, at_w_qkv, at_sink, at_w_o):
    B, L, D = x.shape
    M = B * L
    P = CONV_BLOCKS
    n = L // P
    depth = norm_mix_pre.shape[0]
    fwd, inv, filt_c, filt_s = [jnp.asarray(a).astype(BF16) for a in _dft_tables(n)]

    h = x.reshape(M, D)
    for i in range(depth):
        j = i // 2
        if i % 2 == 0:
            hfilt = _hyena_filters(L, D, hy_f_w1[j], hy_f_b1[j], hy_f_w2[j], hy_f_b2[j],
                                   hy_f_w3[j], hy_f_b3[j], hy_f_freq[j], hy_f_wout[j])
            gc, gs = _filter_spectra(hfilt, hy_bias[j], filt_c, filt_s, P)
            hn = _prenorm(h, norm_mix_pre[i]).reshape(B, L, D)
            z = _hyena_core(hn, hy_w_in[j], hy_conv_w[j], hy_conv_b[j].reshape(1, -1),
                            gc, gs, fwd, inv, P)
            a, w_o = z.reshape(M, D), hy_w_out[j]
        else:
            q, k, v = _qkv(h, norm_mix_pre[i], at_w_qkv[j], L)
            o = _attention(q.reshape(B, L, -1), k.reshape(B, L, -1), v.reshape(B, L, -1), at_sink[j])
            a, w_o = o.reshape(M, -1), at_w_o[j]
        h = _tail(a, w_o, h, norm_mix_post[i], norm_mlp_pre[i], norm_mlp_post[i], w_up, w_down, i)
    return h.reshape(B, L, D)
```

```python
import functools
import math

import numpy as np
import jax
import jax.numpy as jnp
from jax import lax
from jax.experimental import pallas as pl
from jax.experimental.pallas import tpu as pltpu

BF16 = jnp.bfloat16
F32 = jnp.float32

EPS = 1e-6
NEG = -1e30
LOG2E = math.log2(math.e)

HY_ORDER = 2
HY_EMB = 33
HY_BANDS = (HY_EMB - 1) // 2
HY_DECAY_TARGET = 1e-2
HY_FAST_DECAY = 0.3
HY_SLOW_DECAY = 1.5
HEAD_DIM = 64
KV_GROUP = 4
WINDOW = 128
ROT_DIM = HEAD_DIM // 4
ROPE_THETA = 500000.0

CONV_BLOCKS = 4
CH_BLOCK = 256
SPEC_DTYPE = BF16
STAGE_SLOTS = 4

VMEM_LIMIT = 56 * 1024 * 1024


def _cparams(n_axes):
    return pltpu.CompilerParams(
        dimension_semantics=("arbitrary",) * n_axes,
        vmem_limit_bytes=VMEM_LIMIT)


def _resident(shape, index_map):
    return pl.BlockSpec(shape, index_map, pipeline_mode=pl.Buffered(1))


def _rms(x, g):
    ms = jnp.mean(x * x, axis=-1, keepdims=True)
    return x * lax.rsqrt(ms + EPS) * g


def _bdot(a, b):
    return jnp.dot(a, b, preferred_element_type=F32)


def _dft_tables(n):
    f = np.arange(n, dtype=np.int64)[:, None]
    r = np.arange(n, dtype=np.int64)[None, :]
    k_pos = ((2 * f + 1) * r) % (4 * n)
    k_neg = ((2 * f + 1) * (r - n)) % (4 * n)
    ang_pos = np.pi * k_pos / (2 * n)
    ang_neg = np.pi * k_neg / (2 * n)
    cpos, spos = np.cos(ang_pos), np.sin(ang_pos)
    cneg, sneg = np.cos(ang_neg), np.sin(ang_neg)
    cneg[:, 0] = 0.0
    sneg[:, 0] = 0.0
    fwd = np.concatenate([cpos, spos], axis=0)
    inv = np.concatenate([cpos.T, spos.T], axis=1) / n
    filt_c = np.concatenate([cneg, cpos], axis=1)
    filt_s = np.concatenate([sneg, spos], axis=1)
    return [np.asarray(a, np.float32) for a in (fwd, inv, filt_c, filt_s)]


def _filter_positions(L):
    t = np.linspace(0.0, 1.0, L)[:, None]
    w = (2.0 * np.pi / L) * np.arange(L)[:, None]
    f = np.linspace(1e-4, HY_BANDS - 1, HY_BANDS)[None, :]
    z = np.concatenate([t, np.cos(f * w), -np.sin(f * w)], axis=-1)
    return np.asarray(z, np.float32)


def _rope_tables(S):
    half = ROT_DIM // 2
    inv = ROPE_THETA ** (-np.arange(0, ROT_DIM, 2, dtype=np.float64) / ROT_DIM)
    ang = np.arange(S, dtype=np.float64)[:, None] * inv[None, :]
    lane = np.arange(128) % HEAD_DIM
    ang_l = ang[:, lane % half]
    cos_t = np.where(lane[None, :] < ROT_DIM, np.cos(ang_l), 1.0)
    sin_lo = np.where(lane[None, :] < half, -np.sin(ang_l), 0.0)
    sin_hi = np.where((lane[None, :] >= half) & (lane[None, :] < ROT_DIM), np.sin(ang_l), 0.0)
    return [np.asarray(a, np.float32) for a in (cos_t, sin_lo, sin_hi)]


def _prenorm_kernel(x_ref, g_ref, o_ref):
    o_ref[...] = _rms(x_ref[...], g_ref[...]).astype(o_ref.dtype)


def _prenorm(x2d, g, tm=1024):
    M, D = x2d.shape
    return pl.pallas_call(
        _prenorm_kernel,
        grid=(M // tm,),
        in_specs=[pl.BlockSpec((tm, D), lambda i: (i, 0)),
                  pl.BlockSpec((1, D), lambda i: (0, 0))],
        out_specs=pl.BlockSpec((tm, D), lambda i: (i, 0)),
        out_shape=jax.ShapeDtypeStruct((M, D), BF16),
        compiler_params=_cparams(1),
        name="prenorm",
    )(x2d, g.reshape(1, D))


def _filter_kernel(z_ref, w1_ref, b1_ref, w2_ref, b2_ref, w3_ref, b3_ref, fr_ref,
                   wout_ref, dl_ref, o_ref, hid_ref, *, n_fwd_blocks):
    j = pl.program_id(0)
    hi = lax.Precision.HIGHEST

    @pl.when(j == 0)
    def _():
        fr = fr_ref[...]
        h = jnp.sin(fr * (jnp.dot(z_ref[...], w1_ref[...], precision=hi) + b1_ref[...]))
        h = jnp.sin(fr * (jnp.dot(h, w2_ref[...], precision=hi) + b2_ref[...]))
        h = jnp.sin(fr * (jnp.dot(h, w3_ref[...], precision=hi) + b3_ref[...]))
        hid_ref[...] = h

    L, W = o_ref.shape
    row = lax.broadcasted_iota(jnp.int32, (L, W), 0)
    t = row.astype(F32) * (1.0 / (L - 1))
    decay = jnp.exp(-t * dl_ref[...])
    val = jnp.dot(hid_ref[...], wout_ref[...], precision=hi) * decay
    keep = jnp.logical_or(row > 0, j < n_fwd_blocks)
    o_ref[...] = jnp.where(keep, val, 0.0)


def _hyena_filters(L, D, f_w1, f_b1, f_w2, f_b2, f_w3, f_b3, f_freq, f_wout):
    FW = f_w1.shape[1]
    z = jnp.asarray(np.pad(_filter_positions(L), ((0, 0), (0, FW - HY_EMB))))
    w1 = jnp.pad(f_w1, ((0, FW - HY_EMB), (0, 0)))
    max_decay = math.log(HY_DECAY_TARGET) / HY_FAST_DECAY
    min_decay = math.log(HY_DECAY_TARGET) / HY_SLOW_DECAY
    absdelta = jnp.asarray(np.abs(np.linspace(min_decay, max_decay, D)).astype(np.float32))[None, :]
    n_blocks = f_wout.shape[1] // D
    small = lambda shape: pl.BlockSpec(shape, lambda j: (0, 0))
    return pl.pallas_call(
        functools.partial(_filter_kernel, n_fwd_blocks=n_blocks // 2),
        grid=(n_blocks,),
        in_specs=[small((L, FW)), small((FW, FW)), small((1, FW)), small((FW, FW)), small((1, FW)),
                  small((FW, FW)), small((1, FW)), small((1, FW)),
                  pl.BlockSpec((FW, D), lambda j: (0, j)),
                  small((1, D))],
        out_specs=pl.BlockSpec((L, D), lambda j: (0, j)),
        out_shape=jax.ShapeDtypeStruct((L, n_blocks * D), F32),
        scratch_shapes=[pltpu.VMEM((L, FW), F32)],
        compiler_params=_cparams(1),
        name="hyena_filter",
    )(z, w1, f_b1.reshape(1, FW), f_w2, f_b2.reshape(1, FW), f_w3, f_b3.reshape(1, FW),
      f_freq.reshape(1, FW), f_wout, absdelta)


def _spectra_kernel(hf_ref, hb_ref, bias_ref, mc_ref, ms_ref, gc_ref, gs_ref, *, P, n):
    mc = mc_ref[...]
    ms = ms_ref[...]
    cpos = mc[:, n:]
    spos = ms[:, n:]
    for d in range(-(P - 1), P):
        if d == 0:
            hf0 = hf_ref[0:n, :]
            hb0 = hb_ref[0:n, :]
            gc = _bdot(cpos, (hf0 + hb0).astype(BF16)) + bias_ref[...]
            gs = _bdot(spos, (hf0 - hb0).astype(BF16))
        elif d > 0:
            taps = hf_ref[n * (d - 1):n * (d + 1), :].astype(BF16)
            gc = _bdot(mc, taps)
            gs = _bdot(ms, taps)
        else:
            a = -d
            taps = hb_ref[n * (a - 1):n * (a + 1), :].astype(BF16)
            gc = _bdot(mc, taps)
            gs = -_bdot(ms, taps)
        gc_ref[d + P - 1] = gc.astype(gc_ref.dtype)
        gs_ref[d + P - 1] = gs.astype(gs_ref.dtype)


def _filter_spectra(hfilt, bias, mc, ms, P, cw=256):
    L, W = hfilt.shape
    n = L // P
    half = W // 2
    nblk = half // cw
    out_sd = jax.ShapeDtypeStruct((2 * P - 1, n, half), SPEC_DTYPE)
    return pl.pallas_call(
        functools.partial(_spectra_kernel, P=P, n=n),
        grid=(nblk,),
        in_specs=[pl.BlockSpec((L, cw), lambda j: (0, j)),
                  pl.BlockSpec((L, cw), lambda j: (0, nblk + j)),
                  pl.BlockSpec((1, cw), lambda j: (0, j)),
                  pl.BlockSpec((n, 2 * n), lambda j: (0, 0)),
                  pl.BlockSpec((n, 2 * n), lambda j: (0, 0))],
        out_specs=[pl.BlockSpec((2 * P - 1, n, cw), lambda j: (0, 0, j)),
                   pl.BlockSpec((2 * P - 1, n, cw), lambda j: (0, 0, j))],
        out_shape=[out_sd, out_sd],
        compiler_params=_cparams(1),
        name="filter_spectra",
    )(hfilt, hfilt, bias.reshape(1, half), mc, ms)


def _hyena_kernel(hn_ref, wv_ref, w1_ref, w2_ref, cwv_ref, cw1_ref, cw2_ref,
                  cbv_ref, cb1_ref, cb2_ref, gc0_ref, gs0_ref, gc1_ref, gs1_ref,
                  fwd_ref, inv_ref, z_ref,
                  raw_ref, v_ref, gate_ref, uspec_ref, yspec_ref, *, P, n, fchunk):
    row = lax.broadcasted_iota(jnp.int32, (n, 1), 0)

    def short_conv(cw_ref, cb_ref, dst):
        cw = cw_ref[...]
        cb = cb_ref[...]
        for i in range(P):
            cur = raw_ref[n * i:n * (i + 1), :]
            prev = pltpu.roll(cur, 1, axis=0)
            nxt = pltpu.roll(cur, n - 1, axis=0)
            first = raw_ref[n * i - 1:n * i, :] if i > 0 else jnp.zeros_like(cur[0:1])
            last = raw_ref[n * (i + 1):n * (i + 1) + 1, :] if i < P - 1 else jnp.zeros_like(cur[0:1])
            prev = jnp.where(row == 0, first, prev)
            nxt = jnp.where(row == n - 1, last, nxt)
            out = prev * cw[0:1] + cur * cw[1:2] + nxt * cw[2:3] + cb
            dst[n * i:n * (i + 1), :] = out.astype(dst.dtype)

    def long_conv(src_ref, gc_ref, gs_ref, wgate_ref):
        fwd = fwd_ref[...]
        for j in range(P):
            uspec_ref[j] = _bdot(fwd, src_ref[n * j:n * (j + 1), :]).astype(uspec_ref.dtype)

        raw_ref[...] = _bdot(hn_ref[0], wgate_ref[...].astype(BF16))
        for c in range(n // fchunk):
            fc = slice(c * fchunk, (c + 1) * fchunk)
            fs = slice(n + c * fchunk, n + (c + 1) * fchunk)
            uc = [uspec_ref[j, fc, :] for j in range(P)]
            us = [uspec_ref[j, fs, :] for j in range(P)]
            for i in range(P):
                yc = None
                ys = None
                for j in range(P):
                    gc = gc_ref[i - j + P - 1, fc, :]
                    gs = gs_ref[i - j + P - 1, fc, :]
                    tc = gc * uc[j] - gs * us[j]
                    ts = gc * us[j] + gs * uc[j]
                    yc = tc if yc is None else yc + tc
                    ys = ts if ys is None else ys + ts
                yspec_ref[i, fc, :] = yc.astype(BF16)
                yspec_ref[i, fs, :] = ys.astype(BF16)

    inv = inv_ref[...]

    raw_ref[...] = _bdot(hn_ref[0], wv_ref[...].astype(BF16))
    short_conv(cwv_ref, cbv_ref, v_ref)

    long_conv(v_ref, gc0_ref, gs0_ref, w1_ref)
    short_conv(cw1_ref, cb1_ref, gate_ref)
    for i in range(P):
        sl = slice(n * i, n * (i + 1))
        v_ref[sl, :] = (gate_ref[sl, :] * _bdot(inv, yspec_ref[i])).astype(v_ref.dtype)

    long_conv(v_ref, gc1_ref, gs1_ref, w2_ref)
    short_conv(cw2_ref, cb2_ref, gate_ref)
    for i in range(P):
        sl = slice(n * i, n * (i + 1))
        z_ref[0, sl, :] = (gate_ref[sl, :] * _bdot(inv, yspec_ref[i])).astype(z_ref.dtype)


def _hyena_core(hn, w_in, conv_w, conv_b, gc, gs, fwd, inv, P):
    B, L, D = hn.shape
    n = L // P
    cb_n = D // CH_BLOCK
    C = CH_BLOCK
    nd = 2 * P - 1

    def stream_spec(shape, s):
        return pl.BlockSpec(shape, lambda c, b: (0, s * cb_n + c))

    in_specs = (
        [pl.BlockSpec((1, L, D), lambda c, b: (b, 0, 0))]
        + [stream_spec((D, C), s) for s in range(3)]
        + [stream_spec((3, C), s) for s in range(3)]
        + [stream_spec((1, C), s) for s in range(3)]
        + [_resident((nd, n, C), lambda c, b: (0, 0, c)),
           _resident((nd, n, C), lambda c, b: (0, 0, c)),
           _resident((nd, n, C), lambda c, b: (0, 0, cb_n + c)),
           _resident((nd, n, C), lambda c, b: (0, 0, cb_n + c)),
           pl.BlockSpec((2 * n, n), lambda c, b: (0, 0)),
           pl.BlockSpec((n, 2 * n), lambda c, b: (0, 0))])
    return pl.pallas_call(
        functools.partial(_hyena_kernel, P=P, n=n, fchunk=16),
        grid=(cb_n, B),
        in_specs=in_specs,
        out_specs=pl.BlockSpec((1, L, C), lambda c, b: (b, 0, c)),
        out_shape=jax.ShapeDtypeStruct((B, L, D), BF16),
        scratch_shapes=[pltpu.VMEM((L, C), F32),
                        pltpu.VMEM((L, C), BF16),
                        pltpu.VMEM((L, C), F32),
                        pltpu.VMEM((P, 2 * n, C), SPEC_DTYPE),
                        pltpu.VMEM((P, 2 * n, C), BF16)],
        compiler_params=_cparams(2),
        name="hyena_core",
    )(hn, w_in, w_in, w_in, conv_w, conv_w, conv_w, conv_b, conv_b, conv_b,
      gc, gs, gc, gs, fwd, inv)


def _stream_cast(src_hbm, layer, dst_ref, stage_ref, sem_ref):
    slots, rows = stage_ref.shape[0], stage_ref.shape[1]
    n_chunks = dst_ref.shape[0] // rows

    def copy(c):
        return pltpu.make_async_copy(src_hbm.at[layer, pl.ds(c * rows, rows), :],
                                     stage_ref.at[c % slots], sem_ref.at[c % slots])

    for c in range(min(slots - 1, n_chunks)):
        copy(c).start()
    for c in range(n_chunks):
        ahead = c + slots - 1
        if ahead < n_chunks:
            copy(ahead).start()
        copy(c).wait()
        dst_ref[c * rows:(c + 1) * rows, :] = stage_ref[c % slots].astype(dst_ref.dtype)


def _tail_kernel(a_ref, wo32_ref, r_ref, gmix_ref, gpre_ref, gpost_ref, wu_hbm, wd_hbm, o_ref,
                 wo_ref, wu_ref, wd_ref, stage_u, stage_d, sem_u, sem_d, *, fchunk, layer):
    @pl.when(pl.program_id(0) == 0)
    def _():
        wo_ref[...] = wo32_ref[...].astype(wo_ref.dtype)
        _stream_cast(wu_hbm, layer, wu_ref, stage_u, sem_u)
        _stream_cast(wd_hbm, layer, wd_ref, stage_d, sem_d)

    m = _bdot(a_ref[...], wo_ref[...])
    h = r_ref[...] + _rms(m, gmix_ref[...])
    hn = _rms(h, gpre_ref[...]).astype(BF16)
    FF = wu_ref.shape[1]
    acc = None
    for c in range(FF // fchunk):
        a = _bdot(hn, wu_ref[:, c * fchunk:(c + 1) * fchunk])
        a = jnp.maximum(a, 0.0)
        a = (a * a).astype(BF16)
        part = _bdot(a, wd_ref[c * fchunk:(c + 1) * fchunk, :])
        acc = part if acc is None else acc + part
    o_ref[...] = h + _rms(acc, gpost_ref[...])


def _tail(a, w_o, resid, g_mix, g_pre, g_post, w_up_all, w_down_all, layer, tm=512, fchunk=1024):
    M, K = a.shape
    D = w_o.shape[1]
    FF = w_up_all.shape[2]
    stage_bytes = 1024 * 1024
    rows_u = stage_bytes // (FF * 4)
    rows_d = stage_bytes // (D * 4)
    row = lambda shape: pl.BlockSpec(shape, lambda i: (i, 0))
    gain = pl.BlockSpec((1, D), lambda i: (0, 0))
    return pl.pallas_call(
        functools.partial(_tail_kernel, fchunk=fchunk, layer=layer),
        grid=(M // tm,),
        in_specs=[row((tm, K)), _resident((K, D), lambda i: (0, 0)), row((tm, D)),
                  gain, gain, gain,
                  pl.BlockSpec(memory_space=pl.ANY),
                  pl.BlockSpec(memory_space=pl.ANY)],
        out_specs=row((tm, D)),
        out_shape=jax.ShapeDtypeStruct((M, D), F32),
        scratch_shapes=[pltpu.VMEM((K, D), BF16),
                        pltpu.VMEM((D, FF), BF16),
                        pltpu.VMEM((FF, D), BF16),
                        pltpu.VMEM((STAGE_SLOTS, rows_u, FF), F32),
                        pltpu.VMEM((STAGE_SLOTS, rows_d, D), F32),
                        pltpu.SemaphoreType.DMA((STAGE_SLOTS,)),
                        pltpu.SemaphoreType.DMA((STAGE_SLOTS,))],
        compiler_params=_cparams(1),
        name="mixer_tail_mlp",
    )(a, w_o, resid, g_mix.reshape(1, D), g_pre.reshape(1, D), g_post.reshape(1, D),
      w_up_all, w_down_all)


def _qkv_kernel(h_ref, g_ref, w32_ref, ct_ref, slo_ref, shi_ref, q_ref, k_ref, v_ref, w_ref, *, nq, nk):
    @pl.when(pl.program_id(0) == 0)
    def _():
        w_ref[...] = w32_ref[...].astype(w_ref.dtype)

    hn = _rms(h_ref[...], g_ref[...]).astype(BF16)
    qkv = _bdot(hn, w_ref[...])
    ct = ct_ref[...]
    slo = slo_ref[...]
    shi = shi_ref[...]
    half = ROT_DIM // 2
    scale = LOG2E * HEAD_DIM ** -0.5
    for j in range((nq + nk) // 128):
        t = qkv[:, 128 * j:128 * (j + 1)]
        r = t * ct + pltpu.roll(t, 128 - half, axis=1) * slo + pltpu.roll(t, half, axis=1) * shi
        if 128 * j < nq:
            q_ref[:, 128 * j:128 * (j + 1)] = (r * scale).astype(q_ref.dtype)
        else:
            k_ref[:, 128 * j - nq:128 * (j + 1) - nq] = r.astype(k_ref.dtype)
    v_ref[...] = qkv[:, nq + nk:].astype(v_ref.dtype)


def _qkv(h, g, w, S, tm=512):
    M, D = h.shape
    nq = D
    nk = (w.shape[1] - nq) // 2
    ct, slo, shi = [jnp.asarray(a) for a in _rope_tables(S)]
    pb = S // tm
    tab = pl.BlockSpec((tm, 128), lambda i: (i % pb, 0))
    return pl.pallas_call(
        functools.partial(_qkv_kernel, nq=nq, nk=nk),
        grid=(M // tm,),
        in_specs=[pl.BlockSpec((tm, D), lambda i: (i, 0)),
                  pl.BlockSpec((1, D), lambda i: (0, 0)),
                  _resident((D, nq + 2 * nk), lambda i: (0, 0)),
                  tab, tab, tab],
        out_specs=[pl.BlockSpec((tm, nq), lambda i: (i, 0)),
                   pl.BlockSpec((tm, nk), lambda i: (i, 0)),
                   pl.BlockSpec((tm, nk), lambda i: (i, 0))],
        out_shape=[jax.ShapeDtypeStruct((M, nq), BF16),
                   jax.ShapeDtypeStruct((M, nk), BF16),
                   jax.ShapeDtypeStruct((M, nk), BF16)],
        scratch_shapes=[pltpu.VMEM(w.shape, BF16)],
        compiler_params=_cparams(1),
        name="qkv_rope",
    )(h, g.reshape(1, D), w, ct, slo, shi)


def _attn_kernel(sink_ref, q_ref, k_ref, v_ref, o_ref, krep_ref, vrep_ref, *, S, n_kv):
    BQ = WINDOW
    BK = 3 * WINDOW
    GW = KV_GROUP * HEAD_DIM
    dn = (((1,), (1,)), ((), ()))

    for g in range(n_kv):
        kg = k_ref[0, :, g * HEAD_DIM:(g + 1) * HEAD_DIM]
        vg = v_ref[0, :, g * HEAD_DIM:(g + 1) * HEAD_DIM]
        krep_ref[g] = jnp.concatenate([kg] * KV_GROUP, axis=1)
        vrep_ref[g] = jnp.concatenate([vg] * KV_GROUP, axis=1)

    lane = lax.broadcasted_iota(jnp.int32, (BQ, GW), 1)
    head_lanes = [(lane >= hh * HEAD_DIM) & (lane < (hh + 1) * HEAD_DIM) for hh in range(KV_GROUP)]
    rowblk = lax.broadcasted_iota(jnp.int32, (KV_GROUP * BQ, 1), 0) // BQ

    def body(qb, carry):
        qs = pl.multiple_of(qb * BQ, BQ)
        ks = pl.multiple_of(jnp.clip(qs - WINDOW, 0, S - BK), WINDOW)
        qpos = qs + lax.broadcasted_iota(jnp.int32, (BQ, BK), 0)
        kpos = ks + lax.broadcasted_iota(jnp.int32, (BQ, BK), 1)
        maskadd = jnp.where(jnp.abs(kpos - qpos) <= WINDOW, 0.0, NEG).astype(F32)
        maskadd = jnp.concatenate([maskadd] * KV_GROUP, axis=0)
        for g in range(n_kv):
            qg = q_ref[0, pl.ds(qs, BQ), g * GW:(g + 1) * GW]
            qst = jnp.concatenate([jnp.where(head_lanes[hh], qg, jnp.zeros_like(qg))
                                   for hh in range(KV_GROUP)], axis=0)
            sink = jnp.zeros((KV_GROUP * BQ, 1), F32)
            for hh in range(KV_GROUP):
                sink = jnp.where(rowblk == hh, sink_ref[g * KV_GROUP + hh] * LOG2E, sink)
            s = lax.dot_general(qst, krep_ref[g, pl.ds(ks, BK), :], dn,
                                preferred_element_type=F32) + maskadd
            m = jnp.maximum(jnp.max(s, axis=-1, keepdims=True), sink)
            p = jnp.exp2(s - m)
            denom = jnp.sum(p, axis=-1, keepdims=True) + jnp.exp2(sink - m)
            ost = _bdot(p.astype(BF16), vrep_ref[g, pl.ds(ks, BK), :]) / denom
            o = jnp.where(head_lanes[0], ost[0:BQ], 0.0)
            for hh in range(1, KV_GROUP):
                o = jnp.where(head_lanes[hh], ost[hh * BQ:(hh + 1) * BQ], o)
            o_ref[0, pl.ds(qs, BQ), g * GW:(g + 1) * GW] = o.astype(o_ref.dtype)
        return carry

    lax.fori_loop(0, S // BQ, body, 0, unroll=4)


def _attention(q, k, v, sink):
    B, S, NQ = q.shape
    NK = k.shape[2]
    n_kv = NK // HEAD_DIM
    return pl.pallas_call(
        functools.partial(_attn_kernel, S=S, n_kv=n_kv),
        grid=(B,),
        in_specs=[pl.BlockSpec(memory_space=pltpu.SMEM),
                  pl.BlockSpec((1, S, NQ), lambda b: (b, 0, 0)),
                  pl.BlockSpec((1, S, NK), lambda b: (b, 0, 0)),
                  pl.BlockSpec((1, S, NK), lambda b: (b, 0, 0))],
        out_specs=pl.BlockSpec((1, S, NQ), lambda b: (b, 0, 0)),
        out_shape=jax.ShapeDtypeStruct((B, S, NQ), BF16),
        scratch_shapes=[pltpu.VMEM((n_kv, S, KV_GROUP * HEAD_DIM), BF16),
                        pltpu.VMEM((n_kv, S, KV_GROUP * HEAD_DIM), BF16)],
        compiler_params=_cparams(1),
        name="window_attention",
    )(sink, q, k, v)


def kernel(x, norm_mix_pre, norm_mix_post, norm_mlp_pre, norm_mlp_post, w_up, w_down, hy_w_in, hy_conv_w, hy_conv_b, hy_f_w1, hy_f_b1, hy_f_w2, hy_f_b2, hy_f_w3, hy_f_b3, hy_f_freq, hy_f_wout, hy_bias, hy_w_out, at_w_qkv, at_sink, at_w_o):
    B, L, D = x.shape
    M = B * L
    P = CONV_BLOCKS
    n = L // P
    depth = norm_mix_pre.shape[0]
    fwd, inv, filt_c, filt_s = [jnp.asarray(a).astype(BF16) for a in _dft_tables(n)]

    h = x.reshape(M, D)
    for i in range(depth):
        j = i // 2
        if i % 2 == 0:
            hfilt = _hyena_filters(L, D, hy_f_w1[j], hy_f_b1[j], hy_f_w2[j], hy_f_b2[j],
                                   hy_f_w3[j], hy_f_b3[j], hy_f_freq[j], hy_f_wout[j])
            gc, gs = _filter_spectra(hfilt, hy_bias[j], filt_c, filt_s, P)
            hn = _prenorm(h, norm_mix_pre[i]).reshape(B, L, D)
            z = _hyena_core(hn, hy_w_in[j], hy_conv_w[j], hy_conv_b[j].reshape(1, -1),
                            gc, gs, fwd, inv, P)
            a, w_o = z.reshape(M, D), hy_w_out[j]
        else:
            q, k, v = _qkv(h, norm_mix_pre[i], at_w_qkv[j], L)
            o = _attention(q.reshape(B, L, -1), k.reshape(B, L, -1), v.reshape(B, L, -1), at_sink[j])
            a, w_o = o.reshape(M, -1), at_w_o[j]
        h = _tail(a, w_o, h, norm_mix_post[i], norm_mlp_pre[i], norm_mlp_post[i], w_up, w_down, i)
    return h.reshape(B, L, D)
```

```python
import functools
import math

import numpy as np
import jax
import jax.numpy as jnp
from jax import lax
from jax.experimental import pallas as pl
from jax.experimental.pallas import tpu as pltpu

BF16 = jnp.bfloat16
F32 = jnp.float32

EPS = 1e-6
NEG = -1e30
LOG2E = math.log2(math.e)

HY_ORDER = 2
HY_EMB = 33
HY_BANDS = (HY_EMB - 1) // 2
HY_DECAY_TARGET = 1e-2
HY_FAST_DECAY = 0.3
HY_SLOW_DECAY = 1.5
HEAD_DIM = 64
KV_GROUP = 4
WINDOW = 128
ROT_DIM = HEAD_DIM // 4
ROPE_THETA = 500000.0

CONV_BLOCKS = 4
CH_BLOCK = 256
SPEC_DTYPE = BF16
STAGE_SLOTS = 4

VMEM_LIMIT = 56 * 1024 * 1024


def _cparams(n_axes):
    return pltpu.CompilerParams(
        dimension_semantics=("arbitrary",) * n_axes,
        vmem_limit_bytes=VMEM_LIMIT)


def _resident(shape, index_map):
    return pl.BlockSpec(shape, index_map, pipeline_mode=pl.Buffered(1))


def _rms(x, g):
    ms = jnp.mean(x * x, axis=-1, keepdims=True)
    return x * lax.rsqrt(ms + EPS) * g


def _bdot(a, b):
    return jnp.dot(a, b, preferred_element_type=F32)


def _dft_tables(n):
    f = np.arange(n, dtype=np.int64)[:, None]
    r = np.arange(n, dtype=np.int64)[None, :]
    k_pos = ((2 * f + 1) * r) % (4 * n)
    k_neg = ((2 * f + 1) * (r - n)) % (4 * n)
    ang_pos = np.pi * k_pos / (2 * n)
    ang_neg = np.pi * k_neg / (2 * n)
    cpos, spos = np.cos(ang_pos), np.sin(ang_pos)
    cneg, sneg = np.cos(ang_neg), np.sin(ang_neg)
    cneg[:, 0] = 0.0
    sneg[:, 0] = 0.0
    fwd = np.concatenate([cpos, spos], axis=0)
    inv = np.concatenate([cpos.T, spos.T], axis=1) / n
    filt_c = np.concatenate([cneg, cpos], axis=1)
    filt_s = np.concatenate([sneg, spos], axis=1)
    return [np.asarray(a, np.float32) for a in (fwd, inv, filt_c, filt_s)]


def _filter_positions(L):
    t = np.linspace(0.0, 1.0, L)[:, None]
    w = (2.0 * np.pi / L) * np.arange(L)[:, None]
    f = np.linspace(1e-4, HY_BANDS - 1, HY_BANDS)[None, :]
    z = np.concatenate([t, np.cos(f * w), -np.sin(f * w)], axis=-1)
    return np.asarray(z, np.float32)


def _rope_tables(S):
    half = ROT_DIM // 2
    inv = ROPE_THETA ** (-np.arange(0, ROT_DIM, 2, dtype=np.float64) / ROT_DIM)
    ang = np.arange(S, dtype=np.float64)[:, None] * inv[None, :]
    lane = np.arange(128) % HEAD_DIM
    ang_l = ang[:, lane % half]
    cos_t = np.where(lane[None, :] < ROT_DIM, np.cos(ang_l), 1.0)
    sin_lo = np.where(lane[None, :] < half, -np.sin(ang_l), 0.0)
    sin_hi = np.where((lane[None, :] >= half) & (lane[None, :] < ROT_DIM), np.sin(ang_l), 0.0)
    return [np.asarray(a, np.float32) for a in (cos_t, sin_lo, sin_hi)]


def _prenorm_kernel(x_ref, g_ref, o_ref):
    o_ref[...] = _rms(x_ref[...], g_ref[...]).astype(o_ref.dtype)


def _prenorm(x2d, g, tm=1024):
    M, D = x2d.shape
    return pl.pallas_call(
        _prenorm_kernel,
        grid=(M // tm,),
        in_specs=[pl.BlockSpec((tm, D), lambda i: (i, 0)),
                  pl.BlockSpec((1, D), lambda i: (0, 0))],
        out_specs=pl.BlockSpec((tm, D), lambda i: (i, 0)),
        out_shape=jax.ShapeDtypeStruct((M, D), BF16),
        compiler_params=_cparams(1),
        name="prenorm",
    )(x2d, g.reshape(1, D))


def _filter_kernel(z_ref, w1_ref, b1_ref, w2_ref, b2_ref, w3_ref, b3_ref, fr_ref,
                   wout_ref, dl_ref, o_ref, hid_ref, *, n_fwd_blocks):
    j = pl.program_id(0)
    hi = lax.Precision.HIGHEST

    @pl.when(j == 0)
    def _():
        fr = fr_ref[...]
        h = jnp.sin(fr * (jnp.dot(z_ref[...], w1_ref[...], precision=hi) + b1_ref[...]))
        h = jnp.sin(fr * (jnp.dot(h, w2_ref[...], precision=hi) + b2_ref[...]))
        h = jnp.sin(fr * (jnp.dot(h, w3_ref[...], precision=hi) + b3_ref[...]))
        hid_ref[...] = h

    L, W = o_ref.shape
    row = lax.broadcasted_iota(jnp.int32, (L, W), 0)
    t = row.astype(F32) * (1.0 / (L - 1))
    decay = jnp.exp(-t * dl_ref[...])
    val = jnp.dot(hid_ref[...], wout_ref[...], precision=hi) * decay
    keep = jnp.logical_or(row > 0, j < n_fwd_blocks)
    o_ref[...] = jnp.where(keep, val, 0.0)


def _hyena_filters(L, D, f_w1, f_b1, f_w2, f_b2, f_w3, f_b3, f_freq, f_wout):
    FW = f_w1.shape[1]
    z = jnp.asarray(np.pad(_filter_positions(L), ((0, 0), (0, FW - HY_EMB))))
    w1 = jnp.pad(f_w1, ((0, FW - HY_EMB), (0, 0)))
    max_decay = math.log(HY_DECAY_TARGET) / HY_FAST_DECAY
    min_decay = math.log(HY_DECAY_TARGET) / HY_SLOW_DECAY
    absdelta = jnp.asarray(np.abs(np.linspace(min_decay, max_decay, D)).astype(np.float32))[None, :]
    n_blocks = f_wout.shape[1] // D
    small = lambda shape: pl.BlockSpec(shape, lambda j: (0, 0))
    return pl.pallas_call(
        functools.partial(_filter_kernel, n_fwd_blocks=n_blocks // 2),
        grid=(n_blocks,),
        in_specs=[small((L, FW)), small((FW, FW)), small((1, FW)), small((FW, FW)), small((1, FW)),
                  small((FW, FW)), small((1, FW)), small((1, FW)),
                  pl.BlockSpec((FW, D), lambda j: (0, j)),
                  small((1, D))],
        out_specs=pl.BlockSpec((L, D), lambda j: (0, j)),
        out_shape=jax.ShapeDtypeStruct((L, n_blocks * D), F32),
        scratch_shapes=[pltpu.VMEM((L, FW), F32)],
        compiler_params=_cparams(1),
        name="hyena_filter",
    )(z, w1, f_b1.reshape(1, FW), f_w2, f_b2.reshape(1, FW), f_w3, f_b3.reshape(1, FW),
      f_freq.reshape(1, FW), f_wout, absdelta)


def _spectra_kernel(hf_ref, hb_ref, bias_ref, mc_ref, ms_ref, gc_ref, gs_ref, *, P, n):
    mc = mc_ref[...]
    ms = ms_ref[...]
    cpos = mc[:, n:]
    spos = ms[:, n:]
    for d in range(-(P - 1), P):
        if d == 0:
            hf0 = hf_ref[0:n, :]
            hb0 = hb_ref[0:n, :]
            gc = _bdot(cpos, (hf0 + hb0).astype(BF16)) + bias_ref[...]
            gs = _bdot(spos, (hf0 - hb0).astype(BF16))
        elif d > 0:
            taps = hf_ref[n * (d - 1):n * (d + 1), :].astype(BF16)
            gc = _bdot(mc, taps)
            gs = _bdot(ms, taps)
        else:
            a = -d
            taps = hb_ref[n * (a - 1):n * (a + 1), :].astype(BF16)
            gc = _bdot(mc, taps)
            gs = -_bdot(ms, taps)
        gc_ref[d + P - 1] = gc.astype(gc_ref.dtype)
        gs_ref[d + P - 1] = gs.astype(gs_ref.dtype)


def _filter_spectra(hfilt, bias, mc, ms, P, cw=256):
    L, W = hfilt.shape
    n = L // P
    half = W // 2
    nblk = half // cw
    out_sd = jax.ShapeDtypeStruct((2 * P - 1, n, half), SPEC_DTYPE)
    return pl.pallas_call(
        functools.partial(_spectra_kernel, P=P, n=n),
        grid=(nblk,),
        in_specs=[pl.BlockSpec((L, cw), lambda j: (0, j)),
                  pl.BlockSpec((L, cw), lambda j: (0, nblk + j)),
                  pl.BlockSpec((1, cw), lambda j: (0, j)),
                  pl.BlockSpec((n, 2 * n), lambda j: (0, 0)),
                  pl.BlockSpec((n, 2 * n), lambda j: (0, 0))],
        out_specs=[pl.BlockSpec((2 * P - 1, n, cw), lambda j: (0, 0, j)),
                   pl.BlockSpec((2 * P - 1, n, cw), lambda j: (0, 0, j))],
        out_shape=[out_sd, out_sd],
        compiler_params=_cparams(1),
        name="filter_spectra",
    )(hfilt, hfilt, bias.reshape(1, half), mc, ms)


def _hyena_kernel(hn_ref, wv_ref, w1_ref, w2_ref, cwv_ref, cw1_ref, cw2_ref,
                  cbv_ref, cb1_ref, cb2_ref, gc0_ref, gs0_ref, gc1_ref, gs1_ref,
                  fwd_ref, inv_ref, z_ref,
                  rawa_ref, rawb_ref, v_ref, gatea_ref, gateb_ref, uspec_ref, yspec_ref,
                  *, P, n, fchunk):
    row = lax.broadcasted_iota(jnp.int32, (n, 1), 0)
    blk = [slice(n * i, n * (i + 1)) for i in range(P)]
    chunks = n // fchunk
    fwd = fwd_ref[...]
    inv = inv_ref[...]

    def project(w_ref, dst_ref, i):
        dst_ref[blk[i], :] = _bdot(hn_ref[0, blk[i], :], w_ref[...].astype(BF16))

    def short_conv(raw_ref, cw_ref, cb_ref, dst, i):
        cw = cw_ref[...]
        cur = raw_ref[blk[i], :]
        prev = pltpu.roll(cur, 1, axis=0)
        nxt = pltpu.roll(cur, n - 1, axis=0)
        first = raw_ref[n * i - 1:n * i, :] if i > 0 else jnp.zeros_like(cur[0:1])
        last = raw_ref[n * (i + 1):n * (i + 1) + 1, :] if i < P - 1 else jnp.zeros_like(cur[0:1])
        prev = jnp.where(row == 0, first, prev)
        nxt = jnp.where(row == n - 1, last, nxt)
        out = prev * cw[0:1] + cur * cw[1:2] + nxt * cw[2:3] + cb_ref[...]
        dst[blk[i], :] = out.astype(dst.dtype)

    def forward(src_ref, j):
        uspec_ref[j] = _bdot(fwd, src_ref[blk[j], :]).astype(uspec_ref.dtype)

    def combine(gc_ref, gs_ref, c):
        fc = slice(c * fchunk, (c + 1) * fchunk)
        fs = slice(n + c * fchunk, n + (c + 1) * fchunk)
        uc = [uspec_ref[j, fc, :] for j in range(P)]
        us = [uspec_ref[j, fs, :] for j in range(P)]
        for i in range(P):
            yc = None
            ys = None
            for j in range(P):
                gc = gc_ref[i - j + P - 1, fc, :]
                gs = gs_ref[i - j + P - 1, fc, :]
                tc = gc * uc[j] - gs * us[j]
                ts = gc * us[j] + gs * uc[j]
                yc = tc if yc is None else yc + tc
                ys = ts if ys is None else ys + ts
            yspec_ref[i, fc, :] = yc.astype(BF16)
            yspec_ref[i, fs, :] = ys.astype(BF16)

    for i in range(P):
        project(wv_ref, rawa_ref, i)
    for i in range(P):
        short_conv(rawa_ref, cwv_ref, cbv_ref, v_ref, i)
        project(w1_ref, rawb_ref, i)
    for i in range(P):
        forward(v_ref, i)
        short_conv(rawb_ref, cw1_ref, cb1_ref, gatea_ref, i)
    for c in range(chunks):
        combine(gc0_ref, gs0_ref, c)
        if c % (chunks // P) == 0:
            project(w2_ref, rawa_ref, c // (chunks // P))
    for i in range(P):
        short_conv(rawa_ref, cw2_ref, cb2_ref, gateb_ref, i)
        v_ref[blk[i], :] = (gatea_ref[blk[i], :] * _bdot(inv, yspec_ref[i])).astype(v_ref.dtype)
    for i in range(P):
        forward(v_ref, i)
    for c in range(chunks):
        combine(gc1_ref, gs1_ref, c)
    for i in range(P):
        z_ref[0, blk[i], :] = (gateb_ref[blk[i], :] * _bdot(inv, yspec_ref[i])).astype(z_ref.dtype)


def _hyena_core(hn, w_in, conv_w, conv_b, gc, gs, fwd, inv, P):
    B, L, D = hn.shape
    n = L // P
    cb_n = D // CH_BLOCK
    C = CH_BLOCK
    nd = 2 * P - 1

    def stream_spec(shape, s):
        return pl.BlockSpec(shape, lambda c, b: (0, s * cb_n + c))

    in_specs = (
        [pl.BlockSpec((1, L, D), lambda c, b: (b, 0, 0))]
        + [stream_spec((D, C), s) for s in range(3)]
        + [stream_spec((3, C), s) for s in range(3)]
        + [stream_spec((1, C), s) for s in range(3)]
        + [_resident((nd, n, C), lambda c, b: (0, 0, c)),
           _resident((nd, n, C), lambda c, b: (0, 0, c)),
           _resident((nd, n, C), lambda c, b: (0, 0, cb_n + c)),
           _resident((nd, n, C), lambda c, b: (0, 0, cb_n + c)),
           pl.BlockSpec((2 * n, n), lambda c, b: (0, 0)),
           pl.BlockSpec((n, 2 * n), lambda c, b: (0, 0))])
    return pl.pallas_call(
        functools.partial(_hyena_kernel, P=P, n=n, fchunk=16),
        grid=(cb_n, B),
        in_specs=in_specs,
        out_specs=pl.BlockSpec((1, L, C), lambda c, b: (b, 0, c)),
        out_shape=jax.ShapeDtypeStruct((B, L, D), BF16),
        scratch_shapes=[pltpu.VMEM((L, C), F32),
                        pltpu.VMEM((L, C), F32),
                        pltpu.VMEM((L, C), BF16),
                        pltpu.VMEM((L, C), F32),
                        pltpu.VMEM((L, C), F32),
                        pltpu.VMEM((P, 2 * n, C), SPEC_DTYPE),
                        pltpu.VMEM((P, 2 * n, C), BF16)],
        compiler_params=_cparams(2),
        name="hyena_core",
    )(hn, w_in, w_in, w_in, conv_w, conv_w, conv_w, conv_b, conv_b, conv_b,
      gc, gs, gc, gs, fwd, inv)


def _stream_cast(src_hbm, layer, dst_ref, stage_ref, sem_ref):
    slots, rows = stage_ref.shape[0], stage_ref.shape[1]
    n_chunks = dst_ref.shape[0] // rows

    def copy(c):
        return pltpu.make_async_copy(src_hbm.at[layer, pl.ds(c * rows, rows), :],
                                     stage_ref.at[c % slots], sem_ref.at[c % slots])

    for c in range(min(slots - 1, n_chunks)):
        copy(c).start()
    for c in range(n_chunks):
        ahead = c + slots - 1
        if ahead < n_chunks:
            copy(ahead).start()
        copy(c).wait()
        dst_ref[c * rows:(c + 1) * rows, :] = stage_ref[c % slots].astype(dst_ref.dtype)


def _tail_kernel(a_ref, wo32_ref, r_ref, gmix_ref, gpre_ref, gpost_ref, wu_hbm, wd_hbm, o_ref,
                 wo_ref, wu_ref, wd_ref, stage_u, stage_d, sem_u, sem_d, *, fchunk, layer):
    @pl.when(pl.program_id(0) == 0)
    def _():
        wo_ref[...] = wo32_ref[...].astype(wo_ref.dtype)
        _stream_cast(wu_hbm, layer, wu_ref, stage_u, sem_u)
        _stream_cast(wd_hbm, layer, wd_ref, stage_d, sem_d)

    m = _bdot(a_ref[...], wo_ref[...])
    h = r_ref[...] + _rms(m, gmix_ref[...])
    hn = _rms(h, gpre_ref[...]).astype(BF16)
    FF = wu_ref.shape[1]
    acc = None
    for c in range(FF // fchunk):
        a = _bdot(hn, wu_ref[:, c * fchunk:(c + 1) * fchunk])
        a = jnp.maximum(a, 0.0)
        a = (a * a).astype(BF16)
        part = _bdot(a, wd_ref[c * fchunk:(c + 1) * fchunk, :])
        acc = part if acc is None else acc + part
    o_ref[...] = h + _rms(acc, gpost_ref[...])


def _tail(a, w_o, resid, g_mix, g_pre, g_post, w_up_all, w_down_all, layer, tm=512, fchunk=1024):
    M, K = a.shape
    D = w_o.shape[1]
    FF = w_up_all.shape[2]
    stage_bytes = 1024 * 1024
    rows_u = stage_bytes // (FF * 4)
    rows_d = stage_bytes // (D * 4)
    row = lambda shape: pl.BlockSpec(shape, lambda i: (i, 0))
    gain = pl.BlockSpec((1, D), lambda i: (0, 0))
    return pl.pallas_call(
        functools.partial(_tail_kernel, fchunk=fchunk, layer=layer),
        grid=(M // tm,),
        in_specs=[row((tm, K)), _resident((K, D), lambda i: (0, 0)), row((tm, D)),
                  gain, gain, gain,
                  pl.BlockSpec(memory_space=pl.ANY),
                  pl.BlockSpec(memory_space=pl.ANY)],
        out_specs=row((tm, D)),
        out_shape=jax.ShapeDtypeStruct((M, D), F32),
        scratch_shapes=[pltpu.VMEM((K, D), BF16),
                        pltpu.VMEM((D, FF), BF16),
                        pltpu.VMEM((FF, D), BF16),
                        pltpu.VMEM((STAGE_SLOTS, rows_u, FF), F32),
                        pltpu.VMEM((STAGE_SLOTS, rows_d, D), F32),
                        pltpu.SemaphoreType.DMA((STAGE_SLOTS,)),
                        pltpu.SemaphoreType.DMA((STAGE_SLOTS,))],
        compiler_params=_cparams(1),
        name="mixer_tail_mlp",
    )(a, w_o, resid, g_mix.reshape(1, D), g_pre.reshape(1, D), g_post.reshape(1, D),
      w_up_all, w_down_all)


def _qkv_kernel(h_ref, g_ref, w32_ref, ct_ref, slo_ref, shi_ref, q_ref, k_ref, v_ref, w_ref, *, nq, nk):
    @pl.when(pl.program_id(0) == 0)
    def _():
        w_ref[...] = w32_ref[...].astype(w_ref.dtype)

    hn = _rms(h_ref[...], g_ref[...]).astype(BF16)
    qkv = _bdot(hn, w_ref[...])
    ct = ct_ref[...]
    slo = slo_ref[...]
    shi = shi_ref[...]
    half = ROT_DIM // 2
    scale = LOG2E * HEAD_DIM ** -0.5
    for j in range((nq + nk) // 128):
        t = qkv[:, 128 * j:128 * (j + 1)]
        r = t * ct + pltpu.roll(t, 128 - half, axis=1) * slo + pltpu.roll(t, half, axis=1) * shi
        if 128 * j < nq:
            q_ref[:, 128 * j:128 * (j + 1)] = (r * scale).astype(q_ref.dtype)
        else:
            k_ref[:, 128 * j - nq:128 * (j + 1) - nq] = r.astype(k_ref.dtype)
    v_ref[...] = qkv[:, nq + nk:].astype(v_ref.dtype)


def _qkv(h, g, w, S, tm=512):
    M, D = h.shape
    nq = D
    nk = (w.shape[1] - nq) // 2
    ct, slo, shi = [jnp.asarray(a) for a in _rope_tables(S)]
    pb = S // tm
    tab = pl.BlockSpec((tm, 128), lambda i: (i % pb, 0))
    return pl.pallas_call(
        functools.partial(_qkv_kernel, nq=nq, nk=nk),
        grid=(M // tm,),
        in_specs=[pl.BlockSpec((tm, D), lambda i: (i, 0)),
                  pl.BlockSpec((1, D), lambda i: (0, 0)),
                  _resident((D, nq + 2 * nk), lambda i: (0, 0)),
                  tab, tab, tab],
        out_specs=[pl.BlockSpec((tm, nq), lambda i: (i, 0)),
                   pl.BlockSpec((tm, nk), lambda i: (i, 0)),
                   pl.BlockSpec((tm, nk), lambda i: (i, 0))],
        out_shape=[jax.ShapeDtypeStruct((M, nq), BF16),
                   jax.ShapeDtypeStruct((M, nk), BF16),
                   jax.ShapeDtypeStruct((M, nk), BF16)],
        scratch_shapes=[pltpu.VMEM(w.shape, BF16)],
        compiler_params=_cparams(1),
        name="qkv_rope",
    )(h, g.reshape(1, D), w, ct, slo, shi)


def _attn_kernel(sink_ref, q_ref, k_ref, v_ref, o_ref, krep_ref, vrep_ref, *, S, n_kv):
    BQ = WINDOW
    BK = 3 * WINDOW
    GW = KV_GROUP * HEAD_DIM
    dn = (((1,), (1,)), ((), ()))

    for g in range(n_kv):
        kg = k_ref[0, :, g * HEAD_DIM:(g + 1) * HEAD_DIM]
        vg = v_ref[0, :, g * HEAD_DIM:(g + 1) * HEAD_DIM]
        krep_ref[g] = jnp.concatenate([kg] * KV_GROUP, axis=1)
        vrep_ref[g] = jnp.concatenate([vg] * KV_GROUP, axis=1)

    lane = lax.broadcasted_iota(jnp.int32, (BQ, GW), 1)
    head_lanes = [(lane >= hh * HEAD_DIM) & (lane < (hh + 1) * HEAD_DIM) for hh in range(KV_GROUP)]
    rowblk = lax.broadcasted_iota(jnp.int32, (KV_GROUP * BQ, 1), 0) // BQ

    def body(qb, carry):
        qs = pl.multiple_of(qb * BQ, BQ)
        ks = pl.multiple_of(jnp.clip(qs - WINDOW, 0, S - BK), WINDOW)
        qpos = qs + lax.broadcasted_iota(jnp.int32, (BQ, BK), 0)
        kpos = ks + lax.broadcasted_iota(jnp.int32, (BQ, BK), 1)
        maskadd = jnp.where(jnp.abs(kpos - qpos) <= WINDOW, 0.0, NEG).astype(F32)
        maskadd = jnp.concatenate([maskadd] * KV_GROUP, axis=0)
        for g in range(n_kv):
            qg = q_ref[0, pl.ds(qs, BQ), g * GW:(g + 1) * GW]
            qst = jnp.concatenate([jnp.where(head_lanes[hh], qg, jnp.zeros_like(qg))
                                   for hh in range(KV_GROUP)], axis=0)
            sink = jnp.zeros((KV_GROUP * BQ, 1), F32)
            for hh in range(KV_GROUP):
                sink = jnp.where(rowblk == hh, sink_ref[g * KV_GROUP + hh] * LOG2E, sink)
            s = lax.dot_general(qst, krep_ref[g, pl.ds(ks, BK), :], dn,
                                preferred_element_type=F32) + maskadd
            m = jnp.maximum(jnp.max(s, axis=-1, keepdims=True), sink)
            p = jnp.exp2(s - m)
            denom = jnp.sum(p, axis=-1, keepdims=True) + jnp.exp2(sink - m)
            ost = _bdot(p.astype(BF16), vrep_ref[g, pl.ds(ks, BK), :]) / denom
            o = jnp.where(head_lanes[0], ost[0:BQ], 0.0)
            for hh in range(1, KV_GROUP):
                o = jnp.where(head_lanes[hh], ost[hh * BQ:(hh + 1) * BQ], o)
            o_ref[0, pl.ds(qs, BQ), g * GW:(g + 1) * GW] = o.astype(o_ref.dtype)
        return carry

    lax.fori_loop(0, S // BQ, body, 0, unroll=4)


def _attention(q, k, v, sink):
    B, S, NQ = q.shape
    NK = k.shape[2]
    n_kv = NK // HEAD_DIM
    return pl.pallas_call(
        functools.partial(_attn_kernel, S=S, n_kv=n_kv),
        grid=(B,),
        in_specs=[pl.BlockSpec(memory_space=pltpu.SMEM),
                  pl.BlockSpec((1, S, NQ), lambda b: (b, 0, 0)),
                  pl.BlockSpec((1, S, NK), lambda b: (b, 0, 0)),
                  pl.BlockSpec((1, S, NK), lambda b: (b, 0, 0))],
        out_specs=pl.BlockSpec((1, S, NQ), lambda b: (b, 0, 0)),
        out_shape=jax.ShapeDtypeStruct((B, S, NQ), BF16),
        scratch_shapes=[pltpu.VMEM((n_kv, S, KV_GROUP * HEAD_DIM), BF16),
                        pltpu.VMEM((n_kv, S, KV_GROUP * HEAD_DIM), BF16)],
        compiler_params=_cparams(1),
        name="window_attention",
    )(sink, q, k, v)


def kernel(x, norm_mix_pre, norm_mix_post, norm_mlp_pre, norm_mlp_post, w_up, w_down, hy_w_in, hy_conv_w, hy_conv_b, hy_f_w1, hy_f_b1, hy_f_w2, hy_f_b2, hy_f_w3, hy_f_b3, hy_f_freq, hy_f_wout, hy_bias, hy_w_out, at_w_qkv, at_sink, at_w_o):
    B, L, D = x.shape
    M = B * L
    P = CONV_BLOCKS
    n = L // P
    depth = norm_mix_pre.shape[0]
    fwd, inv, filt_c, filt_s = [jnp.asarray(a).astype(BF16) for a in _dft_tables(n)]

    h = x.reshape(M, D)
    for i in range(depth):
        j = i // 2
        if i % 2 == 0:
            hfilt = _hyena_filters(L, D, hy_f_w1[j], hy_f_b1[j], hy_f_w2[j], hy_f_b2[j],
                                   hy_f_w3[j], hy_f_b3[j], hy_f_freq[j], hy_f_wout[j])
            gc, gs = _filter_spectra(hfilt, hy_bias[j], filt_c, filt_s, P)
            hn = _prenorm(h, norm_mix_pre[i]).reshape(B, L, D)
            z = _hyena_core(hn, hy_w_in[j], hy_conv_w[j], hy_conv_b[j].reshape(1, -1),
                            gc, gs, fwd, inv, P)
            a, w_o = z.reshape(M, D), hy_w_out[j]
        else:
            q, k, v = _qkv(h, norm_mix_pre[i], at_w_qkv[j], L)
            o = _attention(q.reshape(B, L, -1), k.reshape(B, L, -1), v.reshape(B, L, -1), at_sink[j])
            a, w_o = o.reshape(M, -1), at_w_o[j]
        h = _tail(a, w_o, h, norm_mix_post[i], norm_mlp_pre[i], norm_mlp_post[i], w_up, w_down, i)
    return h.reshape(B, L, D)
```

```python
import functools
import math

import numpy as np
import jax
import jax.numpy as jnp
from jax import lax
from jax.experimental import pallas as pl
from jax.experimental.pallas import tpu as pltpu

BF16 = jnp.bfloat16
F32 = jnp.float32

EPS = 1e-6
NEG = -1e30
LOG2E = math.log2(math.e)

HY_ORDER = 2
HY_EMB = 33
HY_BANDS = (HY_EMB - 1) // 2
HY_DECAY_TARGET = 1e-2
HY_FAST_DECAY = 0.3
HY_SLOW_DECAY = 1.5
HEAD_DIM = 64
KV_GROUP = 4
WINDOW = 128
ROT_DIM = HEAD_DIM // 4
ROPE_THETA = 500000.0

CONV_BLOCKS = 4
CH_BLOCK = 256
SPEC_DTYPE = BF16
STAGE_SLOTS = 4
Q_UNROLL = 4

VMEM_LIMIT = 56 * 1024 * 1024


def _cparams(n_axes):
    return pltpu.CompilerParams(
        dimension_semantics=("arbitrary",) * n_axes,
        vmem_limit_bytes=VMEM_LIMIT)


def _resident(shape, index_map):
    return pl.BlockSpec(shape, index_map, pipeline_mode=pl.Buffered(1))


def _rms(x, g):
    ms = jnp.mean(x * x, axis=-1, keepdims=True)
    return x * lax.rsqrt(ms + EPS) * g


def _bdot(a, b):
    return jnp.dot(a, b, preferred_element_type=F32)


def _dft_tables(n):
    f = np.arange(n, dtype=np.int64)[:, None]
    r = np.arange(n, dtype=np.int64)[None, :]
    k_pos = ((2 * f + 1) * r) % (4 * n)
    k_neg = ((2 * f + 1) * (r - n)) % (4 * n)
    ang_pos = np.pi * k_pos / (2 * n)
    ang_neg = np.pi * k_neg / (2 * n)
    cpos, spos = np.cos(ang_pos), np.sin(ang_pos)
    cneg, sneg = np.cos(ang_neg), np.sin(ang_neg)
    cneg[:, 0] = 0.0
    sneg[:, 0] = 0.0
    fwd = np.concatenate([cpos, spos], axis=0)
    inv = np.concatenate([cpos.T, spos.T], axis=1) / n
    filt_c = np.concatenate([cneg, cpos], axis=1)
    filt_s = np.concatenate([sneg, spos], axis=1)
    return [np.asarray(a, np.float32) for a in (fwd, inv, filt_c, filt_s)]


def _filter_positions(L):
    t = np.linspace(0.0, 1.0, L)[:, None]
    w = (2.0 * np.pi / L) * np.arange(L)[:, None]
    f = np.linspace(1e-4, HY_BANDS - 1, HY_BANDS)[None, :]
    z = np.concatenate([t, np.cos(f * w), -np.sin(f * w)], axis=-1)
    return np.asarray(z, np.float32)


def _rope_tables(S):
    half = ROT_DIM // 2
    inv = ROPE_THETA ** (-np.arange(0, ROT_DIM, 2, dtype=np.float64) / ROT_DIM)
    ang = np.arange(S, dtype=np.float64)[:, None] * inv[None, :]
    lane = np.arange(128) % HEAD_DIM
    ang_l = ang[:, lane % half]
    cos_t = np.where(lane[None, :] < ROT_DIM, np.cos(ang_l), 1.0)
    sin_lo = np.where(lane[None, :] < half, -np.sin(ang_l), 0.0)
    sin_hi = np.where((lane[None, :] >= half) & (lane[None, :] < ROT_DIM), np.sin(ang_l), 0.0)
    return [np.asarray(a, np.float32) for a in (cos_t, sin_lo, sin_hi)]


def _prenorm_kernel(x_ref, g_ref, o_ref):
    o_ref[...] = _rms(x_ref[...], g_ref[...]).astype(o_ref.dtype)


def _prenorm(x2d, g, tm=1024):
    M, D = x2d.shape
    return pl.pallas_call(
        _prenorm_kernel,
        grid=(M // tm,),
        in_specs=[pl.BlockSpec((tm, D), lambda i: (i, 0)),
                  pl.BlockSpec((1, D), lambda i: (0, 0))],
        out_specs=pl.BlockSpec((tm, D), lambda i: (i, 0)),
        out_shape=jax.ShapeDtypeStruct((M, D), BF16),
        compiler_params=_cparams(1),
        name="prenorm",
    )(x2d, g.reshape(1, D))


def _filter_kernel(z_ref, w1_ref, b1_ref, w2_ref, b2_ref, w3_ref, b3_ref, fr_ref,
                   wout_ref, dl_ref, o_ref, hid_ref, *, n_fwd_blocks):
    j = pl.program_id(0)
    hi = lax.Precision.HIGHEST

    @pl.when(j == 0)
    def _():
        fr = fr_ref[...]
        h = jnp.sin(fr * (jnp.dot(z_ref[...], w1_ref[...], precision=hi) + b1_ref[...]))
        h = jnp.sin(fr * (jnp.dot(h, w2_ref[...], precision=hi) + b2_ref[...]))
        h = jnp.sin(fr * (jnp.dot(h, w3_ref[...], precision=hi) + b3_ref[...]))
        hid_ref[...] = h

    L, W = o_ref.shape
    row = lax.broadcasted_iota(jnp.int32, (L, W), 0)
    t = row.astype(F32) * (1.0 / (L - 1))
    decay = jnp.exp(-t * dl_ref[...])
    hid = hid_ref[...]
    wout = wout_ref[...]
    hid_hi = hid.astype(BF16)
    hid_lo = (hid - hid_hi.astype(F32)).astype(BF16)
    w_hi = wout.astype(BF16)
    w_lo = (wout - w_hi.astype(F32)).astype(BF16)
    val = (_bdot(hid_hi, w_hi) + (_bdot(hid_hi, w_lo) + _bdot(hid_lo, w_hi))) * decay
    keep = jnp.logical_or(row > 0, j < n_fwd_blocks)
    o_ref[...] = jnp.where(keep, val, 0.0).astype(o_ref.dtype)


def _hyena_filters(L, D, f_w1, f_b1, f_w2, f_b2, f_w3, f_b3, f_freq, f_wout):
    FW = f_w1.shape[1]
    z = jnp.asarray(np.pad(_filter_positions(L), ((0, 0), (0, FW - HY_EMB))))
    w1 = jnp.pad(f_w1, ((0, FW - HY_EMB), (0, 0)))
    max_decay = math.log(HY_DECAY_TARGET) / HY_FAST_DECAY
    min_decay = math.log(HY_DECAY_TARGET) / HY_SLOW_DECAY
    absdelta = jnp.asarray(np.abs(np.linspace(min_decay, max_decay, D)).astype(np.float32))[None, :]
    n_blocks = f_wout.shape[1] // D
    small = lambda shape: pl.BlockSpec(shape, lambda j: (0, 0))
    return pl.pallas_call(
        functools.partial(_filter_kernel, n_fwd_blocks=n_blocks // 2),
        grid=(n_blocks,),
        in_specs=[small((L, FW)), small((FW, FW)), small((1, FW)), small((FW, FW)), small((1, FW)),
                  small((FW, FW)), small((1, FW)), small((1, FW)),
                  pl.BlockSpec((FW, D), lambda j: (0, j)),
                  small((1, D))],
        out_specs=pl.BlockSpec((L, D), lambda j: (0, j)),
        out_shape=jax.ShapeDtypeStruct((L, n_blocks * D), BF16),
        scratch_shapes=[pltpu.VMEM((L, FW), F32)],
        compiler_params=_cparams(1),
        name="hyena_filter",
    )(z, w1, f_b1.reshape(1, FW), f_w2, f_b2.reshape(1, FW), f_w3, f_b3.reshape(1, FW),
      f_freq.reshape(1, FW), f_wout, absdelta)


def _spectra_kernel(hf_ref, hb_ref, bias_ref, mc_ref, ms_ref, gc_ref, gs_ref, *, P, n):
    mc = mc_ref[...]
    ms = ms_ref[...]
    cpos = mc[:, n:]
    spos = ms[:, n:]
    for d in range(-(P - 1), P):
        if d == 0:
            hf0 = hf_ref[0:n, :].astype(F32)
            hb0 = hb_ref[0:n, :].astype(F32)
            gc = _bdot(cpos, (hf0 + hb0).astype(BF16)) + bias_ref[...]
            gs = _bdot(spos, (hf0 - hb0).astype(BF16))
        elif d > 0:
            taps = hf_ref[n * (d - 1):n * (d + 1), :].astype(BF16)
            gc = _bdot(mc, taps)
            gs = _bdot(ms, taps)
        else:
            a = -d
            taps = hb_ref[n * (a - 1):n * (a + 1), :].astype(BF16)
            gc = _bdot(mc, taps)
            gs = -_bdot(ms, taps)
        gc_ref[d + P - 1] = gc.astype(gc_ref.dtype)
        gs_ref[d + P - 1] = gs.astype(gs_ref.dtype)


def _filter_spectra(hfilt, bias, mc, ms, P, cw=256):
    L, W = hfilt.shape
    n = L // P
    half = W // 2
    nblk = half // cw
    out_sd = jax.ShapeDtypeStruct((2 * P - 1, n, half), SPEC_DTYPE)
    return pl.pallas_call(
        functools.partial(_spectra_kernel, P=P, n=n),
        grid=(nblk,),
        in_specs=[pl.BlockSpec((L, cw), lambda j: (0, j)),
                  pl.BlockSpec((L, cw), lambda j: (0, nblk + j)),
                  pl.BlockSpec((1, cw), lambda j: (0, j)),
                  pl.BlockSpec((n, 2 * n), lambda j: (0, 0)),
                  pl.BlockSpec((n, 2 * n), lambda j: (0, 0))],
        out_specs=[pl.BlockSpec((2 * P - 1, n, cw), lambda j: (0, 0, j)),
                   pl.BlockSpec((2 * P - 1, n, cw), lambda j: (0, 0, j))],
        out_shape=[out_sd, out_sd],
        compiler_params=_cparams(1),
        name="filter_spectra",
    )(hfilt, hfilt, bias.reshape(1, half), mc, ms)


def _hyena_kernel(hn_ref, wv_ref, w1_ref, w2_ref, cwv_ref, cw1_ref, cw2_ref,
                  cbv_ref, cb1_ref, cb2_ref, gc0_ref, gs0_ref, gc1_ref, gs1_ref,
                  fwd_ref, inv_ref, z_ref,
                  rawa_ref, rawb_ref, v_ref, gatea_ref, gateb_ref, uspec_ref, yspec_ref,
                  *, P, n, fchunk):
    row = lax.broadcasted_iota(jnp.int32, (n, 1), 0)
    blk = [slice(n * i, n * (i + 1)) for i in range(P)]
    chunks = n // fchunk
    fwd = fwd_ref[...]
    inv = inv_ref[...]

    def project(w_ref, dst_ref, i):
        dst_ref[blk[i], :] = _bdot(hn_ref[0, blk[i], :], w_ref[...].astype(BF16))

    def short_conv(raw_ref, cw_ref, cb_ref, dst, i):
        cw = cw_ref[...]
        cur = raw_ref[blk[i], :]
        prev = pltpu.roll(cur, 1, axis=0)
        nxt = pltpu.roll(cur, n - 1, axis=0)
        first = raw_ref[n * i - 1:n * i, :] if i > 0 else jnp.zeros_like(cur[0:1])
        last = raw_ref[n * (i + 1):n * (i + 1) + 1, :] if i < P - 1 else jnp.zeros_like(cur[0:1])
        prev = jnp.where(row == 0, first, prev)
        nxt = jnp.where(row == n - 1, last, nxt)
        out = prev * cw[0:1] + cur * cw[1:2] + nxt * cw[2:3] + cb_ref[...]
        dst[blk[i], :] = out.astype(dst.dtype)

    def forward(src_ref, j):
        uspec_ref[j] = _bdot(fwd, src_ref[blk[j], :]).astype(uspec_ref.dtype)

    def combine(gc_ref, gs_ref, c):
        fc = slice(c * fchunk, (c + 1) * fchunk)
        fs = slice(n + c * fchunk, n + (c + 1) * fchunk)
        uc = [uspec_ref[j, fc, :] for j in range(P)]
        us = [uspec_ref[j, fs, :] for j in range(P)]
        for i in range(P):
            yc = None
            ys = None
            for j in range(P):
                gc = gc_ref[i - j + P - 1, fc, :]
                gs = gs_ref[i - j + P - 1, fc, :]
                tc = gc * uc[j] - gs * us[j]
                ts = gc * us[j] + gs * uc[j]
                yc = tc if yc is None else yc + tc
                ys = ts if ys is None else ys + ts
            yspec_ref[i, fc, :] = yc.astype(BF16)
            yspec_ref[i, fs, :] = ys.astype(BF16)

    for i in range(P):
        project(wv_ref, rawa_ref, i)
    for i in range(P):
        short_conv(rawa_ref, cwv_ref, cbv_ref, v_ref, i)
        project(w1_ref, rawb_ref, i)
    for i in range(P):
        forward(v_ref, i)
        short_conv(rawb_ref, cw1_ref, cb1_ref, gatea_ref, i)
    for c in range(chunks):
        combine(gc0_ref, gs0_ref, c)
        if c % (chunks // P) == 0:
            project(w2_ref, rawa_ref, c // (chunks // P))
    for i in range(P):
        short_conv(rawa_ref, cw2_ref, cb2_ref, gateb_ref, i)
        v_ref[blk[i], :] = (gatea_ref[blk[i], :] * _bdot(inv, yspec_ref[i])).astype(v_ref.dtype)
    for i in range(P):
        forward(v_ref, i)
    for c in range(chunks):
        combine(gc1_ref, gs1_ref, c)
    for i in range(P):
        z_ref[0, blk[i], :] = (gateb_ref[blk[i], :] * _bdot(inv, yspec_ref[i])).astype(z_ref.dtype)


def _hyena_core(hn, w_in, conv_w, conv_b, gc, gs, fwd, inv, P):
    B, L, D = hn.shape
    n = L // P
    cb_n = D // CH_BLOCK
    C = CH_BLOCK
    nd = 2 * P - 1

    def stream_spec(shape, s):
        return pl.BlockSpec(shape, lambda c, b: (0, s * cb_n + c))

    in_specs = (
        [pl.BlockSpec((1, L, D), lambda c, b: (b, 0, 0))]
        + [stream_spec((D, C), s) for s in range(3)]
        + [stream_spec((3, C), s) for s in range(3)]
        + [stream_spec((1, C), s) for s in range(3)]
        + [_resident((nd, n, C), lambda c, b: (0, 0, c)),
           _resident((nd, n, C), lambda c, b: (0, 0, c)),
           _resident((nd, n, C), lambda c, b: (0, 0, cb_n + c)),
           _resident((nd, n, C), lambda c, b: (0, 0, cb_n + c)),
           pl.BlockSpec((2 * n, n), lambda c, b: (0, 0)),
           pl.BlockSpec((n, 2 * n), lambda c, b: (0, 0))])
    return pl.pallas_call(
        functools.partial(_hyena_kernel, P=P, n=n, fchunk=16),
        grid=(cb_n, B),
        in_specs=in_specs,
        out_specs=pl.BlockSpec((1, L, C), lambda c, b: (b, 0, c)),
        out_shape=jax.ShapeDtypeStruct((B, L, D), BF16),
        scratch_shapes=[pltpu.VMEM((L, C), F32),
                        pltpu.VMEM((L, C), F32),
                        pltpu.VMEM((L, C), BF16),
                        pltpu.VMEM((L, C), F32),
                        pltpu.VMEM((L, C), F32),
                        pltpu.VMEM((P, 2 * n, C), SPEC_DTYPE),
                        pltpu.VMEM((P, 2 * n, C), BF16)],
        compiler_params=_cparams(2),
        name="hyena_core",
    )(hn, w_in, w_in, w_in, conv_w, conv_w, conv_w, conv_b, conv_b, conv_b,
      gc, gs, gc, gs, fwd, inv)


def _stream_cast(src_hbm, layer, dst_ref, stage_ref, sem_ref):
    slots, rows = stage_ref.shape[0], stage_ref.shape[1]
    n_chunks = dst_ref.shape[0] // rows

    def copy(c):
        return pltpu.make_async_copy(src_hbm.at[layer, pl.ds(c * rows, rows), :],
                                     stage_ref.at[c % slots], sem_ref.at[c % slots])

    for c in range(min(slots - 1, n_chunks)):
        copy(c).start()
    for c in range(n_chunks):
        ahead = c + slots - 1
        if ahead < n_chunks:
            copy(ahead).start()
        copy(c).wait()
        dst_ref[c * rows:(c + 1) * rows, :] = stage_ref[c % slots].astype(dst_ref.dtype)


def _rope_store(qkv, ct, slo, shi, q_ref, k_ref, v_ref):
    nq, nk = q_ref.shape[1], k_ref.shape[1]
    half = ROT_DIM // 2
    scale = LOG2E * HEAD_DIM ** -0.5
    for j in range((nq + nk) // 128):
        t = qkv[:, 128 * j:128 * (j + 1)]
        r = t * ct + pltpu.roll(t, 128 - half, axis=1) * slo + pltpu.roll(t, half, axis=1) * shi
        if 128 * j < nq:
            q_ref[:, 128 * j:128 * (j + 1)] = (r * scale).astype(q_ref.dtype)
        else:
            k_ref[:, 128 * j - nq:128 * (j + 1) - nq] = r.astype(k_ref.dtype)
    v_ref[...] = qkv[:, nq + nk:].astype(v_ref.dtype)


def _tail_kernel(*refs, fchunk, layers, with_qkv):
    (a_ref, r_ref, gmix_ref, gpre_ref, gpost_ref, wo_hbm, wu_hbm, wd_hbm), refs = refs[:8], refs[8:]
    if with_qkv:
        (gq_ref, ct_ref, slo_ref, shi_ref, wq_hbm), refs = refs[:5], refs[5:]
        (o_ref, q_ref, k_ref, v_ref), refs = refs[:4], refs[4:]
    else:
        o_ref, refs = refs[0], refs[1:]
    (wo_ref, wu_ref, wd_ref, stage_u, stage_d, sem_u, sem_d), refs = refs[:7], refs[7:]
    wq_ref, stage_q, sem_q = refs if with_qkv else (None, None, None)

    @pl.when(pl.program_id(0) == 0)
    def _():
        _stream_cast(wo_hbm, layers[0], wo_ref, stage_d, sem_d)
        _stream_cast(wu_hbm, layers[1], wu_ref, stage_u, sem_u)
        _stream_cast(wd_hbm, layers[1], wd_ref, stage_d, sem_d)
        if with_qkv:
            _stream_cast(wq_hbm, layers[2], wq_ref, stage_q, sem_q)

    m = _bdot(a_ref[...], wo_ref[...])
    h = r_ref[...] + _rms(m, gmix_ref[...])
    hn = _rms(h, gpre_ref[...]).astype(BF16)
    FF = wu_ref.shape[1]
    acc = None
    for c in range(FF // fchunk):
        a = _bdot(hn, wu_ref[:, c * fchunk:(c + 1) * fchunk])
        a = jnp.maximum(a, 0.0)
        a = (a * a).astype(BF16)
        part = _bdot(a, wd_ref[c * fchunk:(c + 1) * fchunk, :])
        acc = part if acc is None else acc + part
    out = h + _rms(acc, gpost_ref[...])
    o_ref[...] = out
    if with_qkv:
        qkv = _bdot(_rms(out, gq_ref[...]).astype(BF16), wq_ref[...])
        _rope_store(qkv, ct_ref[...], slo_ref[...], shi_ref[...], q_ref, k_ref, v_ref)


def _tail(a, w_o_all, resid, g_mix, g_pre, g_post, w_up_all, w_down_all, layers,
          qkv_next=None, tm=512, fchunk=1024):
    M, K = a.shape
    D = w_o_all.shape[2]
    FF = w_up_all.shape[2]
    slot_bytes = 512 * 1024
    row = lambda shape: pl.BlockSpec(shape, lambda i: (i, 0))
    gain = pl.BlockSpec((1, D), lambda i: (0, 0))
    hbm = pl.BlockSpec(memory_space=pl.ANY)
    in_specs = [row((tm, K)), row((tm, D)), gain, gain, gain, hbm, hbm, hbm]
    args = [a, resid, g_mix.reshape(1, D), g_pre.reshape(1, D), g_post.reshape(1, D),
            w_o_all, w_up_all, w_down_all]
    out_specs = [row((tm, D))]
    out_shape = [jax.ShapeDtypeStruct((M, D), F32)]
    scratch = [pltpu.VMEM((K, D), BF16),
               pltpu.VMEM((D, FF), BF16),
               pltpu.VMEM((FF, D), BF16),
               pltpu.VMEM((STAGE_SLOTS, slot_bytes // (FF * 4), FF), F32),
               pltpu.VMEM((STAGE_SLOTS, slot_bytes // (D * 4), D), F32),
               pltpu.SemaphoreType.DMA((STAGE_SLOTS,)),
               pltpu.SemaphoreType.DMA((STAGE_SLOTS,))]
    if qkv_next is not None:
        g_q, w_qkv_all, S = qkv_next
        NQKV = w_qkv_all.shape[2]
        nq = D
        nk = (NQKV - nq) // 2
        pb = S // tm
        tab = pl.BlockSpec((tm, 128), lambda i: (i % pb, 0))
        in_specs += [gain, tab, tab, tab, hbm]
        args += [g_q.reshape(1, D)] + [jnp.asarray(t) for t in _rope_tables(S)] + [w_qkv_all]
        out_specs += [row((tm, nq)), row((tm, nk)), row((tm, nk))]
        out_shape += [jax.ShapeDtypeStruct((M, nq), BF16),
                      jax.ShapeDtypeStruct((M, nk), BF16),
                      jax.ShapeDtypeStruct((M, nk), BF16)]
        scratch += [pltpu.VMEM((D, NQKV), BF16),
                    pltpu.VMEM((STAGE_SLOTS, 64, NQKV), F32),
                    pltpu.SemaphoreType.DMA((STAGE_SLOTS,))]
    return pl.pallas_call(
        functools.partial(_tail_kernel, fchunk=fchunk, layers=layers, with_qkv=qkv_next is not None),
        grid=(M // tm,),
        in_specs=in_specs,
        out_specs=out_specs,
        out_shape=out_shape,
        scratch_shapes=scratch,
        compiler_params=_cparams(1),
        name="mixer_tail_mlp",
    )(*args)


def _attn_kernel(sink_ref, q_ref, k_ref, v_ref, o_ref, krep_ref, vrep_ref, *, S, n_kv):
    BQ = WINDOW
    BK = 3 * WINDOW
    GW = KV_GROUP * HEAD_DIM
    dn = (((1,), (1,)), ((), ()))

    for g in range(n_kv):
        kg = k_ref[0, :, g * HEAD_DIM:(g + 1) * HEAD_DIM]
        vg = v_ref[0, :, g * HEAD_DIM:(g + 1) * HEAD_DIM]
        krep_ref[g] = jnp.concatenate([kg] * KV_GROUP, axis=1)
        vrep_ref[g] = jnp.concatenate([vg] * KV_GROUP, axis=1)

    lane = lax.broadcasted_iota(jnp.int32, (BQ, GW), 1)
    head_lanes = [(lane >= hh * HEAD_DIM) & (lane < (hh + 1) * HEAD_DIM) for hh in range(KV_GROUP)]
    rowblk = lax.broadcasted_iota(jnp.int32, (KV_GROUP * BQ, 1), 0) // BQ

    sinks = []
    for g in range(n_kv):
        sink = jnp.zeros((KV_GROUP * BQ, 1), F32)
        for hh in range(KV_GROUP):
            sink = jnp.where(rowblk == hh, sink_ref[g * KV_GROUP + hh] * LOG2E, sink)
        sinks.append(sink)

    def scores(qs, ks, maskadd, g):
        qg = q_ref[0, pl.ds(qs, BQ), g * GW:(g + 1) * GW]
        qst = jnp.concatenate([jnp.where(head_lanes[hh], qg, jnp.zeros_like(qg))
                               for hh in range(KV_GROUP)], axis=0)
        return lax.dot_general(qst, krep_ref[g, pl.ds(ks, BK), :], dn,
                               preferred_element_type=F32) + maskadd

    def finish(qs, ks, g, s):
        sink = sinks[g]
        m = jnp.maximum(jnp.max(s, axis=-1, keepdims=True), sink)
        p = jnp.exp2(s - m)
        denom = jnp.sum(p, axis=-1, keepdims=True) + jnp.exp2(sink - m)
        ost = _bdot(p.astype(BF16), vrep_ref[g, pl.ds(ks, BK), :]) / denom
        o = jnp.where(head_lanes[0], ost[0:BQ], 0.0)
        for hh in range(1, KV_GROUP):
            o = jnp.where(head_lanes[hh], ost[hh * BQ:(hh + 1) * BQ], o)
        o_ref[0, pl.ds(qs, BQ), g * GW:(g + 1) * GW] = o.astype(o_ref.dtype)

    def body(t, carry):
        items = []
        for u in range(Q_UNROLL):
            qs = pl.multiple_of((t * Q_UNROLL + u) * BQ, BQ)
            ks = pl.multiple_of(jnp.clip(qs - WINDOW, 0, S - BK), WINDOW)
            qpos = qs + lax.broadcasted_iota(jnp.int32, (BQ, BK), 0)
            kpos = ks + lax.broadcasted_iota(jnp.int32, (BQ, BK), 1)
            maskadd = jnp.where(jnp.abs(kpos - qpos) <= WINDOW, 0.0, NEG).astype(F32)
            maskadd = jnp.concatenate([maskadd] * KV_GROUP, axis=0)
            items += [(qs, ks, maskadd, g) for g in range(n_kv)]
        pending = None
        for qs, ks, maskadd, g in items:
            s = scores(qs, ks, maskadd, g)
            if pending is not None:
                finish(*pending)
            pending = (qs, ks, g, s)
        finish(*pending)
        return carry

    lax.fori_loop(0, S // (BQ * Q_UNROLL), body, 0)


def _attention(q, k, v, sink):
    B, S, NQ = q.shape
    NK = k.shape[2]
    n_kv = NK // HEAD_DIM
    return pl.pallas_call(
        functools.partial(_attn_kernel, S=S, n_kv=n_kv),
        grid=(B,),
        in_specs=[pl.BlockSpec(memory_space=pltpu.SMEM),
                  pl.BlockSpec((1, S, NQ), lambda b: (b, 0, 0)),
                  pl.BlockSpec((1, S, NK), lambda b: (b, 0, 0)),
                  pl.BlockSpec((1, S, NK), lambda b: (b, 0, 0))],
        out_specs=pl.BlockSpec((1, S, NQ), lambda b: (b, 0, 0)),
        out_shape=jax.ShapeDtypeStruct((B, S, NQ), BF16),
        scratch_shapes=[pltpu.VMEM((n_kv, S, KV_GROUP * HEAD_DIM), BF16),
                        pltpu.VMEM((n_kv, S, KV_GROUP * HEAD_DIM), BF16)],
        compiler_params=_cparams(1),
        name="window_attention",
    )(sink, q, k, v)


def kernel(x, norm_mix_pre, norm_mix_post, norm_mlp_pre, norm_mlp_post, w_up, w_down, hy_w_in, hy_conv_w, hy_conv_b, hy_f_w1, hy_f_b1, hy_f_w2, hy_f_b2, hy_f_w3, hy_f_b3, hy_f_freq, hy_f_wout, hy_bias, hy_w_out, at_w_qkv, at_sink, at_w_o):
    B, L, D = x.shape
    M = B * L
    P = CONV_BLOCKS
    n = L // P
    depth = norm_mix_pre.shape[0]
    fwd, inv, filt_c, filt_s = [jnp.asarray(a).astype(BF16) for a in _dft_tables(n)]

    h = x.reshape(M, D)
    qkv = None
    for i in range(depth):
        j = i // 2
        if i % 2 == 0:
            hfilt = _hyena_filters(L, D, hy_f_w1[j], hy_f_b1[j], hy_f_w2[j], hy_f_b2[j],
                                   hy_f_w3[j], hy_f_b3[j], hy_f_freq[j], hy_f_wout[j])
            gc, gs = _filter_spectra(hfilt, hy_bias[j], filt_c, filt_s, P)
            hn = _prenorm(h, norm_mix_pre[i]).reshape(B, L, D)
            z = _hyena_core(hn, hy_w_in[j], hy_conv_w[j], hy_conv_b[j].reshape(1, -1),
                            gc, gs, fwd, inv, P)
            a, w_o_all = z.reshape(M, D), hy_w_out
        else:
            q, k, v = qkv
            o = _attention(q.reshape(B, L, -1), k.reshape(B, L, -1), v.reshape(B, L, -1), at_sink[j])
            a, w_o_all = o.reshape(M, -1), at_w_o
        next_is_attention = i + 1 < depth and (i + 1) % 2 == 1
        qkv_next = (norm_mix_pre[i + 1], at_w_qkv, L) if next_is_attention else None
        outs = _tail(a, w_o_all, h, norm_mix_post[i], norm_mlp_pre[i], norm_mlp_post[i],
                     w_up, w_down, (j, i, (i + 1) // 2), qkv_next=qkv_next)
        h, qkv = outs[0], outs[1:]
    return h.reshape(B, L, D)
```

```python
import functools
import math

import numpy as np
import jax
import jax.numpy as jnp
from jax import lax
from jax.experimental import pallas as pl
from jax.experimental.pallas import tpu as pltpu

BF16 = jnp.bfloat16
F32 = jnp.float32

EPS = 1e-6
NEG = -1e30
LOG2E = math.log2(math.e)

HY_ORDER = 2
HY_EMB = 33
HY_BANDS = (HY_EMB - 1) // 2
HY_DECAY_TARGET = 1e-2
HY_FAST_DECAY = 0.3
HY_SLOW_DECAY = 1.5
HEAD_DIM = 64
KV_GROUP = 4
WINDOW = 128
ROT_DIM = HEAD_DIM // 4
ROPE_THETA = 500000.0

CONV_BLOCKS = 4
CH_BLOCK = 256
SPEC_DTYPE = BF16
STAGE_SLOTS = 4

VMEM_LIMIT = 56 * 1024 * 1024


def _cparams(n_axes):
    return pltpu.CompilerParams(
        dimension_semantics=("arbitrary",) * n_axes,
        vmem_limit_bytes=VMEM_LIMIT)


def _resident(shape, index_map):
    return pl.BlockSpec(shape, index_map, pipeline_mode=pl.Buffered(1))


def _rms(x, g):
    ms = jnp.mean(x * x, axis=-1, keepdims=True)
    return x * lax.rsqrt(ms + EPS) * g


def _bdot(a, b):
    return jnp.dot(a, b, preferred_element_type=F32)


def _dft_tables(n):
    f = np.arange(n, dtype=np.int64)[:, None]
    r = np.arange(n, dtype=np.int64)[None, :]
    k_pos = ((2 * f + 1) * r) % (4 * n)
    k_neg = ((2 * f + 1) * (r - n)) % (4 * n)
    ang_pos = np.pi * k_pos / (2 * n)
    ang_neg = np.pi * k_neg / (2 * n)
    cpos, spos = np.cos(ang_pos), np.sin(ang_pos)
    cneg, sneg = np.cos(ang_neg), np.sin(ang_neg)
    cneg[:, 0] = 0.0
    sneg[:, 0] = 0.0
    fwd = np.concatenate([cpos, spos], axis=0)
    inv = np.concatenate([cpos.T, spos.T], axis=1) / n
    filt_c = np.concatenate([cneg, cpos], axis=1)
    filt_s = np.concatenate([sneg, spos], axis=1)
    return [np.asarray(a, np.float32) for a in (fwd, inv, filt_c, filt_s)]


def _filter_positions(L):
    t = np.linspace(0.0, 1.0, L)[:, None]
    w = (2.0 * np.pi / L) * np.arange(L)[:, None]
    f = np.linspace(1e-4, HY_BANDS - 1, HY_BANDS)[None, :]
    z = np.concatenate([t, np.cos(f * w), -np.sin(f * w)], axis=-1)
    return np.asarray(z, np.float32)


def _rope_tables(S):
    half = ROT_DIM // 2
    inv = ROPE_THETA ** (-np.arange(0, ROT_DIM, 2, dtype=np.float64) / ROT_DIM)
    ang = np.arange(S, dtype=np.float64)[:, None] * inv[None, :]
    lane = np.arange(128) % HEAD_DIM
    ang_l = ang[:, lane % half]
    cos_t = np.where(lane[None, :] < ROT_DIM, np.cos(ang_l), 1.0)
    sin_lo = np.where(lane[None, :] < half, -np.sin(ang_l), 0.0)
    sin_hi = np.where((lane[None, :] >= half) & (lane[None, :] < ROT_DIM), np.sin(ang_l), 0.0)
    return [np.asarray(a, np.float32) for a in (cos_t, sin_lo, sin_hi)]


def _prenorm_kernel(x_ref, g_ref, o_ref):
    o_ref[...] = _rms(x_ref[...], g_ref[...]).astype(o_ref.dtype)


def _prenorm(x2d, g, tm=1024):
    M, D = x2d.shape
    return pl.pallas_call(
        _prenorm_kernel,
        grid=(M // tm,),
        in_specs=[pl.BlockSpec((tm, D), lambda i: (i, 0)),
                  pl.BlockSpec((1, D), lambda i: (0, 0))],
        out_specs=pl.BlockSpec((tm, D), lambda i: (i, 0)),
        out_shape=jax.ShapeDtypeStruct((M, D), BF16),
        compiler_params=_cparams(1),
        name="prenorm",
    )(x2d, g.reshape(1, D))


def _filter_kernel(z_ref, w1_ref, b1_ref, w2_ref, b2_ref, w3_ref, b3_ref, fr_ref,
                   wout_ref, dl_ref, o_ref, hid_ref, *, n_fwd_blocks):
    j = pl.program_id(0)
    hi = lax.Precision.HIGHEST

    @pl.when(j == 0)
    def _():
        fr = fr_ref[...]
        h = jnp.sin(fr * (jnp.dot(z_ref[...], w1_ref[...], precision=hi) + b1_ref[...]))
        h = jnp.sin(fr * (jnp.dot(h, w2_ref[...], precision=hi) + b2_ref[...]))
        h = jnp.sin(fr * (jnp.dot(h, w3_ref[...], precision=hi) + b3_ref[...]))
        hid_ref[...] = h

    L, W = o_ref.shape
    row = lax.broadcasted_iota(jnp.int32, (L, W), 0)
    t = row.astype(F32) * (1.0 / (L - 1))
    decay = jnp.exp(-t * dl_ref[...])
    hid = hid_ref[...]
    wout = wout_ref[...]
    hid_hi = hid.astype(BF16)
    hid_lo = (hid - hid_hi.astype(F32)).astype(BF16)
    w_hi = wout.astype(BF16)
    w_lo = (wout - w_hi.astype(F32)).astype(BF16)
    val = (_bdot(hid_hi, w_hi) + (_bdot(hid_hi, w_lo) + _bdot(hid_lo, w_hi))) * decay
    keep = jnp.logical_or(row > 0, j < n_fwd_blocks)
    o_ref[...] = jnp.where(keep, val, 0.0).astype(o_ref.dtype)


def _hyena_filters(L, D, f_w1, f_b1, f_w2, f_b2, f_w3, f_b3, f_freq, f_wout):
    FW = f_w1.shape[1]
    z = jnp.asarray(np.pad(_filter_positions(L), ((0, 0), (0, FW - HY_EMB))))
    w1 = jnp.pad(f_w1, ((0, FW - HY_EMB), (0, 0)))
    max_decay = math.log(HY_DECAY_TARGET) / HY_FAST_DECAY
    min_decay = math.log(HY_DECAY_TARGET) / HY_SLOW_DECAY
    absdelta = jnp.asarray(np.abs(np.linspace(min_decay, max_decay, D)).astype(np.float32))[None, :]
    n_blocks = f_wout.shape[1] // D
    small = lambda shape: pl.BlockSpec(shape, lambda j: (0, 0))
    return pl.pallas_call(
        functools.partial(_filter_kernel, n_fwd_blocks=n_blocks // 2),
        grid=(n_blocks,),
        in_specs=[small((L, FW)), small((FW, FW)), small((1, FW)), small((FW, FW)), small((1, FW)),
                  small((FW, FW)), small((1, FW)), small((1, FW)),
                  pl.BlockSpec((FW, D), lambda j: (0, j)),
                  small((1, D))],
        out_specs=pl.BlockSpec((L, D), lambda j: (0, j)),
        out_shape=jax.ShapeDtypeStruct((L, n_blocks * D), BF16),
        scratch_shapes=[pltpu.VMEM((L, FW), F32)],
        compiler_params=_cparams(1),
        name="hyena_filter",
    )(z, w1, f_b1.reshape(1, FW), f_w2, f_b2.reshape(1, FW), f_w3, f_b3.reshape(1, FW),
      f_freq.reshape(1, FW), f_wout, absdelta)


def _spectra_kernel(hf_ref, hb_ref, bias_ref, mc_ref, ms_ref, gc_ref, gs_ref, *, P, n):
    mc = mc_ref[...]
    ms = ms_ref[...]
    cpos = mc[:, n:]
    spos = ms[:, n:]
    for d in range(-(P - 1), P):
        if d == 0:
            hf0 = hf_ref[0:n, :].astype(F32)
            hb0 = hb_ref[0:n, :].astype(F32)
            gc = _bdot(cpos, (hf0 + hb0).astype(BF16)) + bias_ref[...]
            gs = _bdot(spos, (hf0 - hb0).astype(BF16))
        elif d > 0:
            taps = hf_ref[n * (d - 1):n * (d + 1), :].astype(BF16)
            gc = _bdot(mc, taps)
            gs = _bdot(ms, taps)
        else:
            a = -d
            taps = hb_ref[n * (a - 1):n * (a + 1), :].astype(BF16)
            gc = _bdot(mc, taps)
            gs = -_bdot(ms, taps)
        gc_ref[d + P - 1] = gc.astype(gc_ref.dtype)
        gs_ref[d + P - 1] = gs.astype(gs_ref.dtype)


def _filter_spectra(hfilt, bias, mc, ms, P, cw=256):
    L, W = hfilt.shape
    n = L // P
    half = W // 2
    nblk = half // cw
    out_sd = jax.ShapeDtypeStruct((2 * P - 1, n, half), SPEC_DTYPE)
    return pl.pallas_call(
        functools.partial(_spectra_kernel, P=P, n=n),
        grid=(nblk,),
        in_specs=[pl.BlockSpec((L, cw), lambda j: (0, j)),
                  pl.BlockSpec((L, cw), lambda j: (0, nblk + j)),
                  pl.BlockSpec((1, cw), lambda j: (0, j)),
                  pl.BlockSpec((n, 2 * n), lambda j: (0, 0)),
                  pl.BlockSpec((n, 2 * n), lambda j: (0, 0))],
        out_specs=[pl.BlockSpec((2 * P - 1, n, cw), lambda j: (0, 0, j)),
                   pl.BlockSpec((2 * P - 1, n, cw), lambda j: (0, 0, j))],
        out_shape=[out_sd, out_sd],
        compiler_params=_cparams(1),
        name="filter_spectra",
    )(hfilt, hfilt, bias.reshape(1, half), mc, ms)


def _hyena_kernel(hn_ref, wv_ref, w1_ref, w2_ref, cwv_ref, cw1_ref, cw2_ref,
                  cbv_ref, cb1_ref, cb2_ref, gc0_ref, gs0_ref, gc1_ref, gs1_ref,
                  fwd_ref, inv_ref, z_ref,
                  rawa_ref, rawb_ref, v_ref, gatea_ref, gateb_ref, uspec_ref, yspec_ref,
                  *, P, n, fchunk):
    row = lax.broadcasted_iota(jnp.int32, (n, 1), 0)
    blk = [slice(n * i, n * (i + 1)) for i in range(P)]
    chunks = n // fchunk
    fwd = fwd_ref[...]
    inv = inv_ref[...]

    def project(w_ref, dst_ref, i):
        dst_ref[blk[i], :] = _bdot(hn_ref[0, blk[i], :], w_ref[...].astype(BF16))

    def short_conv(raw_ref, cw_ref, cb_ref, dst, i):
        cw = cw_ref[...]
        cur = raw_ref[blk[i], :]
        prev = pltpu.roll(cur, 1, axis=0)
        nxt = pltpu.roll(cur, n - 1, axis=0)
        first = raw_ref[n * i - 1:n * i, :] if i > 0 else jnp.zeros_like(cur[0:1])
        last = raw_ref[n * (i + 1):n * (i + 1) + 1, :] if i < P - 1 else jnp.zeros_like(cur[0:1])
        prev = jnp.where(row == 0, first, prev)
        nxt = jnp.where(row == n - 1, last, nxt)
        out = prev * cw[0:1] + cur * cw[1:2] + nxt * cw[2:3] + cb_ref[...]
        dst[blk[i], :] = out.astype(dst.dtype)

    def forward(src_ref, j):
        uspec_ref[j] = _bdot(fwd, src_ref[blk[j], :]).astype(uspec_ref.dtype)

    def combine(gc_ref, gs_ref, c):
        fc = slice(c * fchunk, (c + 1) * fchunk)
        fs = slice(n + c * fchunk, n + (c + 1) * fchunk)
        uc = [uspec_ref[j, fc, :] for j in range(P)]
        us = [uspec_ref[j, fs, :] for j in range(P)]
        for i in range(P):
            yc = None
            ys = None
            for j in range(P):
                gc = gc_ref[i - j + P - 1, fc, :]
                gs = gs_ref[i - j + P - 1, fc, :]
                tc = gc * uc[j] - gs * us[j]
                ts = gc * us[j] + gs * uc[j]
                yc = tc if yc is None else yc + tc
                ys = ts if ys is None else ys + ts
            yspec_ref[i, fc, :] = yc.astype(BF16)
            yspec_ref[i, fs, :] = ys.astype(BF16)

    for i in range(P):
        project(wv_ref, rawa_ref, i)
    for i in range(P):
        short_conv(rawa_ref, cwv_ref, cbv_ref, v_ref, i)
        project(w1_ref, rawb_ref, i)
    for i in range(P):
        forward(v_ref, i)
        short_conv(rawb_ref, cw1_ref, cb1_ref, gatea_ref, i)
    for c in range(chunks):
        combine(gc0_ref, gs0_ref, c)
        if c % (chunks // P) == 0:
            project(w2_ref, rawa_ref, c // (chunks // P))
    for i in range(P):
        short_conv(rawa_ref, cw2_ref, cb2_ref, gateb_ref, i)
        v_ref[blk[i], :] = (gatea_ref[blk[i], :] * _bdot(inv, yspec_ref[i])).astype(v_ref.dtype)
    for i in range(P):
        forward(v_ref, i)
    for c in range(chunks):
        combine(gc1_ref, gs1_ref, c)
    for i in range(P):
        z_ref[0, blk[i], :] = (gateb_ref[blk[i], :] * _bdot(inv, yspec_ref[i])).astype(z_ref.dtype)


def _hyena_core(hn, w_in, conv_w, conv_b, gc, gs, fwd, inv, P):
    B, L, D = hn.shape
    n = L // P
    cb_n = D // CH_BLOCK
    C = CH_BLOCK
    nd = 2 * P - 1

    def stream_spec(shape, s):
        return pl.BlockSpec(shape, lambda c, b: (0, s * cb_n + c))

    in_specs = (
        [pl.BlockSpec((1, L, D), lambda c, b: (b, 0, 0))]
        + [stream_spec((D, C), s) for s in range(3)]
        + [stream_spec((3, C), s) for s in range(3)]
        + [stream_spec((1, C), s) for s in range(3)]
        + [_resident((nd, n, C), lambda c, b: (0, 0, c)),
           _resident((nd, n, C), lambda c, b: (0, 0, c)),
           _resident((nd, n, C), lambda c, b: (0, 0, cb_n + c)),
           _resident((nd, n, C), lambda c, b: (0, 0, cb_n + c)),
           pl.BlockSpec((2 * n, n), lambda c, b: (0, 0)),
           pl.BlockSpec((n, 2 * n), lambda c, b: (0, 0))])
    return pl.pallas_call(
        functools.partial(_hyena_kernel, P=P, n=n, fchunk=16),
        grid=(cb_n, B),
        in_specs=in_specs,
        out_specs=pl.BlockSpec((1, L, C), lambda c, b: (b, 0, c)),
        out_shape=jax.ShapeDtypeStruct((B, L, D), BF16),
        scratch_shapes=[pltpu.VMEM((L, C), F32),
                        pltpu.VMEM((L, C), F32),
                        pltpu.VMEM((L, C), BF16),
                        pltpu.VMEM((L, C), F32),
                        pltpu.VMEM((L, C), F32),
                        pltpu.VMEM((P, 2 * n, C), SPEC_DTYPE),
                        pltpu.VMEM((P, 2 * n, C), BF16)],
        compiler_params=_cparams(2),
        name="hyena_core",
    )(hn, w_in, w_in, w_in, conv_w, conv_w, conv_w, conv_b, conv_b, conv_b,
      gc, gs, gc, gs, fwd, inv)


def _stream_cast(src_hbm, layer, dst_ref, stage_ref, sem_ref):
    slots, rows = stage_ref.shape[0], stage_ref.shape[1]
    n_chunks = dst_ref.shape[0] // rows

    def copy(c):
        return pltpu.make_async_copy(src_hbm.at[layer, pl.ds(c * rows, rows), :],
                                     stage_ref.at[c % slots], sem_ref.at[c % slots])

    for c in range(min(slots - 1, n_chunks)):
        copy(c).start()
    for c in range(n_chunks):
        ahead = c + slots - 1
        if ahead < n_chunks:
            copy(ahead).start()
        copy(c).wait()
        dst_ref[c * rows:(c + 1) * rows, :] = stage_ref[c % slots].astype(dst_ref.dtype)


def _tail_kernel(a_ref, wo32_ref, r_ref, gmix_ref, gpre_ref, gpost_ref, wu_hbm, wd_hbm, o_ref,
                 wo_ref, wu_ref, wd_ref, stage_u, stage_d, sem_u, sem_d, *, fchunk, layer):
    @pl.when(pl.program_id(0) == 0)
    def _():
        wo_ref[...] = wo32_ref[...].astype(wo_ref.dtype)
        _stream_cast(wu_hbm, layer, wu_ref, stage_u, sem_u)
        _stream_cast(wd_hbm, layer, wd_ref, stage_d, sem_d)

    m = _bdot(a_ref[...], wo_ref[...])
    h = r_ref[...] + _rms(m, gmix_ref[...])
    hn = _rms(h, gpre_ref[...]).astype(BF16)
    FF = wu_ref.shape[1]
    acc = None
    for c in range(FF // fchunk):
        a = _bdot(hn, wu_ref[:, c * fchunk:(c + 1) * fchunk])
        a = jnp.maximum(a, 0.0)
        a = (a * a).astype(BF16)
        part = _bdot(a, wd_ref[c * fchunk:(c + 1) * fchunk, :])
        acc = part if acc is None else acc + part
    o_ref[...] = h + _rms(acc, gpost_ref[...])


def _tail(a, w_o, resid, g_mix, g_pre, g_post, w_up_all, w_down_all, layer, tm=512, fchunk=1024):
    M, K = a.shape
    D = w_o.shape[1]
    FF = w_up_all.shape[2]
    stage_bytes = 1024 * 1024
    rows_u = stage_bytes // (FF * 4)
    rows_d = stage_bytes // (D * 4)
    row = lambda shape: pl.BlockSpec(shape, lambda i: (i, 0))
    gain = pl.BlockSpec((1, D), lambda i: (0, 0))
    return pl.pallas_call(
        functools.partial(_tail_kernel, fchunk=fchunk, layer=layer),
        grid=(M // tm,),
        in_specs=[row((tm, K)), _resident((K, D), lambda i: (0, 0)), row((tm, D)),
                  gain, gain, gain,
                  pl.BlockSpec(memory_space=pl.ANY),
                  pl.BlockSpec(memory_space=pl.ANY)],
        out_specs=row((tm, D)),
        out_shape=jax.ShapeDtypeStruct((M, D), F32),
        scratch_shapes=[pltpu.VMEM((K, D), BF16),
                        pltpu.VMEM((D, FF), BF16),
                        pltpu.VMEM((FF, D), BF16),
                        pltpu.VMEM((STAGE_SLOTS, rows_u, FF), F32),
                        pltpu.VMEM((STAGE_SLOTS, rows_d, D), F32),
                        pltpu.SemaphoreType.DMA((STAGE_SLOTS,)),
                        pltpu.SemaphoreType.DMA((STAGE_SLOTS,))],
        compiler_params=_cparams(1),
        name="mixer_tail_mlp",
    )(a, w_o, resid, g_mix.reshape(1, D), g_pre.reshape(1, D), g_post.reshape(1, D),
      w_up_all, w_down_all)


def _qkv_kernel(h_ref, g_ref, w32_ref, ct_ref, slo_ref, shi_ref, q_ref, k_ref, v_ref, w_ref, *, halves):
    @pl.when(pl.program_id(0) == 0)
    def _():
        w_ref[...] = w32_ref[...].astype(w_ref.dtype)

    nq, nk = q_ref.shape[1], k_ref.shape[1]
    hm = h_ref.shape[0] // halves
    rows = [slice(hm * u, hm * (u + 1)) for u in range(halves)]
    half = ROT_DIM // 2
    scale = LOG2E * HEAD_DIM ** -0.5
    qkvs = [_bdot(_rms(h_ref[r, :], g_ref[...]).astype(BF16), w_ref[...]) for r in rows]
    for r, qkv in zip(rows, qkvs):
        ct = ct_ref[r, :]
        slo = slo_ref[r, :]
        shi = shi_ref[r, :]
        for j in range((nq + nk) // 128):
            t = qkv[:, 128 * j:128 * (j + 1)]
            rot = t * ct + pltpu.roll(t, 128 - half, axis=1) * slo + pltpu.roll(t, half, axis=1) * shi
            if 128 * j < nq:
                q_ref[r, 128 * j:128 * (j + 1)] = (rot * scale).astype(q_ref.dtype)
            else:
                k_ref[r, 128 * j - nq:128 * (j + 1) - nq] = rot.astype(k_ref.dtype)
        v_ref[r, :] = qkv[:, nq + nk:].astype(v_ref.dtype)


def _qkv(h, g, w, S, tm=1024, halves=2):
    M, D = h.shape
    nq = D
    nk = (w.shape[1] - nq) // 2
    ct, slo, shi = [jnp.asarray(a) for a in _rope_tables(S)]
    pb = S // tm
    tab = pl.BlockSpec((tm, 128), lambda i: (i % pb, 0))
    return pl.pallas_call(
        functools.partial(_qkv_kernel, halves=halves),
        grid=(M // tm,),
        in_specs=[pl.BlockSpec((tm, D), lambda i: (i, 0)),
                  pl.BlockSpec((1, D), lambda i: (0, 0)),
                  _resident((D, nq + 2 * nk), lambda i: (0, 0)),
                  tab, tab, tab],
        out_specs=[pl.BlockSpec((tm, nq), lambda i: (i, 0)),
                   pl.BlockSpec((tm, nk), lambda i: (i, 0)),
                   pl.BlockSpec((tm, nk), lambda i: (i, 0))],
        out_shape=[jax.ShapeDtypeStruct((M, nq), BF16),
                   jax.ShapeDtypeStruct((M, nk), BF16),
                   jax.ShapeDtypeStruct((M, nk), BF16)],
        scratch_shapes=[pltpu.VMEM(w.shape, BF16)],
        compiler_params=_cparams(1),
        name="qkv_rope",
    )(h, g.reshape(1, D), w, ct, slo, shi)


def _attn_kernel(sink_ref, q_ref, k_ref, v_ref, o_ref, krep_ref, vrep_ref, *, S, n_kv):
    BQ = WINDOW
    BK = 3 * WINDOW
    GW = KV_GROUP * HEAD_DIM
    dn = (((1,), (1,)), ((), ()))

    for g in range(n_kv):
        kg = k_ref[0, :, g * HEAD_DIM:(g + 1) * HEAD_DIM]
        vg = v_ref[0, :, g * HEAD_DIM:(g + 1) * HEAD_DIM]
        krep_ref[g] = jnp.concatenate([kg] * KV_GROUP, axis=1)
        vrep_ref[g] = jnp.concatenate([vg] * KV_GROUP, axis=1)

    lane = lax.broadcasted_iota(jnp.int32, (BQ, GW), 1)
    head_lanes = [(lane >= hh * HEAD_DIM) & (lane < (hh + 1) * HEAD_DIM) for hh in range(KV_GROUP)]
    rowblk = lax.broadcasted_iota(jnp.int32, (KV_GROUP * BQ, 1), 0) // BQ

    def body(qb, carry):
        qs = pl.multiple_of(qb * BQ, BQ)
        ks = pl.multiple_of(jnp.clip(qs - WINDOW, 0, S - BK), WINDOW)
        qpos = qs + lax.broadcasted_iota(jnp.int32, (BQ, BK), 0)
        kpos = ks + lax.broadcasted_iota(jnp.int32, (BQ, BK), 1)
        maskadd = jnp.where(jnp.abs(kpos - qpos) <= WINDOW, 0.0, NEG).astype(F32)
        maskadd = jnp.concatenate([maskadd] * KV_GROUP, axis=0)
        for g in range(n_kv):
            qg = q_ref[0, pl.ds(qs, BQ), g * GW:(g + 1) * GW]
            qst = jnp.concatenate([jnp.where(head_lanes[hh], qg, jnp.zeros_like(qg))
                                   for hh in range(KV_GROUP)], axis=0)
            sink = jnp.zeros((KV_GROUP * BQ, 1), F32)
            for hh in range(KV_GROUP):
                sink = jnp.where(rowblk == hh, sink_ref[g * KV_GROUP + hh] * LOG2E, sink)
            s = lax.dot_general(qst, krep_ref[g, pl.ds(ks, BK), :], dn,
                                preferred_element_type=F32) + maskadd
            m = jnp.maximum(jnp.max(s, axis=-1, keepdims=True), sink)
            p = jnp.exp2(s - m)
            denom = jnp.sum(p, axis=-1, keepdims=True) + jnp.exp2(sink - m)
            ost = _bdot(p.astype(BF16), vrep_ref[g, pl.ds(ks, BK), :]) / denom
            o = jnp.where(head_lanes[0], ost[0:BQ], 0.0)
            for hh in range(1, KV_GROUP):
                o = jnp.where(head_lanes[hh], ost[hh * BQ:(hh + 1) * BQ], o)
            o_ref[0, pl.ds(qs, BQ), g * GW:(g + 1) * GW] = o.astype(o_ref.dtype)
        return carry

    lax.fori_loop(0, S // BQ, body, 0, unroll=4)


def _attention(q, k, v, sink):
    B, S, NQ = q.shape
    NK = k.shape[2]
    n_kv = NK // HEAD_DIM
    return pl.pallas_call(
        functools.partial(_attn_kernel, S=S, n_kv=n_kv),
        grid=(B,),
        in_specs=[pl.BlockSpec(memory_space=pltpu.SMEM),
                  pl.BlockSpec((1, S, NQ), lambda b: (b, 0, 0)),
                  pl.BlockSpec((1, S, NK), lambda b: (b, 0, 0)),
                  pl.BlockSpec((1, S, NK), lambda b: (b, 0, 0))],
        out_specs=pl.BlockSpec((1, S, NQ), lambda b: (b, 0, 0)),
        out_shape=jax.ShapeDtypeStruct((B, S, NQ), BF16),
        scratch_shapes=[pltpu.VMEM((n_kv, S, KV_GROUP * HEAD_DIM), BF16),
                        pltpu.VMEM((n_kv, S, KV_GROUP * HEAD_DIM), BF16)],
        compiler_params=_cparams(1),
        name="window_attention",
    )(sink, q, k, v)


def kernel(x, norm_mix_pre, norm_mix_post, norm_mlp_pre, norm_mlp_post, w_up, w_down, hy_w_in, hy_conv_w, hy_conv_b, hy_f_w1, hy_f_b1, hy_f_w2, hy_f_b2, hy_f_w3, hy_f_b3, hy_f_freq, hy_f_wout, hy_bias, hy_w_out, at_w_qkv, at_sink, at_w_o):
    B, L, D = x.shape
    M = B * L
    P = CONV_BLOCKS
    n = L // P
    depth = norm_mix_pre.shape[0]
    fwd, inv, filt_c, filt_s = [jnp.asarray(a).astype(BF16) for a in _dft_tables(n)]

    h = x.reshape(M, D)
    for i in range(depth):
        j = i // 2
        if i % 2 == 0:
            hfilt = _hyena_filters(L, D, hy_f_w1[j], hy_f_b1[j], hy_f_w2[j], hy_f_b2[j],
                                   hy_f_w3[j], hy_f_b3[j], hy_f_freq[j], hy_f_wout[j])
            gc, gs = _filter_spectra(hfilt, hy_bias[j], filt_c, filt_s, P)
            hn = _prenorm(h, norm_mix_pre[i]).reshape(B, L, D)
            z = _hyena_core(hn, hy_w_in[j], hy_conv_w[j], hy_conv_b[j].reshape(1, -1),
                            gc, gs, fwd, inv, P)
            a, w_o = z.reshape(M, D), hy_w_out[j]
        else:
            q, k, v = _qkv(h, norm_mix_pre[i], at_w_qkv[j], L)
            o = _attention(q.reshape(B, L, -1), k.reshape(B, L, -1), v.reshape(B, L, -1), at_sink[j])
            a, w_o = o.reshape(M, -1), at_w_o[j]
        h = _tail(a, w_o, h, norm_mix_post[i], norm_mlp_pre[i], norm_mlp_post[i], w_up, w_down, i)
    return h.reshape(B, L, D)
```

```python
import functools
import math

import numpy as np
import jax
import jax.numpy as jnp
from jax import lax
from jax.experimental import pallas as pl
from jax.experimental.pallas import tpu as pltpu

BF16 = jnp.bfloat16
F32 = jnp.float32

EPS = 1e-6
NEG = -1e30
LOG2E = math.log2(math.e)

HY_ORDER = 2
HY_EMB = 33
HY_BANDS = (HY_EMB - 1) // 2
HY_DECAY_TARGET = 1e-2
HY_FAST_DECAY = 0.3
HY_SLOW_DECAY = 1.5
HEAD_DIM = 64
KV_GROUP = 4
WINDOW = 128
ROT_DIM = HEAD_DIM // 4
ROPE_THETA = 500000.0

CONV_BLOCKS = 4
CH_BLOCK = 256
SPEC_DTYPE = BF16
STAGE_SLOTS = 4

VMEM_LIMIT = 56 * 1024 * 1024


def _cparams(n_axes, flags=None):
    return pltpu.CompilerParams(
        dimension_semantics=("arbitrary",) * n_axes,
        vmem_limit_bytes=VMEM_LIMIT,
        flags=flags)


def _resident(shape, index_map):
    return pl.BlockSpec(shape, index_map, pipeline_mode=pl.Buffered(1))


def _rms(x, g):
    ms = jnp.mean(x * x, axis=-1, keepdims=True)
    return x * lax.rsqrt(ms + EPS) * g


def _bdot(a, b):
    return jnp.dot(a, b, preferred_element_type=F32)


def _dft_tables(n):
    f = np.arange(n, dtype=np.int64)[:, None]
    r = np.arange(n, dtype=np.int64)[None, :]
    k_pos = ((2 * f + 1) * r) % (4 * n)
    k_neg = ((2 * f + 1) * (r - n)) % (4 * n)
    ang_pos = np.pi * k_pos / (2 * n)
    ang_neg = np.pi * k_neg / (2 * n)
    cpos, spos = np.cos(ang_pos), np.sin(ang_pos)
    cneg, sneg = np.cos(ang_neg), np.sin(ang_neg)
    cneg[:, 0] = 0.0
    sneg[:, 0] = 0.0
    fwd = np.concatenate([cpos, spos], axis=0)
    inv = np.concatenate([cpos.T, spos.T], axis=1) / n
    filt_c = np.concatenate([cneg, cpos], axis=1)
    filt_s = np.concatenate([sneg, spos], axis=1)
    return [np.asarray(a, np.float32) for a in (fwd, inv, filt_c, filt_s)]


def _filter_positions(L):
    t = np.linspace(0.0, 1.0, L)[:, None]
    w = (2.0 * np.pi / L) * np.arange(L)[:, None]
    f = np.linspace(1e-4, HY_BANDS - 1, HY_BANDS)[None, :]
    z = np.concatenate([t, np.cos(f * w), -np.sin(f * w)], axis=-1)
    return np.asarray(z, np.float32)


def _rope_tables(S):
    half = ROT_DIM // 2
    inv = ROPE_THETA ** (-np.arange(0, ROT_DIM, 2, dtype=np.float64) / ROT_DIM)
    ang = np.arange(S, dtype=np.float64)[:, None] * inv[None, :]
    lane = np.arange(128) % HEAD_DIM
    ang_l = ang[:, lane % half]
    cos_t = np.where(lane[None, :] < ROT_DIM, np.cos(ang_l), 1.0)
    sin_lo = np.where(lane[None, :] < half, -np.sin(ang_l), 0.0)
    sin_hi = np.where((lane[None, :] >= half) & (lane[None, :] < ROT_DIM), np.sin(ang_l), 0.0)
    return [np.asarray(a, np.float32) for a in (cos_t, sin_lo, sin_hi)]


def _filter_kernel(z_ref, w1_ref, b1_ref, w2_ref, b2_ref, w3_ref, b3_ref, fr_ref,
                   wout_ref, dl_ref, o_ref, hid_ref, *, n_fwd_blocks):
    j = pl.program_id(0)
    hi = lax.Precision.HIGHEST

    @pl.when(j == 0)
    def _():
        fr = fr_ref[...]
        h = jnp.sin(fr * (jnp.dot(z_ref[...], w1_ref[...], precision=hi) + b1_ref[...]))
        h = jnp.sin(fr * (jnp.dot(h, w2_ref[...], precision=hi) + b2_ref[...]))
        h = jnp.sin(fr * (jnp.dot(h, w3_ref[...], precision=hi) + b3_ref[...]))
        hid_ref[...] = h

    L, W = o_ref.shape
    row = lax.broadcasted_iota(jnp.int32, (L, W), 0)
    t = row.astype(F32) * (1.0 / (L - 1))
    decay = jnp.exp(-t * dl_ref[...])
    hid = hid_ref[...]
    wout = wout_ref[...]
    hid_hi = hid.astype(BF16)
    hid_lo = (hid - hid_hi.astype(F32)).astype(BF16)
    w_hi = wout.astype(BF16)
    w_lo = (wout - w_hi.astype(F32)).astype(BF16)
    val = (_bdot(hid_hi, w_hi) + (_bdot(hid_hi, w_lo) + _bdot(hid_lo, w_hi))) * decay
    keep = jnp.logical_or(row > 0, j < n_fwd_blocks)
    o_ref[...] = jnp.where(keep, val, 0.0).astype(o_ref.dtype)


def _hyena_filters(L, D, f_w1, f_b1, f_w2, f_b2, f_w3, f_b3, f_freq, f_wout):
    FW = f_w1.shape[1]
    z = jnp.asarray(np.pad(_filter_positions(L), ((0, 0), (0, FW - HY_EMB))))
    w1 = jnp.pad(f_w1, ((0, FW - HY_EMB), (0, 0)))
    max_decay = math.log(HY_DECAY_TARGET) / HY_FAST_DECAY
    min_decay = math.log(HY_DECAY_TARGET) / HY_SLOW_DECAY
    absdelta = jnp.asarray(np.abs(np.linspace(min_decay, max_decay, D)).astype(np.float32))[None, :]
    n_blocks = f_wout.shape[1] // D
    small = lambda shape: pl.BlockSpec(shape, lambda j: (0, 0))
    return pl.pallas_call(
        functools.partial(_filter_kernel, n_fwd_blocks=n_blocks // 2),
        grid=(n_blocks,),
        in_specs=[small((L, FW)), small((FW, FW)), small((1, FW)), small((FW, FW)), small((1, FW)),
                  small((FW, FW)), small((1, FW)), small((1, FW)),
                  pl.BlockSpec((FW, D), lambda j: (0, j)),
                  small((1, D))],
        out_specs=pl.BlockSpec((L, D), lambda j: (0, j)),
        out_shape=jax.ShapeDtypeStruct((L, n_blocks * D), BF16),
        scratch_shapes=[pltpu.VMEM((L, FW), F32)],
        compiler_params=_cparams(1),
        name="hyena_filter",
    )(z, w1, f_b1.reshape(1, FW), f_w2, f_b2.reshape(1, FW), f_w3, f_b3.reshape(1, FW),
      f_freq.reshape(1, FW), f_wout, absdelta)


def _spectra_kernel(hf_ref, hb_ref, bias_ref, mc_ref, ms_ref, x_ref, g_ref, gc_ref, gs_ref, hn_ref,
                    *, P, n, norm_rows):
    mc = mc_ref[...]
    ms = ms_ref[...]
    cpos = mc[:, n:]
    spos = ms[:, n:]
    norm_chunks = [slice(r, r + norm_rows) for r in range(0, x_ref.shape[0], norm_rows)]
    nd = 2 * P - 1
    for idx, d in enumerate(range(-(P - 1), P)):
        for r in norm_chunks[idx * len(norm_chunks) // nd:(idx + 1) * len(norm_chunks) // nd]:
            hn_ref[r, :] = _rms(x_ref[r, :], g_ref[...]).astype(hn_ref.dtype)
        if d == 0:
            hf0 = hf_ref[0:n, :].astype(F32)
            hb0 = hb_ref[0:n, :].astype(F32)
            gc = _bdot(cpos, (hf0 + hb0).astype(BF16)) + bias_ref[...]
            gs = _bdot(spos, (hf0 - hb0).astype(BF16))
        elif d > 0:
            taps = hf_ref[n * (d - 1):n * (d + 1), :].astype(BF16)
            gc = _bdot(mc, taps)
            gs = _bdot(ms, taps)
        else:
            a = -d
            taps = hb_ref[n * (a - 1):n * (a + 1), :].astype(BF16)
            gc = _bdot(mc, taps)
            gs = -_bdot(ms, taps)
        gc_ref[d + P - 1] = gc.astype(gc_ref.dtype)
        gs_ref[d + P - 1] = gs.astype(gs_ref.dtype)


def _filter_spectra(hfilt, bias, mc, ms, P, x2d, g, cw=256, norm_rows=256):
    L, W = hfilt.shape
    n = L // P
    half = W // 2
    nblk = half // cw
    M, D = x2d.shape
    xm = M // nblk
    out_sd = jax.ShapeDtypeStruct((2 * P - 1, n, half), SPEC_DTYPE)
    return pl.pallas_call(
        functools.partial(_spectra_kernel, P=P, n=n, norm_rows=norm_rows),
        grid=(nblk,),
        in_specs=[pl.BlockSpec((L, cw), lambda j: (0, j)),
                  pl.BlockSpec((L, cw), lambda j: (0, nblk + j)),
                  pl.BlockSpec((1, cw), lambda j: (0, j)),
                  pl.BlockSpec((n, 2 * n), lambda j: (0, 0)),
                  pl.BlockSpec((n, 2 * n), lambda j: (0, 0)),
                  pl.BlockSpec((xm, D), lambda j: (j, 0)),
                  pl.BlockSpec((1, D), lambda j: (0, 0))],
        out_specs=[pl.BlockSpec((2 * P - 1, n, cw), lambda j: (0, 0, j)),
                   pl.BlockSpec((2 * P - 1, n, cw), lambda j: (0, 0, j)),
                   pl.BlockSpec((xm, D), lambda j: (j, 0))],
        out_shape=[out_sd, out_sd, jax.ShapeDtypeStruct((M, D), BF16)],
        compiler_params=_cparams(1),
        name="filter_spectra",
    )(hfilt, hfilt, bias.reshape(1, half), mc, ms, x2d, g.reshape(1, D))


def _hyena_kernel(hn_ref, wv_ref, w1_ref, w2_ref, cwv_ref, cw1_ref, cw2_ref,
                  cbv_ref, cb1_ref, cb2_ref, gc0_ref, gs0_ref, gc1_ref, gs1_ref,
                  fwd_ref, inv_ref, z_ref,
                  rawa_ref, rawb_ref, v_ref, gatea_ref, gateb_ref, uspec_ref, yspec_ref,
                  *, P, n, fchunk):
    row = lax.broadcasted_iota(jnp.int32, (n, 1), 0)
    blk = [slice(n * i, n * (i + 1)) for i in range(P)]
    chunks = n // fchunk
    fwd = fwd_ref[...]
    inv = inv_ref[...]

    def project(w_ref, dst_ref, i):
        dst_ref[blk[i], :] = _bdot(hn_ref[0, blk[i], :], w_ref[...].astype(BF16))

    def short_conv(raw_ref, cw_ref, cb_ref, dst, i):
        cw = cw_ref[...]
        cur = raw_ref[blk[i], :]
        prev = pltpu.roll(cur, 1, axis=0)
        nxt = pltpu.roll(cur, n - 1, axis=0)
        first = raw_ref[n * i - 1:n * i, :] if i > 0 else jnp.zeros_like(cur[0:1])
        last = raw_ref[n * (i + 1):n * (i + 1) + 1, :] if i < P - 1 else jnp.zeros_like(cur[0:1])
        prev = jnp.where(row == 0, first, prev)
        nxt = jnp.where(row == n - 1, last, nxt)
        out = prev * cw[0:1] + cur * cw[1:2] + nxt * cw[2:3] + cb_ref[...]
        dst[blk[i], :] = out.astype(dst.dtype)

    def forward(src_ref, j):
        uspec_ref[j] = _bdot(fwd, src_ref[blk[j], :]).astype(uspec_ref.dtype)

    def combine(gc_ref, gs_ref, c):
        fc = slice(c * fchunk, (c + 1) * fchunk)
        fs = slice(n + c * fchunk, n + (c + 1) * fchunk)
        uc = [uspec_ref[j, fc, :] for j in range(P)]
        us = [uspec_ref[j, fs, :] for j in range(P)]
        for i in range(P):
            yc = None
            ys = None
            for j in range(P):
                gc = gc_ref[i - j + P - 1, fc, :]
                gs = gs_ref[i - j + P - 1, fc, :]
                tc = gc * uc[j] - gs * us[j]
                ts = gc * us[j] + gs * uc[j]
                yc = tc if yc is None else yc + tc
                ys = ts if ys is None else ys + ts
            yspec_ref[i, fc, :] = yc.astype(BF16)
            yspec_ref[i, fs, :] = ys.astype(BF16)

    for i in range(P):
        project(wv_ref, rawa_ref, i)
    for i in range(P):
        short_conv(rawa_ref, cwv_ref, cbv_ref, v_ref, i)
        project(w1_ref, rawb_ref, i)
    for i in range(P):
        forward(v_ref, i)
        short_conv(rawb_ref, cw1_ref, cb1_ref, gatea_ref, i)
    for c in range(chunks):
        combine(gc0_ref, gs0_ref, c)
        if c % (chunks // P) == 0:
            project(w2_ref, rawa_ref, c // (chunks // P))
    for i in range(P):
        short_conv(rawa_ref, cw2_ref, cb2_ref, gateb_ref, i)
        v_ref[blk[i], :] = (gatea_ref[blk[i], :] * _bdot(inv, yspec_ref[i])).astype(v_ref.dtype)
    for i in range(P):
        forward(v_ref, i)
    for c in range(chunks):
        combine(gc1_ref, gs1_ref, c)
    for i in range(P):
        z_ref[0, blk[i], :] = (gateb_ref[blk[i], :] * _bdot(inv, yspec_ref[i])).astype(z_ref.dtype)


def _hyena_core(hn, w_in, conv_w, conv_b, gc, gs, fwd, inv, P):
    B, L, D = hn.shape
    n = L // P
    cb_n = D // CH_BLOCK
    C = CH_BLOCK
    nd = 2 * P - 1

    def stream_spec(shape, s):
        return pl.BlockSpec(shape, lambda c, b: (0, s * cb_n + c))

    in_specs = (
        [pl.BlockSpec((1, L, D), lambda c, b: (b, 0, 0))]
        + [stream_spec((D, C), s) for s in range(3)]
        + [stream_spec((3, C), s) for s in range(3)]
        + [stream_spec((1, C), s) for s in range(3)]
        + [_resident((nd, n, C), lambda c, b: (0, 0, c)),
           _resident((nd, n, C), lambda c, b: (0, 0, c)),
           _resident((nd, n, C), lambda c, b: (0, 0, cb_n + c)),
           _resident((nd, n, C), lambda c, b: (0, 0, cb_n + c)),
           pl.BlockSpec((2 * n, n), lambda c, b: (0, 0)),
           pl.BlockSpec((n, 2 * n), lambda c, b: (0, 0))])
    return pl.pallas_call(
        functools.partial(_hyena_kernel, P=P, n=n, fchunk=16),
        grid=(cb_n, B),
        in_specs=in_specs,
        out_specs=pl.BlockSpec((1, L, C), lambda c, b: (b, 0, c)),
        out_shape=jax.ShapeDtypeStruct((B, L, D), BF16),
        scratch_shapes=[pltpu.VMEM((L, C), F32),
                        pltpu.VMEM((L, C), F32),
                        pltpu.VMEM((L, C), BF16),
                        pltpu.VMEM((L, C), F32),
                        pltpu.VMEM((L, C), F32),
                        pltpu.VMEM((P, 2 * n, C), SPEC_DTYPE),
                        pltpu.VMEM((P, 2 * n, C), BF16)],
        compiler_params=_cparams(2),
        name="hyena_core",
    )(hn, w_in, w_in, w_in, conv_w, conv_w, conv_w, conv_b, conv_b, conv_b,
      gc, gs, gc, gs, fwd, inv)


def _stream_cast(src_hbm, layer, dst_ref, stage_ref, sem_ref):
    slots, rows = stage_ref.shape[0], stage_ref.shape[1]
    n_chunks = dst_ref.shape[0] // rows

    def copy(c):
        return pltpu.make_async_copy(src_hbm.at[layer, pl.ds(c * rows, rows), :],
                                     stage_ref.at[c % slots], sem_ref.at[c % slots])

    for c in range(min(slots - 1, n_chunks)):
        copy(c).start()
    for c in range(n_chunks):
        ahead = c + slots - 1
        if ahead < n_chunks:
            copy(ahead).start()
        copy(c).wait()
        dst_ref[c * rows:(c + 1) * rows, :] = stage_ref[c % slots].astype(dst_ref.dtype)


def _tail_kernel(a_ref, wo32_ref, r_ref, gmix_ref, gpre_ref, gpost_ref, wu_hbm, wd_hbm, o_ref,
                 wo_ref, wu_ref, wd_ref, stage_u, stage_d, sem_u, sem_d, *, fchunk, layer):
    @pl.when(pl.program_id(0) == 0)
    def _():
        wo_ref[...] = wo32_ref[...].astype(wo_ref.dtype)
        _stream_cast(wu_hbm, layer, wu_ref, stage_u, sem_u)
        _stream_cast(wd_hbm, layer, wd_ref, stage_d, sem_d)

    slabs = 2
    sm = a_ref.shape[0] // slabs
    rows = [slice(sm * u, sm * (u + 1)) for u in range(slabs)]
    FF = wu_ref.shape[1]
    ms = [_bdot(a_ref[r, :], wo_ref[...]) for r in rows]
    hs = [r_ref[r, :] + _rms(m, gmix_ref[...]) for r, m in zip(rows, ms)]
    hns = [_rms(h, gpre_ref[...]).astype(BF16) for h in hs]
    accs = [None] * slabs
    for c in range(FF // fchunk):
        acts = []
        for u in range(slabs):
            a = jnp.maximum(_bdot(hns[u], wu_ref[:, c * fchunk:(c + 1) * fchunk]), 0.0)
            acts.append((a * a).astype(BF16))
        for u in range(slabs):
            part = _bdot(acts[u], wd_ref[c * fchunk:(c + 1) * fchunk, :])
            accs[u] = part if accs[u] is None else accs[u] + part
    for u in range(slabs):
        o_ref[rows[u], :] = hs[u] + _rms(accs[u], gpost_ref[...])


def _tail(a, w_o, resid, g_mix, g_pre, g_post, w_up_all, w_down_all, layer, tm=512, fchunk=1024):
    M, K = a.shape
    D = w_o.shape[1]
    FF = w_up_all.shape[2]
    stage_bytes = 1024 * 1024
    rows_u = stage_bytes // (FF * 4)
    rows_d = stage_bytes // (D * 4)
    row = lambda shape: pl.BlockSpec(shape, lambda i: (i, 0))
    gain = pl.BlockSpec((1, D), lambda i: (0, 0))
    return pl.pallas_call(
        functools.partial(_tail_kernel, fchunk=fchunk, layer=layer),
        grid=(M // tm,),
        in_specs=[row((tm, K)), _resident((K, D), lambda i: (0, 0)), row((tm, D)),
                  gain, gain, gain,
                  pl.BlockSpec(memory_space=pl.ANY),
                  pl.BlockSpec(memory_space=pl.ANY)],
        out_specs=row((tm, D)),
        out_shape=jax.ShapeDtypeStruct((M, D), F32),
        scratch_shapes=[pltpu.VMEM((K, D), BF16),
                        pltpu.VMEM((D, FF), BF16),
                        pltpu.VMEM((FF, D), BF16),
                        pltpu.VMEM((STAGE_SLOTS, rows_u, FF), F32),
                        pltpu.VMEM((STAGE_SLOTS, rows_d, D), F32),
                        pltpu.SemaphoreType.DMA((STAGE_SLOTS,)),
                        pltpu.SemaphoreType.DMA((STAGE_SLOTS,))],
        compiler_params=_cparams(1),
        name="mixer_tail_mlp",
    )(a, w_o, resid, g_mix.reshape(1, D), g_pre.reshape(1, D), g_post.reshape(1, D),
      w_up_all, w_down_all)


def _qkv_kernel(h_ref, g_ref, w32_ref, ct_ref, slo_ref, shi_ref, q_ref, k_ref, v_ref, w_ref, *, halves):
    @pl.when(pl.program_id(0) == 0)
    def _():
        w_ref[...] = w32_ref[...].astype(w_ref.dtype)

    nq, nk = q_ref.shape[1], k_ref.shape[1]
    hm = h_ref.shape[0] // halves
    rows = [slice(hm * u, hm * (u + 1)) for u in range(halves)]
    half = ROT_DIM // 2
    scale = LOG2E * HEAD_DIM ** -0.5
    qkvs = [_bdot(_rms(h_ref[r, :], g_ref[...]).astype(BF16), w_ref[...]) for r in rows]
    for r, qkv in zip(rows, qkvs):
        ct = ct_ref[r, :]
        slo = slo_ref[r, :]
        shi = shi_ref[r, :]
        for j in range((nq + nk) // 128):
            t = qkv[:, 128 * j:128 * (j + 1)]
            rot = t * ct + pltpu.roll(t, 128 - half, axis=1) * slo + pltpu.roll(t, half, axis=1) * shi
            if 128 * j < nq:
                q_ref[r, 128 * j:128 * (j + 1)] = (rot * scale).astype(q_ref.dtype)
            else:
                k_ref[r, 128 * j - nq:128 * (j + 1) - nq] = rot.astype(k_ref.dtype)
        v_ref[r, :] = qkv[:, nq + nk:].astype(v_ref.dtype)


def _qkv(h, g, w, S, tm=1024, halves=2):
    M, D = h.shape
    nq = D
    nk = (w.shape[1] - nq) // 2
    ct, slo, shi = [jnp.asarray(a) for a in _rope_tables(S)]
    pb = S // tm
    tab = pl.BlockSpec((tm, 128), lambda i: (i % pb, 0))
    return pl.pallas_call(
        functools.partial(_qkv_kernel, halves=halves),
        grid=(M // tm,),
        in_specs=[pl.BlockSpec((tm, D), lambda i: (i, 0)),
                  pl.BlockSpec((1, D), lambda i: (0, 0)),
                  _resident((D, nq + 2 * nk), lambda i: (0, 0)),
                  tab, tab, tab],
        out_specs=[pl.BlockSpec((tm, nq), lambda i: (i, 0)),
                   pl.BlockSpec((tm, nk), lambda i: (i, 0)),
                   pl.BlockSpec((tm, nk), lambda i: (i, 0))],
        out_shape=[jax.ShapeDtypeStruct((M, nq), BF16),
                   jax.ShapeDtypeStruct((M, nk), BF16),
                   jax.ShapeDtypeStruct((M, nk), BF16)],
        scratch_shapes=[pltpu.VMEM(w.shape, BF16)],
        compiler_params=_cparams(1),
        name="qkv_rope",
    )(h, g.reshape(1, D), w, ct, slo, shi)


def _attn_kernel(sink_ref, q_ref, k_ref, v_ref, o_ref, krep_ref, vrep_ref, *, S, n_kv):
    BQ = WINDOW
    BK = 3 * WINDOW
    GW = KV_GROUP * HEAD_DIM
    dn = (((1,), (1,)), ((), ()))

    for g in range(n_kv):
        kg = k_ref[0, :, g * HEAD_DIM:(g + 1) * HEAD_DIM]
        vg = v_ref[0, :, g * HEAD_DIM:(g + 1) * HEAD_DIM]
        krep_ref[g] = jnp.concatenate([kg] * KV_GROUP, axis=1)
        vrep_ref[g] = jnp.concatenate([vg] * KV_GROUP, axis=1)

    lane = lax.broadcasted_iota(jnp.int32, (BQ, GW), 1)
    head_lanes = [(lane >= hh * HEAD_DIM) & (lane < (hh + 1) * HEAD_DIM) for hh in range(KV_GROUP)]
    rowblk = lax.broadcasted_iota(jnp.int32, (KV_GROUP * BQ, 1), 0) // BQ

    def body(qb, carry):
        qs = pl.multiple_of(qb * BQ, BQ)
        ks = pl.multiple_of(jnp.clip(qs - WINDOW, 0, S - BK), WINDOW)
        qpos = qs + lax.broadcasted_iota(jnp.int32, (BQ, BK), 0)
        kpos = ks + lax.broadcasted_iota(jnp.int32, (BQ, BK), 1)
        maskadd = jnp.where(jnp.abs(kpos - qpos) <= WINDOW, 0.0, NEG).astype(F32)
        maskadd = jnp.concatenate([maskadd] * KV_GROUP, axis=0)
        for g in range(n_kv):
            qg = q_ref[0, pl.ds(qs, BQ), g * GW:(g + 1) * GW]
            qst = jnp.concatenate([jnp.where(head_lanes[hh], qg, jnp.zeros_like(qg))
                                   for hh in range(KV_GROUP)], axis=0)
            sink = jnp.zeros((KV_GROUP * BQ, 1), F32)
            for hh in range(KV_GROUP):
                sink = jnp.where(rowblk == hh, sink_ref[g * KV_GROUP + hh] * LOG2E, sink)
            s = lax.dot_general(qst, krep_ref[g, pl.ds(ks, BK), :], dn,
                                preferred_element_type=F32) + maskadd
            m = jnp.maximum(jnp.max(s, axis=-1, keepdims=True), sink)
            p = jnp.exp2(s - m)
            denom = jnp.sum(p, axis=-1, keepdims=True) + jnp.exp2(sink - m)
            ost = _bdot(p.astype(BF16), vrep_ref[g, pl.ds(ks, BK), :]) / denom
            o = jnp.where(head_lanes[0], ost[0:BQ], 0.0)
            for hh in range(1, KV_GROUP):
                o = jnp.where(head_lanes[hh], ost[hh * BQ:(hh + 1) * BQ], o)
            o_ref[0, pl.ds(qs, BQ), g * GW:(g + 1) * GW] = o.astype(o_ref.dtype)
        return carry

    lax.fori_loop(0, S // BQ, body, 0, unroll=4)


def _attention(q, k, v, sink):
    B, S, NQ = q.shape
    NK = k.shape[2]
    n_kv = NK // HEAD_DIM
    return pl.pallas_call(
        functools.partial(_attn_kernel, S=S, n_kv=n_kv),
        grid=(B,),
        in_specs=[pl.BlockSpec(memory_space=pltpu.SMEM),
                  pl.BlockSpec((1, S, NQ), lambda b: (b, 0, 0)),
                  pl.BlockSpec((1, S, NK), lambda b: (b, 0, 0)),
                  pl.BlockSpec((1, S, NK), lambda b: (b, 0, 0))],
        out_specs=pl.BlockSpec((1, S, NQ), lambda b: (b, 0, 0)),
        out_shape=jax.ShapeDtypeStruct((B, S, NQ), BF16),
        scratch_shapes=[pltpu.VMEM((n_kv, S, KV_GROUP * HEAD_DIM), BF16),
                        pltpu.VMEM((n_kv, S, KV_GROUP * HEAD_DIM), BF16)],
        compiler_params=_cparams(1),
        name="window_attention",
    )(sink, q, k, v)


def kernel(x, norm_mix_pre, norm_mix_post, norm_mlp_pre, norm_mlp_post, w_up, w_down, hy_w_in, hy_conv_w, hy_conv_b, hy_f_w1, hy_f_b1, hy_f_w2, hy_f_b2, hy_f_w3, hy_f_b3, hy_f_freq, hy_f_wout, hy_bias, hy_w_out, at_w_qkv, at_sink, at_w_o):
    B, L, D = x.shape
    M = B * L
    P = CONV_BLOCKS
    n = L // P
    depth = norm_mix_pre.shape[0]
    fwd, inv, filt_c, filt_s = [jnp.asarray(a).astype(BF16) for a in _dft_tables(n)]

    h = x.reshape(M, D)
    for i in range(depth):
        j = i // 2
        if i % 2 == 0:
            hfilt = _hyena_filters(L, D, hy_f_w1[j], hy_f_b1[j], hy_f_w2[j], hy_f_b2[j],
                                   hy_f_w3[j], hy_f_b3[j], hy_f_freq[j], hy_f_wout[j])
            gc, gs, hn = _filter_spectra(hfilt, hy_bias[j], filt_c, filt_s, P, h, norm_mix_pre[i])
            z = _hyena_core(hn.reshape(B, L, D), hy_w_in[j], hy_conv_w[j], hy_conv_b[j].reshape(1, -1),
                            gc, gs, fwd, inv, P)
            a, w_o = z.reshape(M, D), hy_w_out[j]
        else:
            q, k, v = _qkv(h, norm_mix_pre[i], at_w_qkv[j], L)
            o = _attention(q.reshape(B, L, -1), k.reshape(B, L, -1), v.reshape(B, L, -1), at_sink[j])
            a, w_o = o.reshape(M, -1), at_w_o[j]
        h = _tail(a, w_o, h, norm_mix_post[i], norm_mlp_pre[i], norm_mlp_post[i], w_up, w_down, i)
    return h.reshape(B, L, D)
```

```python
import functools
import math

import numpy as np
import jax
import jax.numpy as jnp
from jax import lax
from jax.experimental import pallas as pl
from jax.experimental.pallas import tpu as pltpu

BF16 = jnp.bfloat16
F32 = jnp.float32

EPS = 1e-6
NEG = -1e30
LOG2E = math.log2(math.e)

HY_ORDER = 2
HY_EMB = 33
HY_BANDS = (HY_EMB - 1) // 2
HY_DECAY_TARGET = 1e-2
HY_FAST_DECAY = 0.3
HY_SLOW_DECAY = 1.5
HEAD_DIM = 64
KV_GROUP = 4
WINDOW = 128
ROT_DIM = HEAD_DIM // 4
ROPE_THETA = 500000.0

CONV_BLOCKS = 4
CH_BLOCK = 256
SPEC_DTYPE = BF16
STAGE_SLOTS = 4

VMEM_LIMIT = 56 * 1024 * 1024


def _cparams(n_axes, flags=None):
    return pltpu.CompilerParams(
        dimension_semantics=("arbitrary",) * n_axes,
        vmem_limit_bytes=VMEM_LIMIT,
        flags=flags)


def _resident(shape, index_map):
    return pl.BlockSpec(shape, index_map, pipeline_mode=pl.Buffered(1))


def _rms(x, g):
    ms = jnp.mean(x * x, axis=-1, keepdims=True)
    return x * lax.rsqrt(ms + EPS) * g


def _bdot(a, b):
    return jnp.dot(a, b, preferred_element_type=F32)


def _dft_tables(n):
    f = np.arange(n, dtype=np.int64)[:, None]
    r = np.arange(n, dtype=np.int64)[None, :]
    k_pos = ((2 * f + 1) * r) % (4 * n)
    k_neg = ((2 * f + 1) * (r - n)) % (4 * n)
    ang_pos = np.pi * k_pos / (2 * n)
    ang_neg = np.pi * k_neg / (2 * n)
    cpos, spos = np.cos(ang_pos), np.sin(ang_pos)
    cneg, sneg = np.cos(ang_neg), np.sin(ang_neg)
    cneg[:, 0] = 0.0
    sneg[:, 0] = 0.0
    fwd = np.concatenate([cpos, spos], axis=0)
    inv = np.concatenate([cpos.T, spos.T], axis=1) / n
    filt_c = np.concatenate([cneg, cpos], axis=1)
    filt_s = np.concatenate([sneg, spos], axis=1)
    return [np.asarray(a, np.float32) for a in (fwd, inv, filt_c, filt_s)]


def _filter_positions(L):
    t = np.linspace(0.0, 1.0, L)[:, None]
    w = (2.0 * np.pi / L) * np.arange(L)[:, None]
    f = np.linspace(1e-4, HY_BANDS - 1, HY_BANDS)[None, :]
    z = np.concatenate([t, np.cos(f * w), -np.sin(f * w)], axis=-1)
    return np.asarray(z, np.float32)


def _rope_tables(S):
    half = ROT_DIM // 2
    inv = ROPE_THETA ** (-np.arange(0, ROT_DIM, 2, dtype=np.float64) / ROT_DIM)
    ang = np.arange(S, dtype=np.float64)[:, None] * inv[None, :]
    lane = np.arange(128) % HEAD_DIM
    ang_l = ang[:, lane % half]
    cos_t = np.where(lane[None, :] < ROT_DIM, np.cos(ang_l), 1.0)
    sin_lo = np.where(lane[None, :] < half, -np.sin(ang_l), 0.0)
    sin_hi = np.where((lane[None, :] >= half) & (lane[None, :] < ROT_DIM), np.sin(ang_l), 0.0)
    return [np.asarray(a, np.float32) for a in (cos_t, sin_lo, sin_hi)]


def _spectra_kernel(z_ref, w1_ref, b1_ref, w2_ref, b2_ref, w3_ref, b3_ref, fr_ref,
                    woutf_ref, woutb_ref, dl_ref, bias_ref, mc_ref, ms_ref, x_ref, g_ref,
                    gc_ref, gs_ref, hn_ref, hid_ref, hf_ref, hb_ref, *, P, n, norm_rows):
    hi = lax.Precision.HIGHEST

    @pl.when(pl.program_id(0) == 0)
    def _():
        fr = fr_ref[...]
        h = jnp.sin(fr * (jnp.dot(z_ref[...], w1_ref[...], precision=hi) + b1_ref[...]))
        h = jnp.sin(fr * (jnp.dot(h, w2_ref[...], precision=hi) + b2_ref[...]))
        h = jnp.sin(fr * (jnp.dot(h, w3_ref[...], precision=hi) + b3_ref[...]))
        hid_ref[...] = h

    L, cw = hf_ref.shape
    row = lax.broadcasted_iota(jnp.int32, (L, cw), 0)
    decay = jnp.exp(-(row.astype(F32) * (1.0 / (L - 1))) * dl_ref[...])
    hid = hid_ref[...]
    hid_hi = hid.astype(BF16)
    hid_lo = (hid - hid_hi.astype(F32)).astype(BF16)

    def taps(wout_ref):
        wout = wout_ref[...]
        w_hi = wout.astype(BF16)
        w_lo = (wout - w_hi.astype(F32)).astype(BF16)
        return (_bdot(hid_hi, w_hi) + (_bdot(hid_hi, w_lo) + _bdot(hid_lo, w_hi))) * decay

    hf_ref[...] = taps(woutf_ref).astype(hf_ref.dtype)
    hb_ref[...] = jnp.where(row > 0, taps(woutb_ref), 0.0).astype(hb_ref.dtype)

    mc = mc_ref[...]
    ms = ms_ref[...]
    cpos = mc[:, n:]
    spos = ms[:, n:]
    norm_chunks = [slice(r, r + norm_rows) for r in range(0, x_ref.shape[0], norm_rows)]
    nd = 2 * P - 1
    for idx, d in enumerate(range(-(P - 1), P)):
        for r in norm_chunks[idx * len(norm_chunks) // nd:(idx + 1) * len(norm_chunks) // nd]:
            hn_ref[r, :] = _rms(x_ref[r, :], g_ref[...]).astype(hn_ref.dtype)
        if d == 0:
            hf0 = hf_ref[0:n, :].astype(F32)
            hb0 = hb_ref[0:n, :].astype(F32)
            gc = _bdot(cpos, (hf0 + hb0).astype(BF16)) + bias_ref[...]
            gs = _bdot(spos, (hf0 - hb0).astype(BF16))
        elif d > 0:
            taps = hf_ref[n * (d - 1):n * (d + 1), :].astype(BF16)
            gc = _bdot(mc, taps)
            gs = _bdot(ms, taps)
        else:
            a = -d
            taps = hb_ref[n * (a - 1):n * (a + 1), :].astype(BF16)
            gc = _bdot(mc, taps)
            gs = -_bdot(ms, taps)
        gc_ref[d + P - 1] = gc.astype(gc_ref.dtype)
        gs_ref[d + P - 1] = gs.astype(gs_ref.dtype)


def _filter_spectra(L, f_w1, f_b1, f_w2, f_b2, f_w3, f_b3, f_freq, f_wout, bias, mc, ms, P, x2d, g,
                    cw=256, norm_rows=256):
    M, D = x2d.shape
    FW = f_w1.shape[1]
    n = L // P
    half = f_wout.shape[1] // 2
    nblk = half // cw
    xm = M // nblk
    z = jnp.asarray(np.pad(_filter_positions(L), ((0, 0), (0, FW - HY_EMB))))
    w1 = jnp.pad(f_w1, ((0, FW - HY_EMB), (0, 0)))
    max_decay = math.log(HY_DECAY_TARGET) / HY_FAST_DECAY
    min_decay = math.log(HY_DECAY_TARGET) / HY_SLOW_DECAY
    absdelta = jnp.asarray(np.abs(np.linspace(min_decay, max_decay, D)).astype(np.float32))[None, :]
    small = lambda shape: pl.BlockSpec(shape, lambda j: (0, 0))
    out_sd = jax.ShapeDtypeStruct((2 * P - 1, n, half), SPEC_DTYPE)
    return pl.pallas_call(
        functools.partial(_spectra_kernel, P=P, n=n, norm_rows=norm_rows),
        grid=(nblk,),
        in_specs=[small((L, FW)), small((FW, FW)), small((1, FW)), small((FW, FW)), small((1, FW)),
                  small((FW, FW)), small((1, FW)), small((1, FW)),
                  pl.BlockSpec((FW, cw), lambda j: (0, j)),
                  pl.BlockSpec((FW, cw), lambda j: (0, nblk + j)),
                  pl.BlockSpec((1, cw), lambda j: (0, j % (D // cw))),
                  pl.BlockSpec((1, cw), lambda j: (0, j)),
                  small((n, 2 * n)), small((n, 2 * n)),
                  pl.BlockSpec((xm, D), lambda j: (j, 0)),
                  small((1, D))],
        out_specs=[pl.BlockSpec((2 * P - 1, n, cw), lambda j: (0, 0, j)),
                   pl.BlockSpec((2 * P - 1, n, cw), lambda j: (0, 0, j)),
                   pl.BlockSpec((xm, D), lambda j: (j, 0))],
        out_shape=[out_sd, out_sd, jax.ShapeDtypeStruct((M, D), BF16)],
        scratch_shapes=[pltpu.VMEM((L, FW), F32),
                        pltpu.VMEM((L, cw), BF16),
                        pltpu.VMEM((L, cw), BF16)],
        compiler_params=_cparams(1),
        name="filter_spectra",
    )(z, w1, f_b1.reshape(1, FW), f_w2, f_b2.reshape(1, FW), f_w3, f_b3.reshape(1, FW),
      f_freq.reshape(1, FW), f_wout, f_wout, absdelta, bias.reshape(1, half), mc, ms,
      x2d, g.reshape(1, D))


def _hyena_kernel(hn_ref, wv_ref, w1_ref, w2_ref, cwv_ref, cw1_ref, cw2_ref,
                  cbv_ref, cb1_ref, cb2_ref, gc0_ref, gs0_ref, gc1_ref, gs1_ref,
                  fwd_ref, inv_ref, z_ref,
                  rawa_ref, rawb_ref, v_ref, gatea_ref, gateb_ref, uspec_ref, yspec_ref,
                  *, P, n, fchunk):
    row = lax.broadcasted_iota(jnp.int32, (n, 1), 0)
    blk = [slice(n * i, n * (i + 1)) for i in range(P)]
    chunks = n // fchunk
    fwd = fwd_ref[...]
    inv = inv_ref[...]

    def project(w_ref, dst_ref, i):
        dst_ref[blk[i], :] = _bdot(hn_ref[0, blk[i], :], w_ref[...].astype(BF16))

    def short_conv(raw_ref, cw_ref, cb_ref, dst, i):
        cw = cw_ref[...]
        cur = raw_ref[blk[i], :]
        prev = pltpu.roll(cur, 1, axis=0)
        nxt = pltpu.roll(cur, n - 1, axis=0)
        first = raw_ref[n * i - 1:n * i, :] if i > 0 else jnp.zeros_like(cur[0:1])
        last = raw_ref[n * (i + 1):n * (i + 1) + 1, :] if i < P - 1 else jnp.zeros_like(cur[0:1])
        prev = jnp.where(row == 0, first, prev)
        nxt = jnp.where(row == n - 1, last, nxt)
        out = prev * cw[0:1] + cur * cw[1:2] + nxt * cw[2:3] + cb_ref[...]
        dst[blk[i], :] = out.astype(dst.dtype)

    def forward(src_ref, j):
        uspec_ref[j] = _bdot(fwd, src_ref[blk[j], :]).astype(uspec_ref.dtype)

    def combine(gc_ref, gs_ref, c):
        fc = slice(c * fchunk, (c + 1) * fchunk)
        fs = slice(n + c * fchunk, n + (c + 1) * fchunk)
        uc = [uspec_ref[j, fc, :] for j in range(P)]
        us = [uspec_ref[j, fs, :] for j in range(P)]
        for i in range(P):
            yc = None
            ys = None
            for j in range(P):
                gc = gc_ref[i - j + P - 1, fc, :]
                gs = gs_ref[i - j + P - 1, fc, :]
                tc = gc * uc[j] - gs * us[j]
                ts = gc * us[j] + gs * uc[j]
                yc = tc if yc is None else yc + tc
                ys = ts if ys is None else ys + ts
            yspec_ref[i, fc, :] = yc.astype(BF16)
            yspec_ref[i, fs, :] = ys.astype(BF16)

    for i in range(P):
        project(wv_ref, rawa_ref, i)
    for i in range(P):
        short_conv(rawa_ref, cwv_ref, cbv_ref, v_ref, i)
        project(w1_ref, rawb_ref, i)
    for i in range(P):
        forward(v_ref, i)
        short_conv(rawb_ref, cw1_ref, cb1_ref, gatea_ref, i)
    for c in range(chunks):
        combine(gc0_ref, gs0_ref, c)
        if c % (chunks // P) == 0:
            project(w2_ref, rawa_ref, c // (chunks // P))
    for i in range(P):
        short_conv(rawa_ref, cw2_ref, cb2_ref, gateb_ref, i)
        v_ref[blk[i], :] = (gatea_ref[blk[i], :] * _bdot(inv, yspec_ref[i])).astype(v_ref.dtype)
    for i in range(P):
        forward(v_ref, i)
    for c in range(chunks):
        combine(gc1_ref, gs1_ref, c)
    for i in range(P):
        z_ref[0, blk[i], :] = (gateb_ref[blk[i], :] * _bdot(inv, yspec_ref[i])).astype(z_ref.dtype)


def _hyena_core(hn, w_in, conv_w, conv_b, gc, gs, fwd, inv, P):
    B, L, D = hn.shape
    n = L // P
    cb_n = D // CH_BLOCK
    C = CH_BLOCK
    nd = 2 * P - 1

    def stream_spec(shape, s):
        return pl.BlockSpec(shape, lambda c, b: (0, s * cb_n + c))

    in_specs = (
        [pl.BlockSpec((1, L, D), lambda c, b: (b, 0, 0))]
        + [stream_spec((D, C), s) for s in range(3)]
        + [stream_spec((3, C), s) for s in range(3)]
        + [stream_spec((1, C), s) for s in range(3)]
        + [_resident((nd, n, C), lambda c, b: (0, 0, c)),
           _resident((nd, n, C), lambda c, b: (0, 0, c)),
           _resident((nd, n, C), lambda c, b: (0, 0, cb_n + c)),
           _resident((nd, n, C), lambda c, b: (0, 0, cb_n + c)),
           pl.BlockSpec((2 * n, n), lambda c, b: (0, 0)),
           pl.BlockSpec((n, 2 * n), lambda c, b: (0, 0))])
    return pl.pallas_call(
        functools.partial(_hyena_kernel, P=P, n=n, fchunk=16),
        grid=(cb_n, B),
        in_specs=in_specs,
        out_specs=pl.BlockSpec((1, L, C), lambda c, b: (b, 0, c)),
        out_shape=jax.ShapeDtypeStruct((B, L, D), BF16),
        scratch_shapes=[pltpu.VMEM((L, C), F32),
                        pltpu.VMEM((L, C), F32),
                        pltpu.VMEM((L, C), BF16),
                        pltpu.VMEM((L, C), F32),
                        pltpu.VMEM((L, C), F32),
                        pltpu.VMEM((P, 2 * n, C), SPEC_DTYPE),
                        pltpu.VMEM((P, 2 * n, C), BF16)],
        compiler_params=_cparams(2),
        name="hyena_core",
    )(hn, w_in, w_in, w_in, conv_w, conv_w, conv_w, conv_b, conv_b, conv_b,
      gc, gs, gc, gs, fwd, inv)


def _stream_cast(src_hbm, layer, dst_ref, stage_ref, sem_ref):
    slots, rows = stage_ref.shape[0], stage_ref.shape[1]
    n_chunks = dst_ref.shape[0] // rows

    def copy(c):
        return pltpu.make_async_copy(src_hbm.at[layer, pl.ds(c * rows, rows), :],
                                     stage_ref.at[c % slots], sem_ref.at[c % slots])

    for c in range(min(slots - 1, n_chunks)):
        copy(c).start()
    for c in range(n_chunks):
        ahead = c + slots - 1
        if ahead < n_chunks:
            copy(ahead).start()
        copy(c).wait()
        dst_ref[c * rows:(c + 1) * rows, :] = stage_ref[c % slots].astype(dst_ref.dtype)


def _tail_kernel(a_ref, wo32_ref, r_ref, gmix_ref, gpre_ref, gpost_ref, wu_hbm, wd_hbm, o_ref,
                 wo_ref, wu_ref, wd_ref, stage_u, stage_d, sem_u, sem_d, *, fchunk, layer):
    @pl.when(pl.program_id(0) == 0)
    def _():
        wo_ref[...] = wo32_ref[...].astype(wo_ref.dtype)
        _stream_cast(wu_hbm, layer, wu_ref, stage_u, sem_u)
        _stream_cast(wd_hbm, layer, wd_ref, stage_d, sem_d)

    slabs = 2
    sm = a_ref.shape[0] // slabs
    rows = [slice(sm * u, sm * (u + 1)) for u in range(slabs)]
    FF = wu_ref.shape[1]
    ms = [_bdot(a_ref[r, :], wo_ref[...]) for r in rows]
    hs = [r_ref[r, :] + _rms(m, gmix_ref[...]) for r, m in zip(rows, ms)]
    hns = [_rms(h, gpre_ref[...]).astype(BF16) for h in hs]
    accs = [None] * slabs
    for c in range(FF // fchunk):
        acts = []
        for u in range(slabs):
            a = jnp.maximum(_bdot(hns[u], wu_ref[:, c * fchunk:(c + 1) * fchunk]), 0.0)
            acts.append((a * a).astype(BF16))
        for u in range(slabs):
            part = _bdot(acts[u], wd_ref[c * fchunk:(c + 1) * fchunk, :])
            accs[u] = part if accs[u] is None else accs[u] + part
    for u in range(slabs):
        o_ref[rows[u], :] = hs[u] + _rms(accs[u], gpost_ref[...])


def _tail(a, w_o, resid, g_mix, g_pre, g_post, w_up_all, w_down_all, layer, tm=512, fchunk=1024):
    M, K = a.shape
    D = w_o.shape[1]
    FF = w_up_all.shape[2]
    stage_bytes = 1024 * 1024
    rows_u = stage_bytes // (FF * 4)
    rows_d = stage_bytes // (D * 4)
    row = lambda shape: pl.BlockSpec(shape, lambda i: (i, 0))
    gain = pl.BlockSpec((1, D), lambda i: (0, 0))
    return pl.pallas_call(
        functools.partial(_tail_kernel, fchunk=fchunk, layer=layer),
        grid=(M // tm,),
        in_specs=[row((tm, K)), _resident((K, D), lambda i: (0, 0)), row((tm, D)),
                  gain, gain, gain,
                  pl.BlockSpec(memory_space=pl.ANY),
                  pl.BlockSpec(memory_space=pl.ANY)],
        out_specs=row((tm, D)),
        out_shape=jax.ShapeDtypeStruct((M, D), F32),
        scratch_shapes=[pltpu.VMEM((K, D), BF16),
                        pltpu.VMEM((D, FF), BF16),
                        pltpu.VMEM((FF, D), BF16),
                        pltpu.VMEM((STAGE_SLOTS, rows_u, FF), F32),
                        pltpu.VMEM((STAGE_SLOTS, rows_d, D), F32),
                        pltpu.SemaphoreType.DMA((STAGE_SLOTS,)),
                        pltpu.SemaphoreType.DMA((STAGE_SLOTS,))],
        compiler_params=_cparams(1),
        name="mixer_tail_mlp",
    )(a, w_o, resid, g_mix.reshape(1, D), g_pre.reshape(1, D), g_post.reshape(1, D),
      w_up_all, w_down_all)


def _qkv_kernel(h_ref, g_ref, w32_ref, ct_ref, slo_ref, shi_ref, q_ref, k_ref, v_ref, w_ref, *, halves):
    @pl.when(pl.program_id(0) == 0)
    def _():
        w_ref[...] = w32_ref[...].astype(w_ref.dtype)

    nq, nk = q_ref.shape[1], k_ref.shape[1]
    hm = h_ref.shape[0] // halves
    rows = [slice(hm * u, hm * (u + 1)) for u in range(halves)]
    half = ROT_DIM // 2
    scale = LOG2E * HEAD_DIM ** -0.5
    qkvs = [_bdot(_rms(h_ref[r, :], g_ref[...]).astype(BF16), w_ref[...]) for r in rows]
    for r, qkv in zip(rows, qkvs):
        ct = ct_ref[r, :]
        slo = slo_ref[r, :]
        shi = shi_ref[r, :]
        for j in range((nq + nk) // 128):
            t = qkv[:, 128 * j:128 * (j + 1)]
            rot = t * ct + pltpu.roll(t, 128 - half, axis=1) * slo + pltpu.roll(t, half, axis=1) * shi
            if 128 * j < nq:
                q_ref[r, 128 * j:128 * (j + 1)] = (rot * scale).astype(q_ref.dtype)
            else:
                k_ref[r, 128 * j - nq:128 * (j + 1) - nq] = rot.astype(k_ref.dtype)
        v_ref[r, :] = qkv[:, nq + nk:].astype(v_ref.dtype)


def _qkv(h, g, w, S, tm=1024, halves=2):
    M, D = h.shape
    nq = D
    nk = (w.shape[1] - nq) // 2
    ct, slo, shi = [jnp.asarray(a) for a in _rope_tables(S)]
    pb = S // tm
    tab = pl.BlockSpec((tm, 128), lambda i: (i % pb, 0))
    return pl.pallas_call(
        functools.partial(_qkv_kernel, halves=halves),
        grid=(M // tm,),
        in_specs=[pl.BlockSpec((tm, D), lambda i: (i, 0)),
                  pl.BlockSpec((1, D), lambda i: (0, 0)),
                  _resident((D, nq + 2 * nk), lambda i: (0, 0)),
                  tab, tab, tab],
        out_specs=[pl.BlockSpec((tm, nq), lambda i: (i, 0)),
                   pl.BlockSpec((tm, nk), lambda i: (i, 0)),
                   pl.BlockSpec((tm, nk), lambda i: (i, 0))],
        out_shape=[jax.ShapeDtypeStruct((M, nq), BF16),
                   jax.ShapeDtypeStruct((M, nk), BF16),
                   jax.ShapeDtypeStruct((M, nk), BF16)],
        scratch_shapes=[pltpu.VMEM(w.shape, BF16)],
        compiler_params=_cparams(1),
        name="qkv_rope",
    )(h, g.reshape(1, D), w, ct, slo, shi)


def _attn_kernel(sink_ref, q_ref, k_ref, v_ref, o_ref, krep_ref, vrep_ref, *, S, n_kv):
    BQ = WINDOW
    BK = 3 * WINDOW
    GW = KV_GROUP * HEAD_DIM
    dn = (((1,), (1,)), ((), ()))

    for g in range(n_kv):
        kg = k_ref[0, :, g * HEAD_DIM:(g + 1) * HEAD_DIM]
        vg = v_ref[0, :, g * HEAD_DIM:(g + 1) * HEAD_DIM]
        krep_ref[g] = jnp.concatenate([kg] * KV_GROUP, axis=1)
        vrep_ref[g] = jnp.concatenate([vg] * KV_GROUP, axis=1)

    lane = lax.broadcasted_iota(jnp.int32, (BQ, GW), 1)
    head_lanes = [(lane >= hh * HEAD_DIM) & (lane < (hh + 1) * HEAD_DIM) for hh in range(KV_GROUP)]
    rowblk = lax.broadcasted_iota(jnp.int32, (KV_GROUP * BQ, 1), 0) // BQ

    def body(qb, carry):
        qs = pl.multiple_of(qb * BQ, BQ)
        ks = pl.multiple_of(jnp.clip(qs - WINDOW, 0, S - BK), WINDOW)
        qpos = qs + lax.broadcasted_iota(jnp.int32, (BQ, BK), 0)
        kpos = ks + lax.broadcasted_iota(jnp.int32, (BQ, BK), 1)
        maskadd = jnp.where(jnp.abs(kpos - qpos) <= WINDOW, 0.0, NEG).astype(F32)
        maskadd = jnp.concatenate([maskadd] * KV_GROUP, axis=0)
        for g in range(n_kv):
            qg = q_ref[0, pl.ds(qs, BQ), g * GW:(g + 1) * GW]
            qst = jnp.concatenate([jnp.where(head_lanes[hh], qg, jnp.zeros_like(qg))
                                   for hh in range(KV_GROUP)], axis=0)
            sink = jnp.zeros((KV_GROUP * BQ, 1), F32)
            for hh in range(KV_GROUP):
                sink = jnp.where(rowblk == hh, sink_ref[g * KV_GROUP + hh] * LOG2E, sink)
            s = lax.dot_general(qst, krep_ref[g, pl.ds(ks, BK), :], dn,
                                preferred_element_type=F32) + maskadd
            m = jnp.maximum(jnp.max(s, axis=-1, keepdims=True), sink)
            p = jnp.exp2(s - m)
            denom = jnp.sum(p, axis=-1, keepdims=True) + jnp.exp2(sink - m)
            ost = _bdot(p.astype(BF16), vrep_ref[g, pl.ds(ks, BK), :]) / denom
            o = jnp.where(head_lanes[0], ost[0:BQ], 0.0)
            for hh in range(1, KV_GROUP):
                o = jnp.where(head_lanes[hh], ost[hh * BQ:(hh + 1) * BQ], o)
            o_ref[0, pl.ds(qs, BQ), g * GW:(g + 1) * GW] = o.astype(o_ref.dtype)
        return carry

    lax.fori_loop(0, S // BQ, body, 0, unroll=4)


def _attention(q, k, v, sink):
    B, S, NQ = q.shape
    NK = k.shape[2]
    n_kv = NK // HEAD_DIM
    return pl.pallas_call(
        functools.partial(_attn_kernel, S=S, n_kv=n_kv),
        grid=(B,),
        in_specs=[pl.BlockSpec(memory_space=pltpu.SMEM),
                  pl.BlockSpec((1, S, NQ), lambda b: (b, 0, 0)),
                  pl.BlockSpec((1, S, NK), lambda b: (b, 0, 0)),
                  pl.BlockSpec((1, S, NK), lambda b: (b, 0, 0))],
        out_specs=pl.BlockSpec((1, S, NQ), lambda b: (b, 0, 0)),
        out_shape=jax.ShapeDtypeStruct((B, S, NQ), BF16),
        scratch_shapes=[pltpu.VMEM((n_kv, S, KV_GROUP * HEAD_DIM), BF16),
                        pltpu.VMEM((n_kv, S, KV_GROUP * HEAD_DIM), BF16)],
        compiler_params=_cparams(1),
        name="window_attention",
    )(sink, q, k, v)


def kernel(x, norm_mix_pre, norm_mix_post, norm_mlp_pre, norm_mlp_post, w_up, w_down, hy_w_in, hy_conv_w, hy_conv_b, hy_f_w1, hy_f_b1, hy_f_w2, hy_f_b2, hy_f_w3, hy_f_b3, hy_f_freq, hy_f_wout, hy_bias, hy_w_out, at_w_qkv, at_sink, at_w_o):
    B, L, D = x.shape
    M = B * L
    P = CONV_BLOCKS
    n = L // P
    depth = norm_mix_pre.shape[0]
    fwd, inv, filt_c, filt_s = [jnp.asarray(a).astype(BF16) for a in _dft_tables(n)]

    h = x.reshape(M, D)
    for i in range(depth):
        j = i // 2
        if i % 2 == 0:
            gc, gs, hn = _filter_spectra(L, hy_f_w1[j], hy_f_b1[j], hy_f_w2[j], hy_f_b2[j],
                                         hy_f_w3[j], hy_f_b3[j], hy_f_freq[j], hy_f_wout[j],
                                         hy_bias[j], filt_c, filt_s, P, h, norm_mix_pre[i])
            z = _hyena_core(hn.reshape(B, L, D), hy_w_in[j], hy_conv_w[j], hy_conv_b[j].reshape(1, -1),
                            gc, gs, fwd, inv, P)
            a, w_o = z.reshape(M, D), hy_w_out[j]
        else:
            q, k, v = _qkv(h, norm_mix_pre[i], at_w_qkv[j], L)
            o = _attention(q.reshape(B, L, -1), k.reshape(B, L, -1), v.reshape(B, L, -1), at_sink[j])
            a, w_o = o.reshape(M, -1), at_w_o[j]
        h = _tail(a, w_o, h, norm_mix_post[i], norm_mlp_pre[i], norm_mlp_post[i], w_up, w_down, i)
    return h.reshape(B, L, D)
```

```python
import functools
import math

import numpy as np
import jax
import jax.numpy as jnp
from jax import lax
from jax.experimental import pallas as pl
from jax.experimental.pallas import tpu as pltpu

BF16 = jnp.bfloat16
F32 = jnp.float32

EPS = 1e-6
NEG = -1e30
LOG2E = math.log2(math.e)

HY_ORDER = 2
HY_EMB = 33
HY_BANDS = (HY_EMB - 1) // 2
HY_DECAY_TARGET = 1e-2
HY_FAST_DECAY = 0.3
HY_SLOW_DECAY = 1.5
HEAD_DIM = 64
KV_GROUP = 4
WINDOW = 128
ROT_DIM = HEAD_DIM // 4
ROPE_THETA = 500000.0

CONV_BLOCKS = 4
CH_BLOCK = 256
SPEC_DTYPE = BF16
STAGE_SLOTS = 4

VMEM_LIMIT = 56 * 1024 * 1024


def _cparams(n_axes, flags=None):
    return pltpu.CompilerParams(
        dimension_semantics=("arbitrary",) * n_axes,
        vmem_limit_bytes=VMEM_LIMIT,
        flags=flags)


def _resident(shape, index_map):
    return pl.BlockSpec(shape, index_map, pipeline_mode=pl.Buffered(1))


def _rms(x, g):
    ms = jnp.mean(x * x, axis=-1, keepdims=True)
    return x * lax.rsqrt(ms + EPS) * g


def _bdot(a, b):
    return jnp.dot(a, b, preferred_element_type=F32)


def _dft_tables(n):
    f = np.arange(n, dtype=np.int64)[:, None]
    r = np.arange(n, dtype=np.int64)[None, :]
    k_pos = ((2 * f + 1) * r) % (4 * n)
    k_neg = ((2 * f + 1) * (r - n)) % (4 * n)
    ang_pos = np.pi * k_pos / (2 * n)
    ang_neg = np.pi * k_neg / (2 * n)
    cpos, spos = np.cos(ang_pos), np.sin(ang_pos)
    cneg, sneg = np.cos(ang_neg), np.sin(ang_neg)
    cneg[:, 0] = 0.0
    sneg[:, 0] = 0.0
    fwd = np.concatenate([cpos, spos], axis=0)
    inv = np.concatenate([cpos.T, spos.T], axis=1) / n
    filt_c = np.concatenate([cneg, cpos], axis=1)
    filt_s = np.concatenate([sneg, spos], axis=1)
    return [np.asarray(a, np.float32) for a in (fwd, inv, filt_c, filt_s)]


def _filter_positions(L):
    t = np.linspace(0.0, 1.0, L)[:, None]
    w = (2.0 * np.pi / L) * np.arange(L)[:, None]
    f = np.linspace(1e-4, HY_BANDS - 1, HY_BANDS)[None, :]
    z = np.concatenate([t, np.cos(f * w), -np.sin(f * w)], axis=-1)
    return np.asarray(z, np.float32)


def _rope_tables(S):
    half = ROT_DIM // 2
    inv = ROPE_THETA ** (-np.arange(0, ROT_DIM, 2, dtype=np.float64) / ROT_DIM)
    ang = np.arange(S, dtype=np.float64)[:, None] * inv[None, :]
    lane = np.arange(128) % HEAD_DIM
    ang_l = ang[:, lane % half]
    cos_t = np.where(lane[None, :] < ROT_DIM, np.cos(ang_l), 1.0)
    sin_lo = np.where(lane[None, :] < half, -np.sin(ang_l), 0.0)
    sin_hi = np.where((lane[None, :] >= half) & (lane[None, :] < ROT_DIM), np.sin(ang_l), 0.0)
    return [np.asarray(a, np.float32) for a in (cos_t, sin_lo, sin_hi)]


def _spectra_kernel(z_ref, w1_ref, b1_ref, w2_ref, b2_ref, w3_ref, b3_ref, fr_ref,
                    woutf_ref, woutb_ref, dl_ref, bias_ref, mc_ref, ms_ref, x_ref, g_ref,
                    gc_ref, gs_ref, hn_ref, hid_ref, hf_ref, hb_ref, *, P, n, norm_rows):
    hi = lax.Precision.HIGHEST

    @pl.when(pl.program_id(0) == 0)
    def _():
        fr = fr_ref[...]
        h = jnp.sin(fr * (jnp.dot(z_ref[...], w1_ref[...], precision=hi) + b1_ref[...]))
        h = jnp.sin(fr * (jnp.dot(h, w2_ref[...], precision=hi) + b2_ref[...]))
        h = jnp.sin(fr * (jnp.dot(h, w3_ref[...], precision=hi) + b3_ref[...]))
        hid_ref[...] = h

    L, cw = hf_ref.shape
    row = lax.broadcasted_iota(jnp.int32, (L, cw), 0)
    decay = jnp.exp(-(row.astype(F32) * (1.0 / (L - 1))) * dl_ref[...])
    hid = hid_ref[...]
    hid_hi = hid.astype(BF16)
    hid_lo = (hid - hid_hi.astype(F32)).astype(BF16)

    def taps(wout_ref):
        wout = wout_ref[...]
        w_hi = wout.astype(BF16)
        w_lo = (wout - w_hi.astype(F32)).astype(BF16)
        return (_bdot(hid_hi, w_hi) + (_bdot(hid_hi, w_lo) + _bdot(hid_lo, w_hi))) * decay

    hf_ref[...] = taps(woutf_ref).astype(hf_ref.dtype)
    hb_ref[...] = jnp.where(row > 0, taps(woutb_ref), 0.0).astype(hb_ref.dtype)

    mc = mc_ref[...]
    ms = ms_ref[...]
    cpos = mc[:, n:]
    spos = ms[:, n:]
    norm_chunks = [slice(r, r + norm_rows) for r in range(0, x_ref.shape[0], norm_rows)]
    nd = 2 * P - 1
    for idx, d in enumerate(range(-(P - 1), P)):
        for r in norm_chunks[idx * len(norm_chunks) // nd:(idx + 1) * len(norm_chunks) // nd]:
            hn_ref[r, :] = _rms(x_ref[r, :], g_ref[...]).astype(hn_ref.dtype)
        if d == 0:
            hf0 = hf_ref[0:n, :].astype(F32)
            hb0 = hb_ref[0:n, :].astype(F32)
            gc = _bdot(cpos, (hf0 + hb0).astype(BF16)) + bias_ref[...]
            gs = _bdot(spos, (hf0 - hb0).astype(BF16))
        elif d > 0:
            taps = hf_ref[n * (d - 1):n * (d + 1), :].astype(BF16)
            gc = _bdot(mc, taps)
            gs = _bdot(ms, taps)
        else:
            a = -d
            taps = hb_ref[n * (a - 1):n * (a + 1), :].astype(BF16)
            gc = _bdot(mc, taps)
            gs = -_bdot(ms, taps)
        gc_ref[d + P - 1] = gc.astype(gc_ref.dtype)
        gs_ref[d + P - 1] = gs.astype(gs_ref.dtype)


def _filter_spectra(L, f_w1, f_b1, f_w2, f_b2, f_w3, f_b3, f_freq, f_wout, bias, mc, ms, P, x2d, g,
                    cw=256, norm_rows=256):
    M, D = x2d.shape
    FW = f_w1.shape[1]
    n = L // P
    half = f_wout.shape[1] // 2
    nblk = half // cw
    xm = M // nblk
    z = jnp.asarray(np.pad(_filter_positions(L), ((0, 0), (0, FW - HY_EMB))))
    w1 = jnp.pad(f_w1, ((0, FW - HY_EMB), (0, 0)))
    max_decay = math.log(HY_DECAY_TARGET) / HY_FAST_DECAY
    min_decay = math.log(HY_DECAY_TARGET) / HY_SLOW_DECAY
    absdelta = jnp.asarray(np.abs(np.linspace(min_decay, max_decay, D)).astype(np.float32))[None, :]
    small = lambda shape: pl.BlockSpec(shape, lambda j: (0, 0))
    out_sd = jax.ShapeDtypeStruct((2 * P - 1, n, half), SPEC_DTYPE)
    return pl.pallas_call(
        functools.partial(_spectra_kernel, P=P, n=n, norm_rows=norm_rows),
        grid=(nblk,),
        in_specs=[small((L, FW)), small((FW, FW)), small((1, FW)), small((FW, FW)), small((1, FW)),
                  small((FW, FW)), small((1, FW)), small((1, FW)),
                  pl.BlockSpec((FW, cw), lambda j: (0, j)),
                  pl.BlockSpec((FW, cw), lambda j: (0, nblk + j)),
                  pl.BlockSpec((1, cw), lambda j: (0, j % (D // cw))),
                  pl.BlockSpec((1, cw), lambda j: (0, j)),
                  small((n, 2 * n)), small((n, 2 * n)),
                  pl.BlockSpec((xm, D), lambda j: (j, 0)),
                  small((1, D))],
        out_specs=[pl.BlockSpec((2 * P - 1, n, cw), lambda j: (0, 0, j)),
                   pl.BlockSpec((2 * P - 1, n, cw), lambda j: (0, 0, j)),
                   pl.BlockSpec((xm, D), lambda j: (j, 0))],
        out_shape=[out_sd, out_sd, jax.ShapeDtypeStruct((M, D), BF16)],
        scratch_shapes=[pltpu.VMEM((L, FW), F32),
                        pltpu.VMEM((L, cw), BF16),
                        pltpu.VMEM((L, cw), BF16)],
        compiler_params=_cparams(1),
        name="filter_spectra",
    )(z, w1, f_b1.reshape(1, FW), f_w2, f_b2.reshape(1, FW), f_w3, f_b3.reshape(1, FW),
      f_freq.reshape(1, FW), f_wout, f_wout, absdelta, bias.reshape(1, half), mc, ms,
      x2d, g.reshape(1, D))


def _hyena_kernel(hn_ref, wv_ref, w1_ref, w2_ref, cwv_ref, cw1_ref, cw2_ref,
                  cbv_ref, cb1_ref, cb2_ref, gc0_ref, gs0_ref, gc1_ref, gs1_ref,
                  fwd_ref, inv_ref, z_ref,
                  rawa_ref, rawb_ref, v_ref, gatea_ref, gateb_ref, uspec_ref, yspec_ref,
                  *, P, n, fchunk):
    row = lax.broadcasted_iota(jnp.int32, (n, 1), 0)
    blk = [slice(n * i, n * (i + 1)) for i in range(P)]
    chunks = n // fchunk
    fwd = fwd_ref[...]
    inv = inv_ref[...]

    def project(w_ref, dst_ref, i):
        dst_ref[blk[i], :] = _bdot(hn_ref[0, blk[i], :], w_ref[...].astype(BF16))

    def short_conv(raw_ref, cw_ref, cb_ref, dst, i):
        cw = cw_ref[...]
        cur = raw_ref[blk[i], :]
        prev = pltpu.roll(cur, 1, axis=0)
        nxt = pltpu.roll(cur, n - 1, axis=0)
        first = raw_ref[n * i - 1:n * i, :] if i > 0 else jnp.zeros_like(cur[0:1])
        last = raw_ref[n * (i + 1):n * (i + 1) + 1, :] if i < P - 1 else jnp.zeros_like(cur[0:1])
        prev = jnp.where(row == 0, first, prev)
        nxt = jnp.where(row == n - 1, last, nxt)
        out = prev * cw[0:1] + cur * cw[1:2] + nxt * cw[2:3] + cb_ref[...]
        dst[blk[i], :] = out.astype(dst.dtype)

    def forward(src_ref, j):
        uspec_ref[j] = _bdot(fwd, src_ref[blk[j], :]).astype(uspec_ref.dtype)

    def combine(gc_ref, gs_ref, c):
        fc = slice(c * fchunk, (c + 1) * fchunk)
        fs = slice(n + c * fchunk, n + (c + 1) * fchunk)
        uc = [uspec_ref[j, fc, :] for j in range(P)]
        us = [uspec_ref[j, fs, :] for j in range(P)]
        for i in range(P):
            yc = None
            ys = None
            for j in range(P):
                gc = gc_ref[i - j + P - 1, fc, :]
                gs = gs_ref[i - j + P - 1, fc, :]
                tc = gc * uc[j] - gs * us[j]
                ts = gc * us[j] + gs * uc[j]
                yc = tc if yc is None else yc + tc
                ys = ts if ys is None else ys + ts
            yspec_ref[i, fc, :] = yc.astype(BF16)
            yspec_ref[i, fs, :] = ys.astype(BF16)

    for i in range(P):
        project(wv_ref, rawa_ref, i)
    for i in range(P):
        short_conv(rawa_ref, cwv_ref, cbv_ref, v_ref, i)
        project(w1_ref, rawb_ref, i)
    for i in range(P):
        forward(v_ref, i)
        short_conv(rawb_ref, cw1_ref, cb1_ref, gatea_ref, i)
    for c in range(chunks):
        combine(gc0_ref, gs0_ref, c)
        if c % (chunks // P) == 0:
            project(w2_ref, rawa_ref, c // (chunks // P))
    for i in range(P):
        short_conv(rawa_ref, cw2_ref, cb2_ref, gateb_ref, i)
        v_ref[blk[i], :] = (gatea_ref[blk[i], :] * _bdot(inv, yspec_ref[i])).astype(v_ref.dtype)
    for i in range(P):
        forward(v_ref, i)
    for c in range(chunks):
        combine(gc1_ref, gs1_ref, c)
    for i in range(P):
        z_ref[0, blk[i], :] = (gateb_ref[blk[i], :] * _bdot(inv, yspec_ref[i])).astype(z_ref.dtype)


def _hyena_core(hn, w_in, conv_w, conv_b, gc, gs, fwd, inv, P):
    B, L, D = hn.shape
    n = L // P
    cb_n = D // CH_BLOCK
    C = CH_BLOCK
    nd = 2 * P - 1

    def stream_spec(shape, s):
        return pl.BlockSpec(shape, lambda b, c: (0, s * cb_n + c))

    in_specs = (
        [pl.BlockSpec((1, L, D), lambda b, c: (b, 0, 0))]
        + [stream_spec((D, C), s) for s in range(3)]
        + [stream_spec((3, C), s) for s in range(3)]
        + [stream_spec((1, C), s) for s in range(3)]
        + [pl.BlockSpec((nd, n, C), lambda b, c: (0, 0, c)),
           pl.BlockSpec((nd, n, C), lambda b, c: (0, 0, c)),
           pl.BlockSpec((nd, n, C), lambda b, c: (0, 0, cb_n + c)),
           pl.BlockSpec((nd, n, C), lambda b, c: (0, 0, cb_n + c)),
           _resident((2 * n, n), lambda b, c: (0, 0)),
           _resident((n, 2 * n), lambda b, c: (0, 0))])
    return pl.pallas_call(
        functools.partial(_hyena_kernel, P=P, n=n, fchunk=16),
        grid=(B, cb_n),
        in_specs=in_specs,
        out_specs=pl.BlockSpec((1, L, C), lambda b, c: (b, 0, c)),
        out_shape=jax.ShapeDtypeStruct((B, L, D), BF16),
        scratch_shapes=[pltpu.VMEM((L, C), F32),
                        pltpu.VMEM((L, C), F32),
                        pltpu.VMEM((L, C), BF16),
                        pltpu.VMEM((L, C), F32),
                        pltpu.VMEM((L, C), F32),
                        pltpu.VMEM((P, 2 * n, C), SPEC_DTYPE),
                        pltpu.VMEM((P, 2 * n, C), BF16)],
        compiler_params=_cparams(2),
        name="hyena_core",
    )(hn, w_in, w_in, w_in, conv_w, conv_w, conv_w, conv_b, conv_b, conv_b,
      gc, gs, gc, gs, fwd, inv)


def _stream_cast(src_hbm, layer, dst_ref, stage_ref, sem_ref):
    slots, rows = stage_ref.shape[0], stage_ref.shape[1]
    n_chunks = dst_ref.shape[0] // rows

    def copy(c):
        return pltpu.make_async_copy(src_hbm.at[layer, pl.ds(c * rows, rows), :],
                                     stage_ref.at[c % slots], sem_ref.at[c % slots])

    for c in range(min(slots - 1, n_chunks)):
        copy(c).start()
    for c in range(n_chunks):
        ahead = c + slots - 1
        if ahead < n_chunks:
            copy(ahead).start()
        copy(c).wait()
        dst_ref[c * rows:(c + 1) * rows, :] = stage_ref[c % slots].astype(dst_ref.dtype)


def _tail_kernel(a_ref, wo32_ref, r_ref, gmix_ref, gpre_ref, gpost_ref, wu_hbm, wd_hbm, o_ref,
                 wo_ref, wu_ref, wd_ref, stage_u, stage_d, sem_u, sem_d, *, fchunk, layer):
    @pl.when(pl.program_id(0) == 0)
    def _():
        wo_ref[...] = wo32_ref[...].astype(wo_ref.dtype)
        _stream_cast(wu_hbm, layer, wu_ref, stage_u, sem_u)
        _stream_cast(wd_hbm, layer, wd_ref, stage_d, sem_d)

    slabs = 2
    sm = a_ref.shape[0] // slabs
    rows = [slice(sm * u, sm * (u + 1)) for u in range(slabs)]
    FF = wu_ref.shape[1]
    ms = [_bdot(a_ref[r, :], wo_ref[...]) for r in rows]
    hs = [r_ref[r, :] + _rms(m, gmix_ref[...]) for r, m in zip(rows, ms)]
    hns = [_rms(h, gpre_ref[...]).astype(BF16) for h in hs]
    accs = [None] * slabs
    for c in range(FF // fchunk):
        acts = []
        for u in range(slabs):
            a = jnp.maximum(_bdot(hns[u], wu_ref[:, c * fchunk:(c + 1) * fchunk]), 0.0)
            acts.append((a * a).astype(BF16))
        for u in range(slabs):
            part = _bdot(acts[u], wd_ref[c * fchunk:(c + 1) * fchunk, :])
            accs[u] = part if accs[u] is None else accs[u] + part
    for u in range(slabs):
        o_ref[rows[u], :] = hs[u] + _rms(accs[u], gpost_ref[...])


def _tail(a, w_o, resid, g_mix, g_pre, g_post, w_up_all, w_down_all, layer, tm=512, fchunk=1024):
    M, K = a.shape
    D = w_o.shape[1]
    FF = w_up_all.shape[2]
    stage_bytes = 1024 * 1024
    rows_u = stage_bytes // (FF * 4)
    rows_d = stage_bytes // (D * 4)
    row = lambda shape: pl.BlockSpec(shape, lambda i: (i, 0))
    gain = pl.BlockSpec((1, D), lambda i: (0, 0))
    return pl.pallas_call(
        functools.partial(_tail_kernel, fchunk=fchunk, layer=layer),
        grid=(M // tm,),
        in_specs=[row((tm, K)), _resident((K, D), lambda i: (0, 0)), row((tm, D)),
                  gain, gain, gain,
                  pl.BlockSpec(memory_space=pl.ANY),
                  pl.BlockSpec(memory_space=pl.ANY)],
        out_specs=row((tm, D)),
        out_shape=jax.ShapeDtypeStruct((M, D), F32),
        scratch_shapes=[pltpu.VMEM((K, D), BF16),
                        pltpu.VMEM((D, FF), BF16),
                        pltpu.VMEM((FF, D), BF16),
                        pltpu.VMEM((STAGE_SLOTS, rows_u, FF), F32),
                        pltpu.VMEM((STAGE_SLOTS, rows_d, D), F32),
                        pltpu.SemaphoreType.DMA((STAGE_SLOTS,)),
                        pltpu.SemaphoreType.DMA((STAGE_SLOTS,))],
        compiler_params=_cparams(1),
        name="mixer_tail_mlp",
    )(a, w_o, resid, g_mix.reshape(1, D), g_pre.reshape(1, D), g_post.reshape(1, D),
      w_up_all, w_down_all)


def _qkv_kernel(h_ref, g_ref, w32_ref, ct_ref, slo_ref, shi_ref, q_ref, k_ref, v_ref, w_ref, *, halves):
    @pl.when(pl.program_id(0) == 0)
    def _():
        w_ref[...] = w32_ref[...].astype(w_ref.dtype)

    nq, nk = q_ref.shape[1], k_ref.shape[1]
    hm = h_ref.shape[0] // halves
    rows = [slice(hm * u, hm * (u + 1)) for u in range(halves)]
    half = ROT_DIM // 2
    scale = LOG2E * HEAD_DIM ** -0.5
    qkvs = [_bdot(_rms(h_ref[r, :], g_ref[...]).astype(BF16), w_ref[...]) for r in rows]
    for r, qkv in zip(rows, qkvs):
        ct = ct_ref[r, :]
        slo = slo_ref[r, :]
        shi = shi_ref[r, :]
        for j in range((nq + nk) // 128):
            t = qkv[:, 128 * j:128 * (j + 1)]
            rot = t * ct + pltpu.roll(t, 128 - half, axis=1) * slo + pltpu.roll(t, half, axis=1) * shi
            if 128 * j < nq:
                q_ref[r, 128 * j:128 * (j + 1)] = (rot * scale).astype(q_ref.dtype)
            else:
                k_ref[r, 128 * j - nq:128 * (j + 1) - nq] = rot.astype(k_ref.dtype)
        v_ref[r, :] = qkv[:, nq + nk:].astype(v_ref.dtype)


def _qkv(h, g, w, S, tm=1024, halves=2):
    M, D = h.shape
    nq = D
    nk = (w.shape[1] - nq) // 2
    ct, slo, shi = [jnp.asarray(a) for a in _rope_tables(S)]
    pb = S // tm
    tab = pl.BlockSpec((tm, 128), lambda i: (i % pb, 0))
    return pl.pallas_call(
        functools.partial(_qkv_kernel, halves=halves),
        grid=(M // tm,),
        in_specs=[pl.BlockSpec((tm, D), lambda i: (i, 0)),
                  pl.BlockSpec((1, D), lambda i: (0, 0)),
                  _resident((D, nq + 2 * nk), lambda i: (0, 0)),
                  tab, tab, tab],
        out_specs=[pl.BlockSpec((tm, nq), lambda i: (i, 0)),
                   pl.BlockSpec((tm, nk), lambda i: (i, 0)),
                   pl.BlockSpec((tm, nk), lambda i: (i, 0))],
        out_shape=[jax.ShapeDtypeStruct((M, nq), BF16),
                   jax.ShapeDtypeStruct((M, nk), BF16),
                   jax.ShapeDtypeStruct((M, nk), BF16)],
        scratch_shapes=[pltpu.VMEM(w.shape, BF16)],
        compiler_params=_cparams(1),
        name="qkv_rope",
    )(h, g.reshape(1, D), w, ct, slo, shi)


def _attn_kernel(sink_ref, q_ref, k_ref, v_ref, o_ref, krep_ref, vrep_ref, *, S, n_kv):
    BQ = WINDOW
    BK = 3 * WINDOW
    GW = KV_GROUP * HEAD_DIM
    dn = (((1,), (1,)), ((), ()))

    for g in range(n_kv):
        kg = k_ref[0, :, g * HEAD_DIM:(g + 1) * HEAD_DIM]
        vg = v_ref[0, :, g * HEAD_DIM:(g + 1) * HEAD_DIM]
        krep_ref[g] = jnp.concatenate([kg] * KV_GROUP, axis=1)
        vrep_ref[g] = jnp.concatenate([vg] * KV_GROUP, axis=1)

    lane = lax.broadcasted_iota(jnp.int32, (BQ, GW), 1)
    head_lanes = [(lane >= hh * HEAD_DIM) & (lane < (hh + 1) * HEAD_DIM) for hh in range(KV_GROUP)]
    rowblk = lax.broadcasted_iota(jnp.int32, (KV_GROUP * BQ, 1), 0) // BQ

    def body(qb, carry):
        qs = pl.multiple_of(qb * BQ, BQ)
        ks = pl.multiple_of(jnp.clip(qs - WINDOW, 0, S - BK), WINDOW)
        qpos = qs + lax.broadcasted_iota(jnp.int32, (BQ, BK), 0)
        kpos = ks + lax.broadcasted_iota(jnp.int32, (BQ, BK), 1)
        maskadd = jnp.where(jnp.abs(kpos - qpos) <= WINDOW, 0.0, NEG).astype(F32)
        maskadd = jnp.concatenate([maskadd] * KV_GROUP, axis=0)
        for g in range(n_kv):
            qg = q_ref[0, pl.ds(qs, BQ), g * GW:(g + 1) * GW]
            qst = jnp.concatenate([jnp.where(head_lanes[hh], qg, jnp.zeros_like(qg))
                                   for hh in range(KV_GROUP)], axis=0)
            sink = jnp.zeros((KV_GROUP * BQ, 1), F32)
            for hh in range(KV_GROUP):
                sink = jnp.where(rowblk == hh, sink_ref[g * KV_GROUP + hh] * LOG2E, sink)
            s = lax.dot_general(qst, krep_ref[g, pl.ds(ks, BK), :], dn,
                                preferred_element_type=F32) + maskadd
            m = jnp.maximum(jnp.max(s, axis=-1, keepdims=True), sink)
            p = jnp.exp2(s - m)
            denom = jnp.sum(p, axis=-1, keepdims=True) + jnp.exp2(sink - m)
            ost = _bdot(p.astype(BF16), vrep_ref[g, pl.ds(ks, BK), :]) / denom
            o = jnp.where(head_lanes[0], ost[0:BQ], 0.0)
            for hh in range(1, KV_GROUP):
                o = jnp.where(head_lanes[hh], ost[hh * BQ:(hh + 1) * BQ], o)
            o_ref[0, pl.ds(qs, BQ), g * GW:(g + 1) * GW] = o.astype(o_ref.dtype)
        return carry

    lax.fori_loop(0, S // BQ, body, 0, unroll=4)


def _attention(q, k, v, sink):
    B, S, NQ = q.shape
    NK = k.shape[2]
    n_kv = NK // HEAD_DIM
    return pl.pallas_call(
        functools.partial(_attn_kernel, S=S, n_kv=n_kv),
        grid=(B,),
        in_specs=[pl.BlockSpec(memory_space=pltpu.SMEM),
                  pl.BlockSpec((1, S, NQ), lambda b: (b, 0, 0)),
                  pl.BlockSpec((1, S, NK), lambda b: (b, 0, 0)),
                  pl.BlockSpec((1, S, NK), lambda b: (b, 0, 0))],
        out_specs=pl.BlockSpec((1, S, NQ), lambda b: (b, 0, 0)),
        out_shape=jax.ShapeDtypeStruct((B, S, NQ), BF16),
        scratch_shapes=[pltpu.VMEM((n_kv, S, KV_GROUP * HEAD_DIM), BF16),
                        pltpu.VMEM((n_kv, S, KV_GROUP * HEAD_DIM), BF16)],
        compiler_params=_cparams(1),
        name="window_attention",
    )(sink, q, k, v)


def kernel(x, norm_mix_pre, norm_mix_post, norm_mlp_pre, norm_mlp_post, w_up, w_down, hy_w_in, hy_conv_w, hy_conv_b, hy_f_w1, hy_f_b1, hy_f_w2, hy_f_b2, hy_f_w3, hy_f_b3, hy_f_freq, hy_f_wout, hy_bias, hy_w_out, at_w_qkv, at_sink, at_w_o):
    B, L, D = x.shape
    M = B * L
    P = CONV_BLOCKS
    n = L // P
    depth = norm_mix_pre.shape[0]
    fwd, inv, filt_c, filt_s = [jnp.asarray(a).astype(BF16) for a in _dft_tables(n)]

    h = x.reshape(M, D)
    for i in range(depth):
        j = i // 2
        if i % 2 == 0:
            gc, gs, hn = _filter_spectra(L, hy_f_w1[j], hy_f_b1[j], hy_f_w2[j], hy_f_b2[j],
                                         hy_f_w3[j], hy_f_b3[j], hy_f_freq[j], hy_f_wout[j],
                                         hy_bias[j], filt_c, filt_s, P, h, norm_mix_pre[i])
            z = _hyena_core(hn.reshape(B, L, D), hy_w_in[j], hy_conv_w[j], hy_conv_b[j].reshape(1, -1),
                            gc, gs, fwd, inv, P)
            a, w_o = z.reshape(M, D), hy_w_out[j]
        else:
            q, k, v = _qkv(h, norm_mix_pre[i], at_w_qkv[j], L)
            o = _attention(q.reshape(B, L, -1), k.reshape(B, L, -1), v.reshape(B, L, -1), at_sink[j])
            a, w_o = o.reshape(M, -1), at_w_o[j]
        h = _tail(a, w_o, h, norm_mix_post[i], norm_mlp_pre[i], norm_mlp_post[i], w_up, w_down, i)
    return h.reshape(B, L, D)
```

```python
import functools
import math

import numpy as np
import jax
import jax.numpy as jnp
from jax import lax
from jax.experimental import pallas as pl
from jax.experimental.pallas import tpu as pltpu

BF16 = jnp.bfloat16
F32 = jnp.float32

EPS = 1e-6
NEG = -1e30
LOG2E = math.log2(math.e)

HY_ORDER = 2
HY_EMB = 33
HY_BANDS = (HY_EMB - 1) // 2
HY_DECAY_TARGET = 1e-2
HY_FAST_DECAY = 0.3
HY_SLOW_DECAY = 1.5
HEAD_DIM = 64
KV_GROUP = 4
WINDOW = 128
ROT_DIM = HEAD_DIM // 4
ROPE_THETA = 500000.0

CONV_BLOCKS = 4
CH_BLOCK = 256
SPEC_DTYPE = BF16
STAGE_SLOTS = 4
TAIL_SLAB_ROWS = 256

VMEM_LIMIT = 56 * 1024 * 1024


def _cparams(n_axes, flags=None):
    return pltpu.CompilerParams(
        dimension_semantics=("arbitrary",) * n_axes,
        vmem_limit_bytes=VMEM_LIMIT,
        flags=flags)


def _resident(shape, index_map):
    return pl.BlockSpec(shape, index_map, pipeline_mode=pl.Buffered(1))


def _rms(x, g):
    ms = jnp.mean(x * x, axis=-1, keepdims=True)
    return x * lax.rsqrt(ms + EPS) * g


def _bdot(a, b):
    return jnp.dot(a, b, preferred_element_type=F32)


def _dft_tables(n):
    f = np.arange(n, dtype=np.int64)[:, None]
    r = np.arange(n, dtype=np.int64)[None, :]
    k_pos = ((2 * f + 1) * r) % (4 * n)
    k_neg = ((2 * f + 1) * (r - n)) % (4 * n)
    ang_pos = np.pi * k_pos / (2 * n)
    ang_neg = np.pi * k_neg / (2 * n)
    cpos, spos = np.cos(ang_pos), np.sin(ang_pos)
    cneg, sneg = np.cos(ang_neg), np.sin(ang_neg)
    cneg[:, 0] = 0.0
    sneg[:, 0] = 0.0
    fwd = np.concatenate([cpos, spos], axis=0)
    inv = np.concatenate([cpos.T, spos.T], axis=1) / n
    filt_c = np.concatenate([cneg, cpos], axis=1)
    filt_s = np.concatenate([sneg, spos], axis=1)
    return [np.asarray(a, np.float32) for a in (fwd, inv, filt_c, filt_s)]


def _filter_positions(L):
    t = np.linspace(0.0, 1.0, L)[:, None]
    w = (2.0 * np.pi / L) * np.arange(L)[:, None]
    f = np.linspace(1e-4, HY_BANDS - 1, HY_BANDS)[None, :]
    z = np.concatenate([t, np.cos(f * w), -np.sin(f * w)], axis=-1)
    return np.asarray(z, np.float32)


def _rope_tables(S):
    half = ROT_DIM // 2
    inv = ROPE_THETA ** (-np.arange(0, ROT_DIM, 2, dtype=np.float64) / ROT_DIM)
    ang = np.arange(S, dtype=np.float64)[:, None] * inv[None, :]
    lane = np.arange(128) % HEAD_DIM
    ang_l = ang[:, lane % half]
    cos_t = np.where(lane[None, :] < ROT_DIM, np.cos(ang_l), 1.0)
    sin_lo = np.where(lane[None, :] < half, -np.sin(ang_l), 0.0)
    sin_hi = np.where((lane[None, :] >= half) & (lane[None, :] < ROT_DIM), np.sin(ang_l), 0.0)
    return [np.asarray(a, np.float32) for a in (cos_t, sin_lo, sin_hi)]


def _spectra_kernel(z_ref, w1_ref, b1_ref, w2_ref, b2_ref, w3_ref, b3_ref, fr_ref,
                    woutf_ref, woutb_ref, dl_ref, bias_ref, mc_ref, ms_ref, x_ref, g_ref,
                    gc_ref, gs_ref, hn_ref, hid_ref, hf_ref, hb_ref, *, P, n, norm_rows):
    hi = lax.Precision.HIGHEST

    @pl.when(pl.program_id(0) == 0)
    def _():
        fr = fr_ref[...]
        h = jnp.sin(fr * (jnp.dot(z_ref[...], w1_ref[...], precision=hi) + b1_ref[...]))
        h = jnp.sin(fr * (jnp.dot(h, w2_ref[...], precision=hi) + b2_ref[...]))
        h = jnp.sin(fr * (jnp.dot(h, w3_ref[...], precision=hi) + b3_ref[...]))
        hid_ref[...] = h

    L, cw = hf_ref.shape
    row = lax.broadcasted_iota(jnp.int32, (L, cw), 0)
    decay = jnp.exp(-(row.astype(F32) * (1.0 / (L - 1))) * dl_ref[...])
    hid = hid_ref[...]
    hid_hi = hid.astype(BF16)
    hid_lo = (hid - hid_hi.astype(F32)).astype(BF16)

    def taps(wout_ref):
        wout = wout_ref[...]
        w_hi = wout.astype(BF16)
        w_lo = (wout - w_hi.astype(F32)).astype(BF16)
        return (_bdot(hid_hi, w_hi) + (_bdot(hid_hi, w_lo) + _bdot(hid_lo, w_hi))) * decay

    hf_ref[...] = taps(woutf_ref).astype(hf_ref.dtype)
    hb_ref[...] = jnp.where(row > 0, taps(woutb_ref), 0.0).astype(hb_ref.dtype)

    mc = mc_ref[...]
    ms = ms_ref[...]
    cpos = mc[:, n:]
    spos = ms[:, n:]
    norm_chunks = [slice(r, r + norm_rows) for r in range(0, x_ref.shape[0], norm_rows)]
    nd = 2 * P - 1
    for idx, d in enumerate(range(-(P - 1), P)):
        for r in norm_chunks[idx * len(norm_chunks) // nd:(idx + 1) * len(norm_chunks) // nd]:
            hn_ref[r, :] = _rms(x_ref[r, :], g_ref[...]).astype(hn_ref.dtype)
        if d == 0:
            hf0 = hf_ref[0:n, :].astype(F32)
            hb0 = hb_ref[0:n, :].astype(F32)
            gc = _bdot(cpos, (hf0 + hb0).astype(BF16)) + bias_ref[...]
            gs = _bdot(spos, (hf0 - hb0).astype(BF16))
        elif d > 0:
            taps = hf_ref[n * (d - 1):n * (d + 1), :].astype(BF16)
            gc = _bdot(mc, taps)
            gs = _bdot(ms, taps)
        else:
            a = -d
            taps = hb_ref[n * (a - 1):n * (a + 1), :].astype(BF16)
            gc = _bdot(mc, taps)
            gs = -_bdot(ms, taps)
        gc_ref[d + P - 1] = gc.astype(gc_ref.dtype)
        gs_ref[d + P - 1] = gs.astype(gs_ref.dtype)


def _filter_spectra(L, f_w1, f_b1, f_w2, f_b2, f_w3, f_b3, f_freq, f_wout, bias, mc, ms, P, x2d, g,
                    cw=256, norm_rows=256):
    M, D = x2d.shape
    FW = f_w1.shape[1]
    n = L // P
    half = f_wout.shape[1] // 2
    nblk = half // cw
    xm = M // nblk
    z = jnp.asarray(np.pad(_filter_positions(L), ((0, 0), (0, FW - HY_EMB))))
    w1 = jnp.pad(f_w1, ((0, FW - HY_EMB), (0, 0)))
    max_decay = math.log(HY_DECAY_TARGET) / HY_FAST_DECAY
    min_decay = math.log(HY_DECAY_TARGET) / HY_SLOW_DECAY
    absdelta = jnp.asarray(np.abs(np.linspace(min_decay, max_decay, D)).astype(np.float32))[None, :]
    small = lambda shape: pl.BlockSpec(shape, lambda j: (0, 0))
    out_sd = jax.ShapeDtypeStruct((2 * P - 1, n, half), SPEC_DTYPE)
    return pl.pallas_call(
        functools.partial(_spectra_kernel, P=P, n=n, norm_rows=norm_rows),
        grid=(nblk,),
        in_specs=[small((L, FW)), small((FW, FW)), small((1, FW)), small((FW, FW)), small((1, FW)),
                  small((FW, FW)), small((1, FW)), small((1, FW)),
                  pl.BlockSpec((FW, cw), lambda j: (0, j)),
                  pl.BlockSpec((FW, cw), lambda j: (0, nblk + j)),
                  pl.BlockSpec((1, cw), lambda j: (0, j % (D // cw))),
                  pl.BlockSpec((1, cw), lambda j: (0, j)),
                  small((n, 2 * n)), small((n, 2 * n)),
                  pl.BlockSpec((xm, D), lambda j: (j, 0)),
                  small((1, D))],
        out_specs=[pl.BlockSpec((2 * P - 1, n, cw), lambda j: (0, 0, j)),
                   pl.BlockSpec((2 * P - 1, n, cw), lambda j: (0, 0, j)),
                   pl.BlockSpec((xm, D), lambda j: (j, 0))],
        out_shape=[out_sd, out_sd, jax.ShapeDtypeStruct((M, D), BF16)],
        scratch_shapes=[pltpu.VMEM((L, FW), F32),
                        pltpu.VMEM((L, cw), BF16),
                        pltpu.VMEM((L, cw), BF16)],
        compiler_params=_cparams(1),
        name="filter_spectra",
    )(z, w1, f_b1.reshape(1, FW), f_w2, f_b2.reshape(1, FW), f_w3, f_b3.reshape(1, FW),
      f_freq.reshape(1, FW), f_wout, f_wout, absdelta, bias.reshape(1, half), mc, ms,
      x2d, g.reshape(1, D))


def _hyena_kernel(hn_ref, wv_ref, w1_ref, w2_ref, cwv_ref, cw1_ref, cw2_ref,
                  cbv_ref, cb1_ref, cb2_ref, gc0_ref, gs0_ref, gc1_ref, gs1_ref,
                  fwd_ref, inv_ref, z_ref,
                  rawa_ref, rawb_ref, v_ref, gatea_ref, gateb_ref, uspec_ref, yspec_ref,
                  *, P, n, fchunk):
    row = lax.broadcasted_iota(jnp.int32, (n, 1), 0)
    blk = [slice(n * i, n * (i + 1)) for i in range(P)]
    chunks = n // fchunk
    fwd = fwd_ref[...]
    inv = inv_ref[...]

    wv, w1, w2 = [r[...].astype(BF16) for r in (wv_ref, w1_ref, w2_ref)]

    def project(w, dst_ref, i):
        dst_ref[blk[i], :] = _bdot(hn_ref[0, blk[i], :], w)

    def short_conv(raw_ref, cw_ref, cb_ref, dst, i):
        cw = cw_ref[...]
        cur = raw_ref[blk[i], :]
        prev = pltpu.roll(cur, 1, axis=0)
        nxt = pltpu.roll(cur, n - 1, axis=0)
        first = raw_ref[n * i - 1:n * i, :] if i > 0 else jnp.zeros_like(cur[0:1])
        last = raw_ref[n * (i + 1):n * (i + 1) + 1, :] if i < P - 1 else jnp.zeros_like(cur[0:1])
        prev = jnp.where(row == 0, first, prev)
        nxt = jnp.where(row == n - 1, last, nxt)
        out = prev * cw[0:1] + cur * cw[1:2] + nxt * cw[2:3] + cb_ref[...]
        dst[blk[i], :] = out.astype(dst.dtype)

    def forward(src_ref, j):
        uspec_ref[j] = _bdot(fwd, src_ref[blk[j], :]).astype(uspec_ref.dtype)

    def combine(gc_ref, gs_ref, c):
        fc = slice(c * fchunk, (c + 1) * fchunk)
        fs = slice(n + c * fchunk, n + (c + 1) * fchunk)
        uc = [uspec_ref[j, fc, :] for j in range(P)]
        us = [uspec_ref[j, fs, :] for j in range(P)]
        for i in range(P):
            yc = None
            ys = None
            for j in range(P):
                gc = gc_ref[i - j + P - 1, fc, :]
                gs = gs_ref[i - j + P - 1, fc, :]
                tc = gc * uc[j] - gs * us[j]
                ts = gc * us[j] + gs * uc[j]
                yc = tc if yc is None else yc + tc
                ys = ts if ys is None else ys + ts
            yspec_ref[i, fc, :] = yc.astype(BF16)
            yspec_ref[i, fs, :] = ys.astype(BF16)

    for i in range(P):
        project(wv, rawa_ref, i)
    for i in range(P):
        short_conv(rawa_ref, cwv_ref, cbv_ref, v_ref, i)
        project(w1, rawb_ref, i)
    for i in range(P):
        forward(v_ref, i)
        short_conv(rawb_ref, cw1_ref, cb1_ref, gatea_ref, i)
    for c in range(chunks):
        combine(gc0_ref, gs0_ref, c)
        if c % (chunks // P) == 0:
            project(w2, rawa_ref, c // (chunks // P))
    for i in range(P):
        short_conv(rawa_ref, cw2_ref, cb2_ref, gateb_ref, i)
        v_ref[blk[i], :] = (gatea_ref[blk[i], :] * _bdot(inv, yspec_ref[i])).astype(v_ref.dtype)
    for i in range(P):
        forward(v_ref, i)
    for c in range(chunks):
        combine(gc1_ref, gs1_ref, c)
    for i in range(P):
        z_ref[0, blk[i], :] = (gateb_ref[blk[i], :] * _bdot(inv, yspec_ref[i])).astype(z_ref.dtype)


def _hyena_core(hn, w_in, conv_w, conv_b, gc, gs, fwd, inv, P):
    B, L, D = hn.shape
    n = L // P
    cb_n = D // CH_BLOCK
    C = CH_BLOCK
    nd = gc.shape[0]

    def stream_spec(shape, s):
        return pl.BlockSpec(shape, lambda b, c: (0, s * cb_n + c))

    in_specs = (
        [pl.BlockSpec((1, L, D), lambda b, c: (b, 0, 0))]
        + [stream_spec((D, C), s) for s in range(3)]
        + [stream_spec((3, C), s) for s in range(3)]
        + [stream_spec((1, C), s) for s in range(3)]
        + [pl.BlockSpec((nd, n, C), lambda b, c: (0, 0, c)),
           pl.BlockSpec((nd, n, C), lambda b, c: (0, 0, c)),
           pl.BlockSpec((nd, n, C), lambda b, c: (0, 0, cb_n + c)),
           pl.BlockSpec((nd, n, C), lambda b, c: (0, 0, cb_n + c)),
           _resident((2 * n, n), lambda b, c: (0, 0)),
           _resident((n, 2 * n), lambda b, c: (0, 0))])
    return pl.pallas_call(
        functools.partial(_hyena_kernel, P=P, n=n, fchunk=16),
        grid=(B, cb_n),
        in_specs=in_specs,
        out_specs=pl.BlockSpec((1, L, C), lambda b, c: (b, 0, c)),
        out_shape=jax.ShapeDtypeStruct((B, L, D), BF16),
        scratch_shapes=[pltpu.VMEM((L, C), F32),
                        pltpu.VMEM((L, C), F32),
                        pltpu.VMEM((L, C), BF16),
                        pltpu.VMEM((L, C), F32),
                        pltpu.VMEM((L, C), F32),
                        pltpu.VMEM((P, 2 * n, C), SPEC_DTYPE),
                        pltpu.VMEM((P, 2 * n, C), BF16)],
        compiler_params=_cparams(2),
        name="hyena_core",
    )(hn, w_in, w_in, w_in, conv_w, conv_w, conv_w, conv_b, conv_b, conv_b,
      gc, gs, gc, gs, fwd, inv)


def _stream_cast(src_hbm, layer, dst_ref, stage_ref, sem_ref):
    slots, rows = stage_ref.shape[0], stage_ref.shape[1]
    n_chunks = dst_ref.shape[0] // rows

    def copy(c):
        return pltpu.make_async_copy(src_hbm.at[layer, pl.ds(c * rows, rows), :],
                                     stage_ref.at[c % slots], sem_ref.at[c % slots])

    for c in range(min(slots - 1, n_chunks)):
        copy(c).start()
    for c in range(n_chunks):
        ahead = c + slots - 1
        if ahead < n_chunks:
            copy(ahead).start()
        copy(c).wait()
        dst_ref[c * rows:(c + 1) * rows, :] = stage_ref[c % slots].astype(dst_ref.dtype)


def _tail_kernel(a_ref, wo32_ref, r_ref, gmix_ref, gpre_ref, gpost_ref, wu_hbm, wd_hbm, o_ref,
                 wo_ref, wu_ref, wd_ref, stage_u, stage_d, sem_u, sem_d, *, fchunk, layer):
    @pl.when(pl.program_id(0) == 0)
    def _():
        wo_ref[...] = wo32_ref[...].astype(wo_ref.dtype)
        _stream_cast(wu_hbm, layer, wu_ref, stage_u, sem_u)
        _stream_cast(wd_hbm, layer, wd_ref, stage_d, sem_d)

    sm = TAIL_SLAB_ROWS
    FF = wu_ref.shape[1]
    for p0 in range(0, a_ref.shape[0], 2 * sm):
        rows = [slice(p0 + sm * u, p0 + sm * (u + 1)) for u in range(2)]
        ms = [_bdot(a_ref[r, :], wo_ref[...]) for r in rows]
        hs = [r_ref[r, :] + _rms(m, gmix_ref[...]) for r, m in zip(rows, ms)]
        hns = [_rms(h, gpre_ref[...]).astype(BF16) for h in hs]
        accs = [None, None]
        for c in range(FF // fchunk):
            acts = []
            for u in range(2):
                a = jnp.maximum(_bdot(hns[u], wu_ref[:, c * fchunk:(c + 1) * fchunk]), 0.0)
                acts.append((a * a).astype(BF16))
            for u in range(2):
                part = _bdot(acts[u], wd_ref[c * fchunk:(c + 1) * fchunk, :])
                accs[u] = part if accs[u] is None else accs[u] + part
        for u in range(2):
            o_ref[rows[u], :] = hs[u] + _rms(accs[u], gpost_ref[...])


def _tail(a, w_o, resid, g_mix, g_pre, g_post, w_up_all, w_down_all, layer, tm=1024, fchunk=1024):
    M, K = a.shape
    D = w_o.shape[1]
    FF = w_up_all.shape[2]
    stage_bytes = 512 * 1024
    rows_u = stage_bytes // (FF * 4)
    rows_d = stage_bytes // (D * 4)
    row = lambda shape: pl.BlockSpec(shape, lambda i: (i, 0))
    gain = pl.BlockSpec((1, D), lambda i: (0, 0))
    return pl.pallas_call(
        functools.partial(_tail_kernel, fchunk=fchunk, layer=layer),
        grid=(M // tm,),
        in_specs=[row((tm, K)), _resident((K, D), lambda i: (0, 0)), row((tm, D)),
                  gain, gain, gain,
                  pl.BlockSpec(memory_space=pl.ANY),
                  pl.BlockSpec(memory_space=pl.ANY)],
        out_specs=row((tm, D)),
        out_shape=jax.ShapeDtypeStruct((M, D), F32),
        scratch_shapes=[pltpu.VMEM((K, D), BF16),
                        pltpu.VMEM((D, FF), BF16),
                        pltpu.VMEM((FF, D), BF16),
                        pltpu.VMEM((STAGE_SLOTS, rows_u, FF), F32),
                        pltpu.VMEM((STAGE_SLOTS, rows_d, D), F32),
                        pltpu.SemaphoreType.DMA((STAGE_SLOTS,)),
                        pltpu.SemaphoreType.DMA((STAGE_SLOTS,))],
        compiler_params=_cparams(1),
        name="mixer_tail_mlp",
    )(a, w_o, resid, g_mix.reshape(1, D), g_pre.reshape(1, D), g_post.reshape(1, D),
      w_up_all, w_down_all)


def _qkv_kernel(h_ref, g_ref, w32_ref, ct_ref, slo_ref, shi_ref, q_ref, k_ref, v_ref, w_ref, *, halves):
    @pl.when(pl.program_id(0) == 0)
    def _():
        w_ref[...] = w32_ref[...].astype(w_ref.dtype)

    nq, nk = q_ref.shape[1], k_ref.shape[1]
    hm = h_ref.shape[0] // halves
    rows = [slice(hm * u, hm * (u + 1)) for u in range(halves)]
    half = ROT_DIM // 2
    scale = LOG2E * HEAD_DIM ** -0.5
    qkvs = [_bdot(_rms(h_ref[r, :], g_ref[...]).astype(BF16), w_ref[...]) for r in rows]
    for r, qkv in zip(rows, qkvs):
        ct = ct_ref[r, :]
        slo = slo_ref[r, :]
        shi = shi_ref[r, :]
        for j in range((nq + nk) // 128):
            t = qkv[:, 128 * j:128 * (j + 1)]
            rot = t * ct + pltpu.roll(t, 128 - half, axis=1) * slo + pltpu.roll(t, half, axis=1) * shi
            if 128 * j < nq:
                q_ref[r, 128 * j:128 * (j + 1)] = (rot * scale).astype(q_ref.dtype)
            else:
                k_ref[r, 128 * j - nq:128 * (j + 1) - nq] = rot.astype(k_ref.dtype)
        v_ref[r, :] = qkv[:, nq + nk:].astype(v_ref.dtype)


def _qkv(h, g, w, S, tm=1024, halves=2):
    M, D = h.shape
    nq = D
    nk = (w.shape[1] - nq) // 2
    ct, slo, shi = [jnp.asarray(a) for a in _rope_tables(S)]
    pb = S // tm
    tab = pl.BlockSpec((tm, 128), lambda i: (i % pb, 0))
    return pl.pallas_call(
        functools.partial(_qkv_kernel, halves=halves),
        grid=(M // tm,),
        in_specs=[pl.BlockSpec((tm, D), lambda i: (i, 0)),
                  pl.BlockSpec((1, D), lambda i: (0, 0)),
                  _resident((D, nq + 2 * nk), lambda i: (0, 0)),
                  tab, tab, tab],
        out_specs=[pl.BlockSpec((tm, nq), lambda i: (i, 0)),
                   pl.BlockSpec((tm, nk), lambda i: (i, 0)),
                   pl.BlockSpec((tm, nk), lambda i: (i, 0))],
        out_shape=[jax.ShapeDtypeStruct((M, nq), BF16),
                   jax.ShapeDtypeStruct((M, nk), BF16),
                   jax.ShapeDtypeStruct((M, nk), BF16)],
        scratch_shapes=[pltpu.VMEM(w.shape, BF16)],
        compiler_params=_cparams(1),
        name="qkv_rope",
    )(h, g.reshape(1, D), w, ct, slo, shi)


def _attn_kernel(sink_ref, q_ref, k_ref, v_ref, o_ref, krep_ref, vrep_ref, *, S, n_kv):
    BQ = WINDOW
    BK = 3 * WINDOW
    GW = KV_GROUP * HEAD_DIM
    dn = (((1,), (1,)), ((), ()))

    for g in range(n_kv):
        kg = k_ref[0, :, g * HEAD_DIM:(g + 1) * HEAD_DIM]
        vg = v_ref[0, :, g * HEAD_DIM:(g + 1) * HEAD_DIM]
        krep_ref[g] = jnp.concatenate([kg] * KV_GROUP, axis=1)
        vrep_ref[g] = jnp.concatenate([vg] * KV_GROUP, axis=1)

    lane = lax.broadcasted_iota(jnp.int32, (BQ, GW), 1)
    head_lanes = [(lane >= hh * HEAD_DIM) & (lane < (hh + 1) * HEAD_DIM) for hh in range(KV_GROUP)]
    rowblk = lax.broadcasted_iota(jnp.int32, (KV_GROUP * BQ, 1), 0) // BQ

    def body(qb, carry):
        qs = pl.multiple_of(qb * BQ, BQ)
        ks = pl.multiple_of(jnp.clip(qs - WINDOW, 0, S - BK), WINDOW)
        qpos = qs + lax.broadcasted_iota(jnp.int32, (BQ, BK), 0)
        kpos = ks + lax.broadcasted_iota(jnp.int32, (BQ, BK), 1)
        maskadd = jnp.where(jnp.abs(kpos - qpos) <= WINDOW, 0.0, NEG).astype(F32)
        maskadd = jnp.concatenate([maskadd] * KV_GROUP, axis=0)
        for g in range(n_kv):
            qg = q_ref[0, pl.ds(qs, BQ), g * GW:(g + 1) * GW]
            qst = jnp.concatenate([jnp.where(head_lanes[hh], qg, jnp.zeros_like(qg))
                                   for hh in range(KV_GROUP)], axis=0)
            sink = jnp.zeros((KV_GROUP * BQ, 1), F32)
            for hh in range(KV_GROUP):
                sink = jnp.where(rowblk == hh, sink_ref[g * KV_GROUP + hh] * LOG2E, sink)
            s = lax.dot_general(qst, krep_ref[g, pl.ds(ks, BK), :], dn,
                                preferred_element_type=F32) + maskadd
            m = jnp.maximum(jnp.max(s, axis=-1, keepdims=True), sink)
            p = jnp.exp2(s - m)
            denom = jnp.sum(p, axis=-1, keepdims=True) + jnp.exp2(sink - m)
            ost = _bdot(p.astype(BF16), vrep_ref[g, pl.ds(ks, BK), :]) / denom
            o = jnp.where(head_lanes[0], ost[0:BQ], 0.0)
            for hh in range(1, KV_GROUP):
                o = jnp.where(head_lanes[hh], ost[hh * BQ:(hh + 1) * BQ], o)
            o_ref[0, pl.ds(qs, BQ), g * GW:(g + 1) * GW] = o.astype(o_ref.dtype)
        return carry

    lax.fori_loop(0, S // BQ, body, 0, unroll=4)


def _attention(q, k, v, sink):
    B, S, NQ = q.shape
    NK = k.shape[2]
    n_kv = NK // HEAD_DIM
    return pl.pallas_call(
        functools.partial(_attn_kernel, S=S, n_kv=n_kv),
        grid=(B,),
        in_specs=[pl.BlockSpec(memory_space=pltpu.SMEM),
                  pl.BlockSpec((1, S, NQ), lambda b: (b, 0, 0)),
                  pl.BlockSpec((1, S, NK), lambda b: (b, 0, 0)),
                  pl.BlockSpec((1, S, NK), lambda b: (b, 0, 0))],
        out_specs=pl.BlockSpec((1, S, NQ), lambda b: (b, 0, 0)),
        out_shape=jax.ShapeDtypeStruct((B, S, NQ), BF16),
        scratch_shapes=[pltpu.VMEM((n_kv, S, KV_GROUP * HEAD_DIM), BF16),
                        pltpu.VMEM((n_kv, S, KV_GROUP * HEAD_DIM), BF16)],
        compiler_params=_cparams(1),
        name="window_attention",
    )(sink, q, k, v)


def kernel(x, norm_mix_pre, norm_mix_post, norm_mlp_pre, norm_mlp_post, w_up, w_down, hy_w_in, hy_conv_w, hy_conv_b, hy_f_w1, hy_f_b1, hy_f_w2, hy_f_b2, hy_f_w3, hy_f_b3, hy_f_freq, hy_f_wout, hy_bias, hy_w_out, at_w_qkv, at_sink, at_w_o):
    B, L, D = x.shape
    M = B * L
    P = CONV_BLOCKS
    n = L // P
    depth = norm_mix_pre.shape[0]
    fwd, inv, filt_c, filt_s = [jnp.asarray(a).astype(BF16) for a in _dft_tables(n)]

    h = x.reshape(M, D)
    for i in range(depth):
        j = i // 2
        if i % 2 == 0:
            gc, gs, hn = _filter_spectra(L, hy_f_w1[j], hy_f_b1[j], hy_f_w2[j], hy_f_b2[j],
                                         hy_f_w3[j], hy_f_b3[j], hy_f_freq[j], hy_f_wout[j],
                                         hy_bias[j], filt_c, filt_s, P, h, norm_mix_pre[i])
            z = _hyena_core(hn.reshape(B, L, D), hy_w_in[j], hy_conv_w[j], hy_conv_b[j].reshape(1, -1),
                            gc, gs, fwd, inv, P)
            a, w_o = z.reshape(M, D), hy_w_out[j]
        else:
            q, k, v = _qkv(h, norm_mix_pre[i], at_w_qkv[j], L)
            o = _attention(q.reshape(B, L, -1), k.reshape(B, L, -1), v.reshape(B, L, -1), at_sink[j])
            a, w_o = o.reshape(M, -1), at_w_o[j]
        h = _tail(a, w_o, h, norm_mix_post[i], norm_mlp_pre[i], norm_mlp_post[i], w_up, w_down, i)
    return h.reshape(B, L, D)
```

```python
import functools
import math

import numpy as np
import jax
import jax.numpy as jnp
from jax import lax
from jax.experimental import pallas as pl
from jax.experimental.pallas import tpu as pltpu

BF16 = jnp.bfloat16
F32 = jnp.float32

EPS = 1e-6
NEG = -1e30
LOG2E = math.log2(math.e)

HY_ORDER = 2
HY_EMB = 33
HY_BANDS = (HY_EMB - 1) // 2
HY_DECAY_TARGET = 1e-2
HY_FAST_DECAY = 0.3
HY_SLOW_DECAY = 1.5
HEAD_DIM = 64
KV_GROUP = 4
WINDOW = 128
ROT_DIM = HEAD_DIM // 4
ROPE_THETA = 500000.0

CONV_BLOCKS = 4
CH_BLOCK = 256
SPEC_DTYPE = BF16
STAGE_SLOTS = 4
TAIL_SLAB_ROWS = 256

VMEM_LIMIT = 56 * 1024 * 1024


def _cparams(n_axes, flags=None):
    return pltpu.CompilerParams(
        dimension_semantics=("arbitrary",) * n_axes,
        vmem_limit_bytes=VMEM_LIMIT,
        flags=flags)


def _resident(shape, index_map):
    return pl.BlockSpec(shape, index_map, pipeline_mode=pl.Buffered(1))


def _rms(x, g):
    ms = jnp.mean(x * x, axis=-1, keepdims=True)
    return x * lax.rsqrt(ms + EPS) * g


def _bdot(a, b):
    return jnp.dot(a, b, preferred_element_type=F32)


def _dft_tables(n):
    f = np.arange(n, dtype=np.int64)[:, None]
    r = np.arange(n, dtype=np.int64)[None, :]
    k_pos = ((2 * f + 1) * r) % (4 * n)
    k_neg = ((2 * f + 1) * (r - n)) % (4 * n)
    ang_pos = np.pi * k_pos / (2 * n)
    ang_neg = np.pi * k_neg / (2 * n)
    cpos, spos = np.cos(ang_pos), np.sin(ang_pos)
    cneg, sneg = np.cos(ang_neg), np.sin(ang_neg)
    cneg[:, 0] = 0.0
    sneg[:, 0] = 0.0
    fwd = np.concatenate([cpos, spos], axis=0)
    inv = np.concatenate([cpos.T, spos.T], axis=1) / n
    filt_c = np.concatenate([cneg, cpos], axis=1)
    filt_s = np.concatenate([sneg, spos], axis=1)
    return [np.asarray(a, np.float32) for a in (fwd, inv, filt_c, filt_s)]


def _filter_positions(L):
    t = np.linspace(0.0, 1.0, L)[:, None]
    w = (2.0 * np.pi / L) * np.arange(L)[:, None]
    f = np.linspace(1e-4, HY_BANDS - 1, HY_BANDS)[None, :]
    z = np.concatenate([t, np.cos(f * w), -np.sin(f * w)], axis=-1)
    return np.asarray(z, np.float32)


def _rope_tables(S):
    half = ROT_DIM // 2
    inv = ROPE_THETA ** (-np.arange(0, ROT_DIM, 2, dtype=np.float64) / ROT_DIM)
    ang = np.arange(S, dtype=np.float64)[:, None] * inv[None, :]
    lane = np.arange(128) % HEAD_DIM
    ang_l = ang[:, lane % half]
    cos_t = np.where(lane[None, :] < ROT_DIM, np.cos(ang_l), 1.0)
    sin_lo = np.where(lane[None, :] < half, -np.sin(ang_l), 0.0)
    sin_hi = np.where((lane[None, :] >= half) & (lane[None, :] < ROT_DIM), np.sin(ang_l), 0.0)
    return [np.asarray(a, np.float32) for a in (cos_t, sin_lo, sin_hi)]


def _spectra_kernel(z_ref, w1_ref, b1_ref, w2_ref, b2_ref, w3_ref, b3_ref, fr_ref,
                    woutf_ref, woutb_ref, dl_ref, bias_ref, mc_ref, ms_ref, x_ref, g_ref,
                    gc_ref, gs_ref, hn_ref, hid_ref, hf_ref, hb_ref, *, P, n, norm_rows):
    hi = lax.Precision.HIGHEST

    L, cw = hf_ref.shape
    FW = hid_ref.shape[1]

    @pl.when(pl.program_id(0) == 0)
    def _():
        def two(v):
            return jnp.concatenate([v, v], axis=1)

        def blockdiag(w):
            zero = jnp.zeros_like(w)
            return jnp.concatenate([jnp.concatenate([w, zero], axis=1),
                                    jnp.concatenate([zero, w], axis=1)], axis=0)

        fr = two(fr_ref[...])
        h = z_ref[...]
        for w_ref, b_ref in ((w1_ref, b1_ref), (w2_ref, b2_ref), (w3_ref, b3_ref)):
            h = jnp.sin(fr * (jnp.dot(h, blockdiag(w_ref[...]), precision=hi) + two(b_ref[...])))
        hid_ref[0:L // 2, :] = h[:, :FW]
        hid_ref[L // 2:L, :] = h[:, FW:]

    row = lax.broadcasted_iota(jnp.int32, (L, cw), 0)
    decay = jnp.exp(-(row.astype(F32) * (1.0 / (L - 1))) * dl_ref[...])
    hid = hid_ref[...]
    hid_hi = hid.astype(BF16)
    hid_lo = (hid - hid_hi.astype(F32)).astype(BF16)

    def taps(wout_ref):
        wout = wout_ref[...]
        w_hi = wout.astype(BF16)
        w_lo = (wout - w_hi.astype(F32)).astype(BF16)
        return (_bdot(hid_hi, w_hi) + (_bdot(hid_hi, w_lo) + _bdot(hid_lo, w_hi))) * decay

    hf_ref[...] = taps(woutf_ref).astype(hf_ref.dtype)
    hb_ref[...] = jnp.where(row > 0, taps(woutb_ref), 0.0).astype(hb_ref.dtype)

    mc = mc_ref[...]
    ms = ms_ref[...]
    cpos = mc[:, n:]
    spos = ms[:, n:]
    norm_chunks = [slice(r, r + norm_rows) for r in range(0, x_ref.shape[0], norm_rows)]
    nd = 2 * P - 1
    for idx, d in enumerate(range(-(P - 1), P)):
        for r in norm_chunks[idx * len(norm_chunks) // nd:(idx + 1) * len(norm_chunks) // nd]:
            hn_ref[r, :] = _rms(x_ref[r, :], g_ref[...]).astype(hn_ref.dtype)
        if d == 0:
            hf0 = hf_ref[0:n, :].astype(F32)
            hb0 = hb_ref[0:n, :].astype(F32)
            gc = _bdot(cpos, (hf0 + hb0).astype(BF16)) + bias_ref[...]
            gs = _bdot(spos, (hf0 - hb0).astype(BF16))
        elif d > 0:
            taps = hf_ref[n * (d - 1):n * (d + 1), :].astype(BF16)
            gc = _bdot(mc, taps)
            gs = _bdot(ms, taps)
        else:
            a = -d
            taps = hb_ref[n * (a - 1):n * (a + 1), :].astype(BF16)
            gc = _bdot(mc, taps)
            gs = -_bdot(ms, taps)
        gc_ref[d + P - 1] = gc.astype(gc_ref.dtype)
        gs_ref[d + P - 1] = gs.astype(gs_ref.dtype)


def _filter_spectra(L, f_w1, f_b1, f_w2, f_b2, f_w3, f_b3, f_freq, f_wout, bias, mc, ms, P, x2d, g,
                    cw=256, norm_rows=256):
    M, D = x2d.shape
    FW = f_w1.shape[1]
    n = L // P
    half = f_wout.shape[1] // 2
    nblk = half // cw
    xm = M // nblk
    zpad = np.pad(_filter_positions(L), ((0, 0), (0, FW - HY_EMB)))
    z = jnp.asarray(np.concatenate([zpad[:L // 2], zpad[L // 2:]], axis=1))
    w1 = jnp.pad(f_w1, ((0, FW - HY_EMB), (0, 0)))
    max_decay = math.log(HY_DECAY_TARGET) / HY_FAST_DECAY
    min_decay = math.log(HY_DECAY_TARGET) / HY_SLOW_DECAY
    absdelta = jnp.asarray(np.abs(np.linspace(min_decay, max_decay, D)).astype(np.float32))[None, :]
    small = lambda shape: pl.BlockSpec(shape, lambda j: (0, 0))
    out_sd = jax.ShapeDtypeStruct((2 * P - 1, n, half), SPEC_DTYPE)
    return pl.pallas_call(
        functools.partial(_spectra_kernel, P=P, n=n, norm_rows=norm_rows),
        grid=(nblk,),
        in_specs=[small((L // 2, 2 * FW)), small((FW, FW)), small((1, FW)), small((FW, FW)), small((1, FW)),
                  small((FW, FW)), small((1, FW)), small((1, FW)),
                  pl.BlockSpec((FW, cw), lambda j: (0, j)),
                  pl.BlockSpec((FW, cw), lambda j: (0, nblk + j)),
                  pl.BlockSpec((1, cw), lambda j: (0, j % (D // cw))),
                  pl.BlockSpec((1, cw), lambda j: (0, j)),
                  small((n, 2 * n)), small((n, 2 * n)),
                  pl.BlockSpec((xm, D), lambda j: (j, 0)),
                  small((1, D))],
        out_specs=[pl.BlockSpec((2 * P - 1, n, cw), lambda j: (0, 0, j)),
                   pl.BlockSpec((2 * P - 1, n, cw), lambda j: (0, 0, j)),
                   pl.BlockSpec((xm, D), lambda j: (j, 0))],
        out_shape=[out_sd, out_sd, jax.ShapeDtypeStruct((M, D), BF16)],
        scratch_shapes=[pltpu.VMEM((L, FW), F32),
                        pltpu.VMEM((L, cw), BF16),
                        pltpu.VMEM((L, cw), BF16)],
        compiler_params=_cparams(1),
        name="filter_spectra",
    )(z, w1, f_b1.reshape(1, FW), f_w2, f_b2.reshape(1, FW), f_w3, f_b3.reshape(1, FW),
      f_freq.reshape(1, FW), f_wout, f_wout, absdelta, bias.reshape(1, half), mc, ms,
      x2d, g.reshape(1, D))


def _hyena_kernel(hn_ref, wv_ref, w1_ref, w2_ref, cwv_ref, cw1_ref, cw2_ref,
                  cbv_ref, cb1_ref, cb2_ref, gc0_ref, gs0_ref, gc1_ref, gs1_ref,
                  fwd_ref, inv_ref, z_ref,
                  rawa_ref, rawb_ref, v_ref, gate_ref, uspec_ref, yspec_ref,
                  *, P, n, fchunk):
    row = lax.broadcasted_iota(jnp.int32, (n, 1), 0)
    blk = [slice(n * i, n * (i + 1)) for i in range(P)]
    chunks = n // fchunk
    fwd = fwd_ref[...]
    inv = inv_ref[...]

    wv, w1, w2 = [r[...].astype(BF16) for r in (wv_ref, w1_ref, w2_ref)]

    def project(w, dst_ref, i):
        dst_ref[blk[i], :] = _bdot(hn_ref[0, blk[i], :], w)

    def short_conv(raw_ref, cw_ref, cb_ref, dst, i):
        cw = cw_ref[...]
        cur = raw_ref[blk[i], :]
        prev = pltpu.roll(cur, 1, axis=0)
        nxt = pltpu.roll(cur, n - 1, axis=0)
        first = raw_ref[n * i - 1:n * i, :] if i > 0 else jnp.zeros_like(cur[0:1])
        last = raw_ref[n * (i + 1):n * (i + 1) + 1, :] if i < P - 1 else jnp.zeros_like(cur[0:1])
        prev = jnp.where(row == 0, first, prev)
        nxt = jnp.where(row == n - 1, last, nxt)
        out = prev * cw[0:1] + cur * cw[1:2] + nxt * cw[2:3] + cb_ref[...]
        dst[blk[i], :] = out.astype(dst.dtype)

    def forward(src_ref, j):
        uspec_ref[j] = _bdot(fwd, src_ref[blk[j], :]).astype(uspec_ref.dtype)

    def combine(gc_ref, gs_ref, c):
        fc = slice(c * fchunk, (c + 1) * fchunk)
        fs = slice(n + c * fchunk, n + (c + 1) * fchunk)
        yc = [None] * P
        ys = [None] * P
        for j in range(P):
            uc = uspec_ref[j, fc, :]
            us = uspec_ref[j, fs, :]
            for i in range(P):
                gc = gc_ref[i - j + P - 1, fc, :]
                gs = gs_ref[i - j + P - 1, fc, :]
                tc = gc * uc - gs * us
                ts = gc * us + gs * uc
                yc[i] = tc if yc[i] is None else yc[i] + tc
                ys[i] = ts if ys[i] is None else ys[i] + ts
        for i in range(P):
            yspec_ref[i, fc, :] = yc[i].astype(BF16)
            yspec_ref[i, fs, :] = ys[i].astype(BF16)

    gate_proj = [(w1, rawb_ref, i) for i in range(P)] + [(w2, rawa_ref, i) for i in range(P)]
    per_combine = chunks // 3

    for i in range(P):
        project(wv, rawa_ref, i)
    for i in range(P):
        short_conv(rawa_ref, cwv_ref, cbv_ref, v_ref, i)
        if i < 3:
            project(*gate_proj[i])
    for i in range(P):
        forward(v_ref, i)
    for c in range(chunks):
        combine(gc0_ref, gs0_ref, c)
        if c % per_combine == 0 and c // per_combine < 3:
            project(*gate_proj[3 + c // per_combine])
    for i in range(P):
        short_conv(rawb_ref, cw1_ref, cb1_ref, gate_ref, i)
        v_ref[blk[i], :] = (gate_ref[blk[i], :] * _bdot(inv, yspec_ref[i])).astype(v_ref.dtype)
    for i in range(P):
        forward(v_ref, i)
    for c in range(chunks):
        combine(gc1_ref, gs1_ref, c)
        if c % (chunks // 2) == 0:
            project(*gate_proj[6 + c // (chunks // 2)])
    for i in range(P):
        short_conv(rawa_ref, cw2_ref, cb2_ref, gate_ref, i)
        z_ref[0, blk[i], :] = (gate_ref[blk[i], :] * _bdot(inv, yspec_ref[i])).astype(z_ref.dtype)


def _hyena_core(hn, w_in, conv_w, conv_b, gc, gs, fwd, inv, P):
    B, L, D = hn.shape
    n = L // P
    cb_n = D // CH_BLOCK
    C = CH_BLOCK
    nd = gc.shape[0]

    def stream_spec(shape, s):
        return pl.BlockSpec(shape, lambda b, c: (0, s * cb_n + c))

    in_specs = (
        [pl.BlockSpec((1, L, D), lambda b, c: (b, 0, 0))]
        + [stream_spec((D, C), s) for s in range(3)]
        + [stream_spec((3, C), s) for s in range(3)]
        + [stream_spec((1, C), s) for s in range(3)]
        + [pl.BlockSpec((nd, n, C), lambda b, c: (0, 0, c)),
           pl.BlockSpec((nd, n, C), lambda b, c: (0, 0, c)),
           pl.BlockSpec((nd, n, C), lambda b, c: (0, 0, cb_n + c)),
           pl.BlockSpec((nd, n, C), lambda b, c: (0, 0, cb_n + c)),
           _resident((2 * n, n), lambda b, c: (0, 0)),
           _resident((n, 2 * n), lambda b, c: (0, 0))])
    return pl.pallas_call(
        functools.partial(_hyena_kernel, P=P, n=n, fchunk=16),
        grid=(B, cb_n),
        in_specs=in_specs,
        out_specs=pl.BlockSpec((1, L, C), lambda b, c: (b, 0, c)),
        out_shape=jax.ShapeDtypeStruct((B, L, D), BF16),
        scratch_shapes=[pltpu.VMEM((L, C), F32),
                        pltpu.VMEM((L, C), F32),
                        pltpu.VMEM((L, C), BF16),
                        pltpu.VMEM((L, C), F32),
                        pltpu.VMEM((P, 2 * n, C), SPEC_DTYPE),
                        pltpu.VMEM((P, 2 * n, C), BF16)],
        compiler_params=_cparams(2),
        name="hyena_core",
    )(hn, w_in, w_in, w_in, conv_w, conv_w, conv_w, conv_b, conv_b, conv_b,
      gc, gs, gc, gs, fwd, inv)


def _stream_cast(src_hbm, layer, dst_ref, stage_ref, sem_ref):
    slots, rows = stage_ref.shape[0], stage_ref.shape[1]
    n_chunks = dst_ref.shape[0] // rows

    def copy(c):
        return pltpu.make_async_copy(src_hbm.at[layer, pl.ds(c * rows, rows), :],
                                     stage_ref.at[c % slots], sem_ref.at[c % slots])

    for c in range(min(slots - 1, n_chunks)):
        copy(c).start()
    for c in range(n_chunks):
        ahead = c + slots - 1
        if ahead < n_chunks:
            copy(ahead).start()
        copy(c).wait()
        dst_ref[c * rows:(c + 1) * rows, :] = stage_ref[c % slots].astype(dst_ref.dtype)


def _tail_kernel(a_ref, wo32_ref, r_ref, gmix_ref, gpre_ref, gpost_ref, wu_hbm, wd_hbm, o_ref,
                 wo_ref, wu_ref, wd_ref, stage_u, stage_d, sem_u, sem_d, *, fchunk, layer):
    @pl.when(pl.program_id(0) == 0)
    def _():
        wo_ref[...] = wo32_ref[...].astype(wo_ref.dtype)
        _stream_cast(wu_hbm, layer, wu_ref, stage_u, sem_u)
        _stream_cast(wd_hbm, layer, wd_ref, stage_d, sem_d)

    sm = TAIL_SLAB_ROWS
    FF = wu_ref.shape[1]
    for p0 in range(0, a_ref.shape[0], 2 * sm):
        rows = [slice(p0 + sm * u, p0 + sm * (u + 1)) for u in range(2)]
        ms = [_bdot(a_ref[r, :], wo_ref[...]) for r in rows]
        hs = [r_ref[r, :] + _rms(m, gmix_ref[...]) for r, m in zip(rows, ms)]
        hns = [_rms(h, gpre_ref[...]).astype(BF16) for h in hs]
        accs = [None, None]
        for c in range(FF // fchunk):
            acts = []
            for u in range(2):
                a = jnp.maximum(_bdot(hns[u], wu_ref[:, c * fchunk:(c + 1) * fchunk]), 0.0)
                acts.append((a * a).astype(BF16))
            for u in range(2):
                part = _bdot(acts[u], wd_ref[c * fchunk:(c + 1) * fchunk, :])
                accs[u] = part if accs[u] is None else accs[u] + part
        for u in range(2):
            o_ref[rows[u], :] = hs[u] + _rms(accs[u], gpost_ref[...])


def _tail(a, w_o, resid, g_mix, g_pre, g_post, w_up_all, w_down_all, layer, tm=512, fchunk=1024):
    M, K = a.shape
    D = w_o.shape[1]
    FF = w_up_all.shape[2]
    stage_bytes = 1024 * 1024
    rows_u = stage_bytes // (FF * 4)
    rows_d = stage_bytes // (D * 4)
    row = lambda shape: pl.BlockSpec(shape, lambda i: (i, 0))
    gain = pl.BlockSpec((1, D), lambda i: (0, 0))
    return pl.pallas_call(
        functools.partial(_tail_kernel, fchunk=fchunk, layer=layer),
        grid=(M // tm,),
        in_specs=[row((tm, K)), _resident((K, D), lambda i: (0, 0)), row((tm, D)),
                  gain, gain, gain,
                  pl.BlockSpec(memory_space=pl.ANY),
                  pl.BlockSpec(memory_space=pl.ANY)],
        out_specs=row((tm, D)),
        out_shape=jax.ShapeDtypeStruct((M, D), F32),
        scratch_shapes=[pltpu.VMEM((K, D), BF16),
                        pltpu.VMEM((D, FF), BF16),
                        pltpu.VMEM((FF, D), BF16),
                        pltpu.VMEM((STAGE_SLOTS, rows_u, FF), F32),
                        pltpu.VMEM((STAGE_SLOTS, rows_d, D), F32),
                        pltpu.SemaphoreType.DMA((STAGE_SLOTS,)),
                        pltpu.SemaphoreType.DMA((STAGE_SLOTS,))],
        compiler_params=_cparams(1),
        name="mixer_tail_mlp",
    )(a, w_o, resid, g_mix.reshape(1, D), g_pre.reshape(1, D), g_post.reshape(1, D),
      w_up_all, w_down_all)


def _qkv_kernel(h_ref, g_ref, w32_ref, ct_ref, slo_ref, shi_ref, q_ref, k_ref, v_ref, w_ref, *, halves):
    @pl.when(pl.program_id(0) == 0)
    def _():
        w_ref[...] = w32_ref[...].astype(w_ref.dtype)

    nq, nk = q_ref.shape[1], k_ref.shape[1]
    hm = h_ref.shape[0] // halves
    rows = [slice(hm * u, hm * (u + 1)) for u in range(halves)]
    half = ROT_DIM // 2
    scale = LOG2E * HEAD_DIM ** -0.5
    qkvs = [_bdot(_rms(h_ref[r, :], g_ref[...]).astype(BF16), w_ref[...]) for r in rows]
    for r, qkv in zip(rows, qkvs):
        ct = ct_ref[r, :]
        slo = slo_ref[r, :]
        shi = shi_ref[r, :]
        for j in range((nq + nk) // 128):
            t = qkv[:, 128 * j:128 * (j + 1)]
            rot = t * ct + pltpu.roll(t, 128 - half, axis=1) * slo + pltpu.roll(t, half, axis=1) * shi
            if 128 * j < nq:
                q_ref[r, 128 * j:128 * (j + 1)] = (rot * scale).astype(q_ref.dtype)
            else:
                k_ref[r, 128 * j - nq:128 * (j + 1) - nq] = rot.astype(k_ref.dtype)
        v_ref[r, :] = qkv[:, nq + nk:].astype(v_ref.dtype)


def _qkv(h, g, w, S, tm=1024, halves=2):
    M, D = h.shape
    nq = D
    nk = (w.shape[1] - nq) // 2
    ct, slo, shi = [jnp.asarray(a) for a in _rope_tables(S)]
    pb = S // tm
    tab = pl.BlockSpec((tm, 128), lambda i: (i % pb, 0))
    return pl.pallas_call(
        functools.partial(_qkv_kernel, halves=halves),
        grid=(M // tm,),
        in_specs=[pl.BlockSpec((tm, D), lambda i: (i, 0)),
                  pl.BlockSpec((1, D), lambda i: (0, 0)),
                  _resident((D, nq + 2 * nk), lambda i: (0, 0)),
                  tab, tab, tab],
        out_specs=[pl.BlockSpec((tm, nq), lambda i: (i, 0)),
                   pl.BlockSpec((tm, nk), lambda i: (i, 0)),
                   pl.BlockSpec((tm, nk), lambda i: (i, 0))],
        out_shape=[jax.ShapeDtypeStruct((M, nq), BF16),
                   jax.ShapeDtypeStruct((M, nk), BF16),
                   jax.ShapeDtypeStruct((M, nk), BF16)],
        scratch_shapes=[pltpu.VMEM(w.shape, BF16)],
        compiler_params=_cparams(1),
        name="qkv_rope",
    )(h, g.reshape(1, D), w, ct, slo, shi)


def _attn_kernel(sink_ref, q_ref, k_ref, v_ref, o_ref, krep_ref, vrep_ref, *, S, n_kv):
    BQ = WINDOW
    BK = 3 * WINDOW
    GW = KV_GROUP * HEAD_DIM
    dn = (((1,), (1,)), ((), ()))

    for g in range(n_kv):
        kg = k_ref[0, :, g * HEAD_DIM:(g + 1) * HEAD_DIM]
        vg = v_ref[0, :, g * HEAD_DIM:(g + 1) * HEAD_DIM]
        krep_ref[g] = jnp.concatenate([kg] * KV_GROUP, axis=1)
        vrep_ref[g] = jnp.concatenate([vg] * KV_GROUP, axis=1)

    lane = lax.broadcasted_iota(jnp.int32, (BQ, GW), 1)
    head_lanes = [(lane >= hh * HEAD_DIM) & (lane < (hh + 1) * HEAD_DIM) for hh in range(KV_GROUP)]
    rowblk = lax.broadcasted_iota(jnp.int32, (KV_GROUP * BQ, 1), 0) // BQ

    def body(qb, carry):
        qs = pl.multiple_of(qb * BQ, BQ)
        ks = pl.multiple_of(jnp.clip(qs - WINDOW, 0, S - BK), WINDOW)
        qpos = qs + lax.broadcasted_iota(jnp.int32, (BQ, BK), 0)
        kpos = ks + lax.broadcasted_iota(jnp.int32, (BQ, BK), 1)
        maskadd = jnp.where(jnp.abs(kpos - qpos) <= WINDOW, 0.0, NEG).astype(F32)
        maskadd = jnp.concatenate([maskadd] * KV_GROUP, axis=0)
        for g in range(n_kv):
            qg = q_ref[0, pl.ds(qs, BQ), g * GW:(g + 1) * GW]
            qst = jnp.concatenate([jnp.where(head_lanes[hh], qg, jnp.zeros_like(qg))
                                   for hh in range(KV_GROUP)], axis=0)
            sink = jnp.zeros((KV_GROUP * BQ, 1), F32)
            for hh in range(KV_GROUP):
                sink = jnp.where(rowblk == hh, sink_ref[g * KV_GROUP + hh] * LOG2E, sink)
            s = lax.dot_general(qst, krep_ref[g, pl.ds(ks, BK), :], dn,
                                preferred_element_type=F32) + maskadd
            m = jnp.maximum(jnp.max(s, axis=-1, keepdims=True), sink)
            p = jnp.exp2(s - m)
            denom = jnp.sum(p, axis=-1, keepdims=True) + jnp.exp2(sink - m)
            ost = _bdot(p.astype(BF16), vrep_ref[g, pl.ds(ks, BK), :]) / denom
            o = jnp.where(head_lanes[0], ost[0:BQ], 0.0)
            for hh in range(1, KV_GROUP):
                o = jnp.where(head_lanes[hh], ost[hh * BQ:(hh + 1) * BQ], o)
            o_ref[0, pl.ds(qs, BQ), g * GW:(g + 1) * GW] = o.astype(o_ref.dtype)
        return carry

    lax.fori_loop(0, S // BQ, body, 0, unroll=4)


def _attention(q, k, v, sink):
    B, S, NQ = q.shape
    NK = k.shape[2]
    n_kv = NK // HEAD_DIM
    return pl.pallas_call(
        functools.partial(_attn_kernel, S=S, n_kv=n_kv),
        grid=(B,),
        in_specs=[pl.BlockSpec(memory_space=pltpu.SMEM),
                  pl.BlockSpec((1, S, NQ), lambda b: (b, 0, 0)),
                  pl.BlockSpec((1, S, NK), lambda b: (b, 0, 0)),
                  pl.BlockSpec((1, S, NK), lambda b: (b, 0, 0))],
        out_specs=pl.BlockSpec((1, S, NQ), lambda b: (b, 0, 0)),
        out_shape=jax.ShapeDtypeStruct((B, S, NQ), BF16),
        scratch_shapes=[pltpu.VMEM((n_kv, S, KV_GROUP * HEAD_DIM), BF16),
                        pltpu.VMEM((n_kv, S, KV_GROUP * HEAD_DIM), BF16)],
        compiler_params=_cparams(1),
        name="window_attention",
    )(sink, q, k, v)


def kernel(x, norm_mix_pre, norm_mix_post, norm_mlp_pre, norm_mlp_post, w_up, w_down, hy_w_in, hy_conv_w, hy_conv_b, hy_f_w1, hy_f_b1, hy_f_w2, hy_f_b2, hy_f_w3, hy_f_b3, hy_f_freq, hy_f_wout, hy_bias, hy_w_out, at_w_qkv, at_sink, at_w_o):
    B, L, D = x.shape
    M = B * L
    P = CONV_BLOCKS
    n = L // P
    depth = norm_mix_pre.shape[0]
    fwd, inv, filt_c, filt_s = [jnp.asarray(a).astype(BF16) for a in _dft_tables(n)]

    h = x.reshape(M, D)
    for i in range(depth):
        j = i // 2
        if i % 2 == 0:
            gc, gs, hn = _filter_spectra(L, hy_f_w1[j], hy_f_b1[j], hy_f_w2[j], hy_f_b2[j],
                                         hy_f_w3[j], hy_f_b3[j], hy_f_freq[j], hy_f_wout[j],
                                         hy_bias[j], filt_c, filt_s, P, h, norm_mix_pre[i])
            z = _hyena_core(hn.reshape(B, L, D), hy_w_in[j], hy_conv_w[j], hy_conv_b[j].reshape(1, -1),
                            gc, gs, fwd, inv, P)
            a, w_o = z.reshape(M, D), hy_w_out[j]
        else:
            q, k, v = _qkv(h, norm_mix_pre[i], at_w_qkv[j], L)
            o = _attention(q.reshape(B, L, -1), k.reshape(B, L, -1), v.reshape(B, L, -1), at_sink[j])
            a, w_o = o.reshape(M, -1), at_w_o[j]
        h = _tail(a, w_o, h, norm_mix_post[i], norm_mlp_pre[i], norm_mlp_post[i], w_up, w_down, i)
    return h.reshape(B, L, D)
```

```python
import functools
import math

import numpy as np
import jax
import jax.numpy as jnp
from jax import lax
from jax.experimental import pallas as pl
from jax.experimental.pallas import tpu as pltpu

BF16 = jnp.bfloat16
F32 = jnp.float32

EPS = 1e-6
NEG = -1e30
LOG2E = math.log2(math.e)

HY_ORDER = 2
HY_EMB = 33
HY_BANDS = (HY_EMB - 1) // 2
HY_DECAY_TARGET = 1e-2
HY_FAST_DECAY = 0.3
HY_SLOW_DECAY = 1.5
HEAD_DIM = 64
KV_GROUP = 4
WINDOW = 128
ROT_DIM = HEAD_DIM // 4
ROPE_THETA = 500000.0

CONV_BLOCKS = 4
CH_BLOCK = 256
SPEC_DTYPE = BF16
STAGE_SLOTS = 4
TAIL_SLAB_ROWS = 256

VMEM_LIMIT = 56 * 1024 * 1024


def _cparams(n_axes, flags=None):
    return pltpu.CompilerParams(
        dimension_semantics=("arbitrary",) * n_axes,
        vmem_limit_bytes=VMEM_LIMIT,
        flags=flags)


def _resident(shape, index_map):
    return pl.BlockSpec(shape, index_map, pipeline_mode=pl.Buffered(1))


def _rms(x, g):
    ms = jnp.mean(x * x, axis=-1, keepdims=True)
    return x * lax.rsqrt(ms + EPS) * g


def _bdot(a, b):
    return jnp.dot(a, b, preferred_element_type=F32)


def _dft_tables(n):
    f = np.arange(n, dtype=np.int64)[:, None]
    r = np.arange(n, dtype=np.int64)[None, :]
    k_pos = ((2 * f + 1) * r) % (4 * n)
    k_neg = ((2 * f + 1) * (r - n)) % (4 * n)
    ang_pos = np.pi * k_pos / (2 * n)
    ang_neg = np.pi * k_neg / (2 * n)
    cpos, spos = np.cos(ang_pos), np.sin(ang_pos)
    cneg, sneg = np.cos(ang_neg), np.sin(ang_neg)
    cneg[:, 0] = 0.0
    sneg[:, 0] = 0.0
    fwd = np.concatenate([cpos, spos], axis=0)
    inv = np.concatenate([cpos.T, spos.T], axis=1) / n
    filt_c = np.concatenate([cneg, cpos], axis=1)
    filt_s = np.concatenate([sneg, spos], axis=1)
    return [np.asarray(a, np.float32) for a in (fwd, inv, filt_c, filt_s)]


def _filter_positions(L):
    t = np.linspace(0.0, 1.0, L)[:, None]
    w = (2.0 * np.pi / L) * np.arange(L)[:, None]
    f = np.linspace(1e-4, HY_BANDS - 1, HY_BANDS)[None, :]
    z = np.concatenate([t, np.cos(f * w), -np.sin(f * w)], axis=-1)
    return np.asarray(z, np.float32)


def _rope_tables(S):
    half = ROT_DIM // 2
    inv = ROPE_THETA ** (-np.arange(0, ROT_DIM, 2, dtype=np.float64) / ROT_DIM)
    ang = np.arange(S, dtype=np.float64)[:, None] * inv[None, :]
    lane = np.arange(128) % HEAD_DIM
    ang_l = ang[:, lane % half]
    cos_t = np.where(lane[None, :] < ROT_DIM, np.cos(ang_l), 1.0)
    sin_lo = np.where(lane[None, :] < half, -np.sin(ang_l), 0.0)
    sin_hi = np.where((lane[None, :] >= half) & (lane[None, :] < ROT_DIM), np.sin(ang_l), 0.0)
    return [np.asarray(a, np.float32) for a in (cos_t, sin_lo, sin_hi)]


def _spectra_kernel(z_ref, w1_ref, b1_ref, w2_ref, b2_ref, w3_ref, b3_ref, fr_ref,
                    woutf_ref, woutb_ref, dl_ref, bias_ref, mc_ref, ms_ref, x_ref, g_ref, wv_ref,
                    gc_ref, gs_ref, hn_ref, uv_ref, hid_ref, hf_ref, hb_ref,
                    *, P, n, norm_rows):
    hi = lax.Precision.HIGHEST

    L, cw = hf_ref.shape
    FW = hid_ref.shape[1]

    @pl.when(pl.program_id(0) == 0)
    def _():
        def two(v):
            return jnp.concatenate([v, v], axis=1)

        def blockdiag(w):
            zero = jnp.zeros_like(w)
            return jnp.concatenate([jnp.concatenate([w, zero], axis=1),
                                    jnp.concatenate([zero, w], axis=1)], axis=0)

        fr = two(fr_ref[...])
        h = z_ref[...]
        for w_ref, b_ref in ((w1_ref, b1_ref), (w2_ref, b2_ref), (w3_ref, b3_ref)):
            h = jnp.sin(fr * (jnp.dot(h, blockdiag(w_ref[...]), precision=hi) + two(b_ref[...])))
        hid_ref[0:L // 2, :] = h[:, :FW]
        hid_ref[L // 2:L, :] = h[:, FW:]

    row = lax.broadcasted_iota(jnp.int32, (L, cw), 0)
    decay = jnp.exp(-(row.astype(F32) * (1.0 / (L - 1))) * dl_ref[...])
    hid = hid_ref[...]
    hid_hi = hid.astype(BF16)
    hid_lo = (hid - hid_hi.astype(F32)).astype(BF16)

    def taps(wout_ref):
        wout = wout_ref[...]
        w_hi = wout.astype(BF16)
        w_lo = (wout - w_hi.astype(F32)).astype(BF16)
        return (_bdot(hid_hi, w_hi) + (_bdot(hid_hi, w_lo) + _bdot(hid_lo, w_hi))) * decay

    mc = mc_ref[...]
    ms = ms_ref[...]
    cpos = mc[:, n:]
    spos = ms[:, n:]
    norm_chunks = [slice(r, r + norm_rows) for r in range(0, x_ref.shape[0], norm_rows)]
    nd = 2 * P - 1
    hf_ref[...] = taps(woutf_ref).astype(hf_ref.dtype)
    hb_ref[...] = jnp.where(row > 0, taps(woutb_ref), 0.0).astype(hb_ref.dtype)
    for idx, d in enumerate(range(-(P - 1), P)):
        for r in norm_chunks[idx * len(norm_chunks) // nd:(idx + 1) * len(norm_chunks) // nd]:
            hn = _rms(x_ref[r, :], g_ref[...]).astype(hn_ref.dtype)
            hn_ref[r, :] = hn
            uv_ref[r, :] = _bdot(hn, wv_ref[...]).astype(uv_ref.dtype)
        if d == 0:
            hf0 = hf_ref[0:n, :].astype(F32)
            hb0 = hb_ref[0:n, :].astype(F32)
            gc = _bdot(cpos, (hf0 + hb0).astype(BF16)) + bias_ref[...]
            gs = _bdot(spos, (hf0 - hb0).astype(BF16))
        elif d > 0:
            seg = hf_ref[n * (d - 1):n * (d + 1), :]
            gc = _bdot(mc, seg)
            gs = _bdot(ms, seg)
        else:
            seg = hb_ref[n * (-d - 1):n * (-d + 1), :]
            gc = _bdot(mc, seg)
            gs = -_bdot(ms, seg)
        gc_ref[d + P - 1] = gc.astype(gc_ref.dtype)
        gs_ref[d + P - 1] = gs.astype(gs_ref.dtype)


def _filter_spectra(L, f_w1, f_b1, f_w2, f_b2, f_w3, f_b3, f_freq, f_wout, bias, mc, ms, P, x2d, g,
                    w_in, cw=256, norm_rows=256):
    M, D = x2d.shape
    FW = f_w1.shape[1]
    n = L // P
    half = f_wout.shape[1] // 2
    nblk = half // cw
    xm = M // nblk
    zpad = np.pad(_filter_positions(L), ((0, 0), (0, FW - HY_EMB)))
    z = jnp.asarray(np.concatenate([zpad[:L // 2], zpad[L // 2:]], axis=1))
    w1 = jnp.pad(f_w1, ((0, FW - HY_EMB), (0, 0)))
    max_decay = math.log(HY_DECAY_TARGET) / HY_FAST_DECAY
    min_decay = math.log(HY_DECAY_TARGET) / HY_SLOW_DECAY
    absdelta = jnp.asarray(np.abs(np.linspace(min_decay, max_decay, D)).astype(np.float32))[None, :]
    small = lambda shape: pl.BlockSpec(shape, lambda j: (0, 0))
    out_sd = jax.ShapeDtypeStruct((2 * P - 1, n, half), SPEC_DTYPE)
    return pl.pallas_call(
        functools.partial(_spectra_kernel, P=P, n=n, norm_rows=norm_rows),
        grid=(nblk,),
        in_specs=[small((L // 2, 2 * FW)), small((FW, FW)), small((1, FW)), small((FW, FW)), small((1, FW)),
                  small((FW, FW)), small((1, FW)), small((1, FW)),
                  pl.BlockSpec((FW, cw), lambda j: (0, j)),
                  pl.BlockSpec((FW, cw), lambda j: (0, nblk + j)),
                  pl.BlockSpec((1, cw), lambda j: (0, j % (D // cw))),
                  pl.BlockSpec((1, cw), lambda j: (0, j)),
                  _resident((n, 2 * n), lambda j: (0, 0)), _resident((n, 2 * n), lambda j: (0, 0)),
                  pl.BlockSpec((xm, D), lambda j: (j, 0)),
                  small((1, D)),
                  _resident((D, D), lambda j: (0, 0))],
        out_specs=[pl.BlockSpec((2 * P - 1, n, cw), lambda j: (0, 0, j)),
                   pl.BlockSpec((2 * P - 1, n, cw), lambda j: (0, 0, j)),
                   pl.BlockSpec((xm, D), lambda j: (j, 0)),
                   pl.BlockSpec((xm, D), lambda j: (j, 0))],
        out_shape=[out_sd, out_sd, jax.ShapeDtypeStruct((M, D), BF16),
                   jax.ShapeDtypeStruct((M, D), BF16)],
        scratch_shapes=[pltpu.VMEM((L, FW), F32),
                        pltpu.VMEM((L, cw), BF16),
                        pltpu.VMEM((L, cw), BF16)],
        compiler_params=_cparams(1),
        name="filter_spectra",
    )(z, w1, f_b1.reshape(1, FW), f_w2, f_b2.reshape(1, FW), f_w3, f_b3.reshape(1, FW),
      f_freq.reshape(1, FW), f_wout, f_wout, absdelta, bias.reshape(1, half), mc, ms,
      x2d, g.reshape(1, D), w_in[:, :D].astype(BF16))


def _hyena_kernel(hn_ref, uv_ref, w1_ref, w2_ref, cwv_ref, cw1_ref, cw2_ref,
                  cbv_ref, cb1_ref, cb2_ref, gc0_ref, gs0_ref, gc1_ref, gs1_ref,
                  fwd_ref, inv_ref, z_ref,
                  rawa_ref, rawb_ref, v_ref, gate_ref, uspec_ref, yspec_ref,
                  *, P, n, fchunk):
    row = lax.broadcasted_iota(jnp.int32, (n, 1), 0)
    blk = [slice(n * i, n * (i + 1)) for i in range(P)]
    chunks = n // fchunk
    fwd = fwd_ref[...]
    inv = inv_ref[...]

    w1, w2 = [r[...].astype(BF16) for r in (w1_ref, w2_ref)]

    def project(w, dst_ref, i):
        dst_ref[blk[i], :] = _bdot(hn_ref[0, blk[i], :], w)

    def short_conv(raw_ref, cw_ref, cb_ref, dst, i):
        cw = cw_ref[...]
        cur = raw_ref[blk[i], :]
        prev = pltpu.roll(cur, 1, axis=0)
        nxt = pltpu.roll(cur, n - 1, axis=0)
        first = raw_ref[n * i - 1:n * i, :] if i > 0 else jnp.zeros_like(cur[0:1])
        last = raw_ref[n * (i + 1):n * (i + 1) + 1, :] if i < P - 1 else jnp.zeros_like(cur[0:1])
        prev = jnp.where(row == 0, first, prev)
        nxt = jnp.where(row == n - 1, last, nxt)
        out = prev * cw[0:1] + cur * cw[1:2] + nxt * cw[2:3] + cb_ref[...]
        dst[blk[i], :] = out.astype(dst.dtype)

    def forward(src_ref, j):
        uspec_ref[j] = _bdot(fwd, src_ref[blk[j], :]).astype(uspec_ref.dtype)

    def combine(gc_ref, gs_ref, c):
        fc = slice(c * fchunk, (c + 1) * fchunk)
        fs = slice(n + c * fchunk, n + (c + 1) * fchunk)
        yc = [None] * P
        ys = [None] * P
        for j in range(P):
            uc = uspec_ref[j, fc, :]
            us = uspec_ref[j, fs, :]
            for i in range(P):
                gc = gc_ref[i - j + P - 1, fc, :]
                gs = gs_ref[i - j + P - 1, fc, :]
                tc = gc * uc - gs * us
                ts = gc * us + gs * uc
                yc[i] = tc if yc[i] is None else yc[i] + tc
                ys[i] = ts if ys[i] is None else ys[i] + ts
        for i in range(P):
            yspec_ref[i, fc, :] = yc[i].astype(BF16)
            yspec_ref[i, fs, :] = ys[i].astype(BF16)

    gate_proj = [(w1, rawb_ref, i) for i in range(P)] + [(w2, rawa_ref, i) for i in range(P)]
    per_combine = chunks // 3

    for i in range(P):
        rawa_ref[blk[i], :] = uv_ref[0, blk[i], :].astype(F32)
    for i in range(P):
        short_conv(rawa_ref, cwv_ref, cbv_ref, v_ref, i)
        if i < 3:
            project(*gate_proj[i])
    for i in range(P):
        forward(v_ref, i)
    for c in range(chunks):
        combine(gc0_ref, gs0_ref, c)
        if c % per_combine == 0 and c // per_combine < 3:
            project(*gate_proj[3 + c // per_combine])
    for i in range(P):
        short_conv(rawb_ref, cw1_ref, cb1_ref, gate_ref, i)
        v_ref[blk[i], :] = (gate_ref[blk[i], :] * _bdot(inv, yspec_ref[i])).astype(v_ref.dtype)
    for i in range(P):
        forward(v_ref, i)
    for c in range(chunks):
        combine(gc1_ref, gs1_ref, c)
        if c % (chunks // 2) == 0:
            project(*gate_proj[6 + c // (chunks // 2)])
    for i in range(P):
        short_conv(rawa_ref, cw2_ref, cb2_ref, gate_ref, i)
        z_ref[0, blk[i], :] = (gate_ref[blk[i], :] * _bdot(inv, yspec_ref[i])).astype(z_ref.dtype)


def _hyena_core(hn, uv, w_in, conv_w, conv_b, gc, gs, fwd, inv, P):
    B, L, D = hn.shape
    n = L // P
    cb_n = D // CH_BLOCK
    C = CH_BLOCK
    nd = gc.shape[0]

    def stream_spec(shape, s):
        return pl.BlockSpec(shape, lambda b, c: (0, s * cb_n + c))

    in_specs = (
        [pl.BlockSpec((1, L, D), lambda b, c: (b, 0, 0)),
         pl.BlockSpec((1, L, C), lambda b, c: (b, 0, c))]
        + [stream_spec((D, C), s) for s in (1, 2)]
        + [stream_spec((3, C), s) for s in range(3)]
        + [stream_spec((1, C), s) for s in range(3)]
        + [pl.BlockSpec((nd, n, C), lambda b, c: (0, 0, c)),
           pl.BlockSpec((nd, n, C), lambda b, c: (0, 0, c)),
           pl.BlockSpec((nd, n, C), lambda b, c: (0, 0, cb_n + c)),
           pl.BlockSpec((nd, n, C), lambda b, c: (0, 0, cb_n + c)),
           _resident((2 * n, n), lambda b, c: (0, 0)),
           _resident((n, 2 * n), lambda b, c: (0, 0))])
    return pl.pallas_call(
        functools.partial(_hyena_kernel, P=P, n=n, fchunk=16),
        grid=(B, cb_n),
        in_specs=in_specs,
        out_specs=pl.BlockSpec((1, L, C), lambda b, c: (b, 0, c)),
        out_shape=jax.ShapeDtypeStruct((B, L, D), BF16),
        scratch_shapes=[pltpu.VMEM((L, C), F32),
                        pltpu.VMEM((L, C), F32),
                        pltpu.VMEM((L, C), BF16),
                        pltpu.VMEM((L, C), F32),
                        pltpu.VMEM((P, 2 * n, C), SPEC_DTYPE),
                        pltpu.VMEM((P, 2 * n, C), BF16)],
        compiler_params=_cparams(2),
        name="hyena_core",
    )(hn, uv, w_in, w_in, conv_w, conv_w, conv_w, conv_b, conv_b, conv_b,
      gc, gs, gc, gs, fwd, inv)


def _stream_cast(src_hbm, layer, dst_ref, stage_ref, sem_ref):
    slots, rows = stage_ref.shape[0], stage_ref.shape[1]
    n_chunks = dst_ref.shape[0] // rows

    def copy(c):
        return pltpu.make_async_copy(src_hbm.at[layer, pl.ds(c * rows, rows), :],
                                     stage_ref.at[c % slots], sem_ref.at[c % slots])

    for c in range(min(slots - 1, n_chunks)):
        copy(c).start()
    for c in range(n_chunks):
        ahead = c + slots - 1
        if ahead < n_chunks:
            copy(ahead).start()
        copy(c).wait()
        dst_ref[c * rows:(c + 1) * rows, :] = stage_ref[c % slots].astype(dst_ref.dtype)


def _tail_kernel(a_ref, wo32_ref, r_ref, gmix_ref, gpre_ref, gpost_ref, wu_hbm, wd_hbm, o_ref,
                 wo_ref, wu_ref, wd_ref, stage_u, stage_d, sem_u, sem_d, *, fchunk, layer):
    @pl.when(pl.program_id(0) == 0)
    def _():
        wo_ref[...] = wo32_ref[...].astype(wo_ref.dtype)
        _stream_cast(wu_hbm, layer, wu_ref, stage_u, sem_u)
        _stream_cast(wd_hbm, layer, wd_ref, stage_d, sem_d)

    sm = TAIL_SLAB_ROWS
    FF = wu_ref.shape[1]
    for p0 in range(0, a_ref.shape[0], 2 * sm):
        rows = [slice(p0 + sm * u, p0 + sm * (u + 1)) for u in range(2)]
        ms = [_bdot(a_ref[r, :], wo_ref[...]) for r in rows]
        hs = [r_ref[r, :] + _rms(m, gmix_ref[...]) for r, m in zip(rows, ms)]
        hns = [_rms(h, gpre_ref[...]).astype(BF16) for h in hs]
        accs = [None, None]
        for c in range(FF // fchunk):
            acts = []
            for u in range(2):
                a = jnp.maximum(_bdot(hns[u], wu_ref[:, c * fchunk:(c + 1) * fchunk]), 0.0)
                acts.append((a * a).astype(BF16))
            for u in range(2):
                part = _bdot(acts[u], wd_ref[c * fchunk:(c + 1) * fchunk, :])
                accs[u] = part if accs[u] is None else accs[u] + part
        for u in range(2):
            o_ref[rows[u], :] = hs[u] + _rms(accs[u], gpost_ref[...])


def _tail(a, w_o, resid, g_mix, g_pre, g_post, w_up_all, w_down_all, layer, tm=512, fchunk=1024):
    M, K = a.shape
    D = w_o.shape[1]
    FF = w_up_all.shape[2]
    stage_bytes = 1024 * 1024
    rows_u = stage_bytes // (FF * 4)
    rows_d = stage_bytes // (D * 4)
    row = lambda shape: pl.BlockSpec(shape, lambda i: (i, 0))
    gain = pl.BlockSpec((1, D), lambda i: (0, 0))
    return pl.pallas_call(
        functools.partial(_tail_kernel, fchunk=fchunk, layer=layer),
        grid=(M // tm,),
        in_specs=[row((tm, K)), _resident((K, D), lambda i: (0, 0)), row((tm, D)),
                  gain, gain, gain,
                  pl.BlockSpec(memory_space=pl.ANY),
                  pl.BlockSpec(memory_space=pl.ANY)],
        out_specs=row((tm, D)),
        out_shape=jax.ShapeDtypeStruct((M, D), F32),
        scratch_shapes=[pltpu.VMEM((K, D), BF16),
                        pltpu.VMEM((D, FF), BF16),
                        pltpu.VMEM((FF, D), BF16),
                        pltpu.VMEM((STAGE_SLOTS, rows_u, FF), F32),
                        pltpu.VMEM((STAGE_SLOTS, rows_d, D), F32),
                        pltpu.SemaphoreType.DMA((STAGE_SLOTS,)),
                        pltpu.SemaphoreType.DMA((STAGE_SLOTS,))],
        compiler_params=_cparams(1),
        name="mixer_tail_mlp",
    )(a, w_o, resid, g_mix.reshape(1, D), g_pre.reshape(1, D), g_post.reshape(1, D),
      w_up_all, w_down_all)


def _qkv_kernel(h_ref, g_ref, w32_ref, ct_ref, slo_ref, shi_ref, q_ref, k_ref, v_ref, w_ref, *, halves):
    @pl.when(pl.program_id(0) == 0)
    def _():
        w_ref[...] = w32_ref[...].astype(w_ref.dtype)

    nq, nk = q_ref.shape[1], k_ref.shape[1]
    hm = h_ref.shape[0] // halves
    rows = [slice(hm * u, hm * (u + 1)) for u in range(halves)]
    half = ROT_DIM // 2
    scale = LOG2E * HEAD_DIM ** -0.5
    qkvs = [_bdot(_rms(h_ref[r, :], g_ref[...]).astype(BF16), w_ref[...]) for r in rows]
    for r, qkv in zip(rows, qkvs):
        ct = ct_ref[r, :]
        slo = slo_ref[r, :]
        shi = shi_ref[r, :]
        for j in range((nq + nk) // 128):
            t = qkv[:, 128 * j:128 * (j + 1)]
            rot = t * ct + pltpu.roll(t, 128 - half, axis=1) * slo + pltpu.roll(t, half, axis=1) * shi
            if 128 * j < nq:
                q_ref[r, 128 * j:128 * (j + 1)] = (rot * scale).astype(q_ref.dtype)
            else:
                k_ref[r, 128 * j - nq:128 * (j + 1) - nq] = rot.astype(k_ref.dtype)
        v_ref[r, :] = qkv[:, nq + nk:].astype(v_ref.dtype)


def _qkv(h, g, w, S, tm=1024, halves=2):
    M, D = h.shape
    nq = D
    nk = (w.shape[1] - nq) // 2
    ct, slo, shi = [jnp.asarray(a) for a in _rope_tables(S)]
    pb = S // tm
    tab = pl.BlockSpec((tm, 128), lambda i: (i % pb, 0))
    return pl.pallas_call(
        functools.partial(_qkv_kernel, halves=halves),
        grid=(M // tm,),
        in_specs=[pl.BlockSpec((tm, D), lambda i: (i, 0)),
                  pl.BlockSpec((1, D), lambda i: (0, 0)),
                  _resident((D, nq + 2 * nk), lambda i: (0, 0)),
                  tab, tab, tab],
        out_specs=[pl.BlockSpec((tm, nq), lambda i: (i, 0)),
                   pl.BlockSpec((tm, nk), lambda i: (i, 0)),
                   pl.BlockSpec((tm, nk), lambda i: (i, 0))],
        out_shape=[jax.ShapeDtypeStruct((M, nq), BF16),
                   jax.ShapeDtypeStruct((M, nk), BF16),
                   jax.ShapeDtypeStruct((M, nk), BF16)],
        scratch_shapes=[pltpu.VMEM(w.shape, BF16)],
        compiler_params=_cparams(1),
        name="qkv_rope",
    )(h, g.reshape(1, D), w, ct, slo, shi)


def _attn_kernel(sink_ref, q_ref, k_ref, v_ref, o_ref, krep_ref, vrep_ref, *, S, n_kv):
    BQ = WINDOW
    BK = 3 * WINDOW
    GW = KV_GROUP * HEAD_DIM
    dn = (((1,), (1,)), ((), ()))

    for g in range(n_kv):
        kg = k_ref[0, :, g * HEAD_DIM:(g + 1) * HEAD_DIM]
        vg = v_ref[0, :, g * HEAD_DIM:(g + 1) * HEAD_DIM]
        krep_ref[g] = jnp.concatenate([kg] * KV_GROUP, axis=1)
        vrep_ref[g] = jnp.concatenate([vg] * KV_GROUP, axis=1)

    lane = lax.broadcasted_iota(jnp.int32, (BQ, GW), 1)
    head_lanes = [(lane >= hh * HEAD_DIM) & (lane < (hh + 1) * HEAD_DIM) for hh in range(KV_GROUP)]
    rowblk = lax.broadcasted_iota(jnp.int32, (KV_GROUP * BQ, 1), 0) // BQ

    def body(qb, carry):
        qs = pl.multiple_of(qb * BQ, BQ)
        ks = pl.multiple_of(jnp.clip(qs - WINDOW, 0, S - BK), WINDOW)
        qpos = qs + lax.broadcasted_iota(jnp.int32, (BQ, BK), 0)
        kpos = ks + lax.broadcasted_iota(jnp.int32, (BQ, BK), 1)
        maskadd = jnp.where(jnp.abs(kpos - qpos) <= WINDOW, 0.0, NEG).astype(F32)
        maskadd = jnp.concatenate([maskadd] * KV_GROUP, axis=0)
        for g in range(n_kv):
            qg = q_ref[0, pl.ds(qs, BQ), g * GW:(g + 1) * GW]
            qst = jnp.concatenate([jnp.where(head_lanes[hh], qg, jnp.zeros_like(qg))
                                   for hh in range(KV_GROUP)], axis=0)
            sink = jnp.zeros((KV_GROUP * BQ, 1), F32)
            for hh in range(KV_GROUP):
                sink = jnp.where(rowblk == hh, sink_ref[g * KV_GROUP + hh] * LOG2E, sink)
            s = lax.dot_general(qst, krep_ref[g, pl.ds(ks, BK), :], dn,
                                preferred_element_type=F32) + maskadd
            m = jnp.maximum(jnp.max(s, axis=-1, keepdims=True), sink)
            p = jnp.exp2(s - m)
            denom = jnp.sum(p, axis=-1, keepdims=True) + jnp.exp2(sink - m)
            ost = _bdot(p.astype(BF16), vrep_ref[g, pl.ds(ks, BK), :]) / denom
            o = jnp.where(head_lanes[0], ost[0:BQ], 0.0)
            for hh in range(1, KV_GROUP):
                o = jnp.where(head_lanes[hh], ost[hh * BQ:(hh + 1) * BQ], o)
            o_ref[0, pl.ds(qs, BQ), g * GW:(g + 1) * GW] = o.astype(o_ref.dtype)
        return carry

    lax.fori_loop(0, S // BQ, body, 0, unroll=4)


def _attention(q, k, v, sink):
    B, S, NQ = q.shape
    NK = k.shape[2]
    n_kv = NK // HEAD_DIM
    return pl.pallas_call(
        functools.partial(_attn_kernel, S=S, n_kv=n_kv),
        grid=(B,),
        in_specs=[pl.BlockSpec(memory_space=pltpu.SMEM),
                  pl.BlockSpec((1, S, NQ), lambda b: (b, 0, 0)),
                  pl.BlockSpec((1, S, NK), lambda b: (b, 0, 0)),
                  pl.BlockSpec((1, S, NK), lambda b: (b, 0, 0))],
        out_specs=pl.BlockSpec((1, S, NQ), lambda b: (b, 0, 0)),
        out_shape=jax.ShapeDtypeStruct((B, S, NQ), BF16),
        scratch_shapes=[pltpu.VMEM((n_kv, S, KV_GROUP * HEAD_DIM), BF16),
                        pltpu.VMEM((n_kv, S, KV_GROUP * HEAD_DIM), BF16)],
        compiler_params=_cparams(1),
        name="window_attention",
    )(sink, q, k, v)


def kernel(x, norm_mix_pre, norm_mix_post, norm_mlp_pre, norm_mlp_post, w_up, w_down, hy_w_in, hy_conv_w, hy_conv_b, hy_f_w1, hy_f_b1, hy_f_w2, hy_f_b2, hy_f_w3, hy_f_b3, hy_f_freq, hy_f_wout, hy_bias, hy_w_out, at_w_qkv, at_sink, at_w_o):
    B, L, D = x.shape
    M = B * L
    P = CONV_BLOCKS
    n = L // P
    depth = norm_mix_pre.shape[0]
    fwd, inv, filt_c, filt_s = [jnp.asarray(a).astype(BF16) for a in _dft_tables(n)]

    h = x.reshape(M, D)
    for i in range(depth):
        j = i // 2
        if i % 2 == 0:
            gc, gs, hn, uv = _filter_spectra(L, hy_f_w1[j], hy_f_b1[j], hy_f_w2[j], hy_f_b2[j],
                                             hy_f_w3[j], hy_f_b3[j], hy_f_freq[j], hy_f_wout[j],
                                             hy_bias[j], filt_c, filt_s, P, h, norm_mix_pre[i],
                                             hy_w_in[j])
            z = _hyena_core(hn.reshape(B, L, D), uv.reshape(B, L, D), hy_w_in[j],
                            hy_conv_w[j], hy_conv_b[j].reshape(1, -1),
                            gc, gs, fwd, inv, P)
            a, w_o = z.reshape(M, D), hy_w_out[j]
        else:
            q, k, v = _qkv(h, norm_mix_pre[i], at_w_qkv[j], L)
            o = _attention(q.reshape(B, L, -1), k.reshape(B, L, -1), v.reshape(B, L, -1), at_sink[j])
            a, w_o = o.reshape(M, -1), at_w_o[j]
        h = _tail(a, w_o, h, norm_mix_post[i], norm_mlp_pre[i], norm_mlp_post[i], w_up, w_down, i)
    return h.reshape(B, L, D)
```

```python
import functools
import math

import numpy as np
import jax
import jax.numpy as jnp
from jax import lax
from jax.experimental import pallas as pl
from jax.experimental.pallas import tpu as pltpu

BF16 = jnp.bfloat16
F32 = jnp.float32

EPS = 1e-6
NEG = -1e30
LOG2E = math.log2(math.e)

HY_ORDER = 2
HY_EMB = 33
HY_BANDS = (HY_EMB - 1) // 2
HY_DECAY_TARGET = 1e-2
HY_FAST_DECAY = 0.3
HY_SLOW_DECAY = 1.5
HEAD_DIM = 64
KV_GROUP = 4
WINDOW = 128
ROT_DIM = HEAD_DIM // 4
ROPE_THETA = 500000.0

CONV_BLOCKS = 4
CH_BLOCK = 256
SPEC_DTYPE = BF16
TAIL_SLAB_ROWS = 256

VMEM_LIMIT = 56 * 1024 * 1024


def _cparams(n_axes, flags=None):
    return pltpu.CompilerParams(
        dimension_semantics=("arbitrary",) * n_axes,
        vmem_limit_bytes=VMEM_LIMIT,
        flags=flags)


def _resident(shape, index_map):
    return pl.BlockSpec(shape, index_map, pipeline_mode=pl.Buffered(1))


def _rms(x, g):
    ms = jnp.mean(x * x, axis=-1, keepdims=True)
    return x * lax.rsqrt(ms + EPS) * g


def _bdot(a, b):
    return jnp.dot(a, b, preferred_element_type=F32)


def _dft_tables(n):
    f = np.arange(n, dtype=np.int64)[:, None]
    r = np.arange(n, dtype=np.int64)[None, :]
    k_pos = ((2 * f + 1) * r) % (4 * n)
    k_neg = ((2 * f + 1) * (r - n)) % (4 * n)
    ang_pos = np.pi * k_pos / (2 * n)
    ang_neg = np.pi * k_neg / (2 * n)
    cpos, spos = np.cos(ang_pos), np.sin(ang_pos)
    cneg, sneg = np.cos(ang_neg), np.sin(ang_neg)
    cneg[:, 0] = 0.0
    sneg[:, 0] = 0.0
    fwd = np.concatenate([cpos, spos], axis=0)
    inv = np.concatenate([cpos.T, spos.T], axis=1) / n
    filt_c = np.concatenate([cneg, cpos], axis=1)
    filt_s = np.concatenate([sneg, spos], axis=1)
    return [np.asarray(a, np.float32) for a in (fwd, inv, filt_c, filt_s)]


def _filter_positions(L):
    t = np.linspace(0.0, 1.0, L)[:, None]
    w = (2.0 * np.pi / L) * np.arange(L)[:, None]
    f = np.linspace(1e-4, HY_BANDS - 1, HY_BANDS)[None, :]
    z = np.concatenate([t, np.cos(f * w), -np.sin(f * w)], axis=-1)
    return np.asarray(z, np.float32)


def _rope_tables(S):
    half = ROT_DIM // 2
    inv = ROPE_THETA ** (-np.arange(0, ROT_DIM, 2, dtype=np.float64) / ROT_DIM)
    ang = np.arange(S, dtype=np.float64)[:, None] * inv[None, :]
    lane = np.arange(128) % HEAD_DIM
    ang_l = ang[:, lane % half]
    cos_t = np.where(lane[None, :] < ROT_DIM, np.cos(ang_l), 1.0)
    sin_lo = np.where(lane[None, :] < half, -np.sin(ang_l), 0.0)
    sin_hi = np.where((lane[None, :] >= half) & (lane[None, :] < ROT_DIM), np.sin(ang_l), 0.0)
    return [np.asarray(a, np.float32) for a in (cos_t, sin_lo, sin_hi)]


def _spectra_kernel(z_ref, w1_ref, b1_ref, w2_ref, b2_ref, w3_ref, b3_ref, fr_ref,
                    woutf_ref, woutb_ref, dl_ref, bias_ref, mc_ref, ms_ref, x_ref, g_ref,
                    gc_ref, gs_ref, hn_ref, hid_ref, hf_ref, hb_ref, *, P, n, norm_rows):
    hi = lax.Precision.HIGHEST

    L, cw = hf_ref.shape
    FW = hid_ref.shape[1]

    @pl.when(pl.program_id(0) == 0)
    def _():
        def two(v):
            return jnp.concatenate([v, v], axis=1)

        def blockdiag(w):
            zero = jnp.zeros_like(w)
            return jnp.concatenate([jnp.concatenate([w, zero], axis=1),
                                    jnp.concatenate([zero, w], axis=1)], axis=0)

        fr = two(fr_ref[...])
        h = z_ref[...]
        for w_ref, b_ref in ((w1_ref, b1_ref), (w2_ref, b2_ref), (w3_ref, b3_ref)):
            h = jnp.sin(fr * (jnp.dot(h, blockdiag(w_ref[...]), precision=hi) + two(b_ref[...])))
        hid_ref[0:L // 2, :] = h[:, :FW]
        hid_ref[L // 2:L, :] = h[:, FW:]

    row = lax.broadcasted_iota(jnp.int32, (L, cw), 0)
    decay = jnp.exp(-(row.astype(F32) * (1.0 / (L - 1))) * dl_ref[...])
    hid = hid_ref[...]
    hid_hi = hid.astype(BF16)
    hid_lo = (hid - hid_hi.astype(F32)).astype(BF16)

    def taps(wout_ref):
        wout = wout_ref[...]
        w_hi = wout.astype(BF16)
        w_lo = (wout - w_hi.astype(F32)).astype(BF16)
        return (_bdot(hid_hi, w_hi) + (_bdot(hid_hi, w_lo) + _bdot(hid_lo, w_hi))) * decay

    hf_ref[...] = taps(woutf_ref).astype(hf_ref.dtype)
    hb_ref[...] = jnp.where(row > 0, taps(woutb_ref), 0.0).astype(hb_ref.dtype)

    mc = mc_ref[...]
    ms = ms_ref[...]
    cpos = mc[:, n:]
    spos = ms[:, n:]
    norm_chunks = [slice(r, r + norm_rows) for r in range(0, x_ref.shape[0], norm_rows)]
    nd = 2 * P - 1
    for idx, d in enumerate(range(-(P - 1), P)):
        for r in norm_chunks[idx * len(norm_chunks) // nd:(idx + 1) * len(norm_chunks) // nd]:
            hn_ref[r, :] = _rms(x_ref[r, :], g_ref[...]).astype(hn_ref.dtype)
        if d == 0:
            hf0 = hf_ref[0:n, :].astype(F32)
            hb0 = hb_ref[0:n, :].astype(F32)
            gc = _bdot(cpos, (hf0 + hb0).astype(BF16)) + bias_ref[...]
            gs = _bdot(spos, (hf0 - hb0).astype(BF16))
        elif d > 0:
            taps = hf_ref[n * (d - 1):n * (d + 1), :].astype(BF16)
            gc = _bdot(mc, taps)
            gs = _bdot(ms, taps)
        else:
            a = -d
            taps = hb_ref[n * (a - 1):n * (a + 1), :].astype(BF16)
            gc = _bdot(mc, taps)
            gs = -_bdot(ms, taps)
        gc_ref[d + P - 1] = gc.astype(gc_ref.dtype)
        gs_ref[d + P - 1] = gs.astype(gs_ref.dtype)


def _filter_spectra(L, f_w1, f_b1, f_w2, f_b2, f_w3, f_b3, f_freq, f_wout, bias, mc, ms, P, x2d, g,
                    cw=256, norm_rows=256):
    M, D = x2d.shape
    FW = f_w1.shape[1]
    n = L // P
    half = f_wout.shape[1] // 2
    nblk = half // cw
    xm = M // nblk
    zpad = np.pad(_filter_positions(L), ((0, 0), (0, FW - HY_EMB)))
    z = jnp.asarray(np.concatenate([zpad[:L // 2], zpad[L // 2:]], axis=1))
    w1 = jnp.pad(f_w1, ((0, FW - HY_EMB), (0, 0)))
    max_decay = math.log(HY_DECAY_TARGET) / HY_FAST_DECAY
    min_decay = math.log(HY_DECAY_TARGET) / HY_SLOW_DECAY
    absdelta = jnp.asarray(np.abs(np.linspace(min_decay, max_decay, D)).astype(np.float32))[None, :]
    small = lambda shape: pl.BlockSpec(shape, lambda j: (0, 0))
    out_sd = jax.ShapeDtypeStruct((2 * P - 1, n, half), SPEC_DTYPE)
    return pl.pallas_call(
        functools.partial(_spectra_kernel, P=P, n=n, norm_rows=norm_rows),
        grid=(nblk,),
        in_specs=[small((L // 2, 2 * FW)), small((FW, FW)), small((1, FW)), small((FW, FW)), small((1, FW)),
                  small((FW, FW)), small((1, FW)), small((1, FW)),
                  pl.BlockSpec((FW, cw), lambda j: (0, j)),
                  pl.BlockSpec((FW, cw), lambda j: (0, nblk + j)),
                  pl.BlockSpec((1, cw), lambda j: (0, j % (D // cw))),
                  pl.BlockSpec((1, cw), lambda j: (0, j)),
                  small((n, 2 * n)), small((n, 2 * n)),
                  pl.BlockSpec((xm, D), lambda j: (j, 0)),
                  small((1, D))],
        out_specs=[pl.BlockSpec((2 * P - 1, n, cw), lambda j: (0, 0, j)),
                   pl.BlockSpec((2 * P - 1, n, cw), lambda j: (0, 0, j)),
                   pl.BlockSpec((xm, D), lambda j: (j, 0))],
        out_shape=[out_sd, out_sd, jax.ShapeDtypeStruct((M, D), BF16)],
        scratch_shapes=[pltpu.VMEM((L, FW), F32),
                        pltpu.VMEM((L, cw), BF16),
                        pltpu.VMEM((L, cw), BF16)],
        compiler_params=_cparams(1),
        name="filter_spectra",
    )(z, w1, f_b1.reshape(1, FW), f_w2, f_b2.reshape(1, FW), f_w3, f_b3.reshape(1, FW),
      f_freq.reshape(1, FW), f_wout, f_wout, absdelta, bias.reshape(1, half), mc, ms,
      x2d, g.reshape(1, D))


def _hyena_kernel(*refs, P, n, fchunk, n_cast):
    (hn_ref, wv_ref, w1_ref, w2_ref, cwv_ref, cw1_ref, cw2_ref, cbv_ref, cb1_ref, cb2_ref,
     gc0_ref, gs0_ref, gc1_ref, gs1_ref, fwd_ref, inv_ref) = refs[:16]
    cast_in = refs[16:16 + n_cast]
    z_ref = refs[16 + n_cast]
    cast_out = refs[17 + n_cast:17 + 2 * n_cast]
    rawa_ref, rawb_ref, v_ref, gate_ref, uspec_ref, yspec_ref = refs[17 + 2 * n_cast:]

    for src, dst in zip(cast_in, cast_out):
        dst[...] = src[...].astype(dst.dtype)

    row = lax.broadcasted_iota(jnp.int32, (n, 1), 0)
    blk = [slice(n * i, n * (i + 1)) for i in range(P)]
    chunks = n // fchunk
    fwd = fwd_ref[...]
    inv = inv_ref[...]

    wv, w1, w2 = [r[...].astype(BF16) for r in (wv_ref, w1_ref, w2_ref)]

    def project(w, dst_ref, i):
        dst_ref[blk[i], :] = _bdot(hn_ref[0, blk[i], :], w)

    def short_conv(raw_ref, cw_ref, cb_ref, dst, i):
        cw = cw_ref[...]
        cur = raw_ref[blk[i], :]
        prev = pltpu.roll(cur, 1, axis=0)
        nxt = pltpu.roll(cur, n - 1, axis=0)
        first = raw_ref[n * i - 1:n * i, :] if i > 0 else jnp.zeros_like(cur[0:1])
        last = raw_ref[n * (i + 1):n * (i + 1) + 1, :] if i < P - 1 else jnp.zeros_like(cur[0:1])
        prev = jnp.where(row == 0, first, prev)
        nxt = jnp.where(row == n - 1, last, nxt)
        out = prev * cw[0:1] + cur * cw[1:2] + nxt * cw[2:3] + cb_ref[...]
        dst[blk[i], :] = out.astype(dst.dtype)

    def forward(src_ref, j):
        uspec_ref[j] = _bdot(fwd, src_ref[blk[j], :]).astype(uspec_ref.dtype)

    def combine(gc_ref, gs_ref, c):
        fc = slice(c * fchunk, (c + 1) * fchunk)
        fs = slice(n + c * fchunk, n + (c + 1) * fchunk)
        yc = [None] * P
        ys = [None] * P
        for j in range(P):
            uc = uspec_ref[j, fc, :]
            us = uspec_ref[j, fs, :]
            for i in range(P):
                gc = gc_ref[i - j + P - 1, fc, :]
                gs = gs_ref[i - j + P - 1, fc, :]
                tc = gc * uc - gs * us
                ts = gc * us + gs * uc
                yc[i] = tc if yc[i] is None else yc[i] + tc
                ys[i] = ts if ys[i] is None else ys[i] + ts
        for i in range(P):
            yspec_ref[i, fc, :] = yc[i].astype(BF16)
            yspec_ref[i, fs, :] = ys[i].astype(BF16)

    gate_proj = [(w1, rawb_ref, i) for i in range(P)] + [(w2, rawa_ref, i) for i in range(P)]
    per_combine = chunks // 3

    for i in range(P):
        project(wv, rawa_ref, i)
    for i in range(P):
        short_conv(rawa_ref, cwv_ref, cbv_ref, v_ref, i)
        if i < 3:
            project(*gate_proj[i])
    for i in range(P):
        forward(v_ref, i)
    for c in range(chunks):
        combine(gc0_ref, gs0_ref, c)
        if c % per_combine == 0 and c // per_combine < 3:
            project(*gate_proj[3 + c // per_combine])
    for i in range(P):
        short_conv(rawb_ref, cw1_ref, cb1_ref, gate_ref, i)
        v_ref[blk[i], :] = (gate_ref[blk[i], :] * _bdot(inv, yspec_ref[i])).astype(v_ref.dtype)
    for i in range(P):
        forward(v_ref, i)
    for c in range(chunks):
        combine(gc1_ref, gs1_ref, c)
        if c % (chunks // 2) == 0:
            project(*gate_proj[6 + c // (chunks // 2)])
    for i in range(P):
        short_conv(rawa_ref, cw2_ref, cb2_ref, gate_ref, i)
        z_ref[0, blk[i], :] = (gate_ref[blk[i], :] * _bdot(inv, yspec_ref[i])).astype(z_ref.dtype)


def _cast_rider(weights, n_steps, step_of):
    args, in_specs, out_specs, out_shapes = [], [], [], []
    for arr, layer in weights:
        _, R, C = arr.shape
        rps = R // n_steps
        args.append(arr)
        in_specs.append(pl.BlockSpec((None, rps, C), lambda *g, layer=layer: (layer, step_of(*g), 0)))
        out_specs.append(pl.BlockSpec((rps, C), lambda *g: (step_of(*g), 0)))
        out_shapes.append(jax.ShapeDtypeStruct((R, C), BF16))
    return args, in_specs, out_specs, out_shapes


def _hyena_core(hn, w_in, conv_w, conv_b, gc, gs, fwd, inv, P, cast_weights):
    B, L, D = hn.shape
    n = L // P
    cb_n = D // CH_BLOCK
    C = CH_BLOCK
    nd = gc.shape[0]
    c_args, c_in, c_out, c_shapes = _cast_rider(cast_weights, B * cb_n, lambda b, c: b * cb_n + c)

    def stream_spec(shape, s):
        return pl.BlockSpec(shape, lambda b, c: (0, s * cb_n + c))

    in_specs = (
        [pl.BlockSpec((1, L, D), lambda b, c: (b, 0, 0))]
        + [stream_spec((D, C), s) for s in range(3)]
        + [stream_spec((3, C), s) for s in range(3)]
        + [stream_spec((1, C), s) for s in range(3)]
        + [pl.BlockSpec((nd, n, C), lambda b, c: (0, 0, c)),
           pl.BlockSpec((nd, n, C), lambda b, c: (0, 0, c)),
           pl.BlockSpec((nd, n, C), lambda b, c: (0, 0, cb_n + c)),
           pl.BlockSpec((nd, n, C), lambda b, c: (0, 0, cb_n + c)),
           _resident((2 * n, n), lambda b, c: (0, 0)),
           _resident((n, 2 * n), lambda b, c: (0, 0))]
        + c_in)
    return pl.pallas_call(
        functools.partial(_hyena_kernel, P=P, n=n, fchunk=16, n_cast=len(c_args)),
        grid=(B, cb_n),
        in_specs=in_specs,
        out_specs=[pl.BlockSpec((1, L, C), lambda b, c: (b, 0, c))] + c_out,
        out_shape=[jax.ShapeDtypeStruct((B, L, D), BF16)] + c_shapes,
        scratch_shapes=[pltpu.VMEM((L, C), F32),
                        pltpu.VMEM((L, C), F32),
                        pltpu.VMEM((L, C), BF16),
                        pltpu.VMEM((L, C), F32),
                        pltpu.VMEM((P, 2 * n, C), SPEC_DTYPE),
                        pltpu.VMEM((P, 2 * n, C), BF16)],
        compiler_params=_cparams(2),
        name="hyena_core",
    )(hn, w_in, w_in, w_in, conv_w, conv_w, conv_w, conv_b, conv_b, conv_b,
      gc, gs, gc, gs, fwd, inv, *c_args)


def _tail_kernel(a_ref, wo_ref, r_ref, gmix_ref, gpre_ref, gpost_ref, wu_ref, wd_ref, o_ref, *, fchunk):
    sm = TAIL_SLAB_ROWS
    FF = wu_ref.shape[1]
    for p0 in range(0, a_ref.shape[0], 2 * sm):
        rows = [slice(p0 + sm * u, p0 + sm * (u + 1)) for u in range(2)]
        ms = [_bdot(a_ref[r, :], wo_ref[...]) for r in rows]
        hs = [r_ref[r, :] + _rms(m, gmix_ref[...]) for r, m in zip(rows, ms)]
        hns = [_rms(h, gpre_ref[...]).astype(BF16) for h in hs]
        accs = [None, None]
        for c in range(FF // fchunk):
            acts = []
            for u in range(2):
                a = jnp.maximum(_bdot(hns[u], wu_ref[:, c * fchunk:(c + 1) * fchunk]), 0.0)
                acts.append((a * a).astype(BF16))
            for u in range(2):
                part = _bdot(acts[u], wd_ref[c * fchunk:(c + 1) * fchunk, :])
                accs[u] = part if accs[u] is None else accs[u] + part
        for u in range(2):
            o_ref[rows[u], :] = hs[u] + _rms(accs[u], gpost_ref[...])


def _tail(a, w_o, resid, g_mix, g_pre, g_post, w_up, w_down, tm=512, fchunk=1024):
    M, K = a.shape
    D = w_o.shape[1]
    FF = w_up.shape[1]
    row = lambda shape: pl.BlockSpec(shape, lambda i: (i, 0))
    gain = pl.BlockSpec((1, D), lambda i: (0, 0))
    return pl.pallas_call(
        functools.partial(_tail_kernel, fchunk=fchunk),
        grid=(M // tm,),
        in_specs=[row((tm, K)), _resident((K, D), lambda i: (0, 0)), row((tm, D)),
                  gain, gain, gain,
                  _resident((D, FF), lambda i: (0, 0)),
                  _resident((FF, D), lambda i: (0, 0))],
        out_specs=row((tm, D)),
        out_shape=jax.ShapeDtypeStruct((M, D), F32),
        compiler_params=_cparams(1),
        name="mixer_tail_mlp",
    )(a, w_o, resid, g_mix.reshape(1, D), g_pre.reshape(1, D), g_post.reshape(1, D), w_up, w_down)


def _qkv_kernel(*refs, halves, n_cast):
    h_ref, g_ref, w32_ref, ct_ref, slo_ref, shi_ref = refs[:6]
    cast_in = refs[6:6 + n_cast]
    q_ref, k_ref, v_ref = refs[6 + n_cast:9 + n_cast]
    cast_out = refs[9 + n_cast:9 + 2 * n_cast]
    w_ref = refs[9 + 2 * n_cast]

    @pl.when(pl.program_id(0) == 0)
    def _():
        w_ref[...] = w32_ref[...].astype(w_ref.dtype)

    for src, dst in zip(cast_in, cast_out):
        dst[...] = src[...].astype(dst.dtype)

    nq, nk = q_ref.shape[1], k_ref.shape[1]
    hm = h_ref.shape[0] // halves
    rows = [slice(hm * u, hm * (u + 1)) for u in range(halves)]
    half = ROT_DIM // 2
    scale = LOG2E * HEAD_DIM ** -0.5
    qkvs = [_bdot(_rms(h_ref[r, :], g_ref[...]).astype(BF16), w_ref[...]) for r in rows]
    for r, qkv in zip(rows, qkvs):
        ct = ct_ref[r, :]
        slo = slo_ref[r, :]
        shi = shi_ref[r, :]
        for j in range((nq + nk) // 128):
            t = qkv[:, 128 * j:128 * (j + 1)]
            rot = t * ct + pltpu.roll(t, 128 - half, axis=1) * slo + pltpu.roll(t, half, axis=1) * shi
            if 128 * j < nq:
                q_ref[r, 128 * j:128 * (j + 1)] = (rot * scale).astype(q_ref.dtype)
            else:
                k_ref[r, 128 * j - nq:128 * (j + 1) - nq] = rot.astype(k_ref.dtype)
        v_ref[r, :] = qkv[:, nq + nk:].astype(v_ref.dtype)


def _qkv(h, g, w, S, cast_weights, tm=1024, halves=2):
    M, D = h.shape
    nq = D
    nk = (w.shape[1] - nq) // 2
    ct, slo, shi = [jnp.asarray(a) for a in _rope_tables(S)]
    pb = S // tm
    tab = pl.BlockSpec((tm, 128), lambda i: (i % pb, 0))
    c_args, c_in, c_out, c_shapes = _cast_rider(cast_weights, M // tm, lambda i: i)
    return pl.pallas_call(
        functools.partial(_qkv_kernel, halves=halves, n_cast=len(c_args)),
        grid=(M // tm,),
        in_specs=[pl.BlockSpec((tm, D), lambda i: (i, 0)),
                  pl.BlockSpec((1, D), lambda i: (0, 0)),
                  _resident((D, nq + 2 * nk), lambda i: (0, 0)),
                  tab, tab, tab] + c_in,
        out_specs=[pl.BlockSpec((tm, nq), lambda i: (i, 0)),
                   pl.BlockSpec((tm, nk), lambda i: (i, 0)),
                   pl.BlockSpec((tm, nk), lambda i: (i, 0))] + c_out,
        out_shape=[jax.ShapeDtypeStruct((M, nq), BF16),
                   jax.ShapeDtypeStruct((M, nk), BF16),
                   jax.ShapeDtypeStruct((M, nk), BF16)] + c_shapes,
        scratch_shapes=[pltpu.VMEM(w.shape, BF16)],
        compiler_params=_cparams(1),
        name="qkv_rope",
    )(h, g.reshape(1, D), w, ct, slo, shi, *c_args)


def _attn_kernel(sink_ref, q_ref, k_ref, v_ref, o_ref, krep_ref, vrep_ref, *, S, n_kv):
    BQ = WINDOW
    BK = 3 * WINDOW
    GW = KV_GROUP * HEAD_DIM
    dn = (((1,), (1,)), ((), ()))

    for g in range(n_kv):
        kg = k_ref[0, :, g * HEAD_DIM:(g + 1) * HEAD_DIM]
        vg = v_ref[0, :, g * HEAD_DIM:(g + 1) * HEAD_DIM]
        krep_ref[g] = jnp.concatenate([kg] * KV_GROUP, axis=1)
        vrep_ref[g] = jnp.concatenate([vg] * KV_GROUP, axis=1)

    lane = lax.broadcasted_iota(jnp.int32, (BQ, GW), 1)
    head_lanes = [(lane >= hh * HEAD_DIM) & (lane < (hh + 1) * HEAD_DIM) for hh in range(KV_GROUP)]
    rowblk = lax.broadcasted_iota(jnp.int32, (KV_GROUP * BQ, 1), 0) // BQ

    def body(qb, carry):
        qs = pl.multiple_of(qb * BQ, BQ)
        ks = pl.multiple_of(jnp.clip(qs - WINDOW, 0, S - BK), WINDOW)
        qpos = qs + lax.broadcasted_iota(jnp.int32, (BQ, BK), 0)
        kpos = ks + lax.broadcasted_iota(jnp.int32, (BQ, BK), 1)
        maskadd = jnp.where(jnp.abs(kpos - qpos) <= WINDOW, 0.0, NEG).astype(F32)
        maskadd = jnp.concatenate([maskadd] * KV_GROUP, axis=0)
        for g in range(n_kv):
            qg = q_ref[0, pl.ds(qs, BQ), g * GW:(g + 1) * GW]
            qst = jnp.concatenate([jnp.where(head_lanes[hh], qg, jnp.zeros_like(qg))
                                   for hh in range(KV_GROUP)], axis=0)
            sink = jnp.zeros((KV_GROUP * BQ, 1), F32)
            for hh in range(KV_GROUP):
                sink = jnp.where(rowblk == hh, sink_ref[g * KV_GROUP + hh] * LOG2E, sink)
            s = lax.dot_general(qst, krep_ref[g, pl.ds(ks, BK), :], dn,
                                preferred_element_type=F32) + maskadd
            m = jnp.maximum(jnp.max(s, axis=-1, keepdims=True), sink)
            p = jnp.exp2(s - m)
            denom = jnp.sum(p, axis=-1, keepdims=True) + jnp.exp2(sink - m)
            ost = _bdot(p.astype(BF16), vrep_ref[g, pl.ds(ks, BK), :]) / denom
            o = jnp.where(head_lanes[0], ost[0:BQ], 0.0)
            for hh in range(1, KV_GROUP):
                o = jnp.where(head_lanes[hh], ost[hh * BQ:(hh + 1) * BQ], o)
            o_ref[0, pl.ds(qs, BQ), g * GW:(g + 1) * GW] = o.astype(o_ref.dtype)
        return carry

    lax.fori_loop(0, S // BQ, body, 0, unroll=4)


def _attention(q, k, v, sink):
    B, S, NQ = q.shape
    NK = k.shape[2]
    n_kv = NK // HEAD_DIM
    return pl.pallas_call(
        functools.partial(_attn_kernel, S=S, n_kv=n_kv),
        grid=(B,),
        in_specs=[pl.BlockSpec(memory_space=pltpu.SMEM),
                  pl.BlockSpec((1, S, NQ), lambda b: (b, 0, 0)),
                  pl.BlockSpec((1, S, NK), lambda b: (b, 0, 0)),
                  pl.BlockSpec((1, S, NK), lambda b: (b, 0, 0))],
        out_specs=pl.BlockSpec((1, S, NQ), lambda b: (b, 0, 0)),
        out_shape=jax.ShapeDtypeStruct((B, S, NQ), BF16),
        scratch_shapes=[pltpu.VMEM((n_kv, S, KV_GROUP * HEAD_DIM), BF16),
                        pltpu.VMEM((n_kv, S, KV_GROUP * HEAD_DIM), BF16)],
        compiler_params=_cparams(1),
        name="window_attention",
    )(sink, q, k, v)


def kernel(x, norm_mix_pre, norm_mix_post, norm_mlp_pre, norm_mlp_post, w_up, w_down, hy_w_in, hy_conv_w, hy_conv_b, hy_f_w1, hy_f_b1, hy_f_w2, hy_f_b2, hy_f_w3, hy_f_b3, hy_f_freq, hy_f_wout, hy_bias, hy_w_out, at_w_qkv, at_sink, at_w_o):
    B, L, D = x.shape
    M = B * L
    P = CONV_BLOCKS
    n = L // P
    depth = norm_mix_pre.shape[0]
    fwd, inv, filt_c, filt_s = [jnp.asarray(a).astype(BF16) for a in _dft_tables(n)]

    h = x.reshape(M, D)
    for i in range(depth):
        j = i // 2
        if i % 2 == 0:
            gc, gs, hn = _filter_spectra(L, hy_f_w1[j], hy_f_b1[j], hy_f_w2[j], hy_f_b2[j],
                                         hy_f_w3[j], hy_f_b3[j], hy_f_freq[j], hy_f_wout[j],
                                         hy_bias[j], filt_c, filt_s, P, h, norm_mix_pre[i])
            z, wo_b, wu_b, wd_b = _hyena_core(
                hn.reshape(B, L, D), hy_w_in[j], hy_conv_w[j], hy_conv_b[j].reshape(1, -1),
                gc, gs, fwd, inv, P, [(hy_w_out, j), (w_up, i), (w_down, i)])
            a = z.reshape(M, D)
        else:
            q, k, v, wo_b, wu_b, wd_b = _qkv(h, norm_mix_pre[i], at_w_qkv[j], L,
                                             [(at_w_o, j), (w_up, i), (w_down, i)])
            o = _attention(q.reshape(B, L, -1), k.reshape(B, L, -1), v.reshape(B, L, -1), at_sink[j])
            a = o.reshape(M, -1)
        h = _tail(a, wo_b, h, norm_mix_post[i], norm_mlp_pre[i], norm_mlp_post[i], wu_b, wd_b)
    return h.reshape(B, L, D)
```

```python
import functools
import math

import numpy as np
import jax
import jax.numpy as jnp
from jax import lax
from jax.experimental import pallas as pl
from jax.experimental.pallas import tpu as pltpu

BF16 = jnp.bfloat16
F32 = jnp.float32

EPS = 1e-6
NEG = -1e30
LOG2E = math.log2(math.e)

HY_ORDER = 2
HY_EMB = 33
HY_BANDS = (HY_EMB - 1) // 2
HY_DECAY_TARGET = 1e-2
HY_FAST_DECAY = 0.3
HY_SLOW_DECAY = 1.5
HEAD_DIM = 64
KV_GROUP = 4
WINDOW = 128
ROT_DIM = HEAD_DIM // 4
ROPE_THETA = 500000.0

CONV_BLOCKS = 4
CH_BLOCK = 256
SPEC_DTYPE = BF16
TAIL_SLAB_ROWS = 256

VMEM_LIMIT = 56 * 1024 * 1024


def _cparams(n_axes, flags=None):
    return pltpu.CompilerParams(
        dimension_semantics=("arbitrary",) * n_axes,
        vmem_limit_bytes=VMEM_LIMIT,
        flags=flags)


def _resident(shape, index_map):
    return pl.BlockSpec(shape, index_map, pipeline_mode=pl.Buffered(1))


def _rms(x, g):
    ms = jnp.mean(x * x, axis=-1, keepdims=True)
    return x * lax.rsqrt(ms + EPS) * g


def _bdot(a, b):
    return jnp.dot(a, b, preferred_element_type=F32)


def _dft_tables(n):
    f = np.arange(n, dtype=np.int64)[:, None]
    r = np.arange(n, dtype=np.int64)[None, :]
    k_pos = ((2 * f + 1) * r) % (4 * n)
    k_neg = ((2 * f + 1) * (r - n)) % (4 * n)
    ang_pos = np.pi * k_pos / (2 * n)
    ang_neg = np.pi * k_neg / (2 * n)
    cpos, spos = np.cos(ang_pos), np.sin(ang_pos)
    cneg, sneg = np.cos(ang_neg), np.sin(ang_neg)
    cneg[:, 0] = 0.0
    sneg[:, 0] = 0.0
    fwd = np.concatenate([cpos, spos], axis=0)
    inv = np.concatenate([cpos.T, spos.T], axis=1) / n
    filt_c = np.concatenate([cneg, cpos], axis=1)
    filt_s = np.concatenate([sneg, spos], axis=1)
    return [np.asarray(a, np.float32) for a in (fwd, inv, filt_c, filt_s)]


def _filter_positions(L):
    t = np.linspace(0.0, 1.0, L)[:, None]
    w = (2.0 * np.pi / L) * np.arange(L)[:, None]
    f = np.linspace(1e-4, HY_BANDS - 1, HY_BANDS)[None, :]
    z = np.concatenate([t, np.cos(f * w), -np.sin(f * w)], axis=-1)
    return np.asarray(z, np.float32)


def _rope_tables(S):
    half = ROT_DIM // 2
    inv = ROPE_THETA ** (-np.arange(0, ROT_DIM, 2, dtype=np.float64) / ROT_DIM)
    ang = np.arange(S, dtype=np.float64)[:, None] * inv[None, :]
    lane = np.arange(128) % HEAD_DIM
    ang_l = ang[:, lane % half]
    cos_t = np.where(lane[None, :] < ROT_DIM, np.cos(ang_l), 1.0)
    sin_lo = np.where(lane[None, :] < half, -np.sin(ang_l), 0.0)
    sin_hi = np.where((lane[None, :] >= half) & (lane[None, :] < ROT_DIM), np.sin(ang_l), 0.0)
    return [np.asarray(a, np.float32) for a in (cos_t, sin_lo, sin_hi)]


def _spectra_kernel(z_ref, w1_ref, b1_ref, w2_ref, b2_ref, w3_ref, b3_ref, fr_ref,
                    woutf_ref, woutb_ref, dl_ref, bias_ref, mc_ref, ms_ref, x_ref, g_ref,
                    gc_ref, gs_ref, hn_ref, hid_ref, hf_ref, hb_ref, *, P, n, norm_rows):
    hi = lax.Precision.HIGHEST

    L, cw = hf_ref.shape
    FW = hid_ref.shape[1]

    @pl.when(pl.program_id(0) == 0)
    def _():
        def two(v):
            return jnp.concatenate([v, v], axis=1)

        def blockdiag(w):
            zero = jnp.zeros_like(w)
            return jnp.concatenate([jnp.concatenate([w, zero], axis=1),
                                    jnp.concatenate([zero, w], axis=1)], axis=0)

        fr = two(fr_ref[...])
        h = z_ref[...]
        for w_ref, b_ref in ((w1_ref, b1_ref), (w2_ref, b2_ref), (w3_ref, b3_ref)):
            h = jnp.sin(fr * (jnp.dot(h, blockdiag(w_ref[...]), precision=hi) + two(b_ref[...])))
        hid_ref[0:L // 2, :] = h[:, :FW]
        hid_ref[L // 2:L, :] = h[:, FW:]

    row = lax.broadcasted_iota(jnp.int32, (L, cw), 0)
    decay = jnp.exp(-(row.astype(F32) * (1.0 / (L - 1))) * dl_ref[...])
    hid = hid_ref[...]
    hid_hi = hid.astype(BF16)
    hid_lo = (hid - hid_hi.astype(F32)).astype(BF16)

    def taps(wout_ref):
        wout = wout_ref[...]
        w_hi = wout.astype(BF16)
        w_lo = (wout - w_hi.astype(F32)).astype(BF16)
        return (_bdot(hid_hi, w_hi) + (_bdot(hid_hi, w_lo) + _bdot(hid_lo, w_hi))) * decay

    hf_ref[...] = taps(woutf_ref).astype(hf_ref.dtype)
    hb_ref[...] = jnp.where(row > 0, taps(woutb_ref), 0.0).astype(hb_ref.dtype)

    mc = mc_ref[...]
    ms = ms_ref[...]
    cpos = mc[:, n:]
    spos = ms[:, n:]
    norm_chunks = [slice(r, r + norm_rows) for r in range(0, x_ref.shape[0], norm_rows)]
    nd = 2 * P - 1
    for idx, d in enumerate(range(-(P - 1), P)):
        for r in norm_chunks[idx * len(norm_chunks) // nd:(idx + 1) * len(norm_chunks) // nd]:
            hn_ref[r, :] = _rms(x_ref[r, :], g_ref[...]).astype(hn_ref.dtype)
        if d == 0:
            hf0 = hf_ref[0:n, :].astype(F32)
            hb0 = hb_ref[0:n, :].astype(F32)
            gc = _bdot(cpos, (hf0 + hb0).astype(BF16)) + bias_ref[...]
            gs = _bdot(spos, (hf0 - hb0).astype(BF16))
        elif d > 0:
            taps = hf_ref[n * (d - 1):n * (d + 1), :].astype(BF16)
            gc = _bdot(mc, taps)
            gs = _bdot(ms, taps)
        else:
            a = -d
            taps = hb_ref[n * (a - 1):n * (a + 1), :].astype(BF16)
            gc = _bdot(mc, taps)
            gs = -_bdot(ms, taps)
        gc_ref[d + P - 1] = gc.astype(gc_ref.dtype)
        gs_ref[d + P - 1] = gs.astype(gs_ref.dtype)


def _filter_spectra(L, f_w1, f_b1, f_w2, f_b2, f_w3, f_b3, f_freq, f_wout, bias, mc, ms, P, x2d, g,
                    cw=256, norm_rows=256):
    M, D = x2d.shape
    FW = f_w1.shape[1]
    n = L // P
    half = f_wout.shape[1] // 2
    nblk = half // cw
    xm = M // nblk
    zpad = np.pad(_filter_positions(L), ((0, 0), (0, FW - HY_EMB)))
    z = jnp.asarray(np.concatenate([zpad[:L // 2], zpad[L // 2:]], axis=1))
    w1 = jnp.pad(f_w1, ((0, FW - HY_EMB), (0, 0)))
    max_decay = math.log(HY_DECAY_TARGET) / HY_FAST_DECAY
    min_decay = math.log(HY_DECAY_TARGET) / HY_SLOW_DECAY
    absdelta = jnp.asarray(np.abs(np.linspace(min_decay, max_decay, D)).astype(np.float32))[None, :]
    small = lambda shape: pl.BlockSpec(shape, lambda j: (0, 0))
    out_sd = jax.ShapeDtypeStruct((2 * P - 1, n, half), SPEC_DTYPE)
    return pl.pallas_call(
        functools.partial(_spectra_kernel, P=P, n=n, norm_rows=norm_rows),
        grid=(nblk,),
        in_specs=[small((L // 2, 2 * FW)), small((FW, FW)), small((1, FW)), small((FW, FW)), small((1, FW)),
                  small((FW, FW)), small((1, FW)), small((1, FW)),
                  pl.BlockSpec((FW, cw), lambda j: (0, j)),
                  pl.BlockSpec((FW, cw), lambda j: (0, nblk + j)),
                  pl.BlockSpec((1, cw), lambda j: (0, j % (D // cw))),
                  pl.BlockSpec((1, cw), lambda j: (0, j)),
                  small((n, 2 * n)), small((n, 2 * n)),
                  pl.BlockSpec((xm, D), lambda j: (j, 0)),
                  small((1, D))],
        out_specs=[pl.BlockSpec((2 * P - 1, n, cw), lambda j: (0, 0, j)),
                   pl.BlockSpec((2 * P - 1, n, cw), lambda j: (0, 0, j)),
                   pl.BlockSpec((xm, D), lambda j: (j, 0))],
        out_shape=[out_sd, out_sd, jax.ShapeDtypeStruct((M, D), BF16)],
        scratch_shapes=[pltpu.VMEM((L, FW), F32),
                        pltpu.VMEM((L, cw), BF16),
                        pltpu.VMEM((L, cw), BF16)],
        compiler_params=_cparams(1),
        name="filter_spectra",
    )(z, w1, f_b1.reshape(1, FW), f_w2, f_b2.reshape(1, FW), f_w3, f_b3.reshape(1, FW),
      f_freq.reshape(1, FW), f_wout, f_wout, absdelta, bias.reshape(1, half), mc, ms,
      x2d, g.reshape(1, D))


def _hyena_kernel(*refs, P, n, fchunk, n_cast):
    (hn_ref, wv_ref, w1_ref, w2_ref, cwv_ref, cw1_ref, cw2_ref, cbv_ref, cb1_ref, cb2_ref,
     gc0_ref, gs0_ref, gc1_ref, gs1_ref, fwd_ref, inv_ref) = refs[:16]
    cast_in = refs[16:16 + n_cast]
    z_ref = refs[16 + n_cast]
    cast_out = refs[17 + n_cast:17 + 2 * n_cast]
    rawa_ref, rawb_ref, v_ref, gate_ref, uspec_ref, yspec_ref = refs[17 + 2 * n_cast:]

    for src, dst in zip(cast_in, cast_out):
        dst[...] = src[...].astype(dst.dtype)

    row = lax.broadcasted_iota(jnp.int32, (n, 1), 0)
    blk = [slice(n * i, n * (i + 1)) for i in range(P)]
    chunks = n // fchunk
    fwd = fwd_ref[...]
    inv = inv_ref[...]

    wv, w1, w2 = [r[...].astype(BF16) for r in (wv_ref, w1_ref, w2_ref)]

    def project(w, dst_ref, i):
        dst_ref[blk[i], :] = _bdot(hn_ref[0, blk[i], :], w)

    def short_conv(raw_ref, cw_ref, cb_ref, dst, i):
        cw = cw_ref[...]
        cur = raw_ref[blk[i], :]
        prev = pltpu.roll(cur, 1, axis=0)
        nxt = pltpu.roll(cur, n - 1, axis=0)
        first = raw_ref[n * i - 1:n * i, :] if i > 0 else jnp.zeros_like(cur[0:1])
        last = raw_ref[n * (i + 1):n * (i + 1) + 1, :] if i < P - 1 else jnp.zeros_like(cur[0:1])
        prev = jnp.where(row == 0, first, prev)
        nxt = jnp.where(row == n - 1, last, nxt)
        out = prev * cw[0:1] + cur * cw[1:2] + nxt * cw[2:3] + cb_ref[...]
        dst[blk[i], :] = out.astype(dst.dtype)

    def forward(src_ref, j):
        uspec_ref[j] = _bdot(fwd, src_ref[blk[j], :]).astype(uspec_ref.dtype)

    def combine(gc_ref, gs_ref, c):
        fc = slice(c * fchunk, (c + 1) * fchunk)
        fs = slice(n + c * fchunk, n + (c + 1) * fchunk)
        yc = [None] * P
        ys = [None] * P
        for j in range(P):
            uc = uspec_ref[j, fc, :]
            us = uspec_ref[j, fs, :]
            for i in range(P):
                gc = gc_ref[i - j + P - 1, fc, :]
                gs = gs_ref[i - j + P - 1, fc, :]
                tc = gc * uc - gs * us
                ts = gc * us + gs * uc
                yc[i] = tc if yc[i] is None else yc[i] + tc
                ys[i] = ts if ys[i] is None else ys[i] + ts
        for i in range(P):
            yspec_ref[i, fc, :] = yc[i].astype(BF16)
            yspec_ref[i, fs, :] = ys[i].astype(BF16)

    gate_proj = [(w1, rawb_ref, i) for i in range(P)] + [(w2, rawa_ref, i) for i in range(P)]
    per_combine = chunks // 3

    for i in range(P):
        project(wv, rawa_ref, i)
    for i in range(P):
        short_conv(rawa_ref, cwv_ref, cbv_ref, v_ref, i)
        if i < 3:
            project(*gate_proj[i])
    for i in range(P):
        forward(v_ref, i)
    for c in range(chunks):
        combine(gc0_ref, gs0_ref, c)
        if c % per_combine == 0 and c // per_combine < 3:
            project(*gate_proj[3 + c // per_combine])
    for i in range(P):
        short_conv(rawb_ref, cw1_ref, cb1_ref, gate_ref, i)
        v_ref[blk[i], :] = (gate_ref[blk[i], :] * _bdot(inv, yspec_ref[i])).astype(v_ref.dtype)
    for i in range(P):
        forward(v_ref, i)
    for c in range(chunks):
        combine(gc1_ref, gs1_ref, c)
        if c % (chunks // 2) == 0:
            project(*gate_proj[6 + c // (chunks // 2)])
    for i in range(P):
        short_conv(rawa_ref, cw2_ref, cb2_ref, gate_ref, i)
        z_ref[0, blk[i], :] = (gate_ref[blk[i], :] * _bdot(inv, yspec_ref[i])).astype(z_ref.dtype)


def _cast_rider(weights, n_steps, step_of):
    args, in_specs, out_specs, out_shapes = [], [], [], []
    for arr, layer in weights:
        _, R, C = arr.shape
        rps = R // n_steps
        args.append(arr)
        in_specs.append(pl.BlockSpec((None, rps, C), lambda *g, layer=layer: (layer, step_of(*g), 0)))
        out_specs.append(pl.BlockSpec((rps, C), lambda *g: (step_of(*g), 0)))
        out_shapes.append(jax.ShapeDtypeStruct((R, C), BF16))
    return args, in_specs, out_specs, out_shapes


def _hyena_core(hn, w_in, conv_w, conv_b, gc, gs, fwd, inv, P, cast_weights):
    B, L, D = hn.shape
    n = L // P
    cb_n = D // CH_BLOCK
    C = CH_BLOCK
    nd = gc.shape[0]
    c_args, c_in, c_out, c_shapes = _cast_rider(cast_weights, B * cb_n, lambda b, c: b * cb_n + c)

    def stream_spec(shape, s):
        return pl.BlockSpec(shape, lambda b, c: (0, s * cb_n + c))

    in_specs = (
        [pl.BlockSpec((1, L, D), lambda b, c: (b, 0, 0))]
        + [stream_spec((D, C), s) for s in range(3)]
        + [stream_spec((3, C), s) for s in range(3)]
        + [stream_spec((1, C), s) for s in range(3)]
        + [pl.BlockSpec((nd, n, C), lambda b, c: (0, 0, c)),
           pl.BlockSpec((nd, n, C), lambda b, c: (0, 0, c)),
           pl.BlockSpec((nd, n, C), lambda b, c: (0, 0, cb_n + c)),
           pl.BlockSpec((nd, n, C), lambda b, c: (0, 0, cb_n + c)),
           _resident((2 * n, n), lambda b, c: (0, 0)),
           _resident((n, 2 * n), lambda b, c: (0, 0))]
        + c_in)
    return pl.pallas_call(
        functools.partial(_hyena_kernel, P=P, n=n, fchunk=16, n_cast=len(c_args)),
        grid=(B, cb_n),
        in_specs=in_specs,
        out_specs=[pl.BlockSpec((1, L, C), lambda b, c: (b, 0, c))] + c_out,
        out_shape=[jax.ShapeDtypeStruct((B, L, D), BF16)] + c_shapes,
        scratch_shapes=[pltpu.VMEM((L, C), F32),
                        pltpu.VMEM((L, C), F32),
                        pltpu.VMEM((L, C), BF16),
                        pltpu.VMEM((L, C), F32),
                        pltpu.VMEM((P, 2 * n, C), SPEC_DTYPE),
                        pltpu.VMEM((P, 2 * n, C), BF16)],
        compiler_params=_cparams(2),
        name="hyena_core",
    )(hn, w_in, w_in, w_in, conv_w, conv_w, conv_w, conv_b, conv_b, conv_b,
      gc, gs, gc, gs, fwd, inv, *c_args)


def _tail_kernel(a_ref, wo_ref, r_ref, gmix_ref, gpre_ref, gpost_ref, wu_ref, wd_ref, o_ref, *, fchunk):
    sm = TAIL_SLAB_ROWS
    FF = wu_ref.shape[1]
    for p0 in range(0, a_ref.shape[0], 2 * sm):
        rows = [slice(p0 + sm * u, p0 + sm * (u + 1)) for u in range(2)]
        ms = [_bdot(a_ref[r, :], wo_ref[...]) for r in rows]
        hs = [r_ref[r, :] + _rms(m, gmix_ref[...]) for r, m in zip(rows, ms)]
        hns = [_rms(h, gpre_ref[...]).astype(BF16) for h in hs]
        accs = [None, None]
        for c in range(FF // fchunk):
            acts = []
            for u in range(2):
                a = jnp.maximum(_bdot(hns[u], wu_ref[:, c * fchunk:(c + 1) * fchunk]), 0.0)
                acts.append((a * a).astype(BF16))
            for u in range(2):
                part = _bdot(acts[u], wd_ref[c * fchunk:(c + 1) * fchunk, :])
                accs[u] = part if accs[u] is None else accs[u] + part
        for u in range(2):
            o_ref[rows[u], :] = hs[u] + _rms(accs[u], gpost_ref[...])


def _tail(a, w_o, resid, g_mix, g_pre, g_post, w_up, w_down, tm=512, fchunk=1024):
    M, K = a.shape
    D = w_o.shape[1]
    FF = w_up.shape[1]
    row = lambda shape: pl.BlockSpec(shape, lambda i: (i, 0))
    gain = pl.BlockSpec((1, D), lambda i: (0, 0))
    return pl.pallas_call(
        functools.partial(_tail_kernel, fchunk=fchunk),
        grid=(M // tm,),
        in_specs=[row((tm, K)), _resident((K, D), lambda i: (0, 0)), row((tm, D)),
                  gain, gain, gain,
                  _resident((D, FF), lambda i: (0, 0)),
                  _resident((FF, D), lambda i: (0, 0))],
        out_specs=row((tm, D)),
        out_shape=jax.ShapeDtypeStruct((M, D), F32),
        compiler_params=_cparams(1),
        name="mixer_tail_mlp",
    )(a, w_o, resid, g_mix.reshape(1, D), g_pre.reshape(1, D), g_post.reshape(1, D), w_up, w_down)


def _qkv_kernel(h_ref, g_ref, w32_ref, ct_ref, slo_ref, shi_ref, q_ref, k_ref, v_ref, w_ref, *, halves):
    @pl.when(pl.program_id(0) == 0)
    def _():
        w_ref[...] = w32_ref[...].astype(w_ref.dtype)

    nq, nk = q_ref.shape[1], k_ref.shape[1]
    hm = h_ref.shape[0] // halves
    rows = [slice(hm * u, hm * (u + 1)) for u in range(halves)]
    half = ROT_DIM // 2
    scale = LOG2E * HEAD_DIM ** -0.5
    qkvs = [_bdot(_rms(h_ref[r, :], g_ref[...]).astype(BF16), w_ref[...]) for r in rows]
    for r, qkv in zip(rows, qkvs):
        ct = ct_ref[r, :]
        slo = slo_ref[r, :]
        shi = shi_ref[r, :]
        for j in range((nq + nk) // 128):
            t = qkv[:, 128 * j:128 * (j + 1)]
            rot = t * ct + pltpu.roll(t, 128 - half, axis=1) * slo + pltpu.roll(t, half, axis=1) * shi
            if 128 * j < nq:
                q_ref[r, 128 * j:128 * (j + 1)] = (rot * scale).astype(q_ref.dtype)
            else:
                k_ref[r, 128 * j - nq:128 * (j + 1) - nq] = rot.astype(k_ref.dtype)
        v_ref[r, :] = qkv[:, nq + nk:].astype(v_ref.dtype)


def _qkv(h, g, w, S, tm=1024, halves=2):
    M, D = h.shape
    nq = D
    nk = (w.shape[1] - nq) // 2
    ct, slo, shi = [jnp.asarray(a) for a in _rope_tables(S)]
    pb = S // tm
    tab = pl.BlockSpec((tm, 128), lambda i: (i % pb, 0))
    return pl.pallas_call(
        functools.partial(_qkv_kernel, halves=halves),
        grid=(M // tm,),
        in_specs=[pl.BlockSpec((tm, D), lambda i: (i, 0)),
                  pl.BlockSpec((1, D), lambda i: (0, 0)),
                  _resident((D, nq + 2 * nk), lambda i: (0, 0)),
                  tab, tab, tab],
        out_specs=[pl.BlockSpec((tm, nq), lambda i: (i, 0)),
                   pl.BlockSpec((tm, nk), lambda i: (i, 0)),
                   pl.BlockSpec((tm, nk), lambda i: (i, 0))],
        out_shape=[jax.ShapeDtypeStruct((M, nq), BF16),
                   jax.ShapeDtypeStruct((M, nk), BF16),
                   jax.ShapeDtypeStruct((M, nk), BF16)],
        scratch_shapes=[pltpu.VMEM(w.shape, BF16)],
        compiler_params=_cparams(1),
        name="qkv_rope",
    )(h, g.reshape(1, D), w, ct, slo, shi)


def _attn_kernel(*refs, S, n_kv, n_cast):
    sink_ref, q_ref, k_ref, v_ref = refs[:4]
    cast_in = refs[4:4 + n_cast]
    o_ref = refs[4 + n_cast]
    cast_out = refs[5 + n_cast:5 + 2 * n_cast]
    krep_ref, vrep_ref = refs[5 + 2 * n_cast:]
    BQ = WINDOW
    BK = 3 * WINDOW
    GW = KV_GROUP * HEAD_DIM
    dn = (((1,), (1,)), ((), ()))

    for src, dst in zip(cast_in, cast_out):
        dst[...] = src[...].astype(dst.dtype)

    for g in range(n_kv):
        kg = k_ref[0, :, g * HEAD_DIM:(g + 1) * HEAD_DIM]
        vg = v_ref[0, :, g * HEAD_DIM:(g + 1) * HEAD_DIM]
        krep_ref[g] = jnp.concatenate([kg] * KV_GROUP, axis=1)
        vrep_ref[g] = jnp.concatenate([vg] * KV_GROUP, axis=1)

    lane = lax.broadcasted_iota(jnp.int32, (BQ, GW), 1)
    head_lanes = [(lane >= hh * HEAD_DIM) & (lane < (hh + 1) * HEAD_DIM) for hh in range(KV_GROUP)]
    rowblk = lax.broadcasted_iota(jnp.int32, (KV_GROUP * BQ, 1), 0) // BQ

    def body(qb, carry):
        qs = pl.multiple_of(qb * BQ, BQ)
        ks = pl.multiple_of(jnp.clip(qs - WINDOW, 0, S - BK), WINDOW)
        qpos = qs + lax.broadcasted_iota(jnp.int32, (BQ, BK), 0)
        kpos = ks + lax.broadcasted_iota(jnp.int32, (BQ, BK), 1)
        maskadd = jnp.where(jnp.abs(kpos - qpos) <= WINDOW, 0.0, NEG).astype(F32)
        maskadd = jnp.concatenate([maskadd] * KV_GROUP, axis=0)
        for g in range(n_kv):
            qg = q_ref[0, pl.ds(qs, BQ), g * GW:(g + 1) * GW]
            qst = jnp.concatenate([jnp.where(head_lanes[hh], qg, jnp.zeros_like(qg))
                                   for hh in range(KV_GROUP)], axis=0)
            sink = jnp.zeros((KV_GROUP * BQ, 1), F32)
            for hh in range(KV_GROUP):
                sink = jnp.where(rowblk == hh, sink_ref[g * KV_GROUP + hh] * LOG2E, sink)
            s = lax.dot_general(qst, krep_ref[g, pl.ds(ks, BK), :], dn,
                                preferred_element_type=F32) + maskadd
            m = jnp.maximum(jnp.max(s, axis=-1, keepdims=True), sink)
            p = jnp.exp2(s - m)
            denom = jnp.sum(p, axis=-1, keepdims=True) + jnp.exp2(sink - m)
            ost = _bdot(p.astype(BF16), vrep_ref[g, pl.ds(ks, BK), :]) / denom
            o = jnp.where(head_lanes[0], ost[0:BQ], 0.0)
            for hh in range(1, KV_GROUP):
                o = jnp.where(head_lanes[hh], ost[hh * BQ:(hh + 1) * BQ], o)
            o_ref[0, pl.ds(qs, BQ), g * GW:(g + 1) * GW] = o.astype(o_ref.dtype)
        return carry

    lax.fori_loop(0, S // BQ, body, 0, unroll=4)


def _attention(q, k, v, sink, cast_weights):
    B, S, NQ = q.shape
    NK = k.shape[2]
    n_kv = NK // HEAD_DIM
    c_args, c_in, c_out, c_shapes = _cast_rider(cast_weights, B, lambda b: b)
    return pl.pallas_call(
        functools.partial(_attn_kernel, S=S, n_kv=n_kv, n_cast=len(c_args)),
        grid=(B,),
        in_specs=[pl.BlockSpec(memory_space=pltpu.SMEM),
                  pl.BlockSpec((1, S, NQ), lambda b: (b, 0, 0)),
                  pl.BlockSpec((1, S, NK), lambda b: (b, 0, 0)),
                  pl.BlockSpec((1, S, NK), lambda b: (b, 0, 0))] + c_in,
        out_specs=[pl.BlockSpec((1, S, NQ), lambda b: (b, 0, 0))] + c_out,
        out_shape=[jax.ShapeDtypeStruct((B, S, NQ), BF16)] + c_shapes,
        scratch_shapes=[pltpu.VMEM((n_kv, S, KV_GROUP * HEAD_DIM), BF16),
                        pltpu.VMEM((n_kv, S, KV_GROUP * HEAD_DIM), BF16)],
        compiler_params=_cparams(1),
        name="window_attention",
    )(sink, q, k, v, *c_args)


def kernel(x, norm_mix_pre, norm_mix_post, norm_mlp_pre, norm_mlp_post, w_up, w_down, hy_w_in, hy_conv_w, hy_conv_b, hy_f_w1, hy_f_b1, hy_f_w2, hy_f_b2, hy_f_w3, hy_f_b3, hy_f_freq, hy_f_wout, hy_bias, hy_w_out, at_w_qkv, at_sink, at_w_o):
    B, L, D = x.shape
    M = B * L
    P = CONV_BLOCKS
    n = L // P
    depth = norm_mix_pre.shape[0]
    fwd, inv, filt_c, filt_s = [jnp.asarray(a).astype(BF16) for a in _dft_tables(n)]

    h = x.reshape(M, D)
    for i in range(depth):
        j = i // 2
        if i % 2 == 0:
            gc, gs, hn = _filter_spectra(L, hy_f_w1[j], hy_f_b1[j], hy_f_w2[j], hy_f_b2[j],
                                         hy_f_w3[j], hy_f_b3[j], hy_f_freq[j], hy_f_wout[j],
                                         hy_bias[j], filt_c, filt_s, P, h, norm_mix_pre[i])
            z, wo_b, wu_b, wd_b = _hyena_core(
                hn.reshape(B, L, D), hy_w_in[j], hy_conv_w[j], hy_conv_b[j].reshape(1, -1),
                gc, gs, fwd, inv, P, [(hy_w_out, j), (w_up, i), (w_down, i)])
            a = z.reshape(M, D)
        else:
            q, k, v = _qkv(h, norm_mix_pre[i], at_w_qkv[j], L)
            o, wo_b, wu_b, wd_b = _attention(
                q.reshape(B, L, -1), k.reshape(B, L, -1), v.reshape(B, L, -1), at_sink[j],
                [(at_w_o, j), (w_up, i), (w_down, i)])
            a = o.reshape(M, -1)
        h = _tail(a, wo_b, h, norm_mix_post[i], norm_mlp_pre[i], norm_mlp_post[i], wu_b, wd_b)
    return h.reshape(B, L, D)
```

```python
import functools
import math

import numpy as np
import jax
import jax.numpy as jnp
from jax import lax
from jax.experimental import pallas as pl
from jax.experimental.pallas import tpu as pltpu

BF16 = jnp.bfloat16
F32 = jnp.float32

EPS = 1e-6
NEG = -1e30
LOG2E = math.log2(math.e)

HY_ORDER = 2
HY_EMB = 33
HY_BANDS = (HY_EMB - 1) // 2
HY_DECAY_TARGET = 1e-2
HY_FAST_DECAY = 0.3
HY_SLOW_DECAY = 1.5
HEAD_DIM = 64
KV_GROUP = 4
WINDOW = 128
ROT_DIM = HEAD_DIM // 4
ROPE_THETA = 500000.0

CONV_BLOCKS = 4
CH_BLOCK = 256
SPEC_DTYPE = BF16
TAIL_SLAB_ROWS = 256

VMEM_LIMIT = 56 * 1024 * 1024


def _cparams(n_axes, flags=None):
    return pltpu.CompilerParams(
        dimension_semantics=("arbitrary",) * n_axes,
        vmem_limit_bytes=VMEM_LIMIT,
        flags=flags)


def _resident(shape, index_map):
    return pl.BlockSpec(shape, index_map, pipeline_mode=pl.Buffered(1))


def _rms(x, g):
    ms = jnp.mean(x * x, axis=-1, keepdims=True)
    return x * lax.rsqrt(ms + EPS) * g


def _bdot(a, b):
    return jnp.dot(a, b, preferred_element_type=F32)


def _dft_tables(n):
    f = np.arange(n, dtype=np.int64)[:, None]
    r = np.arange(n, dtype=np.int64)[None, :]
    k_pos = ((2 * f + 1) * r) % (4 * n)
    k_neg = ((2 * f + 1) * (r - n)) % (4 * n)
    ang_pos = np.pi * k_pos / (2 * n)
    ang_neg = np.pi * k_neg / (2 * n)
    cpos, spos = np.cos(ang_pos), np.sin(ang_pos)
    cneg, sneg = np.cos(ang_neg), np.sin(ang_neg)
    cneg[:, 0] = 0.0
    sneg[:, 0] = 0.0
    fwd = np.concatenate([cpos, spos], axis=0)
    inv = np.concatenate([cpos.T, spos.T], axis=1) / n
    filt_c = np.concatenate([cneg, cpos], axis=1)
    filt_s = np.concatenate([sneg, spos], axis=1)
    return [np.asarray(a, np.float32) for a in (fwd, inv, filt_c, filt_s)]


def _filter_positions(L):
    t = np.linspace(0.0, 1.0, L)[:, None]
    w = (2.0 * np.pi / L) * np.arange(L)[:, None]
    f = np.linspace(1e-4, HY_BANDS - 1, HY_BANDS)[None, :]
    z = np.concatenate([t, np.cos(f * w), -np.sin(f * w)], axis=-1)
    return np.asarray(z, np.float32)


def _rope_tables(S):
    half = ROT_DIM // 2
    inv = ROPE_THETA ** (-np.arange(0, ROT_DIM, 2, dtype=np.float64) / ROT_DIM)
    ang = np.arange(S, dtype=np.float64)[:, None] * inv[None, :]
    lane = np.arange(128) % HEAD_DIM
    ang_l = ang[:, lane % half]
    cos_t = np.where(lane[None, :] < ROT_DIM, np.cos(ang_l), 1.0)
    sin_lo = np.where(lane[None, :] < half, -np.sin(ang_l), 0.0)
    sin_hi = np.where((lane[None, :] >= half) & (lane[None, :] < ROT_DIM), np.sin(ang_l), 0.0)
    return [np.asarray(a, np.float32) for a in (cos_t, sin_lo, sin_hi)]


def _spectra_kernel(z_ref, w1_ref, b1_ref, w2_ref, b2_ref, w3_ref, b3_ref, fr_ref,
                    woutf_ref, woutb_ref, dl_ref, bias_ref, mc_ref, ms_ref, x_ref, g_ref,
                    gc_ref, gs_ref, hn_ref, hid_ref, hf_ref, hb_ref, *, P, n, norm_rows):
    hi = lax.Precision.HIGHEST

    L, cw = hf_ref.shape
    FW = hid_ref.shape[1]

    @pl.when(pl.program_id(0) == 0)
    def _():
        def two(v):
            return jnp.concatenate([v, v], axis=1)

        def blockdiag(w):
            zero = jnp.zeros_like(w)
            return jnp.concatenate([jnp.concatenate([w, zero], axis=1),
                                    jnp.concatenate([zero, w], axis=1)], axis=0)

        fr = two(fr_ref[...])
        h = z_ref[...]
        for w_ref, b_ref in ((w1_ref, b1_ref), (w2_ref, b2_ref), (w3_ref, b3_ref)):
            h = jnp.sin(fr * (jnp.dot(h, blockdiag(w_ref[...]), precision=hi) + two(b_ref[...])))
        hid_ref[0:L // 2, :] = h[:, :FW]
        hid_ref[L // 2:L, :] = h[:, FW:]

    row = lax.broadcasted_iota(jnp.int32, (L, cw), 0)
    decay = jnp.exp(-(row.astype(F32) * (1.0 / (L - 1))) * dl_ref[...])
    hid = hid_ref[...]
    hid_hi = hid.astype(BF16)
    hid_lo = (hid - hid_hi.astype(F32)).astype(BF16)

    def taps(wout_ref):
        wout = wout_ref[...]
        w_hi = wout.astype(BF16)
        w_lo = (wout - w_hi.astype(F32)).astype(BF16)
        return (_bdot(hid_hi, w_hi) + (_bdot(hid_hi, w_lo) + _bdot(hid_lo, w_hi))) * decay

    hf_ref[...] = taps(woutf_ref).astype(hf_ref.dtype)
    hb_ref[...] = jnp.where(row > 0, taps(woutb_ref), 0.0).astype(hb_ref.dtype)

    mc = mc_ref[...]
    ms = ms_ref[...]
    cpos = mc[:, n:]
    spos = ms[:, n:]
    norm_chunks = [slice(r, r + norm_rows) for r in range(0, x_ref.shape[0], norm_rows)]
    nd = 2 * P - 1
    for idx, d in enumerate(range(-(P - 1), P)):
        for r in norm_chunks[idx * len(norm_chunks) // nd:(idx + 1) * len(norm_chunks) // nd]:
            hn_ref[r, :] = _rms(x_ref[r, :], g_ref[...]).astype(hn_ref.dtype)
        if d == 0:
            hf0 = hf_ref[0:n, :].astype(F32)
            hb0 = hb_ref[0:n, :].astype(F32)
            gc = _bdot(cpos, (hf0 + hb0).astype(BF16)) + bias_ref[...]
            gs = _bdot(spos, (hf0 - hb0).astype(BF16))
        elif d > 0:
            taps = hf_ref[n * (d - 1):n * (d + 1), :].astype(BF16)
            gc = _bdot(mc, taps)
            gs = _bdot(ms, taps)
        else:
            a = -d
            taps = hb_ref[n * (a - 1):n * (a + 1), :].astype(BF16)
            gc = _bdot(mc, taps)
            gs = -_bdot(ms, taps)
        gc_ref[d + P - 1] = gc.astype(gc_ref.dtype)
        gs_ref[d + P - 1] = gs.astype(gs_ref.dtype)


def _filter_spectra(L, f_w1, f_b1, f_w2, f_b2, f_w3, f_b3, f_freq, f_wout, bias, mc, ms, P, x2d, g,
                    cw=256, norm_rows=256):
    M, D = x2d.shape
    FW = f_w1.shape[1]
    n = L // P
    half = f_wout.shape[1] // 2
    nblk = half // cw
    xm = M // nblk
    zpad = np.pad(_filter_positions(L), ((0, 0), (0, FW - HY_EMB)))
    z = jnp.asarray(np.concatenate([zpad[:L // 2], zpad[L // 2:]], axis=1))
    w1 = jnp.pad(f_w1, ((0, FW - HY_EMB), (0, 0)))
    max_decay = math.log(HY_DECAY_TARGET) / HY_FAST_DECAY
    min_decay = math.log(HY_DECAY_TARGET) / HY_SLOW_DECAY
    absdelta = jnp.asarray(np.abs(np.linspace(min_decay, max_decay, D)).astype(np.float32))[None, :]
    small = lambda shape: pl.BlockSpec(shape, lambda j: (0, 0))
    out_sd = jax.ShapeDtypeStruct((2 * P - 1, n, half), SPEC_DTYPE)
    return pl.pallas_call(
        functools.partial(_spectra_kernel, P=P, n=n, norm_rows=norm_rows),
        grid=(nblk,),
        in_specs=[small((L // 2, 2 * FW)), small((FW, FW)), small((1, FW)), small((FW, FW)), small((1, FW)),
                  small((FW, FW)), small((1, FW)), small((1, FW)),
                  pl.BlockSpec((FW, cw), lambda j: (0, j)),
                  pl.BlockSpec((FW, cw), lambda j: (0, nblk + j)),
                  pl.BlockSpec((1, cw), lambda j: (0, j % (D // cw))),
                  pl.BlockSpec((1, cw), lambda j: (0, j)),
                  small((n, 2 * n)), small((n, 2 * n)),
                  pl.BlockSpec((xm, D), lambda j: (j, 0)),
                  small((1, D))],
        out_specs=[pl.BlockSpec((2 * P - 1, n, cw), lambda j: (0, 0, j)),
                   pl.BlockSpec((2 * P - 1, n, cw), lambda j: (0, 0, j)),
                   pl.BlockSpec((xm, D), lambda j: (j, 0))],
        out_shape=[out_sd, out_sd, jax.ShapeDtypeStruct((M, D), BF16)],
        scratch_shapes=[pltpu.VMEM((L, FW), F32),
                        pltpu.VMEM((L, cw), BF16),
                        pltpu.VMEM((L, cw), BF16)],
        compiler_params=_cparams(1),
        name="filter_spectra",
    )(z, w1, f_b1.reshape(1, FW), f_w2, f_b2.reshape(1, FW), f_w3, f_b3.reshape(1, FW),
      f_freq.reshape(1, FW), f_wout, f_wout, absdelta, bias.reshape(1, half), mc, ms,
      x2d, g.reshape(1, D))


def _hyena_kernel(*refs, P, n, fchunk, n_cast):
    (hn_ref, wv_ref, w1_ref, w2_ref, cwv_ref, cw1_ref, cw2_ref, cbv_ref, cb1_ref, cb2_ref,
     gc0_ref, gs0_ref, gc1_ref, gs1_ref, fwd_ref, inv_ref) = refs[:16]
    cast_in = refs[16:16 + n_cast]
    z_ref = refs[16 + n_cast]
    cast_out = refs[17 + n_cast:17 + 2 * n_cast]
    rawa_ref, rawb_ref, v_ref, gate_ref, uspec_ref, yspec_ref = refs[17 + 2 * n_cast:]

    for src, dst in zip(cast_in, cast_out):
        dst[...] = src[...].astype(dst.dtype)

    row = lax.broadcasted_iota(jnp.int32, (n, 1), 0)
    blk = [slice(n * i, n * (i + 1)) for i in range(P)]
    chunks = n // fchunk
    fwd = fwd_ref[...]
    inv = inv_ref[...]

    wv, w1, w2 = [r[...].astype(BF16) for r in (wv_ref, w1_ref, w2_ref)]

    def project(w, dst_ref, i):
        dst_ref[blk[i], :] = _bdot(hn_ref[0, blk[i], :], w)

    def short_conv(raw_ref, cw_ref, cb_ref, dst, i):
        cw = cw_ref[...]
        cur = raw_ref[blk[i], :]
        prev = pltpu.roll(cur, 1, axis=0)
        nxt = pltpu.roll(cur, n - 1, axis=0)
        first = raw_ref[n * i - 1:n * i, :] if i > 0 else jnp.zeros_like(cur[0:1])
        last = raw_ref[n * (i + 1):n * (i + 1) + 1, :] if i < P - 1 else jnp.zeros_like(cur[0:1])
        prev = jnp.where(row == 0, first, prev)
        nxt = jnp.where(row == n - 1, last, nxt)
        out = prev * cw[0:1] + cur * cw[1:2] + nxt * cw[2:3] + cb_ref[...]
        dst[blk[i], :] = out.astype(dst.dtype)

    def forward(src_ref, j):
        uspec_ref[j] = _bdot(fwd, src_ref[blk[j], :]).astype(uspec_ref.dtype)

    def combine(gc_ref, gs_ref, c):
        fc = slice(c * fchunk, (c + 1) * fchunk)
        fs = slice(n + c * fchunk, n + (c + 1) * fchunk)
        yc = [None] * P
        ys = [None] * P
        for j in range(P):
            uc = uspec_ref[j, fc, :]
            us = uspec_ref[j, fs, :]
            for i in range(P):
                gc = gc_ref[i - j + P - 1, fc, :]
                gs = gs_ref[i - j + P - 1, fc, :]
                tc = gc * uc - gs * us
                ts = gc * us + gs * uc
                yc[i] = tc if yc[i] is None else yc[i] + tc
                ys[i] = ts if ys[i] is None else ys[i] + ts
        for i in range(P):
            yspec_ref[i, fc, :] = yc[i].astype(BF16)
            yspec_ref[i, fs, :] = ys[i].astype(BF16)

    gate_proj = [(w1, rawb_ref, i) for i in range(P)] + [(w2, rawa_ref, i) for i in range(P)]
    per_combine = chunks // 3

    for i in range(P):
        project(wv, rawa_ref, i)
    for i in range(P):
        short_conv(rawa_ref, cwv_ref, cbv_ref, v_ref, i)
        if i < 3:
            project(*gate_proj[i])
    for i in range(P):
        forward(v_ref, i)
    for c in range(chunks):
        combine(gc0_ref, gs0_ref, c)
        if c % per_combine == 0 and c // per_combine < 3:
            project(*gate_proj[3 + c // per_combine])
    for i in range(P):
        short_conv(rawb_ref, cw1_ref, cb1_ref, gate_ref, i)
        v_ref[blk[i], :] = (gate_ref[blk[i], :] * _bdot(inv, yspec_ref[i])).astype(v_ref.dtype)
    for i in range(P):
        forward(v_ref, i)
    for c in range(chunks):
        combine(gc1_ref, gs1_ref, c)
        if c % (chunks // 2) == 0:
            project(*gate_proj[6 + c // (chunks // 2)])
    for i in range(P):
        short_conv(rawa_ref, cw2_ref, cb2_ref, gate_ref, i)
        z_ref[0, blk[i], :] = (gate_ref[blk[i], :] * _bdot(inv, yspec_ref[i])).astype(z_ref.dtype)


def _cast_rider(weights, n_steps, step_of):
    args, in_specs, out_specs, out_shapes = [], [], [], []
    for arr, layer in weights:
        _, R, C = arr.shape
        rps = R // n_steps
        args.append(arr)
        in_specs.append(pl.BlockSpec((None, rps, C), lambda *g, layer=layer: (layer, step_of(*g), 0)))
        out_specs.append(pl.BlockSpec((rps, C), lambda *g: (step_of(*g), 0)))
        out_shapes.append(jax.ShapeDtypeStruct((R, C), BF16))
    return args, in_specs, out_specs, out_shapes


def _hyena_core(hn, w_in, conv_w, conv_b, gc, gs, fwd, inv, P, cast_weights):
    B, L, D = hn.shape
    n = L // P
    cb_n = D // CH_BLOCK
    C = CH_BLOCK
    nd = gc.shape[0]
    c_args, c_in, c_out, c_shapes = _cast_rider(cast_weights, B * cb_n, lambda b, c: b * cb_n + c)

    def stream_spec(shape, s):
        return pl.BlockSpec(shape, lambda b, c: (0, s * cb_n + c))

    in_specs = (
        [pl.BlockSpec((1, L, D), lambda b, c: (b, 0, 0))]
        + [stream_spec((D, C), s) for s in range(3)]
        + [stream_spec((3, C), s) for s in range(3)]
        + [stream_spec((1, C), s) for s in range(3)]
        + [pl.BlockSpec((nd, n, C), lambda b, c: (0, 0, c)),
           pl.BlockSpec((nd, n, C), lambda b, c: (0, 0, c)),
           pl.BlockSpec((nd, n, C), lambda b, c: (0, 0, cb_n + c)),
           pl.BlockSpec((nd, n, C), lambda b, c: (0, 0, cb_n + c)),
           _resident((2 * n, n), lambda b, c: (0, 0)),
           _resident((n, 2 * n), lambda b, c: (0, 0))]
        + c_in)
    return pl.pallas_call(
        functools.partial(_hyena_kernel, P=P, n=n, fchunk=16, n_cast=len(c_args)),
        grid=(B, cb_n),
        in_specs=in_specs,
        out_specs=[pl.BlockSpec((1, L, C), lambda b, c: (b, 0, c))] + c_out,
        out_shape=[jax.ShapeDtypeStruct((B, L, D), BF16)] + c_shapes,
        scratch_shapes=[pltpu.VMEM((L, C), F32),
                        pltpu.VMEM((L, C), F32),
                        pltpu.VMEM((L, C), BF16),
                        pltpu.VMEM((L, C), F32),
                        pltpu.VMEM((P, 2 * n, C), SPEC_DTYPE),
                        pltpu.VMEM((P, 2 * n, C), BF16)],
        compiler_params=_cparams(2),
        name="hyena_core",
    )(hn, w_in, w_in, w_in, conv_w, conv_w, conv_w, conv_b, conv_b, conv_b,
      gc, gs, gc, gs, fwd, inv, *c_args)


def _tail_kernel(a_ref, wo_ref, r_ref, gmix_ref, gpre_ref, gpost_ref, wu_ref, wd_ref, o_ref, *, fchunk):
    sm = TAIL_SLAB_ROWS
    FF = wu_ref.shape[1]
    for p0 in range(0, a_ref.shape[0], 2 * sm):
        rows = [slice(p0 + sm * u, p0 + sm * (u + 1)) for u in range(2)]
        ms = [_bdot(a_ref[r, :], wo_ref[...]) for r in rows]
        hs = [r_ref[r, :] + _rms(m, gmix_ref[...]) for r, m in zip(rows, ms)]
        hns = [_rms(h, gpre_ref[...]).astype(BF16) for h in hs]
        accs = [None, None]
        for c in range(FF // fchunk):
            acts = []
            for u in range(2):
                a = jnp.maximum(_bdot(hns[u], wu_ref[:, c * fchunk:(c + 1) * fchunk]), 0.0)
                acts.append((a * a).astype(BF16))
            for u in range(2):
                part = _bdot(acts[u], wd_ref[c * fchunk:(c + 1) * fchunk, :])
                accs[u] = part if accs[u] is None else accs[u] + part
        for u in range(2):
            o_ref[rows[u], :] = hs[u] + _rms(accs[u], gpost_ref[...])


def _tail(a, w_o, resid, g_mix, g_pre, g_post, w_up, w_down, tm=1024, fchunk=1024):
    M, K = a.shape
    D = w_o.shape[1]
    FF = w_up.shape[1]
    row = lambda shape: pl.BlockSpec(shape, lambda i: (i, 0))
    gain = pl.BlockSpec((1, D), lambda i: (0, 0))
    return pl.pallas_call(
        functools.partial(_tail_kernel, fchunk=fchunk),
        grid=(M // tm,),
        in_specs=[row((tm, K)), _resident((K, D), lambda i: (0, 0)), row((tm, D)),
                  gain, gain, gain,
                  _resident((D, FF), lambda i: (0, 0)),
                  _resident((FF, D), lambda i: (0, 0))],
        out_specs=row((tm, D)),
        out_shape=jax.ShapeDtypeStruct((M, D), F32),
        compiler_params=_cparams(1),
        name="mixer_tail_mlp",
    )(a, w_o, resid, g_mix.reshape(1, D), g_pre.reshape(1, D), g_post.reshape(1, D), w_up, w_down)


def _qkv_kernel(h_ref, g_ref, w32_ref, ct_ref, slo_ref, shi_ref, q_ref, k_ref, v_ref, w_ref, *, halves):
    @pl.when(pl.program_id(0) == 0)
    def _():
        w_ref[...] = w32_ref[...].astype(w_ref.dtype)

    nq, nk = q_ref.shape[1], k_ref.shape[1]
    hm = h_ref.shape[0] // halves
    rows = [slice(hm * u, hm * (u + 1)) for u in range(halves)]
    half = ROT_DIM // 2
    scale = LOG2E * HEAD_DIM ** -0.5
    qkvs = [_bdot(_rms(h_ref[r, :], g_ref[...]).astype(BF16), w_ref[...]) for r in rows]
    for r, qkv in zip(rows, qkvs):
        ct = ct_ref[r, :]
        slo = slo_ref[r, :]
        shi = shi_ref[r, :]
        for j in range((nq + nk) // 128):
            t = qkv[:, 128 * j:128 * (j + 1)]
            rot = t * ct + pltpu.roll(t, 128 - half, axis=1) * slo + pltpu.roll(t, half, axis=1) * shi
            if 128 * j < nq:
                q_ref[r, 128 * j:128 * (j + 1)] = (rot * scale).astype(q_ref.dtype)
            else:
                k_ref[r, 128 * j - nq:128 * (j + 1) - nq] = rot.astype(k_ref.dtype)
        v_ref[r, :] = qkv[:, nq + nk:].astype(v_ref.dtype)


def _qkv(h, g, w, S, tm=1024, halves=2):
    M, D = h.shape
    nq = D
    nk = (w.shape[1] - nq) // 2
    ct, slo, shi = [jnp.asarray(a) for a in _rope_tables(S)]
    pb = S // tm
    tab = pl.BlockSpec((tm, 128), lambda i: (i % pb, 0))
    return pl.pallas_call(
        functools.partial(_qkv_kernel, halves=halves),
        grid=(M // tm,),
        in_specs=[pl.BlockSpec((tm, D), lambda i: (i, 0)),
                  pl.BlockSpec((1, D), lambda i: (0, 0)),
                  _resident((D, nq + 2 * nk), lambda i: (0, 0)),
                  tab, tab, tab],
        out_specs=[pl.BlockSpec((tm, nq), lambda i: (i, 0)),
                   pl.BlockSpec((tm, nk), lambda i: (i, 0)),
                   pl.BlockSpec((tm, nk), lambda i: (i, 0))],
        out_shape=[jax.ShapeDtypeStruct((M, nq), BF16),
                   jax.ShapeDtypeStruct((M, nk), BF16),
                   jax.ShapeDtypeStruct((M, nk), BF16)],
        scratch_shapes=[pltpu.VMEM(w.shape, BF16)],
        compiler_params=_cparams(1),
        name="qkv_rope",
    )(h, g.reshape(1, D), w, ct, slo, shi)


def _attn_kernel(*refs, S, n_kv, n_cast):
    sink_ref, q_ref, k_ref, v_ref = refs[:4]
    cast_in = refs[4:4 + n_cast]
    o_ref = refs[4 + n_cast]
    cast_out = refs[5 + n_cast:5 + 2 * n_cast]
    krep_ref, vrep_ref = refs[5 + 2 * n_cast:]
    BQ = WINDOW
    BK = 3 * WINDOW
    GW = KV_GROUP * HEAD_DIM
    dn = (((1,), (1,)), ((), ()))

    for src, dst in zip(cast_in, cast_out):
        dst[...] = src[...].astype(dst.dtype)

    for g in range(n_kv):
        kg = k_ref[0, :, g * HEAD_DIM:(g + 1) * HEAD_DIM]
        vg = v_ref[0, :, g * HEAD_DIM:(g + 1) * HEAD_DIM]
        krep_ref[g] = jnp.concatenate([kg] * KV_GROUP, axis=1)
        vrep_ref[g] = jnp.concatenate([vg] * KV_GROUP, axis=1)

    lane = lax.broadcasted_iota(jnp.int32, (BQ, GW), 1)
    head_lanes = [(lane >= hh * HEAD_DIM) & (lane < (hh + 1) * HEAD_DIM) for hh in range(KV_GROUP)]
    rowblk = lax.broadcasted_iota(jnp.int32, (KV_GROUP * BQ, 1), 0) // BQ

    def body(qb, carry):
        qs = pl.multiple_of(qb * BQ, BQ)
        ks = pl.multiple_of(jnp.clip(qs - WINDOW, 0, S - BK), WINDOW)
        qpos = qs + lax.broadcasted_iota(jnp.int32, (BQ, BK), 0)
        kpos = ks + lax.broadcasted_iota(jnp.int32, (BQ, BK), 1)
        maskadd = jnp.where(jnp.abs(kpos - qpos) <= WINDOW, 0.0, NEG).astype(F32)
        maskadd = jnp.concatenate([maskadd] * KV_GROUP, axis=0)
        for g in range(n_kv):
            qg = q_ref[0, pl.ds(qs, BQ), g * GW:(g + 1) * GW]
            qst = jnp.concatenate([jnp.where(head_lanes[hh], qg, jnp.zeros_like(qg))
                                   for hh in range(KV_GROUP)], axis=0)
            sink = jnp.zeros((KV_GROUP * BQ, 1), F32)
            for hh in range(KV_GROUP):
                sink = jnp.where(rowblk == hh, sink_ref[g * KV_GROUP + hh] * LOG2E, sink)
            s = lax.dot_general(qst, krep_ref[g, pl.ds(ks, BK), :], dn,
                                preferred_element_type=F32) + maskadd
            m = jnp.maximum(jnp.max(s, axis=-1, keepdims=True), sink)
            p = jnp.exp2(s - m)
            denom = jnp.sum(p, axis=-1, keepdims=True) + jnp.exp2(sink - m)
            ost = _bdot(p.astype(BF16), vrep_ref[g, pl.ds(ks, BK), :]) / denom
            o = jnp.where(head_lanes[0], ost[0:BQ], 0.0)
            for hh in range(1, KV_GROUP):
                o = jnp.where(head_lanes[hh], ost[hh * BQ:(hh + 1) * BQ], o)
            o_ref[0, pl.ds(qs, BQ), g * GW:(g + 1) * GW] = o.astype(o_ref.dtype)
        return carry

    lax.fori_loop(0, S // BQ, body, 0, unroll=4)


def _attention(q, k, v, sink, cast_weights):
    B, S, NQ = q.shape
    NK = k.shape[2]
    n_kv = NK // HEAD_DIM
    c_args, c_in, c_out, c_shapes = _cast_rider(cast_weights, B, lambda b: b)
    return pl.pallas_call(
        functools.partial(_attn_kernel, S=S, n_kv=n_kv, n_cast=len(c_args)),
        grid=(B,),
        in_specs=[pl.BlockSpec(memory_space=pltpu.SMEM),
                  pl.BlockSpec((1, S, NQ), lambda b: (b, 0, 0)),
                  pl.BlockSpec((1, S, NK), lambda b: (b, 0, 0)),
                  pl.BlockSpec((1, S, NK), lambda b: (b, 0, 0))] + c_in,
        out_specs=[pl.BlockSpec((1, S, NQ), lambda b: (b, 0, 0))] + c_out,
        out_shape=[jax.ShapeDtypeStruct((B, S, NQ), BF16)] + c_shapes,
        scratch_shapes=[pltpu.VMEM((n_kv, S, KV_GROUP * HEAD_DIM), BF16),
                        pltpu.VMEM((n_kv, S, KV_GROUP * HEAD_DIM), BF16)],
        compiler_params=_cparams(1),
        name="window_attention",
    )(sink, q, k, v, *c_args)


def kernel(x, norm_mix_pre, norm_mix_post, norm_mlp_pre, norm_mlp_post, w_up, w_down, hy_w_in, hy_conv_w, hy_conv_b, hy_f_w1, hy_f_b1, hy_f_w2, hy_f_b2, hy_f_w3, hy_f_b3, hy_f_freq, hy_f_wout, hy_bias, hy_w_out, at_w_qkv, at_sink, at_w_o):
    B, L, D = x.shape
    M = B * L
    P = CONV_BLOCKS
    n = L // P
    depth = norm_mix_pre.shape[0]
    fwd, inv, filt_c, filt_s = [jnp.asarray(a).astype(BF16) for a in _dft_tables(n)]

    h = x.reshape(M, D)
    for i in range(depth):
        j = i // 2
        if i % 2 == 0:
            gc, gs, hn = _filter_spectra(L, hy_f_w1[j], hy_f_b1[j], hy_f_w2[j], hy_f_b2[j],
                                         hy_f_w3[j], hy_f_b3[j], hy_f_freq[j], hy_f_wout[j],
                                         hy_bias[j], filt_c, filt_s, P, h, norm_mix_pre[i])
            z, wo_b, wu_b, wd_b = _hyena_core(
                hn.reshape(B, L, D), hy_w_in[j], hy_conv_w[j], hy_conv_b[j].reshape(1, -1),
                gc, gs, fwd, inv, P, [(hy_w_out, j), (w_up, i), (w_down, i)])
            a = z.reshape(M, D)
        else:
            q, k, v = _qkv(h, norm_mix_pre[i], at_w_qkv[j], L)
            o, wo_b, wu_b, wd_b = _attention(
                q.reshape(B, L, -1), k.reshape(B, L, -1), v.reshape(B, L, -1), at_sink[j],
                [(at_w_o, j), (w_up, i), (w_down, i)])
            a = o.reshape(M, -1)
        h = _tail(a, wo_b, h, norm_mix_post[i], norm_mlp_pre[i], norm_mlp_post[i], wu_b, wd_b)
    return h.reshape(B, L, D)
```

```python
import functools
import math

import numpy as np
import jax
import jax.numpy as jnp
from jax import lax
from jax.experimental import pallas as pl
from jax.experimental.pallas import tpu as pltpu

BF16 = jnp.bfloat16
F32 = jnp.float32

EPS = 1e-6
NEG = -1e30
LOG2E = math.log2(math.e)

HY_ORDER = 2
HY_EMB = 33
HY_BANDS = (HY_EMB - 1) // 2
HY_DECAY_TARGET = 1e-2
HY_FAST_DECAY = 0.3
HY_SLOW_DECAY = 1.5
HEAD_DIM = 64
KV_GROUP = 4
WINDOW = 128
ROT_DIM = HEAD_DIM // 4
ROPE_THETA = 500000.0

CONV_BLOCKS = 4
CH_BLOCK = 256
SPEC_DTYPE = BF16
TAIL_SLAB_ROWS = 256

VMEM_LIMIT = 56 * 1024 * 1024


def _cparams(n_axes, flags=None):
    return pltpu.CompilerParams(
        dimension_semantics=("arbitrary",) * n_axes,
        vmem_limit_bytes=VMEM_LIMIT,
        flags=flags)


def _resident(shape, index_map):
    return pl.BlockSpec(shape, index_map, pipeline_mode=pl.Buffered(1))


def _rms(x, g):
    ms = jnp.mean(x * x, axis=-1, keepdims=True)
    return x * lax.rsqrt(ms + EPS) * g


def _bdot(a, b):
    return jnp.dot(a, b, preferred_element_type=F32)


def _dft_tables(n):
    f = np.arange(n, dtype=np.int64)[:, None]
    r = np.arange(n, dtype=np.int64)[None, :]
    k_pos = ((2 * f + 1) * r) % (4 * n)
    k_neg = ((2 * f + 1) * (r - n)) % (4 * n)
    ang_pos = np.pi * k_pos / (2 * n)
    ang_neg = np.pi * k_neg / (2 * n)
    cpos, spos = np.cos(ang_pos), np.sin(ang_pos)
    cneg, sneg = np.cos(ang_neg), np.sin(ang_neg)
    cneg[:, 0] = 0.0
    sneg[:, 0] = 0.0
    fwd = np.concatenate([cpos, spos], axis=0)
    inv = np.concatenate([cpos.T, spos.T], axis=1) / n
    filt_c = np.concatenate([cneg, cpos], axis=1)
    filt_s = np.concatenate([sneg, spos], axis=1)
    return [np.asarray(a, np.float32) for a in (fwd, inv, filt_c, filt_s)]


def _filter_positions(L):
    t = np.linspace(0.0, 1.0, L)[:, None]
    w = (2.0 * np.pi / L) * np.arange(L)[:, None]
    f = np.linspace(1e-4, HY_BANDS - 1, HY_BANDS)[None, :]
    z = np.concatenate([t, np.cos(f * w), -np.sin(f * w)], axis=-1)
    return np.asarray(z, np.float32)


def _rope_tables(S):
    half = ROT_DIM // 2
    inv = ROPE_THETA ** (-np.arange(0, ROT_DIM, 2, dtype=np.float64) / ROT_DIM)
    ang = np.arange(S, dtype=np.float64)[:, None] * inv[None, :]
    lane = np.arange(128) % HEAD_DIM
    ang_l = ang[:, lane % half]
    cos_t = np.where(lane[None, :] < ROT_DIM, np.cos(ang_l), 1.0)
    sin_lo = np.where(lane[None, :] < half, -np.sin(ang_l), 0.0)
    sin_hi = np.where((lane[None, :] >= half) & (lane[None, :] < ROT_DIM), np.sin(ang_l), 0.0)
    return [np.asarray(a, np.float32) for a in (cos_t, sin_lo, sin_hi)]


def _spectra_kernel(*refs, P, n, norm_rows, n_cast):
    (z_ref, w1_ref, b1_ref, w2_ref, b2_ref, w3_ref, b3_ref, fr_ref,
     woutf_ref, woutb_ref, dl_ref, bias_ref, mc_ref, ms_ref, x_ref, g_ref) = refs[:16]
    cast_in = refs[16:16 + n_cast]
    gc_ref, gs_ref, hn_ref = refs[16 + n_cast:19 + n_cast]
    cast_out = refs[19 + n_cast:19 + 2 * n_cast]
    hid_ref, hf_ref, hb_ref = refs[19 + 2 * n_cast:]
    hi = lax.Precision.HIGHEST

    for src, dst in zip(cast_in, cast_out):
        dst[...] = src[...].astype(dst.dtype)

    L, cw = hf_ref.shape
    FW = hid_ref.shape[1]

    @pl.when(pl.program_id(0) == 0)
    def _():
        def two(v):
            return jnp.concatenate([v, v], axis=1)

        def blockdiag(w):
            zero = jnp.zeros_like(w)
            return jnp.concatenate([jnp.concatenate([w, zero], axis=1),
                                    jnp.concatenate([zero, w], axis=1)], axis=0)

        fr = two(fr_ref[...])
        h = z_ref[...]
        for w_ref, b_ref in ((w1_ref, b1_ref), (w2_ref, b2_ref), (w3_ref, b3_ref)):
            h = jnp.sin(fr * (jnp.dot(h, blockdiag(w_ref[...]), precision=hi) + two(b_ref[...])))
        hid_ref[0:L // 2, :] = h[:, :FW]
        hid_ref[L // 2:L, :] = h[:, FW:]

    row = lax.broadcasted_iota(jnp.int32, (L, cw), 0)
    decay = jnp.exp(-(row.astype(F32) * (1.0 / (L - 1))) * dl_ref[...])
    hid = hid_ref[...]
    hid_hi = hid.astype(BF16)
    hid_lo = (hid - hid_hi.astype(F32)).astype(BF16)

    def taps(wout_ref):
        wout = wout_ref[...]
        w_hi = wout.astype(BF16)
        w_lo = (wout - w_hi.astype(F32)).astype(BF16)
        return (_bdot(hid_hi, w_hi) + (_bdot(hid_hi, w_lo) + _bdot(hid_lo, w_hi))) * decay

    hf_ref[...] = taps(woutf_ref).astype(hf_ref.dtype)
    hb_ref[...] = jnp.where(row > 0, taps(woutb_ref), 0.0).astype(hb_ref.dtype)

    mc = mc_ref[...]
    ms = ms_ref[...]
    cpos = mc[:, n:]
    spos = ms[:, n:]
    norm_chunks = [slice(r, r + norm_rows) for r in range(0, x_ref.shape[0], norm_rows)]
    nd = 2 * P - 1
    for idx, d in enumerate(range(-(P - 1), P)):
        for r in norm_chunks[idx * len(norm_chunks) // nd:(idx + 1) * len(norm_chunks) // nd]:
            hn_ref[r, :] = _rms(x_ref[r, :], g_ref[...]).astype(hn_ref.dtype)
        if d == 0:
            hf0 = hf_ref[0:n, :].astype(F32)
            hb0 = hb_ref[0:n, :].astype(F32)
            gc = _bdot(cpos, (hf0 + hb0).astype(BF16)) + bias_ref[...]
            gs = _bdot(spos, (hf0 - hb0).astype(BF16))
        elif d > 0:
            taps = hf_ref[n * (d - 1):n * (d + 1), :].astype(BF16)
            gc = _bdot(mc, taps)
            gs = _bdot(ms, taps)
        else:
            a = -d
            taps = hb_ref[n * (a - 1):n * (a + 1), :].astype(BF16)
            gc = _bdot(mc, taps)
            gs = -_bdot(ms, taps)
        gc_ref[d + P - 1] = gc.astype(gc_ref.dtype)
        gs_ref[d + P - 1] = gs.astype(gs_ref.dtype)


def _filter_spectra(L, f_w1, f_b1, f_w2, f_b2, f_w3, f_b3, f_freq, f_wout, bias, mc, ms, P, x2d, g,
                    cast_weights, cw=256, norm_rows=256):
    M, D = x2d.shape
    FW = f_w1.shape[1]
    n = L // P
    half = f_wout.shape[1] // 2
    nblk = half // cw
    xm = M // nblk
    zpad = np.pad(_filter_positions(L), ((0, 0), (0, FW - HY_EMB)))
    z = jnp.asarray(np.concatenate([zpad[:L // 2], zpad[L // 2:]], axis=1))
    w1 = jnp.pad(f_w1, ((0, FW - HY_EMB), (0, 0)))
    max_decay = math.log(HY_DECAY_TARGET) / HY_FAST_DECAY
    min_decay = math.log(HY_DECAY_TARGET) / HY_SLOW_DECAY
    absdelta = jnp.asarray(np.abs(np.linspace(min_decay, max_decay, D)).astype(np.float32))[None, :]
    small = lambda shape: pl.BlockSpec(shape, lambda j: (0, 0))
    out_sd = jax.ShapeDtypeStruct((2 * P - 1, n, half), SPEC_DTYPE)
    c_args, c_in, c_out, c_shapes = _cast_rider(cast_weights, nblk, lambda j: j)
    return pl.pallas_call(
        functools.partial(_spectra_kernel, P=P, n=n, norm_rows=norm_rows, n_cast=len(c_args)),
        grid=(nblk,),
        in_specs=[small((L // 2, 2 * FW)), small((FW, FW)), small((1, FW)), small((FW, FW)), small((1, FW)),
                  small((FW, FW)), small((1, FW)), small((1, FW)),
                  pl.BlockSpec((FW, cw), lambda j: (0, j)),
                  pl.BlockSpec((FW, cw), lambda j: (0, nblk + j)),
                  pl.BlockSpec((1, cw), lambda j: (0, j % (D // cw))),
                  pl.BlockSpec((1, cw), lambda j: (0, j)),
                  small((n, 2 * n)), small((n, 2 * n)),
                  pl.BlockSpec((xm, D), lambda j: (j, 0)),
                  small((1, D))] + c_in,
        out_specs=[pl.BlockSpec((2 * P - 1, n, cw), lambda j: (0, 0, j)),
                   pl.BlockSpec((2 * P - 1, n, cw), lambda j: (0, 0, j)),
                   pl.BlockSpec((xm, D), lambda j: (j, 0))] + c_out,
        out_shape=[out_sd, out_sd, jax.ShapeDtypeStruct((M, D), BF16)] + c_shapes,
        scratch_shapes=[pltpu.VMEM((L, FW), F32),
                        pltpu.VMEM((L, cw), BF16),
                        pltpu.VMEM((L, cw), BF16)],
        compiler_params=_cparams(1),
        name="filter_spectra",
    )(z, w1, f_b1.reshape(1, FW), f_w2, f_b2.reshape(1, FW), f_w3, f_b3.reshape(1, FW),
      f_freq.reshape(1, FW), f_wout, f_wout, absdelta, bias.reshape(1, half), mc, ms,
      x2d, g.reshape(1, D), *c_args)


def _hyena_kernel(*refs, P, n, fchunk, n_cast):
    (hn_ref, wv_ref, w1_ref, w2_ref, cwv_ref, cw1_ref, cw2_ref, cbv_ref, cb1_ref, cb2_ref,
     gc0_ref, gs0_ref, gc1_ref, gs1_ref, fwd_ref, inv_ref) = refs[:16]
    cast_in = refs[16:16 + n_cast]
    z_ref = refs[16 + n_cast]
    cast_out = refs[17 + n_cast:17 + 2 * n_cast]
    rawa_ref, rawb_ref, v_ref, gate_ref, uspec_ref, yspec_ref = refs[17 + 2 * n_cast:]

    for src, dst in zip(cast_in, cast_out):
        dst[...] = src[...].astype(dst.dtype)

    row = lax.broadcasted_iota(jnp.int32, (n, 1), 0)
    blk = [slice(n * i, n * (i + 1)) for i in range(P)]
    chunks = n // fchunk
    fwd = fwd_ref[...]
    inv = inv_ref[...]

    wv, w1, w2 = wv_ref[...], w1_ref[...], w2_ref[...]

    def project(w, dst_ref, i):
        dst_ref[blk[i], :] = _bdot(hn_ref[0, blk[i], :], w)

    def short_conv(raw_ref, cw_ref, cb_ref, dst, i):
        cw = cw_ref[...]
        cur = raw_ref[blk[i], :]
        prev = pltpu.roll(cur, 1, axis=0)
        nxt = pltpu.roll(cur, n - 1, axis=0)
        first = raw_ref[n * i - 1:n * i, :] if i > 0 else jnp.zeros_like(cur[0:1])
        last = raw_ref[n * (i + 1):n * (i + 1) + 1, :] if i < P - 1 else jnp.zeros_like(cur[0:1])
        prev = jnp.where(row == 0, first, prev)
        nxt = jnp.where(row == n - 1, last, nxt)
        out = prev * cw[0:1] + cur * cw[1:2] + nxt * cw[2:3] + cb_ref[...]
        dst[blk[i], :] = out.astype(dst.dtype)

    def forward(src_ref, j):
        uspec_ref[j] = _bdot(fwd, src_ref[blk[j], :]).astype(uspec_ref.dtype)

    def combine(gc_ref, gs_ref, c):
        fc = slice(c * fchunk, (c + 1) * fchunk)
        fs = slice(n + c * fchunk, n + (c + 1) * fchunk)
        yc = [None] * P
        ys = [None] * P
        for j in range(P):
            uc = uspec_ref[j, fc, :]
            us = uspec_ref[j, fs, :]
            for i in range(P):
                gc = gc_ref[i - j + P - 1, fc, :]
                gs = gs_ref[i - j + P - 1, fc, :]
                tc = gc * uc - gs * us
                ts = gc * us + gs * uc
                yc[i] = tc if yc[i] is None else yc[i] + tc
                ys[i] = ts if ys[i] is None else ys[i] + ts
        for i in range(P):
            yspec_ref[i, fc, :] = yc[i].astype(BF16)
            yspec_ref[i, fs, :] = ys[i].astype(BF16)

    gate_proj = [(w1, rawb_ref, i) for i in range(P)] + [(w2, rawa_ref, i) for i in range(P)]
    per_combine = chunks // 3

    for i in range(P):
        project(wv, rawa_ref, i)
    for i in range(P):
        short_conv(rawa_ref, cwv_ref, cbv_ref, v_ref, i)
        if i < 3:
            project(*gate_proj[i])
    for i in range(P):
        forward(v_ref, i)
    for c in range(chunks):
        combine(gc0_ref, gs0_ref, c)
        if c % per_combine == 0 and c // per_combine < 3:
            project(*gate_proj[3 + c // per_combine])
    for i in range(P):
        short_conv(rawb_ref, cw1_ref, cb1_ref, gate_ref, i)
        v_ref[blk[i], :] = (gate_ref[blk[i], :] * _bdot(inv, yspec_ref[i])).astype(v_ref.dtype)
    for i in range(P):
        forward(v_ref, i)
    for c in range(chunks):
        combine(gc1_ref, gs1_ref, c)
        if c % (chunks // 2) == 0:
            project(*gate_proj[6 + c // (chunks // 2)])
    for i in range(P):
        short_conv(rawa_ref, cw2_ref, cb2_ref, gate_ref, i)
        z_ref[0, blk[i], :] = (gate_ref[blk[i], :] * _bdot(inv, yspec_ref[i])).astype(z_ref.dtype)


def _cast_rider(weights, n_steps, step_of):
    args, in_specs, out_specs, out_shapes = [], [], [], []
    for arr, layer in weights:
        _, R, C = arr.shape
        rps = R // n_steps
        args.append(arr)
        in_specs.append(pl.BlockSpec((None, rps, C), lambda *g, layer=layer: (layer, step_of(*g), 0)))
        out_specs.append(pl.BlockSpec((rps, C), lambda *g: (step_of(*g), 0)))
        out_shapes.append(jax.ShapeDtypeStruct((R, C), BF16))
    return args, in_specs, out_specs, out_shapes


def _hyena_core(hn, w_in, conv_w, conv_b, gc, gs, fwd, inv, P, cast_weights):
    B, L, D = hn.shape
    n = L // P
    cb_n = D // CH_BLOCK
    C = CH_BLOCK
    nd = gc.shape[0]
    c_args, c_in, c_out, c_shapes = _cast_rider(cast_weights, B * cb_n, lambda b, c: b * cb_n + c)

    def stream_spec(shape, s):
        return pl.BlockSpec(shape, lambda b, c: (0, s * cb_n + c))

    in_specs = (
        [pl.BlockSpec((1, L, D), lambda b, c: (b, 0, 0))]
        + [stream_spec((D, C), s) for s in range(3)]
        + [stream_spec((3, C), s) for s in range(3)]
        + [stream_spec((1, C), s) for s in range(3)]
        + [pl.BlockSpec((nd, n, C), lambda b, c: (0, 0, c)),
           pl.BlockSpec((nd, n, C), lambda b, c: (0, 0, c)),
           pl.BlockSpec((nd, n, C), lambda b, c: (0, 0, cb_n + c)),
           pl.BlockSpec((nd, n, C), lambda b, c: (0, 0, cb_n + c)),
           _resident((2 * n, n), lambda b, c: (0, 0)),
           _resident((n, 2 * n), lambda b, c: (0, 0))]
        + c_in)
    return pl.pallas_call(
        functools.partial(_hyena_kernel, P=P, n=n, fchunk=16, n_cast=len(c_args)),
        grid=(B, cb_n),
        in_specs=in_specs,
        out_specs=[pl.BlockSpec((1, L, C), lambda b, c: (b, 0, c))] + c_out,
        out_shape=[jax.ShapeDtypeStruct((B, L, D), BF16)] + c_shapes,
        scratch_shapes=[pltpu.VMEM((L, C), F32),
                        pltpu.VMEM((L, C), F32),
                        pltpu.VMEM((L, C), BF16),
                        pltpu.VMEM((L, C), F32),
                        pltpu.VMEM((P, 2 * n, C), SPEC_DTYPE),
                        pltpu.VMEM((P, 2 * n, C), BF16)],
        compiler_params=_cparams(2),
        name="hyena_core",
    )(hn, w_in, w_in, w_in, conv_w, conv_w, conv_w, conv_b, conv_b, conv_b,
      gc, gs, gc, gs, fwd, inv, *c_args)


def _tail_kernel(a_ref, wo_ref, r_ref, gmix_ref, gpre_ref, gpost_ref, wu_ref, wd_ref, o_ref, *, fchunk):
    sm = TAIL_SLAB_ROWS
    FF = wu_ref.shape[1]
    for p0 in range(0, a_ref.shape[0], 2 * sm):
        rows = [slice(p0 + sm * u, p0 + sm * (u + 1)) for u in range(2)]
        ms = [_bdot(a_ref[r, :], wo_ref[...]) for r in rows]
        hs = [r_ref[r, :] + _rms(m, gmix_ref[...]) for r, m in zip(rows, ms)]
        hns = [_rms(h, gpre_ref[...]).astype(BF16) for h in hs]
        accs = [None, None]
        for c in range(FF // fchunk):
            acts = []
            for u in range(2):
                a = jnp.maximum(_bdot(hns[u], wu_ref[:, c * fchunk:(c + 1) * fchunk]), 0.0)
                acts.append((a * a).astype(BF16))
            for u in range(2):
                part = _bdot(acts[u], wd_ref[c * fchunk:(c + 1) * fchunk, :])
                accs[u] = part if accs[u] is None else accs[u] + part
        for u in range(2):
            o_ref[rows[u], :] = hs[u] + _rms(accs[u], gpost_ref[...])


def _tail(a, w_o, resid, g_mix, g_pre, g_post, w_up, w_down, tm=512, fchunk=1024):
    M, K = a.shape
    D = w_o.shape[1]
    FF = w_up.shape[1]
    row = lambda shape: pl.BlockSpec(shape, lambda i: (i, 0))
    gain = pl.BlockSpec((1, D), lambda i: (0, 0))
    return pl.pallas_call(
        functools.partial(_tail_kernel, fchunk=fchunk),
        grid=(M // tm,),
        in_specs=[row((tm, K)), _resident((K, D), lambda i: (0, 0)), row((tm, D)),
                  gain, gain, gain,
                  _resident((D, FF), lambda i: (0, 0)),
                  _resident((FF, D), lambda i: (0, 0))],
        out_specs=row((tm, D)),
        out_shape=jax.ShapeDtypeStruct((M, D), F32),
        compiler_params=_cparams(1),
        name="mixer_tail_mlp",
    )(a, w_o, resid, g_mix.reshape(1, D), g_pre.reshape(1, D), g_post.reshape(1, D), w_up, w_down)


def _qkv_kernel(h_ref, g_ref, w32_ref, ct_ref, slo_ref, shi_ref, q_ref, k_ref, v_ref, w_ref, *, halves):
    @pl.when(pl.program_id(0) == 0)
    def _():
        w_ref[...] = w32_ref[...].astype(w_ref.dtype)

    nq, nk = q_ref.shape[1], k_ref.shape[1]
    hm = h_ref.shape[0] // halves
    rows = [slice(hm * u, hm * (u + 1)) for u in range(halves)]
    half = ROT_DIM // 2
    scale = LOG2E * HEAD_DIM ** -0.5
    qkvs = [_bdot(_rms(h_ref[r, :], g_ref[...]).astype(BF16), w_ref[...]) for r in rows]
    for r, qkv in zip(rows, qkvs):
        ct = ct_ref[r, :]
        slo = slo_ref[r, :]
        shi = shi_ref[r, :]
        for j in range((nq + nk) // 128):
            t = qkv[:, 128 * j:128 * (j + 1)]
            rot = t * ct + pltpu.roll(t, 128 - half, axis=1) * slo + pltpu.roll(t, half, axis=1) * shi
            if 128 * j < nq:
                q_ref[r, 128 * j:128 * (j + 1)] = (rot * scale).astype(q_ref.dtype)
            else:
                k_ref[r, 128 * j - nq:128 * (j + 1) - nq] = rot.astype(k_ref.dtype)
        v_ref[r, :] = qkv[:, nq + nk:].astype(v_ref.dtype)


def _qkv(h, g, w, S, tm=1024, halves=2):
    M, D = h.shape
    nq = D
    nk = (w.shape[1] - nq) // 2
    ct, slo, shi = [jnp.asarray(a) for a in _rope_tables(S)]
    pb = S // tm
    tab = pl.BlockSpec((tm, 128), lambda i: (i % pb, 0))
    return pl.pallas_call(
        functools.partial(_qkv_kernel, halves=halves),
        grid=(M // tm,),
        in_specs=[pl.BlockSpec((tm, D), lambda i: (i, 0)),
                  pl.BlockSpec((1, D), lambda i: (0, 0)),
                  _resident((D, nq + 2 * nk), lambda i: (0, 0)),
                  tab, tab, tab],
        out_specs=[pl.BlockSpec((tm, nq), lambda i: (i, 0)),
                   pl.BlockSpec((tm, nk), lambda i: (i, 0)),
                   pl.BlockSpec((tm, nk), lambda i: (i, 0))],
        out_shape=[jax.ShapeDtypeStruct((M, nq), BF16),
                   jax.ShapeDtypeStruct((M, nk), BF16),
                   jax.ShapeDtypeStruct((M, nk), BF16)],
        scratch_shapes=[pltpu.VMEM(w.shape, BF16)],
        compiler_params=_cparams(1),
        name="qkv_rope",
    )(h, g.reshape(1, D), w, ct, slo, shi)


def _attn_kernel(*refs, S, n_kv, n_cast):
    sink_ref, q_ref, k_ref, v_ref = refs[:4]
    cast_in = refs[4:4 + n_cast]
    o_ref = refs[4 + n_cast]
    cast_out = refs[5 + n_cast:5 + 2 * n_cast]
    krep_ref, vrep_ref = refs[5 + 2 * n_cast:]
    BQ = WINDOW
    BK = 3 * WINDOW
    GW = KV_GROUP * HEAD_DIM
    dn = (((1,), (1,)), ((), ()))

    for src, dst in zip(cast_in, cast_out):
        dst[...] = src[...].astype(dst.dtype)

    for g in range(n_kv):
        kg = k_ref[0, :, g * HEAD_DIM:(g + 1) * HEAD_DIM]
        vg = v_ref[0, :, g * HEAD_DIM:(g + 1) * HEAD_DIM]
        krep_ref[g] = jnp.concatenate([kg] * KV_GROUP, axis=1)
        vrep_ref[g] = jnp.concatenate([vg] * KV_GROUP, axis=1)

    lane = lax.broadcasted_iota(jnp.int32, (BQ, GW), 1)
    head_lanes = [(lane >= hh * HEAD_DIM) & (lane < (hh + 1) * HEAD_DIM) for hh in range(KV_GROUP)]
    rowblk = lax.broadcasted_iota(jnp.int32, (KV_GROUP * BQ, 1), 0) // BQ

    def body(qb, carry):
        qs = pl.multiple_of(qb * BQ, BQ)
        ks = pl.multiple_of(jnp.clip(qs - WINDOW, 0, S - BK), WINDOW)
        qpos = qs + lax.broadcasted_iota(jnp.int32, (BQ, BK), 0)
        kpos = ks + lax.broadcasted_iota(jnp.int32, (BQ, BK), 1)
        maskadd = jnp.where(jnp.abs(kpos - qpos) <= WINDOW, 0.0, NEG).astype(F32)
        maskadd = jnp.concatenate([maskadd] * KV_GROUP, axis=0)
        for g in range(n_kv):
            qg = q_ref[0, pl.ds(qs, BQ), g * GW:(g + 1) * GW]
            qst = jnp.concatenate([jnp.where(head_lanes[hh], qg, jnp.zeros_like(qg))
                                   for hh in range(KV_GROUP)], axis=0)
            sink = jnp.zeros((KV_GROUP * BQ, 1), F32)
            for hh in range(KV_GROUP):
                sink = jnp.where(rowblk == hh, sink_ref[g * KV_GROUP + hh] * LOG2E, sink)
            s = lax.dot_general(qst, krep_ref[g, pl.ds(ks, BK), :], dn,
                                preferred_element_type=F32) + maskadd
            m = jnp.maximum(jnp.max(s, axis=-1, keepdims=True), sink)
            p = jnp.exp2(s - m)
            denom = jnp.sum(p, axis=-1, keepdims=True) + jnp.exp2(sink - m)
            ost = _bdot(p.astype(BF16), vrep_ref[g, pl.ds(ks, BK), :]) / denom
            o = jnp.where(head_lanes[0], ost[0:BQ], 0.0)
            for hh in range(1, KV_GROUP):
                o = jnp.where(head_lanes[hh], ost[hh * BQ:(hh + 1) * BQ], o)
            o_ref[0, pl.ds(qs, BQ), g * GW:(g + 1) * GW] = o.astype(o_ref.dtype)
        return carry

    lax.fori_loop(0, S // BQ, body, 0, unroll=4)


def _attention(q, k, v, sink, cast_weights):
    B, S, NQ = q.shape
    NK = k.shape[2]
    n_kv = NK // HEAD_DIM
    c_args, c_in, c_out, c_shapes = _cast_rider(cast_weights, B, lambda b: b)
    return pl.pallas_call(
        functools.partial(_attn_kernel, S=S, n_kv=n_kv, n_cast=len(c_args)),
        grid=(B,),
        in_specs=[pl.BlockSpec(memory_space=pltpu.SMEM),
                  pl.BlockSpec((1, S, NQ), lambda b: (b, 0, 0)),
                  pl.BlockSpec((1, S, NK), lambda b: (b, 0, 0)),
                  pl.BlockSpec((1, S, NK), lambda b: (b, 0, 0))] + c_in,
        out_specs=[pl.BlockSpec((1, S, NQ), lambda b: (b, 0, 0))] + c_out,
        out_shape=[jax.ShapeDtypeStruct((B, S, NQ), BF16)] + c_shapes,
        scratch_shapes=[pltpu.VMEM((n_kv, S, KV_GROUP * HEAD_DIM), BF16),
                        pltpu.VMEM((n_kv, S, KV_GROUP * HEAD_DIM), BF16)],
        compiler_params=_cparams(1),
        name="window_attention",
    )(sink, q, k, v, *c_args)


def kernel(x, norm_mix_pre, norm_mix_post, norm_mlp_pre, norm_mlp_post, w_up, w_down, hy_w_in, hy_conv_w, hy_conv_b, hy_f_w1, hy_f_b1, hy_f_w2, hy_f_b2, hy_f_w3, hy_f_b3, hy_f_freq, hy_f_wout, hy_bias, hy_w_out, at_w_qkv, at_sink, at_w_o):
    B, L, D = x.shape
    M = B * L
    P = CONV_BLOCKS
    n = L // P
    depth = norm_mix_pre.shape[0]
    fwd, inv, filt_c, filt_s = [jnp.asarray(a).astype(BF16) for a in _dft_tables(n)]

    h = x.reshape(M, D)
    for i in range(depth):
        j = i // 2
        if i % 2 == 0:
            gc, gs, hn, w_in_b = _filter_spectra(
                L, hy_f_w1[j], hy_f_b1[j], hy_f_w2[j], hy_f_b2[j], hy_f_w3[j], hy_f_b3[j],
                hy_f_freq[j], hy_f_wout[j], hy_bias[j], filt_c, filt_s, P, h, norm_mix_pre[i],
                [(hy_w_in, j)])
            z, wo_b, wu_b, wd_b = _hyena_core(
                hn.reshape(B, L, D), w_in_b, hy_conv_w[j], hy_conv_b[j].reshape(1, -1),
                gc, gs, fwd, inv, P, [(hy_w_out, j), (w_up, i), (w_down, i)])
            a = z.reshape(M, D)
        else:
            q, k, v = _qkv(h, norm_mix_pre[i], at_w_qkv[j], L)
            o, wo_b, wu_b, wd_b = _attention(
                q.reshape(B, L, -1), k.reshape(B, L, -1), v.reshape(B, L, -1), at_sink[j],
                [(at_w_o, j), (w_up, i), (w_down, i)])
            a = o.reshape(M, -1)
        h = _tail(a, wo_b, h, norm_mix_post[i], norm_mlp_pre[i], norm_mlp_post[i], wu_b, wd_b)
    return h.reshape(B, L, D)
```

```python
import functools
import math

import numpy as np
import jax
import jax.numpy as jnp
from jax import lax
from jax.experimental import pallas as pl
from jax.experimental.pallas import tpu as pltpu

BF16 = jnp.bfloat16
F32 = jnp.float32

EPS = 1e-6
NEG = -1e30
LOG2E = math.log2(math.e)

HY_ORDER = 2
HY_EMB = 33
HY_BANDS = (HY_EMB - 1) // 2
HY_DECAY_TARGET = 1e-2
HY_FAST_DECAY = 0.3
HY_SLOW_DECAY = 1.5
HEAD_DIM = 64
KV_GROUP = 4
WINDOW = 128
ROT_DIM = HEAD_DIM // 4
ROPE_THETA = 500000.0

CONV_BLOCKS = 4
CH_BLOCK = 256
SPEC_DTYPE = BF16
TAIL_SLAB_ROWS = 256

VMEM_LIMIT = 56 * 1024 * 1024


def _cparams(n_axes, flags=None):
    return pltpu.CompilerParams(
        dimension_semantics=("arbitrary",) * n_axes,
        vmem_limit_bytes=VMEM_LIMIT,
        flags=flags)


def _resident(shape, index_map):
    return pl.BlockSpec(shape, index_map, pipeline_mode=pl.Buffered(1))


def _rms(x, g):
    ms = jnp.mean(x * x, axis=-1, keepdims=True)
    return x * lax.rsqrt(ms + EPS) * g


def _bdot(a, b):
    return jnp.dot(a, b, preferred_element_type=F32)


def _dft_tables(n):
    f = np.arange(n, dtype=np.int64)[:, None]
    r = np.arange(n, dtype=np.int64)[None, :]
    k_pos = ((2 * f + 1) * r) % (4 * n)
    k_neg = ((2 * f + 1) * (r - n)) % (4 * n)
    ang_pos = np.pi * k_pos / (2 * n)
    ang_neg = np.pi * k_neg / (2 * n)
    cpos, spos = np.cos(ang_pos), np.sin(ang_pos)
    cneg, sneg = np.cos(ang_neg), np.sin(ang_neg)
    cneg[:, 0] = 0.0
    sneg[:, 0] = 0.0
    fwd = np.concatenate([cpos, spos], axis=0)
    inv = np.concatenate([cpos.T, spos.T], axis=1) / n
    filt_c = np.concatenate([cneg, cpos], axis=1)
    filt_s = np.concatenate([sneg, spos], axis=1)
    return [np.asarray(a, np.float32) for a in (fwd, inv, filt_c, filt_s)]


def _filter_positions(L):
    t = np.linspace(0.0, 1.0, L)[:, None]
    w = (2.0 * np.pi / L) * np.arange(L)[:, None]
    f = np.linspace(1e-4, HY_BANDS - 1, HY_BANDS)[None, :]
    z = np.concatenate([t, np.cos(f * w), -np.sin(f * w)], axis=-1)
    return np.asarray(z, np.float32)


def _rope_tables(S):
    half = ROT_DIM // 2
    inv = ROPE_THETA ** (-np.arange(0, ROT_DIM, 2, dtype=np.float64) / ROT_DIM)
    ang = np.arange(S, dtype=np.float64)[:, None] * inv[None, :]
    lane = np.arange(128) % HEAD_DIM
    ang_l = ang[:, lane % half]
    cos_t = np.where(lane[None, :] < ROT_DIM, np.cos(ang_l), 1.0)
    sin_lo = np.where(lane[None, :] < half, -np.sin(ang_l), 0.0)
    sin_hi = np.where((lane[None, :] >= half) & (lane[None, :] < ROT_DIM), np.sin(ang_l), 0.0)
    return [np.asarray(a, np.float32) for a in (cos_t, sin_lo, sin_hi)]


def _spectra_kernel(*refs, P, n, norm_rows, n_cast):
    (z_ref, w1_ref, b1_ref, w2_ref, b2_ref, w3_ref, b3_ref, fr_ref,
     woutf_ref, woutb_ref, dl_ref, bias_ref, mc_ref, ms_ref, x_ref, g_ref) = refs[:16]
    cast_in = refs[16:16 + n_cast]
    gc_ref, gs_ref, hn_ref = refs[16 + n_cast:19 + n_cast]
    cast_out = refs[19 + n_cast:19 + 2 * n_cast]
    hid_ref, hf_ref, hb_ref = refs[19 + 2 * n_cast:]
    hi = lax.Precision.HIGHEST

    for src, dst in zip(cast_in, cast_out):
        dst[...] = src[...].astype(dst.dtype)

    L, cw = hf_ref.shape
    FW = hid_ref.shape[1]

    @pl.when(pl.program_id(0) == 0)
    def _():
        def two(v):
            return jnp.concatenate([v, v], axis=1)

        def blockdiag(w):
            zero = jnp.zeros_like(w)
            return jnp.concatenate([jnp.concatenate([w, zero], axis=1),
                                    jnp.concatenate([zero, w], axis=1)], axis=0)

        fr = two(fr_ref[...])
        h = z_ref[...]
        for w_ref, b_ref in ((w1_ref, b1_ref), (w2_ref, b2_ref), (w3_ref, b3_ref)):
            h = jnp.sin(fr * (jnp.dot(h, blockdiag(w_ref[...]), precision=hi) + two(b_ref[...])))
        hid_ref[0:L // 2, :] = h[:, :FW]
        hid_ref[L // 2:L, :] = h[:, FW:]

    row = lax.broadcasted_iota(jnp.int32, (L, cw), 0)
    decay = jnp.exp(-(row.astype(F32) * (1.0 / (L - 1))) * dl_ref[...])
    hid = hid_ref[...]
    hid_hi = hid.astype(BF16)
    hid_lo = (hid - hid_hi.astype(F32)).astype(BF16)

    def taps(wout_ref):
        wout = wout_ref[...]
        w_hi = wout.astype(BF16)
        w_lo = (wout - w_hi.astype(F32)).astype(BF16)
        return (_bdot(hid_hi, w_hi) + (_bdot(hid_hi, w_lo) + _bdot(hid_lo, w_hi))) * decay

    hf_ref[...] = taps(woutf_ref).astype(hf_ref.dtype)
    hb_ref[...] = jnp.where(row > 0, taps(woutb_ref), 0.0).astype(hb_ref.dtype)

    mc = mc_ref[...]
    ms = ms_ref[...]
    cpos = mc[:, n:]
    spos = ms[:, n:]
    norm_chunks = [slice(r, r + norm_rows) for r in range(0, x_ref.shape[0], norm_rows)]
    nd = 2 * P - 1
    for idx, d in enumerate(range(-(P - 1), P)):
        for r in norm_chunks[idx * len(norm_chunks) // nd:(idx + 1) * len(norm_chunks) // nd]:
            hn_ref[r, :] = _rms(x_ref[r, :], g_ref[...]).astype(hn_ref.dtype)
        if d == 0:
            hf0 = hf_ref[0:n, :].astype(F32)
            hb0 = hb_ref[0:n, :].astype(F32)
            gc = _bdot(cpos, (hf0 + hb0).astype(BF16)) + bias_ref[...]
            gs = _bdot(spos, (hf0 - hb0).astype(BF16))
        elif d > 0:
            taps = hf_ref[n * (d - 1):n * (d + 1), :].astype(BF16)
            gc = _bdot(mc, taps)
            gs = _bdot(ms, taps)
        else:
            a = -d
            taps = hb_ref[n * (a - 1):n * (a + 1), :].astype(BF16)
            gc = _bdot(mc, taps)
            gs = -_bdot(ms, taps)
        gc_ref[d + P - 1] = gc.astype(gc_ref.dtype)
        gs_ref[d + P - 1] = gs.astype(gs_ref.dtype)


def _filter_spectra(L, f_w1, f_b1, f_w2, f_b2, f_w3, f_b3, f_freq, f_wout, bias, mc, ms, P, x2d, g,
                    cast_weights, cw=256, norm_rows=256):
    M, D = x2d.shape
    FW = f_w1.shape[1]
    n = L // P
    half = f_wout.shape[1] // 2
    nblk = half // cw
    xm = M // nblk
    zpad = np.pad(_filter_positions(L), ((0, 0), (0, FW - HY_EMB)))
    z = jnp.asarray(np.concatenate([zpad[:L // 2], zpad[L // 2:]], axis=1))
    w1 = jnp.pad(f_w1, ((0, FW - HY_EMB), (0, 0)))
    max_decay = math.log(HY_DECAY_TARGET) / HY_FAST_DECAY
    min_decay = math.log(HY_DECAY_TARGET) / HY_SLOW_DECAY
    absdelta = jnp.asarray(np.abs(np.linspace(min_decay, max_decay, D)).astype(np.float32))[None, :]
    small = lambda shape: pl.BlockSpec(shape, lambda j: (0, 0))
    out_sd = jax.ShapeDtypeStruct((2 * P - 1, n, half), SPEC_DTYPE)
    c_args, c_in, c_out, c_shapes = _cast_rider(cast_weights, nblk, lambda j: j)
    return pl.pallas_call(
        functools.partial(_spectra_kernel, P=P, n=n, norm_rows=norm_rows, n_cast=len(c_args)),
        grid=(nblk,),
        in_specs=[small((L // 2, 2 * FW)), small((FW, FW)), small((1, FW)), small((FW, FW)), small((1, FW)),
                  small((FW, FW)), small((1, FW)), small((1, FW)),
                  pl.BlockSpec((FW, cw), lambda j: (0, j)),
                  pl.BlockSpec((FW, cw), lambda j: (0, nblk + j)),
                  pl.BlockSpec((1, cw), lambda j: (0, j % (D // cw))),
                  pl.BlockSpec((1, cw), lambda j: (0, j)),
                  small((n, 2 * n)), small((n, 2 * n)),
                  pl.BlockSpec((xm, D), lambda j: (j, 0)),
                  small((1, D))] + c_in,
        out_specs=[pl.BlockSpec((2 * P - 1, n, cw), lambda j: (0, 0, j)),
                   pl.BlockSpec((2 * P - 1, n, cw), lambda j: (0, 0, j)),
                   pl.BlockSpec((xm, D), lambda j: (j, 0))] + c_out,
        out_shape=[out_sd, out_sd, jax.ShapeDtypeStruct((M, D), BF16)] + c_shapes,
        scratch_shapes=[pltpu.VMEM((L, FW), F32),
                        pltpu.VMEM((L, cw), BF16),
                        pltpu.VMEM((L, cw), BF16)],
        compiler_params=_cparams(1),
        name="filter_spectra",
    )(z, w1, f_b1.reshape(1, FW), f_w2, f_b2.reshape(1, FW), f_w3, f_b3.reshape(1, FW),
      f_freq.reshape(1, FW), f_wout, f_wout, absdelta, bias.reshape(1, half), mc, ms,
      x2d, g.reshape(1, D), *c_args)


def _hyena_kernel(*refs, P, n, fchunk, n_cast):
    (hn_ref, wv_ref, w1_ref, w2_ref, cwv_ref, cw1_ref, cw2_ref, cbv_ref, cb1_ref, cb2_ref,
     gc0_ref, gs0_ref, gc1_ref, gs1_ref, fwd_ref, inv_ref) = refs[:16]
    cast_in = refs[16:16 + n_cast]
    z_ref = refs[16 + n_cast]
    cast_out = refs[17 + n_cast:17 + 2 * n_cast]
    rawa_ref, rawb_ref, v_ref, gate_ref, uspec_ref, yspec_ref = refs[17 + 2 * n_cast:]

    for src, dst in zip(cast_in, cast_out):
        dst[...] = src[...].astype(dst.dtype)

    row = lax.broadcasted_iota(jnp.int32, (n, 1), 0)
    blk = [slice(n * i, n * (i + 1)) for i in range(P)]
    chunks = n // fchunk
    fwd = fwd_ref[...]
    inv = inv_ref[...]

    wv, w1, w2 = wv_ref[...], w1_ref[...], w2_ref[...]

    def project(w, dst_ref, i):
        dst_ref[blk[i], :] = _bdot(hn_ref[0, blk[i], :], w)

    def short_conv(raw_ref, cw_ref, cb_ref, dst, i):
        cw = cw_ref[...]
        cur = raw_ref[blk[i], :]
        prev = pltpu.roll(cur, 1, axis=0)
        nxt = pltpu.roll(cur, n - 1, axis=0)
        first = raw_ref[n * i - 1:n * i, :] if i > 0 else jnp.zeros_like(cur[0:1])
        last = raw_ref[n * (i + 1):n * (i + 1) + 1, :] if i < P - 1 else jnp.zeros_like(cur[0:1])
        prev = jnp.where(row == 0, first, prev)
        nxt = jnp.where(row == n - 1, last, nxt)
        out = prev * cw[0:1] + cur * cw[1:2] + nxt * cw[2:3] + cb_ref[...]
        dst[blk[i], :] = out.astype(dst.dtype)

    def forward(src_ref, j):
        uspec_ref[j] = _bdot(fwd, src_ref[blk[j], :]).astype(uspec_ref.dtype)

    def combine(gc_ref, gs_ref, c):
        fc = slice(c * fchunk, (c + 1) * fchunk)
        fs = slice(n + c * fchunk, n + (c + 1) * fchunk)
        yc = [None] * P
        ys = [None] * P
        for j in range(P):
            uc = uspec_ref[j, fc, :]
            us = uspec_ref[j, fs, :]
            for i in range(P):
                gc = gc_ref[i - j + P - 1, fc, :]
                gs = gs_ref[i - j + P - 1, fc, :]
                tc = gc * uc - gs * us
                ts = gc * us + gs * uc
                yc[i] = tc if yc[i] is None else yc[i] + tc
                ys[i] = ts if ys[i] is None else ys[i] + ts
        for i in range(P):
            yspec_ref[i, fc, :] = yc[i].astype(BF16)
            yspec_ref[i, fs, :] = ys[i].astype(BF16)

    gate_proj = [(w1, rawb_ref, i) for i in range(P)] + [(w2, rawa_ref, i) for i in range(P)]
    per_combine = chunks // 3

    for i in range(P):
        project(wv, rawa_ref, i)
    for i in range(P):
        short_conv(rawa_ref, cwv_ref, cbv_ref, v_ref, i)
        if i < 3:
            project(*gate_proj[i])
    for i in range(P):
        forward(v_ref, i)
    for c in range(chunks):
        combine(gc0_ref, gs0_ref, c)
        if c % per_combine == 0 and c // per_combine < 3:
            project(*gate_proj[3 + c // per_combine])
    for i in range(P):
        short_conv(rawb_ref, cw1_ref, cb1_ref, gate_ref, i)
        v_ref[blk[i], :] = (gate_ref[blk[i], :] * _bdot(inv, yspec_ref[i])).astype(v_ref.dtype)
    for i in range(P):
        forward(v_ref, i)
    for c in range(chunks):
        combine(gc1_ref, gs1_ref, c)
        if c % (chunks // 2) == 0:
            project(*gate_proj[6 + c // (chunks // 2)])
    for i in range(P):
        short_conv(rawa_ref, cw2_ref, cb2_ref, gate_ref, i)
        z_ref[0, blk[i], :] = (gate_ref[blk[i], :] * _bdot(inv, yspec_ref[i])).astype(z_ref.dtype)


def _cast_rider(weights, n_steps, step_of):
    args, in_specs, out_specs, out_shapes = [], [], [], []
    for arr, layer in weights:
        _, R, C = arr.shape
        rps = R // n_steps
        args.append(arr)
        in_specs.append(pl.BlockSpec((None, rps, C), lambda *g, layer=layer: (layer, step_of(*g), 0)))
        out_specs.append(pl.BlockSpec((rps, C), lambda *g: (step_of(*g), 0)))
        out_shapes.append(jax.ShapeDtypeStruct((R, C), BF16))
    return args, in_specs, out_specs, out_shapes


def _hyena_core(hn, w_in, conv_w, conv_b, gc, gs, fwd, inv, P, cast_weights):
    B, L, D = hn.shape
    n = L // P
    cb_n = D // CH_BLOCK
    C = CH_BLOCK
    nd = gc.shape[0]
    c_args, c_in, c_out, c_shapes = _cast_rider(cast_weights, B * cb_n, lambda b, c: b * cb_n + c)

    def stream_spec(shape, s):
        return pl.BlockSpec(shape, lambda b, c: (0, s * cb_n + c))

    in_specs = (
        [pl.BlockSpec((1, L, D), lambda b, c: (b, 0, 0))]
        + [stream_spec((D, C), s) for s in range(3)]
        + [stream_spec((3, C), s) for s in range(3)]
        + [stream_spec((1, C), s) for s in range(3)]
        + [pl.BlockSpec((nd, n, C), lambda b, c: (0, 0, c)),
           pl.BlockSpec((nd, n, C), lambda b, c: (0, 0, c)),
           pl.BlockSpec((nd, n, C), lambda b, c: (0, 0, cb_n + c)),
           pl.BlockSpec((nd, n, C), lambda b, c: (0, 0, cb_n + c)),
           _resident((2 * n, n), lambda b, c: (0, 0)),
           _resident((n, 2 * n), lambda b, c: (0, 0))]
        + c_in)
    return pl.pallas_call(
        functools.partial(_hyena_kernel, P=P, n=n, fchunk=16, n_cast=len(c_args)),
        grid=(B, cb_n),
        in_specs=in_specs,
        out_specs=[pl.BlockSpec((1, L, C), lambda b, c: (b, 0, c))] + c_out,
        out_shape=[jax.ShapeDtypeStruct((B, L, D), BF16)] + c_shapes,
        scratch_shapes=[pltpu.VMEM((L, C), F32),
                        pltpu.VMEM((L, C), F32),
                        pltpu.VMEM((L, C), BF16),
                        pltpu.VMEM((L, C), F32),
                        pltpu.VMEM((P, 2 * n, C), SPEC_DTYPE),
                        pltpu.VMEM((P, 2 * n, C), BF16)],
        compiler_params=_cparams(2),
        name="hyena_core",
    )(hn, w_in, w_in, w_in, conv_w, conv_w, conv_w, conv_b, conv_b, conv_b,
      gc, gs, gc, gs, fwd, inv, *c_args)


def _tail_kernel(a_ref, wo_ref, r_ref, gmix_ref, gpre_ref, gpost_ref, wu_ref, wd_ref, o_ref, *, fchunk):
    sm = TAIL_SLAB_ROWS
    FF = wu_ref.shape[1]
    for p0 in range(0, a_ref.shape[0], 2 * sm):
        rows = [slice(p0 + sm * u, p0 + sm * (u + 1)) for u in range(2)]
        ms = [_bdot(a_ref[r, :], wo_ref[...]) for r in rows]
        hs = [r_ref[r, :] + _rms(m, gmix_ref[...]) for r, m in zip(rows, ms)]
        hns = [_rms(h, gpre_ref[...]).astype(BF16) for h in hs]
        accs = [None, None]
        for c in range(FF // fchunk):
            acts = []
            for u in range(2):
                a = jnp.maximum(_bdot(hns[u], wu_ref[:, c * fchunk:(c + 1) * fchunk]), 0.0)
                acts.append((a * a).astype(BF16))
            for u in range(2):
                part = _bdot(acts[u], wd_ref[c * fchunk:(c + 1) * fchunk, :])
                accs[u] = part if accs[u] is None else accs[u] + part
        for u in range(2):
            o_ref[rows[u], :] = hs[u] + _rms(accs[u], gpost_ref[...])


def _tail(a, w_o, resid, g_mix, g_pre, g_post, w_up, w_down, tm=512, fchunk=1024):
    M, K = a.shape
    D = w_o.shape[1]
    FF = w_up.shape[1]
    row = lambda shape: pl.BlockSpec(shape, lambda i: (i, 0))
    gain = pl.BlockSpec((1, D), lambda i: (0, 0))
    return pl.pallas_call(
        functools.partial(_tail_kernel, fchunk=fchunk),
        grid=(M // tm,),
        in_specs=[row((tm, K)), _resident((K, D), lambda i: (0, 0)), row((tm, D)),
                  gain, gain, gain,
                  _resident((D, FF), lambda i: (0, 0)),
                  _resident((FF, D), lambda i: (0, 0))],
        out_specs=row((tm, D)),
        out_shape=jax.ShapeDtypeStruct((M, D), F32),
        compiler_params=_cparams(1),
        name="mixer_tail_mlp",
    )(a, w_o, resid, g_mix.reshape(1, D), g_pre.reshape(1, D), g_post.reshape(1, D), w_up, w_down)


def _qkv_kernel(h_ref, g_ref, w32_ref, ct_ref, slo_ref, shi_ref, q_ref, k_ref, v_ref, w_ref, *, halves):
    @pl.when(pl.program_id(0) == 0)
    def _():
        w_ref[...] = w32_ref[...].astype(w_ref.dtype)

    nq, nk = q_ref.shape[1], k_ref.shape[1]
    hm = h_ref.shape[0] // halves
    rows = [slice(hm * u, hm * (u + 1)) for u in range(halves)]
    half = ROT_DIM // 2
    scale = LOG2E * HEAD_DIM ** -0.5
    qkvs = [_bdot(_rms(h_ref[r, :], g_ref[...]).astype(BF16), w_ref[...]) for r in rows]
    for r, qkv in zip(rows, qkvs):
        ct = ct_ref[r, :]
        slo = slo_ref[r, :]
        shi = shi_ref[r, :]
        for j in range((nq + nk) // 128):
            t = qkv[:, 128 * j:128 * (j + 1)]
            rot = t * ct + pltpu.roll(t, 128 - half, axis=1) * slo + pltpu.roll(t, half, axis=1) * shi
            if 128 * j < nq:
                q_ref[r, 128 * j:128 * (j + 1)] = (rot * scale).astype(q_ref.dtype)
            else:
                k_ref[r, 128 * j - nq:128 * (j + 1) - nq] = rot.astype(k_ref.dtype)
        v_ref[r, :] = qkv[:, nq + nk:].astype(v_ref.dtype)


def _qkv(h, g, w, S, tm=1024, halves=2):
    M, D = h.shape
    nq = D
    nk = (w.shape[1] - nq) // 2
    ct, slo, shi = [jnp.asarray(a) for a in _rope_tables(S)]
    pb = S // tm
    tab = pl.BlockSpec((tm, 128), lambda i: (i % pb, 0))
    return pl.pallas_call(
        functools.partial(_qkv_kernel, halves=halves),
        grid=(M // tm,),
        in_specs=[pl.BlockSpec((tm, D), lambda i: (i, 0)),
                  pl.BlockSpec((1, D), lambda i: (0, 0)),
                  _resident((D, nq + 2 * nk), lambda i: (0, 0)),
                  tab, tab, tab],
        out_specs=[pl.BlockSpec((tm, nq), lambda i: (i, 0)),
                   pl.BlockSpec((tm, nk), lambda i: (i, 0)),
                   pl.BlockSpec((tm, nk), lambda i: (i, 0))],
        out_shape=[jax.ShapeDtypeStruct((M, nq), BF16),
                   jax.ShapeDtypeStruct((M, nk), BF16),
                   jax.ShapeDtypeStruct((M, nk), BF16)],
        scratch_shapes=[pltpu.VMEM(w.shape, BF16)],
        compiler_params=_cparams(1),
        name="qkv_rope",
    )(h, g.reshape(1, D), w, ct, slo, shi)


def _attn_kernel(*refs, S, n_kv, n_cast):
    sink_ref, q_ref, k_ref, v_ref = refs[:4]
    cast_in = refs[4:4 + n_cast]
    o_ref = refs[4 + n_cast]
    cast_out = refs[5 + n_cast:5 + 2 * n_cast]
    krep_ref, vrep_ref = refs[5 + 2 * n_cast:]
    BQ = WINDOW
    BK = 3 * WINDOW
    GW = KV_GROUP * HEAD_DIM
    dn = (((1,), (1,)), ((), ()))

    for src, dst in zip(cast_in, cast_out):
        dst[...] = src[...].astype(dst.dtype)

    for g in range(n_kv):
        kg = k_ref[0, :, g * HEAD_DIM:(g + 1) * HEAD_DIM]
        vg = v_ref[0, :, g * HEAD_DIM:(g + 1) * HEAD_DIM]
        krep_ref[g] = jnp.concatenate([kg] * KV_GROUP, axis=1)
        vrep_ref[g] = jnp.concatenate([vg] * KV_GROUP, axis=1)

    lane = lax.broadcasted_iota(jnp.int32, (BQ, GW), 1)
    head_lanes = [(lane >= hh * HEAD_DIM) & (lane < (hh + 1) * HEAD_DIM) for hh in range(KV_GROUP)]
    rowblk = lax.broadcasted_iota(jnp.int32, (KV_GROUP * BQ, 1), 0) // BQ

    def body(qb, carry):
        qs = pl.multiple_of(qb * BQ, BQ)
        ks = pl.multiple_of(jnp.clip(qs - WINDOW, 0, S - BK), WINDOW)
        qpos = qs + lax.broadcasted_iota(jnp.int32, (BQ, BK), 0)
        kpos = ks + lax.broadcasted_iota(jnp.int32, (BQ, BK), 1)
        maskadd = jnp.where(jnp.abs(kpos - qpos) <= WINDOW, 0.0, NEG).astype(F32)
        maskadd = jnp.concatenate([maskadd] * KV_GROUP, axis=0)
        for g in range(n_kv):
            qg = q_ref[0, pl.ds(qs, BQ), g * GW:(g + 1) * GW]
            qst = jnp.concatenate([jnp.where(head_lanes[hh], qg, jnp.zeros_like(qg))
                                   for hh in range(KV_GROUP)], axis=0)
            sink = jnp.zeros((KV_GROUP * BQ, 1), F32)
            for hh in range(KV_GROUP):
                sink = jnp.where(rowblk == hh, sink_ref[g * KV_GROUP + hh] * LOG2E, sink)
            s = lax.dot_general(qst, krep_ref[g, pl.ds(ks, BK), :], dn,
                                preferred_element_type=F32) + maskadd
            m = jnp.maximum(jnp.max(s, axis=-1, keepdims=True), sink)
            p = jnp.exp2(s - m)
            denom = jnp.sum(p, axis=-1, keepdims=True) + jnp.exp2(sink - m)
            ost = _bdot(p.astype(BF16), vrep_ref[g, pl.ds(ks, BK), :]) / denom
            o = jnp.where(head_lanes[0], ost[0:BQ], 0.0)
            for hh in range(1, KV_GROUP):
                o = jnp.where(head_lanes[hh], ost[hh * BQ:(hh + 1) * BQ], o)
            o_ref[0, pl.ds(qs, BQ), g * GW:(g + 1) * GW] = o.astype(o_ref.dtype)
        return carry

    lax.fori_loop(0, S // BQ, body, 0, unroll=4)


def _attention(q, k, v, sink, cast_weights):
    B, S, NQ = q.shape
    NK = k.shape[2]
    n_kv = NK // HEAD_DIM
    c_args, c_in, c_out, c_shapes = _cast_rider(cast_weights, B, lambda b: b)
    return pl.pallas_call(
        functools.partial(_attn_kernel, S=S, n_kv=n_kv, n_cast=len(c_args)),
        grid=(B,),
        in_specs=[pl.BlockSpec(memory_space=pltpu.SMEM),
                  pl.BlockSpec((1, S, NQ), lambda b: (b, 0, 0)),
                  pl.BlockSpec((1, S, NK), lambda b: (b, 0, 0)),
                  pl.BlockSpec((1, S, NK), lambda b: (b, 0, 0))] + c_in,
        out_specs=[pl.BlockSpec((1, S, NQ), lambda b: (b, 0, 0))] + c_out,
        out_shape=[jax.ShapeDtypeStruct((B, S, NQ), BF16)] + c_shapes,
        scratch_shapes=[pltpu.VMEM((n_kv, S, KV_GROUP * HEAD_DIM), BF16),
                        pltpu.VMEM((n_kv, S, KV_GROUP * HEAD_DIM), BF16)],
        compiler_params=_cparams(1),
        name="window_attention",
    )(sink, q, k, v, *c_args)


def kernel(x, norm_mix_pre, norm_mix_post, norm_mlp_pre, norm_mlp_post, w_up, w_down, hy_w_in, hy_conv_w, hy_conv_b, hy_f_w1, hy_f_b1, hy_f_w2, hy_f_b2, hy_f_w3, hy_f_b3, hy_f_freq, hy_f_wout, hy_bias, hy_w_out, at_w_qkv, at_sink, at_w_o):
    B, L, D = x.shape
    M = B * L
    P = CONV_BLOCKS
    n = L // P
    depth = norm_mix_pre.shape[0]
    fwd, inv, filt_c, filt_s = [jnp.asarray(a).astype(BF16) for a in _dft_tables(n)]

    h = x.reshape(M, D)
    for i in range(depth):
        j = i // 2
        if i % 2 == 0:
            gc, gs, hn, w_in_b, wd_b = _filter_spectra(
                L, hy_f_w1[j], hy_f_b1[j], hy_f_w2[j], hy_f_b2[j], hy_f_w3[j], hy_f_b3[j],
                hy_f_freq[j], hy_f_wout[j], hy_bias[j], filt_c, filt_s, P, h, norm_mix_pre[i],
                [(hy_w_in, j), (w_down, i)])
            z, wo_b, wu_b = _hyena_core(
                hn.reshape(B, L, D), w_in_b, hy_conv_w[j], hy_conv_b[j].reshape(1, -1),
                gc, gs, fwd, inv, P, [(hy_w_out, j), (w_up, i)])
            a = z.reshape(M, D)
        else:
            q, k, v = _qkv(h, norm_mix_pre[i], at_w_qkv[j], L)
            o, wo_b, wu_b, wd_b = _attention(
                q.reshape(B, L, -1), k.reshape(B, L, -1), v.reshape(B, L, -1), at_sink[j],
                [(at_w_o, j), (w_up, i), (w_down, i)])
            a = o.reshape(M, -1)
        h = _tail(a, wo_b, h, norm_mix_post[i], norm_mlp_pre[i], norm_mlp_post[i], wu_b, wd_b)
    return h.reshape(B, L, D)
```

```python
import functools
import math

import numpy as np
import jax
import jax.numpy as jnp
from jax import lax
from jax.experimental import pallas as pl
from jax.experimental.pallas import tpu as pltpu

BF16 = jnp.bfloat16
F32 = jnp.float32

EPS = 1e-6
NEG = -1e30
LOG2E = math.log2(math.e)

HY_ORDER = 2
HY_EMB = 33
HY_BANDS = (HY_EMB - 1) // 2
HY_DECAY_TARGET = 1e-2
HY_FAST_DECAY = 0.3
HY_SLOW_DECAY = 1.5
HEAD_DIM = 64
KV_GROUP = 4
WINDOW = 128
ROT_DIM = HEAD_DIM // 4
ROPE_THETA = 500000.0

CONV_BLOCKS = 8
CH_BLOCK = 256
SPEC_DTYPE = BF16
TAIL_SLAB_ROWS = 256

VMEM_LIMIT = 56 * 1024 * 1024


def _cparams(n_axes, flags=None):
    return pltpu.CompilerParams(
        dimension_semantics=("arbitrary",) * n_axes,
        vmem_limit_bytes=VMEM_LIMIT,
        flags=flags)


def _resident(shape, index_map):
    return pl.BlockSpec(shape, index_map, pipeline_mode=pl.Buffered(1))


def _rms(x, g):
    ms = jnp.mean(x * x, axis=-1, keepdims=True)
    return x * lax.rsqrt(ms + EPS) * g


def _bdot(a, b):
    return jnp.dot(a, b, preferred_element_type=F32)


def _dft_tables(n):
    f = np.arange(n, dtype=np.int64)[:, None]
    r = np.arange(n, dtype=np.int64)[None, :]
    k_pos = ((2 * f + 1) * r) % (4 * n)
    k_neg = ((2 * f + 1) * (r - n)) % (4 * n)
    ang_pos = np.pi * k_pos / (2 * n)
    ang_neg = np.pi * k_neg / (2 * n)
    cpos, spos = np.cos(ang_pos), np.sin(ang_pos)
    cneg, sneg = np.cos(ang_neg), np.sin(ang_neg)
    cneg[:, 0] = 0.0
    sneg[:, 0] = 0.0
    fwd = np.concatenate([cpos, spos], axis=0)
    inv = np.concatenate([cpos.T, spos.T], axis=1) / n
    filt_c = np.concatenate([cneg, cpos], axis=1)
    filt_s = np.concatenate([sneg, spos], axis=1)
    return [np.asarray(a, np.float32) for a in (fwd, inv, filt_c, filt_s)]


def _filter_positions(L):
    t = np.linspace(0.0, 1.0, L)[:, None]
    w = (2.0 * np.pi / L) * np.arange(L)[:, None]
    f = np.linspace(1e-4, HY_BANDS - 1, HY_BANDS)[None, :]
    z = np.concatenate([t, np.cos(f * w), -np.sin(f * w)], axis=-1)
    return np.asarray(z, np.float32)


def _rope_tables(S):
    half = ROT_DIM // 2
    inv = ROPE_THETA ** (-np.arange(0, ROT_DIM, 2, dtype=np.float64) / ROT_DIM)
    ang = np.arange(S, dtype=np.float64)[:, None] * inv[None, :]
    lane = np.arange(128) % HEAD_DIM
    ang_l = ang[:, lane % half]
    cos_t = np.where(lane[None, :] < ROT_DIM, np.cos(ang_l), 1.0)
    sin_lo = np.where(lane[None, :] < half, -np.sin(ang_l), 0.0)
    sin_hi = np.where((lane[None, :] >= half) & (lane[None, :] < ROT_DIM), np.sin(ang_l), 0.0)
    return [np.asarray(a, np.float32) for a in (cos_t, sin_lo, sin_hi)]


def _spectra_kernel(*refs, P, n, norm_rows, n_cast):
    (z_ref, w1_ref, b1_ref, w2_ref, b2_ref, w3_ref, b3_ref, fr_ref,
     woutf_ref, woutb_ref, dl_ref, bias_ref, mc_ref, ms_ref, x_ref, g_ref) = refs[:16]
    cast_in = refs[16:16 + n_cast]
    gc_ref, gs_ref, hn_ref = refs[16 + n_cast:19 + n_cast]
    cast_out = refs[19 + n_cast:19 + 2 * n_cast]
    hid_ref, hf_ref, hb_ref = refs[19 + 2 * n_cast:]
    hi = lax.Precision.HIGHEST

    for src, dst in zip(cast_in, cast_out):
        dst[...] = src[...].astype(dst.dtype)

    L, cw = hf_ref.shape
    FW = hid_ref.shape[1]

    @pl.when(pl.program_id(0) == 0)
    def _():
        def two(v):
            return jnp.concatenate([v, v], axis=1)

        def blockdiag(w):
            zero = jnp.zeros_like(w)
            return jnp.concatenate([jnp.concatenate([w, zero], axis=1),
                                    jnp.concatenate([zero, w], axis=1)], axis=0)

        fr = two(fr_ref[...])
        h = z_ref[...]
        for w_ref, b_ref in ((w1_ref, b1_ref), (w2_ref, b2_ref), (w3_ref, b3_ref)):
            h = jnp.sin(fr * (jnp.dot(h, blockdiag(w_ref[...]), precision=hi) + two(b_ref[...])))
        hid_ref[0:L // 2, :] = h[:, :FW]
        hid_ref[L // 2:L, :] = h[:, FW:]

    row = lax.broadcasted_iota(jnp.int32, (L, cw), 0)
    decay = jnp.exp(-(row.astype(F32) * (1.0 / (L - 1))) * dl_ref[...])
    hid = hid_ref[...]
    hid_hi = hid.astype(BF16)
    hid_lo = (hid - hid_hi.astype(F32)).astype(BF16)

    def taps(wout_ref):
        wout = wout_ref[...]
        w_hi = wout.astype(BF16)
        w_lo = (wout - w_hi.astype(F32)).astype(BF16)
        return (_bdot(hid_hi, w_hi) + (_bdot(hid_hi, w_lo) + _bdot(hid_lo, w_hi))) * decay

    hf_ref[...] = taps(woutf_ref).astype(hf_ref.dtype)
    hb_ref[...] = jnp.where(row > 0, taps(woutb_ref), 0.0).astype(hb_ref.dtype)

    mc = mc_ref[...]
    ms = ms_ref[...]
    cpos = mc[:, n:]
    spos = ms[:, n:]
    norm_chunks = [slice(r, r + norm_rows) for r in range(0, x_ref.shape[0], norm_rows)]
    nd = 2 * P - 1
    for idx, d in enumerate(range(-(P - 1), P)):
        for r in norm_chunks[idx * len(norm_chunks) // nd:(idx + 1) * len(norm_chunks) // nd]:
            hn_ref[r, :] = _rms(x_ref[r, :], g_ref[...]).astype(hn_ref.dtype)
        if d == 0:
            hf0 = hf_ref[0:n, :].astype(F32)
            hb0 = hb_ref[0:n, :].astype(F32)
            gc = _bdot(cpos, (hf0 + hb0).astype(BF16)) + bias_ref[...]
            gs = _bdot(spos, (hf0 - hb0).astype(BF16))
        elif d > 0:
            taps = hf_ref[n * (d - 1):n * (d + 1), :].astype(BF16)
            gc = _bdot(mc, taps)
            gs = _bdot(ms, taps)
        else:
            a = -d
            taps = hb_ref[n * (a - 1):n * (a + 1), :].astype(BF16)
            gc = _bdot(mc, taps)
            gs = -_bdot(ms, taps)
        gc_ref[d + P - 1] = gc.astype(gc_ref.dtype)
        gs_ref[d + P - 1] = gs.astype(gs_ref.dtype)


def _filter_spectra(L, f_w1, f_b1, f_w2, f_b2, f_w3, f_b3, f_freq, f_wout, bias, mc, ms, P, x2d, g,
                    cast_weights, cw=256, norm_rows=256):
    M, D = x2d.shape
    FW = f_w1.shape[1]
    n = L // P
    half = f_wout.shape[1] // 2
    nblk = half // cw
    xm = M // nblk
    zpad = np.pad(_filter_positions(L), ((0, 0), (0, FW - HY_EMB)))
    z = jnp.asarray(np.concatenate([zpad[:L // 2], zpad[L // 2:]], axis=1))
    w1 = jnp.pad(f_w1, ((0, FW - HY_EMB), (0, 0)))
    max_decay = math.log(HY_DECAY_TARGET) / HY_FAST_DECAY
    min_decay = math.log(HY_DECAY_TARGET) / HY_SLOW_DECAY
    absdelta = jnp.asarray(np.abs(np.linspace(min_decay, max_decay, D)).astype(np.float32))[None, :]
    small = lambda shape: pl.BlockSpec(shape, lambda j: (0, 0))
    out_sd = jax.ShapeDtypeStruct((2 * P - 1, n, half), SPEC_DTYPE)
    c_args, c_in, c_out, c_shapes = _cast_rider(cast_weights, nblk, lambda j: j)
    return pl.pallas_call(
        functools.partial(_spectra_kernel, P=P, n=n, norm_rows=norm_rows, n_cast=len(c_args)),
        grid=(nblk,),
        in_specs=[small((L // 2, 2 * FW)), small((FW, FW)), small((1, FW)), small((FW, FW)), small((1, FW)),
                  small((FW, FW)), small((1, FW)), small((1, FW)),
                  pl.BlockSpec((FW, cw), lambda j: (0, j)),
                  pl.BlockSpec((FW, cw), lambda j: (0, nblk + j)),
                  pl.BlockSpec((1, cw), lambda j: (0, j % (D // cw))),
                  pl.BlockSpec((1, cw), lambda j: (0, j)),
                  small((n, 2 * n)), small((n, 2 * n)),
                  pl.BlockSpec((xm, D), lambda j: (j, 0)),
                  small((1, D))] + c_in,
        out_specs=[pl.BlockSpec((2 * P - 1, n, cw), lambda j: (0, 0, j)),
                   pl.BlockSpec((2 * P - 1, n, cw), lambda j: (0, 0, j)),
                   pl.BlockSpec((xm, D), lambda j: (j, 0))] + c_out,
        out_shape=[out_sd, out_sd, jax.ShapeDtypeStruct((M, D), BF16)] + c_shapes,
        scratch_shapes=[pltpu.VMEM((L, FW), F32),
                        pltpu.VMEM((L, cw), BF16),
                        pltpu.VMEM((L, cw), BF16)],
        compiler_params=_cparams(1),
        name="filter_spectra",
    )(z, w1, f_b1.reshape(1, FW), f_w2, f_b2.reshape(1, FW), f_w3, f_b3.reshape(1, FW),
      f_freq.reshape(1, FW), f_wout, f_wout, absdelta, bias.reshape(1, half), mc, ms,
      x2d, g.reshape(1, D), *c_args)


def _hyena_kernel(*refs, P, n, fchunk, n_cast):
    (hn_ref, wv_ref, w1_ref, w2_ref, cwv_ref, cw1_ref, cw2_ref, cbv_ref, cb1_ref, cb2_ref,
     gc0_ref, gs0_ref, gc1_ref, gs1_ref, fwd_ref, inv_ref) = refs[:16]
    cast_in = refs[16:16 + n_cast]
    z_ref = refs[16 + n_cast]
    cast_out = refs[17 + n_cast:17 + 2 * n_cast]
    rawa_ref, rawb_ref, v_ref, gate_ref, uspec_ref, yspec_ref = refs[17 + 2 * n_cast:]

    for src, dst in zip(cast_in, cast_out):
        dst[...] = src[...].astype(dst.dtype)

    row = lax.broadcasted_iota(jnp.int32, (n, 1), 0)
    blk = [slice(n * i, n * (i + 1)) for i in range(P)]
    chunks = n // fchunk
    fwd = fwd_ref[...]
    inv = inv_ref[...]

    wv, w1, w2 = wv_ref[...], w1_ref[...], w2_ref[...]

    def project(w, dst_ref, i):
        dst_ref[blk[i], :] = _bdot(hn_ref[0, blk[i], :], w)

    def short_conv(raw_ref, cw_ref, cb_ref, dst, i):
        cw = cw_ref[...]
        cur = raw_ref[blk[i], :]
        prev = pltpu.roll(cur, 1, axis=0)
        nxt = pltpu.roll(cur, n - 1, axis=0)
        first = raw_ref[n * i - 1:n * i, :] if i > 0 else jnp.zeros_like(cur[0:1])
        last = raw_ref[n * (i + 1):n * (i + 1) + 1, :] if i < P - 1 else jnp.zeros_like(cur[0:1])
        prev = jnp.where(row == 0, first, prev)
        nxt = jnp.where(row == n - 1, last, nxt)
        out = prev * cw[0:1] + cur * cw[1:2] + nxt * cw[2:3] + cb_ref[...]
        dst[blk[i], :] = out.astype(dst.dtype)

    def forward(src_ref, j):
        uspec_ref[j] = _bdot(fwd, src_ref[blk[j], :]).astype(uspec_ref.dtype)

    def combine(gc_ref, gs_ref, c):
        fc = slice(c * fchunk, (c + 1) * fchunk)
        fs = slice(n + c * fchunk, n + (c + 1) * fchunk)
        yc = [None] * P
        ys = [None] * P
        for j in range(P):
            uc = uspec_ref[j, fc, :]
            us = uspec_ref[j, fs, :]
            for i in range(P):
                gc = gc_ref[i - j + P - 1, fc, :]
                gs = gs_ref[i - j + P - 1, fc, :]
                tc = gc * uc - gs * us
                ts = gc * us + gs * uc
                yc[i] = tc if yc[i] is None else yc[i] + tc
                ys[i] = ts if ys[i] is None else ys[i] + ts
        for i in range(P):
            yspec_ref[i, fc, :] = yc[i].astype(BF16)
            yspec_ref[i, fs, :] = ys[i].astype(BF16)

    gate_proj = [(w1, rawb_ref, i) for i in range(P)] + [(w2, rawa_ref, i) for i in range(P)]
    n1 = n2 = 3 * P // 4
    n3 = 2 * P - n1 - n2
    per_combine = chunks // n2

    for i in range(P):
        project(wv, rawa_ref, i)
    for i in range(P):
        short_conv(rawa_ref, cwv_ref, cbv_ref, v_ref, i)
        if i < n1:
            project(*gate_proj[i])
    for i in range(P):
        forward(v_ref, i)
    for c in range(chunks):
        combine(gc0_ref, gs0_ref, c)
        if c % per_combine == 0 and c // per_combine < n2:
            project(*gate_proj[n1 + c // per_combine])
    for i in range(P):
        short_conv(rawb_ref, cw1_ref, cb1_ref, gate_ref, i)
        v_ref[blk[i], :] = (gate_ref[blk[i], :] * _bdot(inv, yspec_ref[i])).astype(v_ref.dtype)
    for i in range(P):
        forward(v_ref, i)
    for c in range(chunks):
        combine(gc1_ref, gs1_ref, c)
        if c % (chunks // n3) == 0:
            project(*gate_proj[n1 + n2 + c // (chunks // n3)])
    for i in range(P):
        short_conv(rawa_ref, cw2_ref, cb2_ref, gate_ref, i)
        z_ref[0, blk[i], :] = (gate_ref[blk[i], :] * _bdot(inv, yspec_ref[i])).astype(z_ref.dtype)


def _cast_rider(weights, n_steps, step_of):
    args, in_specs, out_specs, out_shapes = [], [], [], []
    for arr, layer in weights:
        _, R, C = arr.shape
        rps = R // n_steps
        args.append(arr)
        in_specs.append(pl.BlockSpec((None, rps, C), lambda *g, layer=layer: (layer, step_of(*g), 0)))
        out_specs.append(pl.BlockSpec((rps, C), lambda *g: (step_of(*g), 0)))
        out_shapes.append(jax.ShapeDtypeStruct((R, C), BF16))
    return args, in_specs, out_specs, out_shapes


def _hyena_core(hn, w_in, conv_w, conv_b, gc, gs, fwd, inv, P, cast_weights):
    B, L, D = hn.shape
    n = L // P
    cb_n = D // CH_BLOCK
    C = CH_BLOCK
    nd = gc.shape[0]
    c_args, c_in, c_out, c_shapes = _cast_rider(cast_weights, B * cb_n, lambda b, c: b * cb_n + c)

    def stream_spec(shape, s):
        return pl.BlockSpec(shape, lambda b, c: (0, s * cb_n + c))

    in_specs = (
        [pl.BlockSpec((1, L, D), lambda b, c: (b, 0, 0))]
        + [stream_spec((D, C), s) for s in range(3)]
        + [stream_spec((3, C), s) for s in range(3)]
        + [stream_spec((1, C), s) for s in range(3)]
        + [pl.BlockSpec((nd, n, C), lambda b, c: (0, 0, c)),
           pl.BlockSpec((nd, n, C), lambda b, c: (0, 0, c)),
           pl.BlockSpec((nd, n, C), lambda b, c: (0, 0, cb_n + c)),
           pl.BlockSpec((nd, n, C), lambda b, c: (0, 0, cb_n + c)),
           _resident((2 * n, n), lambda b, c: (0, 0)),
           _resident((n, 2 * n), lambda b, c: (0, 0))]
        + c_in)
    return pl.pallas_call(
        functools.partial(_hyena_kernel, P=P, n=n, fchunk=16, n_cast=len(c_args)),
        grid=(B, cb_n),
        in_specs=in_specs,
        out_specs=[pl.BlockSpec((1, L, C), lambda b, c: (b, 0, c))] + c_out,
        out_shape=[jax.ShapeDtypeStruct((B, L, D), BF16)] + c_shapes,
        scratch_shapes=[pltpu.VMEM((L, C), F32),
                        pltpu.VMEM((L, C), F32),
                        pltpu.VMEM((L, C), BF16),
                        pltpu.VMEM((L, C), F32),
                        pltpu.VMEM((P, 2 * n, C), SPEC_DTYPE),
                        pltpu.VMEM((P, 2 * n, C), BF16)],
        compiler_params=_cparams(2),
        name="hyena_core",
    )(hn, w_in, w_in, w_in, conv_w, conv_w, conv_w, conv_b, conv_b, conv_b,
      gc, gs, gc, gs, fwd, inv, *c_args)


def _tail_kernel(a_ref, wo_ref, r_ref, gmix_ref, gpre_ref, gpost_ref, wu_ref, wd_ref, o_ref, *, fchunk):
    sm = TAIL_SLAB_ROWS
    FF = wu_ref.shape[1]
    for p0 in range(0, a_ref.shape[0], 2 * sm):
        rows = [slice(p0 + sm * u, p0 + sm * (u + 1)) for u in range(2)]
        ms = [_bdot(a_ref[r, :], wo_ref[...]) for r in rows]
        hs = [r_ref[r, :] + _rms(m, gmix_ref[...]) for r, m in zip(rows, ms)]
        hns = [_rms(h, gpre_ref[...]).astype(BF16) for h in hs]
        accs = [None, None]
        for c in range(FF // fchunk):
            acts = []
            for u in range(2):
                a = jnp.maximum(_bdot(hns[u], wu_ref[:, c * fchunk:(c + 1) * fchunk]), 0.0)
                acts.append((a * a).astype(BF16))
            for u in range(2):
                part = _bdot(acts[u], wd_ref[c * fchunk:(c + 1) * fchunk, :])
                accs[u] = part if accs[u] is None else accs[u] + part
        for u in range(2):
            o_ref[rows[u], :] = hs[u] + _rms(accs[u], gpost_ref[...])


def _tail(a, w_o, resid, g_mix, g_pre, g_post, w_up, w_down, tm=512, fchunk=1024):
    M, K = a.shape
    D = w_o.shape[1]
    FF = w_up.shape[1]
    row = lambda shape: pl.BlockSpec(shape, lambda i: (i, 0))
    gain = pl.BlockSpec((1, D), lambda i: (0, 0))
    return pl.pallas_call(
        functools.partial(_tail_kernel, fchunk=fchunk),
        grid=(M // tm,),
        in_specs=[row((tm, K)), _resident((K, D), lambda i: (0, 0)), row((tm, D)),
                  gain, gain, gain,
                  _resident((D, FF), lambda i: (0, 0)),
                  _resident((FF, D), lambda i: (0, 0))],
        out_specs=row((tm, D)),
        out_shape=jax.ShapeDtypeStruct((M, D), F32),
        compiler_params=_cparams(1),
        name="mixer_tail_mlp",
    )(a, w_o, resid, g_mix.reshape(1, D), g_pre.reshape(1, D), g_post.reshape(1, D), w_up, w_down)


def _qkv_kernel(h_ref, g_ref, w32_ref, ct_ref, slo_ref, shi_ref, q_ref, k_ref, v_ref, w_ref, *, halves):
    @pl.when(pl.program_id(0) == 0)
    def _():
        w_ref[...] = w32_ref[...].astype(w_ref.dtype)

    nq, nk = q_ref.shape[1], k_ref.shape[1]
    hm = h_ref.shape[0] // halves
    rows = [slice(hm * u, hm * (u + 1)) for u in range(halves)]
    half = ROT_DIM // 2
    scale = LOG2E * HEAD_DIM ** -0.5
    qkvs = [_bdot(_rms(h_ref[r, :], g_ref[...]).astype(BF16), w_ref[...]) for r in rows]
    for r, qkv in zip(rows, qkvs):
        ct = ct_ref[r, :]
        slo = slo_ref[r, :]
        shi = shi_ref[r, :]
        for j in range((nq + nk) // 128):
            t = qkv[:, 128 * j:128 * (j + 1)]
            rot = t * ct + pltpu.roll(t, 128 - half, axis=1) * slo + pltpu.roll(t, half, axis=1) * shi
            if 128 * j < nq:
                q_ref[r, 128 * j:128 * (j + 1)] = (rot * scale).astype(q_ref.dtype)
            else:
                k_ref[r, 128 * j - nq:128 * (j + 1) - nq] = rot.astype(k_ref.dtype)
        v_ref[r, :] = qkv[:, nq + nk:].astype(v_ref.dtype)


def _qkv(h, g, w, S, tm=1024, halves=2):
    M, D = h.shape
    nq = D
    nk = (w.shape[1] - nq) // 2
    ct, slo, shi = [jnp.asarray(a) for a in _rope_tables(S)]
    pb = S // tm
    tab = pl.BlockSpec((tm, 128), lambda i: (i % pb, 0))
    return pl.pallas_call(
        functools.partial(_qkv_kernel, halves=halves),
        grid=(M // tm,),
        in_specs=[pl.BlockSpec((tm, D), lambda i: (i, 0)),
                  pl.BlockSpec((1, D), lambda i: (0, 0)),
                  _resident((D, nq + 2 * nk), lambda i: (0, 0)),
                  tab, tab, tab],
        out_specs=[pl.BlockSpec((tm, nq), lambda i: (i, 0)),
                   pl.BlockSpec((tm, nk), lambda i: (i, 0)),
                   pl.BlockSpec((tm, nk), lambda i: (i, 0))],
        out_shape=[jax.ShapeDtypeStruct((M, nq), BF16),
                   jax.ShapeDtypeStruct((M, nk), BF16),
                   jax.ShapeDtypeStruct((M, nk), BF16)],
        scratch_shapes=[pltpu.VMEM(w.shape, BF16)],
        compiler_params=_cparams(1),
        name="qkv_rope",
    )(h, g.reshape(1, D), w, ct, slo, shi)


def _attn_kernel(*refs, S, n_kv, n_cast):
    sink_ref, q_ref, k_ref, v_ref = refs[:4]
    cast_in = refs[4:4 + n_cast]
    o_ref = refs[4 + n_cast]
    cast_out = refs[5 + n_cast:5 + 2 * n_cast]
    krep_ref, vrep_ref = refs[5 + 2 * n_cast:]
    BQ = WINDOW
    BK = 3 * WINDOW
    GW = KV_GROUP * HEAD_DIM
    dn = (((1,), (1,)), ((), ()))

    for src, dst in zip(cast_in, cast_out):
        dst[...] = src[...].astype(dst.dtype)

    for g in range(n_kv):
        kg = k_ref[0, :, g * HEAD_DIM:(g + 1) * HEAD_DIM]
        vg = v_ref[0, :, g * HEAD_DIM:(g + 1) * HEAD_DIM]
        krep_ref[g] = jnp.concatenate([kg] * KV_GROUP, axis=1)
        vrep_ref[g] = jnp.concatenate([vg] * KV_GROUP, axis=1)

    lane = lax.broadcasted_iota(jnp.int32, (BQ, GW), 1)
    head_lanes = [(lane >= hh * HEAD_DIM) & (lane < (hh + 1) * HEAD_DIM) for hh in range(KV_GROUP)]
    rowblk = lax.broadcasted_iota(jnp.int32, (KV_GROUP * BQ, 1), 0) // BQ

    def body(qb, carry):
        qs = pl.multiple_of(qb * BQ, BQ)
        ks = pl.multiple_of(jnp.clip(qs - WINDOW, 0, S - BK), WINDOW)
        qpos = qs + lax.broadcasted_iota(jnp.int32, (BQ, BK), 0)
        kpos = ks + lax.broadcasted_iota(jnp.int32, (BQ, BK), 1)
        maskadd = jnp.where(jnp.abs(kpos - qpos) <= WINDOW, 0.0, NEG).astype(F32)
        maskadd = jnp.concatenate([maskadd] * KV_GROUP, axis=0)
        for g in range(n_kv):
            qg = q_ref[0, pl.ds(qs, BQ), g * GW:(g + 1) * GW]
            qst = jnp.concatenate([jnp.where(head_lanes[hh], qg, jnp.zeros_like(qg))
                                   for hh in range(KV_GROUP)], axis=0)
            sink = jnp.zeros((KV_GROUP * BQ, 1), F32)
            for hh in range(KV_GROUP):
                sink = jnp.where(rowblk == hh, sink_ref[g * KV_GROUP + hh] * LOG2E, sink)
            s = lax.dot_general(qst, krep_ref[g, pl.ds(ks, BK), :], dn,
                                preferred_element_type=F32) + maskadd
            m = jnp.maximum(jnp.max(s, axis=-1, keepdims=True), sink)
            p = jnp.exp2(s - m)
            denom = jnp.sum(p, axis=-1, keepdims=True) + jnp.exp2(sink - m)
            ost = _bdot(p.astype(BF16), vrep_ref[g, pl.ds(ks, BK), :]) / denom
            o = jnp.where(head_lanes[0], ost[0:BQ], 0.0)
            for hh in range(1, KV_GROUP):
                o = jnp.where(head_lanes[hh], ost[hh * BQ:(hh + 1) * BQ], o)
            o_ref[0, pl.ds(qs, BQ), g * GW:(g + 1) * GW] = o.astype(o_ref.dtype)
        return carry

    lax.fori_loop(0, S // BQ, body, 0, unroll=4)


def _attention(q, k, v, sink, cast_weights):
    B, S, NQ = q.shape
    NK = k.shape[2]
    n_kv = NK // HEAD_DIM
    c_args, c_in, c_out, c_shapes = _cast_rider(cast_weights, B, lambda b: b)
    return pl.pallas_call(
        functools.partial(_attn_kernel, S=S, n_kv=n_kv, n_cast=len(c_args)),
        grid=(B,),
        in_specs=[pl.BlockSpec(memory_space=pltpu.SMEM),
                  pl.BlockSpec((1, S, NQ), lambda b: (b, 0, 0)),
                  pl.BlockSpec((1, S, NK), lambda b: (b, 0, 0)),
                  pl.BlockSpec((1, S, NK), lambda b: (b, 0, 0))] + c_in,
        out_specs=[pl.BlockSpec((1, S, NQ), lambda b: (b, 0, 0))] + c_out,
        out_shape=[jax.ShapeDtypeStruct((B, S, NQ), BF16)] + c_shapes,
        scratch_shapes=[pltpu.VMEM((n_kv, S, KV_GROUP * HEAD_DIM), BF16),
                        pltpu.VMEM((n_kv, S, KV_GROUP * HEAD_DIM), BF16)],
        compiler_params=_cparams(1),
        name="window_attention",
    )(sink, q, k, v, *c_args)


def kernel(x, norm_mix_pre, norm_mix_post, norm_mlp_pre, norm_mlp_post, w_up, w_down, hy_w_in, hy_conv_w, hy_conv_b, hy_f_w1, hy_f_b1, hy_f_w2, hy_f_b2, hy_f_w3, hy_f_b3, hy_f_freq, hy_f_wout, hy_bias, hy_w_out, at_w_qkv, at_sink, at_w_o):
    B, L, D = x.shape
    M = B * L
    P = CONV_BLOCKS
    n = L // P
    depth = norm_mix_pre.shape[0]
    fwd, inv, filt_c, filt_s = [jnp.asarray(a).astype(BF16) for a in _dft_tables(n)]

    h = x.reshape(M, D)
    for i in range(depth):
        j = i // 2
        if i % 2 == 0:
            gc, gs, hn, w_in_b = _filter_spectra(
                L, hy_f_w1[j], hy_f_b1[j], hy_f_w2[j], hy_f_b2[j], hy_f_w3[j], hy_f_b3[j],
                hy_f_freq[j], hy_f_wout[j], hy_bias[j], filt_c, filt_s, P, h, norm_mix_pre[i],
                [(hy_w_in, j)])
            z, wo_b, wu_b, wd_b = _hyena_core(
                hn.reshape(B, L, D), w_in_b, hy_conv_w[j], hy_conv_b[j].reshape(1, -1),
                gc, gs, fwd, inv, P, [(hy_w_out, j), (w_up, i), (w_down, i)])
            a = z.reshape(M, D)
        else:
            q, k, v = _qkv(h, norm_mix_pre[i], at_w_qkv[j], L)
            o, wo_b, wu_b, wd_b = _attention(
                q.reshape(B, L, -1), k.reshape(B, L, -1), v.reshape(B, L, -1), at_sink[j],
                [(at_w_o, j), (w_up, i), (w_down, i)])
            a = o.reshape(M, -1)
        h = _tail(a, wo_b, h, norm_mix_post[i], norm_mlp_pre[i], norm_mlp_post[i], wu_b, wd_b)
    return h.reshape(B, L, D)
```

```python
import functools
import math

import numpy as np
import jax
import jax.numpy as jnp
from jax import lax
from jax.experimental import pallas as pl
from jax.experimental.pallas import tpu as pltpu

BF16 = jnp.bfloat16
F32 = jnp.float32

EPS = 1e-6
NEG = -1e30
LOG2E = math.log2(math.e)

HY_ORDER = 2
HY_EMB = 33
HY_BANDS = (HY_EMB - 1) // 2
HY_DECAY_TARGET = 1e-2
HY_FAST_DECAY = 0.3
HY_SLOW_DECAY = 1.5
HEAD_DIM = 64
KV_GROUP = 4
WINDOW = 128
ROT_DIM = HEAD_DIM // 4
ROPE_THETA = 500000.0

CONV_BLOCKS = 4
CH_BLOCK = 256
SPEC_DTYPE = BF16
TAIL_SLAB_ROWS = 256

VMEM_LIMIT = 56 * 1024 * 1024


def _cparams(n_axes, flags=None):
    return pltpu.CompilerParams(
        dimension_semantics=("arbitrary",) * n_axes,
        vmem_limit_bytes=VMEM_LIMIT,
        flags=flags)


def _resident(shape, index_map):
    return pl.BlockSpec(shape, index_map, pipeline_mode=pl.Buffered(1))


def _rms(x, g):
    ms = jnp.mean(x * x, axis=-1, keepdims=True)
    return x * lax.rsqrt(ms + EPS) * g


def _bdot(a, b):
    return jnp.dot(a, b, preferred_element_type=F32)


def _dft_tables(n):
    f = np.arange(n, dtype=np.int64)[:, None]
    r = np.arange(n, dtype=np.int64)[None, :]
    k_pos = ((2 * f + 1) * r) % (4 * n)
    k_neg = ((2 * f + 1) * (r - n)) % (4 * n)
    ang_pos = np.pi * k_pos / (2 * n)
    ang_neg = np.pi * k_neg / (2 * n)
    cpos, spos = np.cos(ang_pos), np.sin(ang_pos)
    cneg, sneg = np.cos(ang_neg), np.sin(ang_neg)
    cneg[:, 0] = 0.0
    sneg[:, 0] = 0.0
    fwd = np.concatenate([cpos, spos], axis=0)
    inv = np.concatenate([cpos.T, spos.T], axis=1) / n
    filt_c = np.concatenate([cneg, cpos], axis=1)
    filt_s = np.concatenate([sneg, spos], axis=1)
    return [np.asarray(a, np.float32) for a in (fwd, inv, filt_c, filt_s)]


def _filter_positions(L):
    t = np.linspace(0.0, 1.0, L)[:, None]
    w = (2.0 * np.pi / L) * np.arange(L)[:, None]
    f = np.linspace(1e-4, HY_BANDS - 1, HY_BANDS)[None, :]
    z = np.concatenate([t, np.cos(f * w), -np.sin(f * w)], axis=-1)
    return np.asarray(z, np.float32)


def _rope_tables(S):
    half = ROT_DIM // 2
    inv = ROPE_THETA ** (-np.arange(0, ROT_DIM, 2, dtype=np.float64) / ROT_DIM)
    ang = np.arange(S, dtype=np.float64)[:, None] * inv[None, :]
    lane = np.arange(128) % HEAD_DIM
    ang_l = ang[:, lane % half]
    cos_t = np.where(lane[None, :] < ROT_DIM, np.cos(ang_l), 1.0)
    sin_lo = np.where(lane[None, :] < half, -np.sin(ang_l), 0.0)
    sin_hi = np.where((lane[None, :] >= half) & (lane[None, :] < ROT_DIM), np.sin(ang_l), 0.0)
    return [np.asarray(a, np.float32) for a in (cos_t, sin_lo, sin_hi)]


def _spectra_kernel(*refs, P, n, norm_rows, n_cast):
    (z_ref, w1_ref, b1_ref, w2_ref, b2_ref, w3_ref, b3_ref, fr_ref,
     woutf_ref, woutb_ref, dl_ref, bias_ref, mc_ref, ms_ref, x_ref, g_ref) = refs[:16]
    cast_in = refs[16:16 + n_cast]
    gc_ref, gs_ref, hn_ref = refs[16 + n_cast:19 + n_cast]
    cast_out = refs[19 + n_cast:19 + 2 * n_cast]
    hid_ref, hf_ref, hb_ref = refs[19 + 2 * n_cast:]
    hi = lax.Precision.HIGHEST

    for src, dst in zip(cast_in, cast_out):
        dst[...] = src[...].astype(dst.dtype)

    L, cw = hf_ref.shape
    FW = hid_ref.shape[1]

    @pl.when(pl.program_id(0) == 0)
    def _():
        def two(v):
            return jnp.concatenate([v, v], axis=1)

        def blockdiag(w):
            zero = jnp.zeros_like(w)
            return jnp.concatenate([jnp.concatenate([w, zero], axis=1),
                                    jnp.concatenate([zero, w], axis=1)], axis=0)

        fr = two(fr_ref[...])
        h = z_ref[...]
        for w_ref, b_ref in ((w1_ref, b1_ref), (w2_ref, b2_ref), (w3_ref, b3_ref)):
            h = jnp.sin(fr * (jnp.dot(h, blockdiag(w_ref[...]), precision=hi) + two(b_ref[...])))
        hid_ref[0:L // 2, :] = h[:, :FW]
        hid_ref[L // 2:L, :] = h[:, FW:]

    row = lax.broadcasted_iota(jnp.int32, (L, cw), 0)
    decay = jnp.exp(-(row.astype(F32) * (1.0 / (L - 1))) * dl_ref[...])
    hid = hid_ref[...]
    hid_hi = hid.astype(BF16)
    hid_lo = (hid - hid_hi.astype(F32)).astype(BF16)

    def taps(wout_ref):
        wout = wout_ref[...]
        w_hi = wout.astype(BF16)
        w_lo = (wout - w_hi.astype(F32)).astype(BF16)
        return (_bdot(hid_hi, w_hi) + (_bdot(hid_hi, w_lo) + _bdot(hid_lo, w_hi))) * decay

    hf_ref[...] = taps(woutf_ref).astype(hf_ref.dtype)
    hb_ref[...] = jnp.where(row > 0, taps(woutb_ref), 0.0).astype(hb_ref.dtype)

    mc = mc_ref[...]
    ms = ms_ref[...]
    cpos = mc[:, n:]
    spos = ms[:, n:]
    norm_chunks = [slice(r, r + norm_rows) for r in range(0, x_ref.shape[0], norm_rows)]
    nd = 2 * P - 1
    for idx, d in enumerate(range(-(P - 1), P)):
        for r in norm_chunks[idx * len(norm_chunks) // nd:(idx + 1) * len(norm_chunks) // nd]:
            hn_ref[r, :] = _rms(x_ref[r, :], g_ref[...]).astype(hn_ref.dtype)
        if d == 0:
            hf0 = hf_ref[0:n, :].astype(F32)
            hb0 = hb_ref[0:n, :].astype(F32)
            gc = _bdot(cpos, (hf0 + hb0).astype(BF16)) + bias_ref[...]
            gs = _bdot(spos, (hf0 - hb0).astype(BF16))
        elif d > 0:
            taps = hf_ref[n * (d - 1):n * (d + 1), :].astype(BF16)
            gc = _bdot(mc, taps)
            gs = _bdot(ms, taps)
        else:
            a = -d
            taps = hb_ref[n * (a - 1):n * (a + 1), :].astype(BF16)
            gc = _bdot(mc, taps)
            gs = -_bdot(ms, taps)
        gc_ref[d + P - 1] = gc.astype(gc_ref.dtype)
        gs_ref[d + P - 1] = gs.astype(gs_ref.dtype)


def _filter_spectra(L, f_w1, f_b1, f_w2, f_b2, f_w3, f_b3, f_freq, f_wout, bias, mc, ms, P, x2d, g,
                    cast_weights, cw=256, norm_rows=256):
    M, D = x2d.shape
    FW = f_w1.shape[1]
    n = L // P
    half = f_wout.shape[1] // 2
    nblk = half // cw
    xm = M // nblk
    zpad = np.pad(_filter_positions(L), ((0, 0), (0, FW - HY_EMB)))
    z = jnp.asarray(np.concatenate([zpad[:L // 2], zpad[L // 2:]], axis=1))
    w1 = jnp.pad(f_w1, ((0, FW - HY_EMB), (0, 0)))
    max_decay = math.log(HY_DECAY_TARGET) / HY_FAST_DECAY
    min_decay = math.log(HY_DECAY_TARGET) / HY_SLOW_DECAY
    absdelta = jnp.asarray(np.abs(np.linspace(min_decay, max_decay, D)).astype(np.float32))[None, :]
    small = lambda shape: pl.BlockSpec(shape, lambda j: (0, 0))
    out_sd = jax.ShapeDtypeStruct((2 * P - 1, n, half), SPEC_DTYPE)
    c_args, c_in, c_out, c_shapes = _cast_rider(cast_weights, nblk, lambda j: j)
    return pl.pallas_call(
        functools.partial(_spectra_kernel, P=P, n=n, norm_rows=norm_rows, n_cast=len(c_args)),
        grid=(nblk,),
        in_specs=[small((L // 2, 2 * FW)), small((FW, FW)), small((1, FW)), small((FW, FW)), small((1, FW)),
                  small((FW, FW)), small((1, FW)), small((1, FW)),
                  pl.BlockSpec((FW, cw), lambda j: (0, j)),
                  pl.BlockSpec((FW, cw), lambda j: (0, nblk + j)),
                  pl.BlockSpec((1, cw), lambda j: (0, j % (D // cw))),
                  pl.BlockSpec((1, cw), lambda j: (0, j)),
                  small((n, 2 * n)), small((n, 2 * n)),
                  pl.BlockSpec((xm, D), lambda j: (j, 0)),
                  small((1, D))] + c_in,
        out_specs=[pl.BlockSpec((2 * P - 1, n, cw), lambda j: (0, 0, j)),
                   pl.BlockSpec((2 * P - 1, n, cw), lambda j: (0, 0, j)),
                   pl.BlockSpec((xm, D), lambda j: (j, 0))] + c_out,
        out_shape=[out_sd, out_sd, jax.ShapeDtypeStruct((M, D), BF16)] + c_shapes,
        scratch_shapes=[pltpu.VMEM((L, FW), F32),
                        pltpu.VMEM((L, cw), BF16),
                        pltpu.VMEM((L, cw), BF16)],
        compiler_params=_cparams(1),
        name="filter_spectra",
    )(z, w1, f_b1.reshape(1, FW), f_w2, f_b2.reshape(1, FW), f_w3, f_b3.reshape(1, FW),
      f_freq.reshape(1, FW), f_wout, f_wout, absdelta, bias.reshape(1, half), mc, ms,
      x2d, g.reshape(1, D), *c_args)


def _hyena_kernel(*refs, P, n, fchunk, n_cast):
    (hn_ref, wv_ref, w1_ref, w2_ref, cwv_ref, cw1_ref, cw2_ref, cbv_ref, cb1_ref, cb2_ref,
     gc0_ref, gs0_ref, gc1_ref, gs1_ref, fwd_ref, inv_ref) = refs[:16]
    cast_in = refs[16:16 + n_cast]
    z_ref = refs[16 + n_cast]
    cast_out = refs[17 + n_cast:17 + 2 * n_cast]
    rawa_ref, rawb_ref, v_ref, gate_ref, uspec_ref, yspec_ref = refs[17 + 2 * n_cast:]

    for src, dst in zip(cast_in, cast_out):
        dst[...] = src[...].astype(dst.dtype)

    row = lax.broadcasted_iota(jnp.int32, (n, 1), 0)
    blk = [slice(n * i, n * (i + 1)) for i in range(P)]
    chunks = n // fchunk
    wv, w1, w2 = wv_ref, w1_ref, w2_ref

    def project(w_ref, dst_ref, i):
        dst_ref[blk[i], :] = _bdot(hn_ref[0, blk[i], :], w_ref[...])

    def short_conv(raw_ref, cw_ref, cb_ref, dst, i):
        cw = cw_ref[...]
        cur = raw_ref[blk[i], :]
        prev = pltpu.roll(cur, 1, axis=0)
        nxt = pltpu.roll(cur, n - 1, axis=0)
        first = raw_ref[n * i - 1:n * i, :] if i > 0 else jnp.zeros_like(cur[0:1])
        last = raw_ref[n * (i + 1):n * (i + 1) + 1, :] if i < P - 1 else jnp.zeros_like(cur[0:1])
        prev = jnp.where(row == 0, first, prev)
        nxt = jnp.where(row == n - 1, last, nxt)
        out = prev * cw[0:1] + cur * cw[1:2] + nxt * cw[2:3] + cb_ref[...]
        dst[blk[i], :] = out.astype(dst.dtype)

    def forward(src_ref, j):
        uspec_ref[j] = _bdot(fwd_ref[...], src_ref[blk[j], :]).astype(uspec_ref.dtype)

    def combine(gc_ref, gs_ref, c):
        fc = slice(c * fchunk, (c + 1) * fchunk)
        fs = slice(n + c * fchunk, n + (c + 1) * fchunk)
        yc = [None] * P
        ys = [None] * P
        for j in range(P):
            uc = uspec_ref[j, fc, :]
            us = uspec_ref[j, fs, :]
            for i in range(P):
                gc = gc_ref[i - j + P - 1, fc, :]
                gs = gs_ref[i - j + P - 1, fc, :]
                tc = gc * uc - gs * us
                ts = gc * us + gs * uc
                yc[i] = tc if yc[i] is None else yc[i] + tc
                ys[i] = ts if ys[i] is None else ys[i] + ts
        for i in range(P):
            yspec_ref[i, fc, :] = yc[i].astype(BF16)
            yspec_ref[i, fs, :] = ys[i].astype(BF16)

    gate_proj = [(w1, rawb_ref, i) for i in range(P)] + [(w2, rawa_ref, i) for i in range(P)]
    per_combine = chunks // 3

    for i in range(P):
        project(wv, rawa_ref, i)
    for i in range(P):
        short_conv(rawa_ref, cwv_ref, cbv_ref, v_ref, i)
        if i < 3:
            project(*gate_proj[i])
    for i in range(P):
        forward(v_ref, i)
    for c in range(chunks):
        combine(gc0_ref, gs0_ref, c)
        if c % per_combine == 0 and c // per_combine < 3:
            project(*gate_proj[3 + c // per_combine])
    for i in range(P):
        short_conv(rawb_ref, cw1_ref, cb1_ref, gate_ref, i)
        v_ref[blk[i], :] = (gate_ref[blk[i], :] * _bdot(inv_ref[...], yspec_ref[i])).astype(v_ref.dtype)
    for i in range(P):
        forward(v_ref, i)
    for c in range(chunks):
        combine(gc1_ref, gs1_ref, c)
        if c % (chunks // 2) == 0:
            project(*gate_proj[6 + c // (chunks // 2)])
    for i in range(P):
        short_conv(rawa_ref, cw2_ref, cb2_ref, gate_ref, i)
        z_ref[0, blk[i], :] = (gate_ref[blk[i], :] * _bdot(inv_ref[...], yspec_ref[i])).astype(z_ref.dtype)


def _cast_rider(weights, n_steps, step_of):
    args, in_specs, out_specs, out_shapes = [], [], [], []
    for arr, layer in weights:
        _, R, C = arr.shape
        rps = R // n_steps
        args.append(arr)
        in_specs.append(pl.BlockSpec((None, rps, C), lambda *g, layer=layer: (layer, step_of(*g), 0)))
        out_specs.append(pl.BlockSpec((rps, C), lambda *g: (step_of(*g), 0)))
        out_shapes.append(jax.ShapeDtypeStruct((R, C), BF16))
    return args, in_specs, out_specs, out_shapes


def _hyena_core(hn, w_in, conv_w, conv_b, gc, gs, fwd, inv, P, cast_weights):
    B, L, D = hn.shape
    n = L // P
    cb_n = D // CH_BLOCK
    C = CH_BLOCK
    nd = gc.shape[0]
    c_args, c_in, c_out, c_shapes = _cast_rider(cast_weights, B * cb_n, lambda b, c: b * cb_n + c)

    def stream_spec(shape, s):
        return pl.BlockSpec(shape, lambda b, c: (0, s * cb_n + c))

    in_specs = (
        [pl.BlockSpec((1, L, D), lambda b, c: (b, 0, 0))]
        + [stream_spec((D, C), s) for s in range(3)]
        + [stream_spec((3, C), s) for s in range(3)]
        + [stream_spec((1, C), s) for s in range(3)]
        + [pl.BlockSpec((nd, n, C), lambda b, c: (0, 0, c)),
           pl.BlockSpec((nd, n, C), lambda b, c: (0, 0, c)),
           pl.BlockSpec((nd, n, C), lambda b, c: (0, 0, cb_n + c)),
           pl.BlockSpec((nd, n, C), lambda b, c: (0, 0, cb_n + c)),
           _resident((2 * n, n), lambda b, c: (0, 0)),
           _resident((n, 2 * n), lambda b, c: (0, 0))]
        + c_in)
    return pl.pallas_call(
        functools.partial(_hyena_kernel, P=P, n=n, fchunk=16, n_cast=len(c_args)),
        grid=(B, cb_n),
        in_specs=in_specs,
        out_specs=[pl.BlockSpec((1, L, C), lambda b, c: (b, 0, c))] + c_out,
        out_shape=[jax.ShapeDtypeStruct((B, L, D), BF16)] + c_shapes,
        scratch_shapes=[pltpu.VMEM((L, C), F32),
                        pltpu.VMEM((L, C), F32),
                        pltpu.VMEM((L, C), BF16),
                        pltpu.VMEM((L, C), F32),
                        pltpu.VMEM((P, 2 * n, C), SPEC_DTYPE),
                        pltpu.VMEM((P, 2 * n, C), BF16)],
        compiler_params=_cparams(2),
        name="hyena_core",
    )(hn, w_in, w_in, w_in, conv_w, conv_w, conv_w, conv_b, conv_b, conv_b,
      gc, gs, gc, gs, fwd, inv, *c_args)


def _tail_kernel(a_ref, wo_ref, r_ref, gmix_ref, gpre_ref, gpost_ref, wu_ref, wd_ref, o_ref, *, fchunk):
    sm = TAIL_SLAB_ROWS
    FF = wu_ref.shape[1]
    for p0 in range(0, a_ref.shape[0], 2 * sm):
        rows = [slice(p0 + sm * u, p0 + sm * (u + 1)) for u in range(2)]
        ms = [_bdot(a_ref[r, :], wo_ref[...]) for r in rows]
        hs = [r_ref[r, :] + _rms(m, gmix_ref[...]) for r, m in zip(rows, ms)]
        hns = [_rms(h, gpre_ref[...]).astype(BF16) for h in hs]
        accs = [None, None]
        for c in range(FF // fchunk):
            acts = []
            for u in range(2):
                a = jnp.maximum(_bdot(hns[u], wu_ref[:, c * fchunk:(c + 1) * fchunk]), 0.0)
                acts.append((a * a).astype(BF16))
            for u in range(2):
                part = _bdot(acts[u], wd_ref[c * fchunk:(c + 1) * fchunk, :])
                accs[u] = part if accs[u] is None else accs[u] + part
        for u in range(2):
            o_ref[rows[u], :] = hs[u] + _rms(accs[u], gpost_ref[...])


def _tail(a, w_o, resid, g_mix, g_pre, g_post, w_up, w_down, tm=512, fchunk=1024):
    M, K = a.shape
    D = w_o.shape[1]
    FF = w_up.shape[1]
    row = lambda shape: pl.BlockSpec(shape, lambda i: (i, 0))
    gain = pl.BlockSpec((1, D), lambda i: (0, 0))
    return pl.pallas_call(
        functools.partial(_tail_kernel, fchunk=fchunk),
        grid=(M // tm,),
        in_specs=[row((tm, K)), _resident((K, D), lambda i: (0, 0)), row((tm, D)),
                  gain, gain, gain,
                  _resident((D, FF), lambda i: (0, 0)),
                  _resident((FF, D), lambda i: (0, 0))],
        out_specs=row((tm, D)),
        out_shape=jax.ShapeDtypeStruct((M, D), F32),
        compiler_params=_cparams(1),
        name="mixer_tail_mlp",
    )(a, w_o, resid, g_mix.reshape(1, D), g_pre.reshape(1, D), g_post.reshape(1, D), w_up, w_down)


def _qkv_kernel(h_ref, g_ref, w32_ref, ct_ref, slo_ref, shi_ref, q_ref, k_ref, v_ref, w_ref, *, halves):
    @pl.when(pl.program_id(0) == 0)
    def _():
        w_ref[...] = w32_ref[...].astype(w_ref.dtype)

    nq, nk = q_ref.shape[1], k_ref.shape[1]
    hm = h_ref.shape[0] // halves
    rows = [slice(hm * u, hm * (u + 1)) for u in range(halves)]
    half = ROT_DIM // 2
    scale = LOG2E * HEAD_DIM ** -0.5
    qkvs = [_bdot(_rms(h_ref[r, :], g_ref[...]).astype(BF16), w_ref[...]) for r in rows]
    for r, qkv in zip(rows, qkvs):
        ct = ct_ref[r, :]
        slo = slo_ref[r, :]
        shi = shi_ref[r, :]
        for j in range((nq + nk) // 128):
            t = qkv[:, 128 * j:128 * (j + 1)]
            rot = t * ct + pltpu.roll(t, 128 - half, axis=1) * slo + pltpu.roll(t, half, axis=1) * shi
            if 128 * j < nq:
                q_ref[r, 128 * j:128 * (j + 1)] = (rot * scale).astype(q_ref.dtype)
            else:
                k_ref[r, 128 * j - nq:128 * (j + 1) - nq] = rot.astype(k_ref.dtype)
        v_ref[r, :] = qkv[:, nq + nk:].astype(v_ref.dtype)


def _qkv(h, g, w, S, tm=1024, halves=2):
    M, D = h.shape
    nq = D
    nk = (w.shape[1] - nq) // 2
    ct, slo, shi = [jnp.asarray(a) for a in _rope_tables(S)]
    pb = S // tm
    tab = pl.BlockSpec((tm, 128), lambda i: (i % pb, 0))
    return pl.pallas_call(
        functools.partial(_qkv_kernel, halves=halves),
        grid=(M // tm,),
        in_specs=[pl.BlockSpec((tm, D), lambda i: (i, 0)),
                  pl.BlockSpec((1, D), lambda i: (0, 0)),
                  _resident((D, nq + 2 * nk), lambda i: (0, 0)),
                  tab, tab, tab],
        out_specs=[pl.BlockSpec((tm, nq), lambda i: (i, 0)),
                   pl.BlockSpec((tm, nk), lambda i: (i, 0)),
                   pl.BlockSpec((tm, nk), lambda i: (i, 0))],
        out_shape=[jax.ShapeDtypeStruct((M, nq), BF16),
                   jax.ShapeDtypeStruct((M, nk), BF16),
                   jax.ShapeDtypeStruct((M, nk), BF16)],
        scratch_shapes=[pltpu.VMEM(w.shape, BF16)],
        compiler_params=_cparams(1),
        name="qkv_rope",
    )(h, g.reshape(1, D), w, ct, slo, shi)


def _attn_kernel(*refs, S, n_kv, n_cast):
    sink_ref, q_ref, k_ref, v_ref = refs[:4]
    cast_in = refs[4:4 + n_cast]
    o_ref = refs[4 + n_cast]
    cast_out = refs[5 + n_cast:5 + 2 * n_cast]
    krep_ref, vrep_ref = refs[5 + 2 * n_cast:]
    BQ = WINDOW
    BK = 3 * WINDOW
    GW = KV_GROUP * HEAD_DIM
    dn = (((1,), (1,)), ((), ()))

    for src, dst in zip(cast_in, cast_out):
        dst[...] = src[...].astype(dst.dtype)

    for g in range(n_kv):
        kg = k_ref[0, :, g * HEAD_DIM:(g + 1) * HEAD_DIM]
        vg = v_ref[0, :, g * HEAD_DIM:(g + 1) * HEAD_DIM]
        krep_ref[g] = jnp.concatenate([kg] * KV_GROUP, axis=1)
        vrep_ref[g] = jnp.concatenate([vg] * KV_GROUP, axis=1)

    lane = lax.broadcasted_iota(jnp.int32, (BQ, GW), 1)
    head_lanes = [(lane >= hh * HEAD_DIM) & (lane < (hh + 1) * HEAD_DIM) for hh in range(KV_GROUP)]
    rowblk = lax.broadcasted_iota(jnp.int32, (KV_GROUP * BQ, 1), 0) // BQ

    def body(qb, carry):
        qs = pl.multiple_of(qb * BQ, BQ)
        ks = pl.multiple_of(jnp.clip(qs - WINDOW, 0, S - BK), WINDOW)
        qpos = qs + lax.broadcasted_iota(jnp.int32, (BQ, BK), 0)
        kpos = ks + lax.broadcasted_iota(jnp.int32, (BQ, BK), 1)
        maskadd = jnp.where(jnp.abs(kpos - qpos) <= WINDOW, 0.0, NEG).astype(F32)
        maskadd = jnp.concatenate([maskadd] * KV_GROUP, axis=0)
        for g in range(n_kv):
            qg = q_ref[0, pl.ds(qs, BQ), g * GW:(g + 1) * GW]
            qst = jnp.concatenate([jnp.where(head_lanes[hh], qg, jnp.zeros_like(qg))
                                   for hh in range(KV_GROUP)], axis=0)
            sink = jnp.zeros((KV_GROUP * BQ, 1), F32)
            for hh in range(KV_GROUP):
                sink = jnp.where(rowblk == hh, sink_ref[g * KV_GROUP + hh] * LOG2E, sink)
            s = lax.dot_general(qst, krep_ref[g, pl.ds(ks, BK), :], dn,
                                preferred_element_type=F32) + maskadd
            m = jnp.maximum(jnp.max(s, axis=-1, keepdims=True), sink)
            p = jnp.exp2(s - m)
            denom = jnp.sum(p, axis=-1, keepdims=True) + jnp.exp2(sink - m)
            ost = _bdot(p.astype(BF16), vrep_ref[g, pl.ds(ks, BK), :]) / denom
            o = jnp.where(head_lanes[0], ost[0:BQ], 0.0)
            for hh in range(1, KV_GROUP):
                o = jnp.where(head_lanes[hh], ost[hh * BQ:(hh + 1) * BQ], o)
            o_ref[0, pl.ds(qs, BQ), g * GW:(g + 1) * GW] = o.astype(o_ref.dtype)
        return carry

    lax.fori_loop(0, S // BQ, body, 0, unroll=4)


def _attention(q, k, v, sink, cast_weights):
    B, S, NQ = q.shape
    NK = k.shape[2]
    n_kv = NK // HEAD_DIM
    c_args, c_in, c_out, c_shapes = _cast_rider(cast_weights, B, lambda b: b)
    return pl.pallas_call(
        functools.partial(_attn_kernel, S=S, n_kv=n_kv, n_cast=len(c_args)),
        grid=(B,),
        in_specs=[pl.BlockSpec(memory_space=pltpu.SMEM),
                  pl.BlockSpec((1, S, NQ), lambda b: (b, 0, 0)),
                  pl.BlockSpec((1, S, NK), lambda b: (b, 0, 0)),
                  pl.BlockSpec((1, S, NK), lambda b: (b, 0, 0))] + c_in,
        out_specs=[pl.BlockSpec((1, S, NQ), lambda b: (b, 0, 0))] + c_out,
        out_shape=[jax.ShapeDtypeStruct((B, S, NQ), BF16)] + c_shapes,
        scratch_shapes=[pltpu.VMEM((n_kv, S, KV_GROUP * HEAD_DIM), BF16),
                        pltpu.VMEM((n_kv, S, KV_GROUP * HEAD_DIM), BF16)],
        compiler_params=_cparams(1),
        name="window_attention",
    )(sink, q, k, v, *c_args)


def kernel(x, norm_mix_pre, norm_mix_post, norm_mlp_pre, norm_mlp_post, w_up, w_down, hy_w_in, hy_conv_w, hy_conv_b, hy_f_w1, hy_f_b1, hy_f_w2, hy_f_b2, hy_f_w3, hy_f_b3, hy_f_freq, hy_f_wout, hy_bias, hy_w_out, at_w_qkv, at_sink, at_w_o):
    B, L, D = x.shape
    M = B * L
    P = CONV_BLOCKS
    n = L // P
    depth = norm_mix_pre.shape[0]
    fwd, inv, filt_c, filt_s = [jnp.asarray(a).astype(BF16) for a in _dft_tables(n)]

    h = x.reshape(M, D)
    for i in range(depth):
        j = i // 2
        if i % 2 == 0:
            gc, gs, hn, w_in_b = _filter_spectra(
                L, hy_f_w1[j], hy_f_b1[j], hy_f_w2[j], hy_f_b2[j], hy_f_w3[j], hy_f_b3[j],
                hy_f_freq[j], hy_f_wout[j], hy_bias[j], filt_c, filt_s, P, h, norm_mix_pre[i],
                [(hy_w_in, j)])
            z, wo_b, wu_b, wd_b = _hyena_core(
                hn.reshape(B, L, D), w_in_b, hy_conv_w[j], hy_conv_b[j].reshape(1, -1),
                gc, gs, fwd, inv, P, [(hy_w_out, j), (w_up, i), (w_down, i)])
            a = z.reshape(M, D)
        else:
            q, k, v = _qkv(h, norm_mix_pre[i], at_w_qkv[j], L)
            o, wo_b, wu_b, wd_b = _attention(
                q.reshape(B, L, -1), k.reshape(B, L, -1), v.reshape(B, L, -1), at_sink[j],
                [(at_w_o, j), (w_up, i), (w_down, i)])
            a = o.reshape(M, -1)
        h = _tail(a, wo_b, h, norm_mix_post[i], norm_mlp_pre[i], norm_mlp_post[i], wu_b, wd_b)
    return h.reshape(B, L, D)
```
